```python
import math
import jax, jax.numpy as jnp
from jax import lax
import numpy as np

D_MODEL = 2048
BATCH = 8
SEQ = 2048
DEPTH = 2
DEC_BATCH = 1
DEC_SEQ = 16384
PAST_LEN = 128

HEAD_DIM = 128
HY_W = 3 * D_MODEL // 8
GM_W = D_MODEL // 4
HG_W = D_MODEL - HY_W - GM_W
MIX_W = HY_W + GM_W + HG_W
HY_ORDER = 2
HY_SHORT = 3
HY_POS_BANDS = 16
HY_POS_DIM = 2 * HY_POS_BANDS + 1
HY_FILT_HID = 64
HY_FAST_DECAY = 0.3
HY_SLOW_DECAY = 1.5
HY_DECAY_TARGET = 1e-2
GM_CHUNK = 128
GM_GROUPS = GM_W // HEAD_DIM
HG_HEADS = HG_W // HEAD_DIM
HG_CHUNK = 32
IN_W = 3 * HY_W + 2 * GM_W + 5 * HG_W
D_FF = 256 * ((8 * D_MODEL // 3 + 255) // 256)
N_EXPERTS = 8
TOP_K = 2
D_FF_EXPERT = D_FF
MOE_BLOCK = 128
N_DENSE = (DEPTH + 1) // 2
N_MOE = DEPTH // 2
ALPHA = (2 * DEPTH) ** 0.25
BETA = (8 * DEPTH) ** -0.25
LN_EPS = 1e-5
RMS_EPS = 1e-6

kernel_name = 'hyena_gmlp_hgrn2_parallel_encoder'


def _layernorm(x, g, b):
    xf = x.astype(jnp.float32)
    mu = jnp.mean(xf, -1, keepdims=True)
    var = jnp.mean(jnp.square(xf - mu), -1, keepdims=True)
    return ((xf - mu) * lax.rsqrt(var + LN_EPS) * g.astype(jnp.float32) + b.astype(jnp.float32)).astype(x.dtype)


def _short_conv3(u, w, b):
    up = jnp.pad(u, ((0, 0), (1, 1), (0, 0)))
    return up[:, :-2] * w[0] + up[:, 1:-1] * w[1] + up[:, 2:] * w[2] + b


def _hyena_filter_spectrum(L, w1, b1, w2, b2, w3, b3, freq, w_out):
    f32 = jnp.float32
    pos = jnp.arange(L, dtype=f32)
    t = pos / max(L - 1, 1)
    bands = jnp.linspace(1e-4, HY_POS_BANDS - 1, HY_POS_BANDS, dtype=f32)
    ang = (2.0 * math.pi / L) * pos[:, None] * bands[None, :]
    feats = jnp.concatenate([t[:, None], jnp.cos(ang), -jnp.sin(ang)], -1)
    fr = freq.astype(f32)
    h = jnp.sin(fr * (feats @ w1.astype(f32) + b1.astype(f32)))
    h = jnp.sin(fr * (h @ w2.astype(f32) + b2.astype(f32)))
    h = jnp.sin(fr * (h @ w3.astype(f32) + b3.astype(f32)))
    h = h @ w_out.astype(f32)
    deltas = jnp.abs(jnp.linspace(math.log(HY_DECAY_TARGET) / HY_SLOW_DECAY,
                                  math.log(HY_DECAY_TARGET) / HY_FAST_DECAY, HY_W, dtype=f32))
    window = jnp.exp(-t[:, None] * deltas[None, :])
    h = h.reshape(L, HY_ORDER, 2, HY_W) * window[:, None, None, :]
    h_fwd, h_bwd = h[:, :, 0], h[:, :, 1]
    k = jnp.concatenate([h_fwd, jnp.zeros((1, HY_ORDER, HY_W), f32), h_bwd[:0:-1]], 0)
    k = k / jnp.sum(jnp.abs(k), axis=0, keepdims=True)
    return jnp.fft.rfft(k, axis=0)


def _hyena(p, conv_w, conv_b, w1, b1, w2, b2, w3, b3, freq, w_out, bias):
    B, L, _ = p.shape
    f32 = jnp.float32
    u = _short_conv3(p, conv_w, conv_b)
    x1, x2, v = jnp.split(u, 3, axis=-1)
    spec = _hyena_filter_spectrum(L, w1, b1, w2, b2, w3, b3, freq, w_out)
    z = v.astype(f32)
    for o, gate in enumerate((x1, x2)):
        zf = jnp.fft.rfft(z, n=2 * L, axis=1)
        z = jnp.fft.irfft(zf * spec[:, o], n=2 * L, axis=1)[:, :L] + z * bias[o].astype(f32)
        z = gate.astype(f32) * z
    return z.astype(p.dtype)


def _gmlp(p, ln_g, ln_b, ws, bs):
    B, L, _ = p.shape
    u, v = jnp.split(jax.nn.gelu(p), 2, axis=-1)
    v = _layernorm(v, ln_g, ln_b)
    v = v.reshape(B, L // GM_CHUNK, GM_CHUNK, GM_GROUPS, HEAD_DIM)
    s = jnp.einsum('gts,bnsgc->bntgc', ws, v) + bs.T[:, :, None]
    return u * s.reshape(B, L, GM_W)


def _hgrn2_chunk_scan(q, k, v, g):
    Bd, L, H, dk = q.shape
    dv = v.shape[-1]
    nC = L // HG_CHUNK

    def to_chunks(a):
        return a.reshape(Bd, nC, HG_CHUNK, H, a.shape[-1]).transpose(1, 0, 3, 2, 4)

    mask = jnp.tril(jnp.ones((HG_CHUNK, HG_CHUNK), bool))

    def step(S, inp):
        qc, kc, vc, gc = inp
        b = jnp.cumsum(gc, axis=2)
        b_last = b[:, :, -1:, :]
        o_inter = jnp.einsum('bhtd,bhde->bhte', qc * jnp.exp(b), S)
        diff = b[:, :, :, None, :] - b[:, :, None, :, :]
        decay = jnp.exp(jnp.where(mask[:, :, None], diff, -jnp.inf))
        A = jnp.einsum('bhtd,bhsd,bhtsd->bhts', qc, kc, decay)
        o = o_inter + jnp.einsum('bhts,bhse->bhte', A, vc)
        S_new = jnp.exp(b_last)[:, :, 0, :, None] * S + jnp.einsum('bhsd,bhse->bhde', kc * jnp.exp(b_last - b), vc)
        return S_new, o

    S0 = jnp.zeros((Bd, H, dk, dv), jnp.float32)
    _, o = lax.scan(step, S0, (to_chunks(q), to_chunks(k), to_chunks(v), to_chunks(g)))
    return o.transpose(1, 0, 3, 2, 4).reshape(Bd, L, H, dv)


def _hgrn2(p, lb, norm_g):
    B, L, _ = p.shape
    f32 = jnp.float32
    q, z_fwd, z_bwd, i, og = jnp.split(p, 5, axis=-1)
    lbf = lb.astype(f32)

    def gates(z):
        z = z.astype(f32)
        logf = jnp.logaddexp(jnp.log(lbf), jnp.log1p(-lbf) + jax.nn.log_sigmoid(z))
        return logf, (1.0 - lbf) * jax.nn.sigmoid(-z)

    g_f, k_f = gates(z_fwd)
    g_b, k_b = gates(z_bwd)
    qf = jax.nn.silu(q.astype(f32))
    vf = i.astype(f32)

    def rev(a):
        return jnp.flip(a, axis=1)

    def heads(a):
        return a.reshape(a.shape[0], L, HG_HEADS, HEAD_DIM)

    qq = heads(jnp.concatenate([qf, rev(qf)], 0))
    kk = heads(jnp.concatenate([k_f, rev(k_b)], 0))
    vv = heads(jnp.concatenate([vf, rev(vf)], 0))
    gg = heads(jnp.concatenate([g_f, rev(g_b)], 0))
    o = _hgrn2_chunk_scan(qq, kk, vv, gg)
    o = o[:B] + rev(o[B:])
    o = o * lax.rsqrt(jnp.mean(jnp.square(o), -1, keepdims=True) + RMS_EPS)
    o = o.reshape(B, L, HG_W) * norm_g.astype(f32) * jax.nn.silu(og.astype(f32))
    return o.astype(p.dtype)


def _mixer(x, lb, w_in, hy_conv_w, hy_conv_b, hy_pos_w1, hy_pos_b1, hy_pos_w2, hy_pos_b2,
           hy_pos_w3, hy_pos_b3, hy_sin_freq, hy_pos_wout, hy_bias, gm_ln_g, gm_ln_b, gm_ws, gm_bs,
           hg_norm_g, w_out):
    p = x @ w_in
    o1 = 3 * HY_W
    o2 = o1 + 2 * GM_W
    y_hy = _hyena(p[..., :o1], hy_conv_w, hy_conv_b, hy_pos_w1, hy_pos_b1, hy_pos_w2, hy_pos_b2,
                  hy_pos_w3, hy_pos_b3, hy_sin_freq, hy_pos_wout, hy_bias)
    y_gm = _gmlp(p[..., o1:o2], gm_ln_g, gm_ln_b, gm_ws, gm_bs)
    y_hg = _hgrn2(p[..., o2:], lb, hg_norm_g)
    return jnp.concatenate([y_hy, y_gm, y_hg], -1) @ w_out


def _swiglu(h, w1, w3, w2):
    return (jax.nn.silu(h @ w1) * (h @ w3)) @ w2


def _moe(h, wr, br, w1, w3, w2):
    N, D = h.shape
    logits = (h @ wr).astype(jnp.float32) + br.astype(jnp.float32)
    top_val, top_idx = lax.top_k(logits, TOP_K)
    gate = jax.nn.softmax(top_val, axis=-1)
    NK = N * TOP_K
    flat_e = top_idx.reshape(NK)
    flat_tok = jnp.arange(NK, dtype=jnp.int32) // TOP_K
    order = jnp.argsort(flat_e)
    se = flat_e[order]
    st = flat_tok[order]
    sg = gate.reshape(NK)[order]
    counts = jnp.bincount(flat_e, length=N_EXPERTS)
    padded = (counts + MOE_BLOCK - 1) // MOE_BLOCK * MOE_BLOCK
    start = jnp.cumsum(counts) - counts
    pstart = jnp.cumsum(padded) - padded
    dest = pstart[se] + (jnp.arange(NK, dtype=jnp.int32) - start[se])
    n_blocks = -(-NK // MOE_BLOCK) + N_EXPERTS
    cap = n_blocks * MOE_BLOCK
    slot_tok = jnp.full((cap,), N, jnp.int32).at[dest].set(st)
    blk_e = jnp.minimum(jnp.searchsorted(jnp.cumsum(padded), jnp.arange(n_blocks) * MOE_BLOCK, side='right'),
                        N_EXPERTS - 1)
    xb = jnp.concatenate([h, jnp.zeros((1, D), h.dtype)], 0)[slot_tok].reshape(n_blocks, MOE_BLOCK, D)

    def expert_block(args):
        xblk, e = args
        return _swiglu(xblk, w1[e], w3[e], w2[e])

    yb = lax.map(expert_block, (xb, blk_e)).reshape(cap, D)
    contrib = yb[dest] * sg[:, None].astype(yb.dtype)
    return jnp.zeros((N, D), h.dtype).at[st].add(contrib)


def _trunk(x, ln_in_g, ln_in_b, w_in, hy_conv_w, hy_conv_b, hy_pos_w1, hy_pos_b1, hy_pos_w2, hy_pos_b2,
           hy_pos_w3, hy_pos_b3, hy_sin_freq, hy_pos_wout, hy_bias, gm_ln_g, gm_ln_b, gm_ws, gm_bs,
           hg_lb_raw, hg_norm_g, w_out, ln1_g, ln1_b, ln2_g, ln2_b, ffn_w1, ffn_w3, ffn_w2,
           moe_router_w, moe_router_b, moe_w1, moe_w3, moe_w2):
    lb_all = jnp.cumsum(jax.nn.softmax(hg_lb_raw.astype(jnp.float32), axis=0), axis=0)
    lb_all = lb_all - lb_all[:1]
    x = _layernorm(x, ln_in_g, ln_in_b)
    B, L, D = x.shape
    for l in range(DEPTH):
        mix = _mixer(x, lb_all[l], w_in[l], hy_conv_w[l], hy_conv_b[l], hy_pos_w1[l], hy_pos_b1[l],
                     hy_pos_w2[l], hy_pos_b2[l], hy_pos_w3[l], hy_pos_b3[l], hy_sin_freq[l], hy_pos_wout[l],
                     hy_bias[l], gm_ln_g[l], gm_ln_b[l], gm_ws[l], gm_bs[l], hg_norm_g[l], w_out[l])
        x = _layernorm(ALPHA * x + mix, ln1_g[l], ln1_b[l])
        h = x.reshape(B * L, D)
        j = l // 2
        if l % 2 == 0:
            ff = _swiglu(h, ffn_w1[j], ffn_w3[j], ffn_w2[j])
        else:
            ff = _moe(h, moe_router_w[j], moe_router_b[j], moe_w1[j], moe_w3[j], moe_w2[j])
        x = _layernorm(ALPHA * x + ff.reshape(B, L, D), ln2_g[l], ln2_b[l])
    return x


def setup_inputs(seed: int = 0) -> dict:
    key = jax.random.key(seed)
    ks = iter(jax.random.split(key, 40))
    f32 = jnp.float32

    def nrm(shape, scale):
        return scale * jax.random.normal(next(ks), shape, f32)

    def gain(shape):
        return 1.0 + nrm(shape, 0.02)

    d = {}
    d['x_prompt'] = nrm((BATCH, SEQ, D_MODEL), 1.0)
    d['x_sample'] = nrm((DEC_BATCH, DEC_SEQ, D_MODEL), 1.0)
    d['ln_in_g'] = gain((D_MODEL,))
    d['ln_in_b'] = nrm((D_MODEL,), 0.02)
    d['w_in'] = nrm((DEPTH, D_MODEL, IN_W), D_MODEL ** -0.5)
    d['hy_conv_w'] = nrm((DEPTH, HY_SHORT, 3 * HY_W), HY_SHORT ** -0.5)
    d['hy_conv_b'] = nrm((DEPTH, 3 * HY_W), 0.02)
    d['hy_pos_w1'] = nrm((DEPTH, HY_POS_DIM, HY_FILT_HID), HY_POS_DIM ** -0.5)
    d['hy_pos_b1'] = nrm((DEPTH, HY_FILT_HID), 0.1)
    d['hy_pos_w2'] = nrm((DEPTH, HY_FILT_HID, HY_FILT_HID), HY_FILT_HID ** -0.5)
    d['hy_pos_b2'] = nrm((DEPTH, HY_FILT_HID), 0.1)
    d['hy_pos_w3'] = nrm((DEPTH, HY_FILT_HID, HY_FILT_HID), HY_FILT_HID ** -0.5)
    d['hy_pos_b3'] = nrm((DEPTH, HY_FILT_HID), 0.1)
    d['hy_sin_freq'] = gain((DEPTH, HY_FILT_HID))
    d['hy_pos_wout'] = nrm((DEPTH, HY_FILT_HID, HY_ORDER * 2 * HY_W), HY_FILT_HID ** -0.5)
    d['hy_bias'] = nrm((DEPTH, HY_ORDER, HY_W), 0.1)
    d['gm_ln_g'] = gain((DEPTH, GM_W))
    d['gm_ln_b'] = nrm((DEPTH, GM_W), 0.02)
    d['gm_ws'] = nrm((DEPTH, GM_GROUPS, GM_CHUNK, GM_CHUNK), GM_CHUNK ** -0.5)
    d['gm_bs'] = gain((DEPTH, GM_GROUPS, GM_CHUNK))
    d['hg_lb_raw'] = nrm((DEPTH, HG_W), 0.1)
    d['hg_norm_g'] = gain((DEPTH, HG_W))
    d['w_out'] = nrm((DEPTH, MIX_W, D_MODEL), BETA * MIX_W ** -0.5)
    d['ln1_g'] = gain((DEPTH, D_MODEL))
    d['ln1_b'] = nrm((DEPTH, D_MODEL), 0.02)
    d['ln2_g'] = gain((DEPTH, D_MODEL))
    d['ln2_b'] = nrm((DEPTH, D_MODEL), 0.02)
    d['ffn_w1'] = nrm((N_DENSE, D_MODEL, D_FF), D_MODEL ** -0.5)
    d['ffn_w3'] = nrm((N_DENSE, D_MODEL, D_FF), D_MODEL ** -0.5)
    d['ffn_w2'] = nrm((N_DENSE, D_FF, D_MODEL), BETA * D_FF ** -0.5)
    d['moe_router_w'] = nrm((N_MOE, D_MODEL, N_EXPERTS), D_MODEL ** -0.5)
    d['moe_router_b'] = nrm((N_MOE, N_EXPERTS), 0.01)
    d['moe_w1'] = nrm((N_MOE, N_EXPERTS, D_MODEL, D_FF_EXPERT), D_MODEL ** -0.5)
    d['moe_w3'] = nrm((N_MOE, N_EXPERTS, D_MODEL, D_FF_EXPERT), D_MODEL ** -0.5)
    d['moe_w2'] = nrm((N_MOE, N_EXPERTS, D_FF_EXPERT, D_MODEL), BETA * D_FF_EXPERT ** -0.5)
    return d


def reference(x_prompt, x_sample, ln_in_g, ln_in_b, w_in, hy_conv_w, hy_conv_b, hy_pos_w1, hy_pos_b1,
              hy_pos_w2, hy_pos_b2, hy_pos_w3, hy_pos_b3, hy_sin_freq, hy_pos_wout, hy_bias, gm_ln_g,
              gm_ln_b, gm_ws, gm_bs, hg_lb_raw, hg_norm_g, w_out, ln1_g, ln1_b, ln2_g, ln2_b, ffn_w1,
              ffn_w3, ffn_w2, moe_router_w, moe_router_b, moe_w1, moe_w3, moe_w2):
    weights = (ln_in_g, ln_in_b, w_in, hy_conv_w, hy_conv_b, hy_pos_w1, hy_pos_b1, hy_pos_w2, hy_pos_b2,
               hy_pos_w3, hy_pos_b3, hy_sin_freq, hy_pos_wout, hy_bias, gm_ln_g, gm_ln_b, gm_ws, gm_bs,
               hg_lb_raw, hg_norm_g, w_out, ln1_g, ln1_b, ln2_g, ln2_b, ffn_w1, ffn_w3, ffn_w2,
               moe_router_w, moe_router_b, moe_w1, moe_w3, moe_w2)
    y_prompt = _trunk(x_prompt, *weights)
    y_sample = _trunk(x_sample, *weights)
    return (y_prompt, y_sample)
```

```python
import functools
import math

import numpy as np
import jax
import jax.numpy as jnp
from jax import lax
from jax.experimental import pallas as pl
from jax.experimental.pallas import tpu as pltpu

HEAD_DIM = 128
HY_ORDER = 2
HY_POS_BANDS = 16
HY_FAST_DECAY = 0.3
HY_SLOW_DECAY = 1.5
HY_DECAY_TARGET = 1e-2
GM_CHUNK = 128
HG_CHUNK = 32
N_EXPERTS = 8
TOP_K = 2
LN_EPS = 1e-5
RMS_EPS = 1e-6

V7X_VMEM_BYTES = 64 * 1024 * 1024
VMEM_LIMIT = V7X_VMEM_BYTES - 8 * 1024 * 1024

F32 = jnp.float32
BF16 = jnp.bfloat16


def _params(*sem):
    return pltpu.CompilerParams(dimension_semantics=sem, vmem_limit_bytes=VMEM_LIMIT)


def _ln_rows(x, g, b):
    mu = jnp.mean(x, -1, keepdims=True)
    xc = x - mu
    var = jnp.mean(xc * xc, -1, keepdims=True)
    return xc * lax.rsqrt(var + LN_EPS) * g + b


def _ln_kernel(x_ref, g_ref, b_ref, of_ref, ob_ref):
    y = _ln_rows(x_ref[...], g_ref[...], b_ref[...])
    of_ref[...] = y
    ob_ref[...] = y.astype(BF16)


def _layernorm(x, g, b, bm=512):
    m, d = x.shape
    row = pl.BlockSpec((bm, d), lambda i: (i, 0))
    vec = pl.BlockSpec((1, d), lambda i: (0, 0))
    return pl.pallas_call(
        _ln_kernel, grid=(m // bm,), in_specs=[row, vec, vec], out_specs=[row, row],
        out_shape=[jax.ShapeDtypeStruct((m, d), F32), jax.ShapeDtypeStruct((m, d), BF16)],
        compiler_params=_params("parallel"), name="layernorm",
    )(x, g.reshape(1, d), b.reshape(1, d))


def _res_ln_kernel(alpha, x_ref, y_ref, g_ref, b_ref, of_ref, ob_ref):
    y = _ln_rows(alpha * x_ref[...] + y_ref[...], g_ref[...], b_ref[...])
    of_ref[...] = y
    ob_ref[...] = y.astype(BF16)


def _residual_layernorm(x, y, g, b, alpha, bm=512):
    m, d = x.shape
    row = pl.BlockSpec((bm, d), lambda i: (i, 0))
    vec = pl.BlockSpec((1, d), lambda i: (0, 0))
    return pl.pallas_call(
        functools.partial(_res_ln_kernel, alpha), grid=(m // bm,), in_specs=[row, row, vec, vec],
        out_specs=[row, row],
        out_shape=[jax.ShapeDtypeStruct((m, d), F32), jax.ShapeDtypeStruct((m, d), BF16)],
        compiler_params=_params("parallel"), name="residual_layernorm",
    )(x, y, g.reshape(1, d), b.reshape(1, d))


def _mm_kernel(a_ref, w_ref, o_ref):
    o_ref[...] = jnp.dot(a_ref[...], w_ref[...], preferred_element_type=F32).astype(o_ref.dtype)


def _matmul(a, w, out_dtype=F32, bm=1024, bn=1024):
    m, k = a.shape
    n = w.shape[1]
    bm, bn = min(bm, m), min(bn, n)
    return pl.pallas_call(
        _mm_kernel, grid=(m // bm, n // bn),
        in_specs=[pl.BlockSpec((bm, k), lambda i, j: (i, 0)), pl.BlockSpec((k, bn), lambda i, j: (0, j))],
        out_specs=pl.BlockSpec((bm, bn), lambda i, j: (i, j)),
        out_shape=jax.ShapeDtypeStruct((m, n), out_dtype),
        compiler_params=_params("parallel", "arbitrary"), name="matmul",
    )(a, w)


def _mm_res_ln_kernel(alpha, a_ref, w_ref, x_ref, g_ref, b_ref, of_ref, ob_ref):
    mix = jnp.dot(a_ref[...], w_ref[...], preferred_element_type=F32)
    y = _ln_rows(alpha * x_ref[...] + mix, g_ref[...], b_ref[...])
    of_ref[...] = y
    ob_ref[...] = y.astype(BF16)


def _matmul_residual_ln(a, w, x, g, b, alpha, bm=512):
    m, k = a.shape
    d = w.shape[1]
    row = pl.BlockSpec((bm, d), lambda i: (i, 0))
    vec = pl.BlockSpec((1, d), lambda i: (0, 0))
    return pl.pallas_call(
        functools.partial(_mm_res_ln_kernel, alpha), grid=(m // bm,),
        in_specs=[pl.BlockSpec((bm, k), lambda i: (i, 0)), pl.BlockSpec((k, d), lambda i: (0, 0)), row, vec, vec],
        out_specs=[row, row],
        out_shape=[jax.ShapeDtypeStruct((m, d), F32), jax.ShapeDtypeStruct((m, d), BF16)],
        compiler_params=_params("parallel"), name="out_proj_residual_ln",
    )(a, w, x, g.reshape(1, d), b.reshape(1, d))


def _swiglu_acc(a_ref, w1_ref, w3_ref, w2_ref, acc_ref):
    f = pl.program_id(1)

    @pl.when(f == 0)
    def _():
        acc_ref[...] = jnp.zeros_like(acc_ref)

    a = a_ref[...]
    h1 = jnp.dot(a, w1_ref[...], preferred_element_type=F32)
    h3 = jnp.dot(a, w3_ref[...], preferred_element_type=F32)
    gated = (h1 * jax.nn.sigmoid(h1) * h3).astype(BF16)
    acc_ref[...] += jnp.dot(gated, w2_ref[...], preferred_element_type=F32)


def _ffn_kernel(alpha, a_ref, w1_ref, w3_ref, w2_ref, x_ref, g_ref, b_ref, of_ref, ob_ref, acc_ref):
    _swiglu_acc(a_ref, w1_ref, w3_ref, w2_ref, acc_ref)

    @pl.when(pl.program_id(1) == pl.num_programs(1) - 1)
    def _():
        y = _ln_rows(alpha * x_ref[...] + acc_ref[...], g_ref[...], b_ref[...])
        of_ref[...] = y
        ob_ref[...] = y.astype(BF16)


def _ffn_residual_ln(a, w1, w3, w2, x, g, b, alpha, bm=512, bf=512):
    m, d = a.shape
    dff = w1.shape[1]
    row = pl.BlockSpec((bm, d), lambda i, f: (i, 0))
    vec = pl.BlockSpec((1, d), lambda i, f: (0, 0))
    up = pl.BlockSpec((d, bf), lambda i, f: (0, f))
    return pl.pallas_call(
        functools.partial(_ffn_kernel, alpha), grid=(m // bm, dff // bf),
        in_specs=[row, up, up, pl.BlockSpec((bf, d), lambda i, f: (f, 0)), row, vec, vec],
        out_specs=[row, row],
        out_shape=[jax.ShapeDtypeStruct((m, d), F32), jax.ShapeDtypeStruct((m, d), BF16)],
        scratch_shapes=[pltpu.VMEM((bm, d), F32)],
        compiler_params=_params("parallel", "arbitrary"), name="ffn_residual_ln",
    )(a, w1, w3, w2, x, g.reshape(1, d), b.reshape(1, d))


def _moe_ffn_kernel(blk_e_ref, a_ref, w1_ref, w3_ref, w2_ref, o_ref, acc_ref):
    del blk_e_ref
    _swiglu_acc(a_ref, w1_ref, w3_ref, w2_ref, acc_ref)

    @pl.when(pl.program_id(1) == pl.num_programs(1) - 1)
    def _():
        o_ref[...] = acc_ref[...]


def _moe_grouped_ffn(blk_e, xb, w1, w3, w2, bm, bf=512):
    cap, d = xb.shape
    dff = w1.shape[2]
    row = pl.BlockSpec((bm, d), lambda i, f, e: (i, 0))
    up = pl.BlockSpec((None, d, bf), lambda i, f, e: (e[i], 0, f))
    grid_spec = pltpu.PrefetchScalarGridSpec(
        num_scalar_prefetch=1, grid=(cap // bm, dff // bf),
        in_specs=[row, up, up, pl.BlockSpec((None, bf, d), lambda i, f, e: (e[i], f, 0))],
        out_specs=row, scratch_shapes=[pltpu.VMEM((bm, d), F32)])
    return pl.pallas_call(
        _moe_ffn_kernel, grid_spec=grid_spec, out_shape=jax.ShapeDtypeStruct((cap, d), F32),
        compiler_params=_params("parallel", "arbitrary"), name="moe_grouped_ffn",
    )(blk_e, xb, w1, w3, w2)


def _moe(h_f32, h_bf16, wr, br, w1, w3, w2, bm=512):
    n, d = h_f32.shape
    logits = jnp.dot(h_f32, wr.astype(F32), precision=lax.Precision.HIGHEST) + br.astype(F32)
    top_val, top_idx = lax.top_k(logits, TOP_K)
    gate = jax.nn.softmax(top_val, axis=-1)
    nk = n * TOP_K
    flat_e = top_idx.reshape(nk)
    order = jnp.argsort(flat_e)
    se = flat_e[order]
    st = (order // TOP_K).astype(jnp.int32)
    counts = jnp.bincount(flat_e, length=N_EXPERTS)
    padded = (counts + bm - 1) // bm * bm
    start = jnp.cumsum(counts) - counts
    pstart = jnp.cumsum(padded) - padded
    dest_sorted = pstart[se] + (jnp.arange(nk, dtype=jnp.int32) - start[se])
    n_blocks = nk // bm + N_EXPERTS
    cap = n_blocks * bm
    slot_tok = jnp.full((cap,), n, jnp.int32).at[dest_sorted].set(st)
    blk_e = jnp.minimum(jnp.searchsorted(jnp.cumsum(padded), jnp.arange(n_blocks) * bm, side='right'),
                        N_EXPERTS - 1).astype(jnp.int32)
    xb = jnp.concatenate([h_bf16, jnp.zeros((1, d), BF16)], 0)[slot_tok]
    yb = _moe_grouped_ffn(blk_e, xb, w1, w3, w2, bm)
    dest = jnp.zeros((nk,), jnp.int32).at[order].set(dest_sorted.astype(jnp.int32)).reshape(n, TOP_K)
    return yb[dest[:, 0]] * gate[:, 0:1] + yb[dest[:, 1]] * gate[:, 1:2]


def _short_conv3(u, w, b):
    up = jnp.pad(u, ((0, 0), (1, 1), (0, 0)))
    return up[:, :-2] * w[0] + up[:, 1:-1] * w[1] + up[:, 2:] * w[2] + b


def _hyena_filter_spectrum(L, w1, b1, w2, b2, w3, b3, freq, w_out, hy_w):
    hp = lax.Precision.HIGHEST
    pos = jnp.arange(L, dtype=F32)
    t = pos / max(L - 1, 1)
    bands = jnp.linspace(1e-4, HY_POS_BANDS - 1, HY_POS_BANDS, dtype=F32)
    ang = (2.0 * math.pi / L) * pos[:, None] * bands[None, :]
    feats = jnp.concatenate([t[:, None], jnp.cos(ang), -jnp.sin(ang)], -1)
    h = jnp.sin(freq * (jnp.dot(feats, w1, precision=hp) + b1))
    h = jnp.sin(freq * (jnp.dot(h, w2, precision=hp) + b2))
    h = jnp.sin(freq * (jnp.dot(h, w3, precision=hp) + b3))
    h = jnp.dot(h, w_out, precision=hp)
    deltas = jnp.abs(jnp.linspace(math.log(HY_DECAY_TARGET) / HY_SLOW_DECAY,
                                  math.log(HY_DECAY_TARGET) / HY_FAST_DECAY, hy_w, dtype=F32))
    window = jnp.exp(-t[:, None] * deltas[None, :])
    h = h.reshape(L, HY_ORDER, 2, hy_w) * window[:, None, None, :]
    h_fwd, h_bwd = h[:, :, 0], h[:, :, 1]
    k = jnp.concatenate([h_fwd, jnp.zeros((1, HY_ORDER, hy_w), F32), h_bwd[:0:-1]], 0)
    k = k / jnp.sum(jnp.abs(k), axis=0, keepdims=True)
    return jnp.fft.rfft(k, axis=0)


def _hyena(p, conv_w, conv_b, w1, b1, w2, b2, w3, b3, freq, w_out, bias):
    B, L, c3 = p.shape
    u = _short_conv3(p, conv_w, conv_b)
    x1, x2, v = jnp.split(u, 3, axis=-1)
    spec = _hyena_filter_spectrum(L, w1, b1, w2, b2, w3, b3, freq, w_out, c3 // 3)
    z = v
    for o, gate in enumerate((x1, x2)):
        zf = jnp.fft.rfft(z, n=2 * L, axis=1)
        z = jnp.fft.irfft(zf * spec[:, o], n=2 * L, axis=1)[:, :L] + z * bias[o]
        z = gate * z
    return z


def _gmlp(p, ln_g, ln_b, ws, bs):
    B, L, c2 = p.shape
    groups = ws.shape[0]
    u, v = jnp.split(jax.nn.gelu(p), 2, axis=-1)
    v = _ln_rows(v, ln_g, ln_b)
    v = v.reshape(B, L // GM_CHUNK, GM_CHUNK, groups, HEAD_DIM)
    s = jnp.einsum('gts,bnsgc->bntgc', ws, v) + bs.T[:, :, None]
    return u * s.reshape(B, L, c2 // 2)


def _hgrn2_chunk_scan(q, k, v, g):
    Bd, L, H, dk = q.shape
    dv = v.shape[-1]
    nC = L // HG_CHUNK

    def to_chunks(a):
        return a.reshape(Bd, nC, HG_CHUNK, H, a.shape[-1]).transpose(1, 0, 3, 2, 4)

    mask = jnp.tril(jnp.ones((HG_CHUNK, HG_CHUNK), bool))

    def step(S, inp):
        qc, kc, vc, gc = inp
        b = jnp.cumsum(gc, axis=2)
        b_last = b[:, :, -1:, :]
        o_inter = jnp.einsum('bhtd,bhde->bhte', qc * jnp.exp(b), S)
        diff = b[:, :, :, None, :] - b[:, :, None, :, :]
        decay = jnp.exp(jnp.where(mask[:, :, None], diff, -jnp.inf))
        A = jnp.einsum('bhtd,bhsd,bhtsd->bhts', qc, kc, decay)
        o = o_inter + jnp.einsum('bhts,bhse->bhte', A, vc)
        S_new = jnp.exp(b_last)[:, :, 0, :, None] * S + jnp.einsum('bhsd,bhse->bhde', kc * jnp.exp(b_last - b), vc)
        return S_new, o

    S0 = jnp.zeros((Bd, H, dk, dv), F32)
    _, o = lax.scan(step, S0, (to_chunks(q), to_chunks(k), to_chunks(v), to_chunks(g)))
    return o.transpose(1, 0, 3, 2, 4).reshape(Bd, L, H, dv)


def _hgrn2(p, lb, norm_g):
    B, L, c5 = p.shape
    hg_w = c5 // 5
    heads = hg_w // HEAD_DIM
    q, z_fwd, z_bwd, i, og = jnp.split(p, 5, axis=-1)

    def gates(z):
        logf = jnp.logaddexp(jnp.log(lb), jnp.log1p(-lb) + jax.nn.log_sigmoid(z))
        return logf, (1.0 - lb) * jax.nn.sigmoid(-z)

    g_f, k_f = gates(z_fwd)
    g_b, k_b = gates(z_bwd)
    qf = jax.nn.silu(q)

    def rev(a):
        return jnp.flip(a, axis=1)

    def hd(a):
        return a.reshape(a.shape[0], L, heads, HEAD_DIM)

    qq = hd(jnp.concatenate([qf, rev(qf)], 0))
    kk = hd(jnp.concatenate([k_f, rev(k_b)], 0))
    vv = hd(jnp.concatenate([i, rev(i)], 0))
    gg = hd(jnp.concatenate([g_f, rev(g_b)], 0))
    o = _hgrn2_chunk_scan(qq, kk, vv, gg)
    o = o[:B] + rev(o[B:])
    o = o * lax.rsqrt(jnp.mean(jnp.square(o), -1, keepdims=True) + RMS_EPS)
    return o.reshape(B, L, hg_w) * norm_g * jax.nn.silu(og)


def kernel(x_prompt, x_sample, ln_in_g, ln_in_b, w_in, hy_conv_w, hy_conv_b, hy_pos_w1, hy_pos_b1, hy_pos_w2, hy_pos_b2, hy_pos_w3, hy_pos_b3, hy_sin_freq, hy_pos_wout, hy_bias, gm_ln_g, gm_ln_b, gm_ws, gm_bs, hg_lb_raw, hg_norm_g, w_out, ln1_g, ln1_b, ln2_g, ln2_b, ffn_w1, ffn_w3, ffn_w2, moe_router_w, moe_router_b, moe_w1, moe_w3, moe_w2):
    depth, d_model, in_w = w_in.shape
    hy_w = hy_bias.shape[-1]
    gm_w = gm_ln_g.shape[-1]
    hg_w = hg_norm_g.shape[-1]
    o1 = 3 * hy_w
    o2 = o1 + 2 * gm_w
    alpha = (2 * depth) ** 0.25
    seqs = (x_prompt.shape[:2], x_sample.shape[:2])
    n_tok = [b * l for b, l in seqs]

    lb_all = jnp.cumsum(jax.nn.softmax(hg_lb_raw.astype(F32), axis=0), axis=0)
    lb_all = lb_all - lb_all[:1]

    x = jnp.concatenate([x_prompt.reshape(-1, d_model), x_sample.reshape(-1, d_model)], 0)
    x, xb = _layernorm(x, ln_in_g, ln_in_b)
    for l in range(depth):
        p = _matmul(xb, w_in[l].astype(BF16))
        ys = []
        off = 0
        for (b, sl), n in zip(seqs, n_tok):
            ps = p[off:off + n].reshape(b, sl, in_w)
            off += n
            y_hy = _hyena(ps[..., :o1], hy_conv_w[l], hy_conv_b[l], hy_pos_w1[l], hy_pos_b1[l], hy_pos_w2[l],
                          hy_pos_b2[l], hy_pos_w3[l], hy_pos_b3[l], hy_sin_freq[l], hy_pos_wout[l], hy_bias[l])
            y_gm = _gmlp(ps[..., o1:o2], gm_ln_g[l], gm_ln_b[l], gm_ws[l], gm_bs[l])
            y_hg = _hgrn2(ps[..., o2:], lb_all[l], hg_norm_g[l])
            ys.append(jnp.concatenate([y_hy, y_gm, y_hg], -1).reshape(n, -1))
        y = jnp.concatenate(ys, 0).astype(BF16)
        x, xb = _matmul_residual_ln(y, w_out[l].astype(BF16), x, ln1_g[l], ln1_b[l], alpha)
        j = l // 2
        if l % 2 == 0:
            x, xb = _ffn_residual_ln(xb, ffn_w1[j].astype(BF16), ffn_w3[j].astype(BF16), ffn_w2[j].astype(BF16),
                                     x, ln2_g[l], ln2_b[l], alpha)
        else:
            ff = _moe(x, xb, moe_router_w[j], moe_router_b[j], moe_w1[j].astype(BF16), moe_w3[j].astype(BF16),
                      moe_w2[j].astype(BF16))
            x, xb = _residual_layernorm(x, ff, ln2_g[l], ln2_b[l], alpha)
    y_prompt = x[:n_tok[0]].reshape(x_prompt.shape)
    y_sample = x[n_tok[0]:].reshape(x_sample.shape)
    return (y_prompt, y_sample)
```

```python
import functools
import math

import numpy as np
import jax
import jax.numpy as jnp
from jax import lax
from jax.experimental import pallas as pl
from jax.experimental.pallas import tpu as pltpu

HEAD_DIM = 128
HY_ORDER = 2
HY_POS_BANDS = 16
HY_FAST_DECAY = 0.3
HY_SLOW_DECAY = 1.5
HY_DECAY_TARGET = 1e-2
GM_CHUNK = 128
N_EXPERTS = 8
TOP_K = 2
LN_EPS = 1e-5
RMS_EPS = 1e-6

HY_BLOCK = 2048
HG_CHUNK = 128
LANES = 128

V7X_VMEM_BYTES = 64 * 1024 * 1024
VMEM_LIMIT = V7X_VMEM_BYTES - 8 * 1024 * 1024

F32 = jnp.float32
BF16 = jnp.bfloat16
NT_DIMS = (((1,), (1,)), ((), ()))
TN_DIMS = (((0,), (0,)), ((), ()))


def _params(*sem):
    return pltpu.CompilerParams(dimension_semantics=sem, vmem_limit_bytes=VMEM_LIMIT)


def _ln_rows(x, g, b):
    mu = jnp.mean(x, -1, keepdims=True)
    xc = x - mu
    var = jnp.mean(xc * xc, -1, keepdims=True)
    return xc * lax.rsqrt(var + LN_EPS) * g + b


def _split_bf16(x):
    hi = x.astype(BF16)
    lo = (x - hi.astype(F32)).astype(BF16)
    return hi, lo


def _dot3(a, b):
    ah, al = _split_bf16(a)
    bh, bl = _split_bf16(b)
    return (jnp.dot(ah, bh, preferred_element_type=F32) + jnp.dot(ah, bl, preferred_element_type=F32)
            + jnp.dot(al, bh, preferred_element_type=F32))


def _ln_kernel(x_ref, g_ref, b_ref, of_ref, ob_ref):
    y = _ln_rows(x_ref[...], g_ref[...], b_ref[...])
    of_ref[...] = y
    ob_ref[...] = y.astype(BF16)


def _layernorm(x, g, b, bm=512):
    m, d = x.shape
    row = pl.BlockSpec((bm, d), lambda i: (i, 0))
    vec = pl.BlockSpec((1, d), lambda i: (0, 0))
    return pl.pallas_call(
        _ln_kernel, grid=(m // bm,), in_specs=[row, vec, vec], out_specs=[row, row],
        out_shape=[jax.ShapeDtypeStruct((m, d), F32), jax.ShapeDtypeStruct((m, d), BF16)],
        compiler_params=_params("parallel"), name="layernorm",
    )(x, g.reshape(1, d), b.reshape(1, d))


def _res_ln_kernel(alpha, x_ref, y_ref, g_ref, b_ref, of_ref, ob_ref):
    y = _ln_rows(alpha * x_ref[...] + y_ref[...], g_ref[...], b_ref[...])
    of_ref[...] = y
    ob_ref[...] = y.astype(BF16)


def _residual_layernorm(x, y, g, b, alpha, bm=512):
    m, d = x.shape
    row = pl.BlockSpec((bm, d), lambda i: (i, 0))
    vec = pl.BlockSpec((1, d), lambda i: (0, 0))
    return pl.pallas_call(
        functools.partial(_res_ln_kernel, alpha), grid=(m // bm,), in_specs=[row, row, vec, vec],
        out_specs=[row, row],
        out_shape=[jax.ShapeDtypeStruct((m, d), F32), jax.ShapeDtypeStruct((m, d), BF16)],
        compiler_params=_params("parallel"), name="residual_layernorm",
    )(x, y, g.reshape(1, d), b.reshape(1, d))


def _mm_kernel(a_ref, w_ref, o_ref):
    o_ref[...] = jnp.dot(a_ref[...], w_ref[...], preferred_element_type=F32).astype(o_ref.dtype)


def _matmul(a, w, out_dtype=F32, bm=1024, bn=1024):
    m, k = a.shape
    n = w.shape[1]
    bm, bn = min(bm, m), min(bn, n)
    return pl.pallas_call(
        _mm_kernel, grid=(m // bm, n // bn),
        in_specs=[pl.BlockSpec((bm, k), lambda i, j: (i, 0)), pl.BlockSpec((k, bn), lambda i, j: (0, j))],
        out_specs=pl.BlockSpec((bm, bn), lambda i, j: (i, j)),
        out_shape=jax.ShapeDtypeStruct((m, n), out_dtype),
        compiler_params=_params("parallel", "arbitrary"), name="matmul",
    )(a, w)


def _mix_out_kernel(alpha, a0_ref, a1_ref, a2_ref, w0_ref, w1_ref, w2_ref, x_ref, g_ref, b_ref, of_ref, ob_ref):
    mix = (jnp.dot(a0_ref[...], w0_ref[...], preferred_element_type=F32)
           + jnp.dot(a1_ref[...], w1_ref[...], preferred_element_type=F32)
           + jnp.dot(a2_ref[...], w2_ref[...], preferred_element_type=F32))
    y = _ln_rows(alpha * x_ref[...] + mix, g_ref[...], b_ref[...])
    of_ref[...] = y
    ob_ref[...] = y.astype(BF16)


def _mix_out_residual_ln(parts, weights, x, g, b, alpha, bm=512):
    m, d = x.shape
    row = pl.BlockSpec((bm, d), lambda i: (i, 0))
    vec = pl.BlockSpec((1, d), lambda i: (0, 0))
    a_specs = [pl.BlockSpec((bm, a.shape[1]), lambda i: (i, 0)) for a in parts]
    w_specs = [pl.BlockSpec(w.shape, lambda i: (0, 0)) for w in weights]
    return pl.pallas_call(
        functools.partial(_mix_out_kernel, alpha), grid=(m // bm,),
        in_specs=a_specs + w_specs + [row, vec, vec], out_specs=[row, row],
        out_shape=[jax.ShapeDtypeStruct((m, d), F32), jax.ShapeDtypeStruct((m, d), BF16)],
        compiler_params=_params("parallel"), name="out_proj_residual_ln",
    )(*parts, *weights, x, g.reshape(1, d), b.reshape(1, d))


def _swiglu_acc(a_ref, w1_ref, w3_ref, w2_ref, acc_ref):
    f = pl.program_id(1)

    @pl.when(f == 0)
    def _():
        acc_ref[...] = jnp.zeros_like(acc_ref)

    a = a_ref[...]
    h1 = jnp.dot(a, w1_ref[...], preferred_element_type=F32)
    h3 = jnp.dot(a, w3_ref[...], preferred_element_type=F32)
    gated = (h1 * jax.nn.sigmoid(h1) * h3).astype(BF16)
    acc_ref[...] += jnp.dot(gated, w2_ref[...], preferred_element_type=F32)


def _ffn_kernel(alpha, a_ref, w1_ref, w3_ref, w2_ref, x_ref, g_ref, b_ref, of_ref, ob_ref, acc_ref):
    _swiglu_acc(a_ref, w1_ref, w3_ref, w2_ref, acc_ref)

    @pl.when(pl.program_id(1) == pl.num_programs(1) - 1)
    def _():
        y = _ln_rows(alpha * x_ref[...] + acc_ref[...], g_ref[...], b_ref[...])
        of_ref[...] = y
        ob_ref[...] = y.astype(BF16)


def _ffn_residual_ln(a, w1, w3, w2, x, g, b, alpha, bm=512, bf=512):
    m, d = a.shape
    dff = w1.shape[1]
    row = pl.BlockSpec((bm, d), lambda i, f: (i, 0))
    vec = pl.BlockSpec((1, d), lambda i, f: (0, 0))
    up = pl.BlockSpec((d, bf), lambda i, f: (0, f))
    return pl.pallas_call(
        functools.partial(_ffn_kernel, alpha), grid=(m // bm, dff // bf),
        in_specs=[row, up, up, pl.BlockSpec((bf, d), lambda i, f: (f, 0)), row, vec, vec],
        out_specs=[row, row],
        out_shape=[jax.ShapeDtypeStruct((m, d), F32), jax.ShapeDtypeStruct((m, d), BF16)],
        scratch_shapes=[pltpu.VMEM((bm, d), F32)],
        compiler_params=_params("parallel", "arbitrary"), name="ffn_residual_ln",
    )(a, w1, w3, w2, x, g.reshape(1, d), b.reshape(1, d))


def _moe_ffn_kernel(blk_e_ref, a_ref, w1_ref, w3_ref, w2_ref, o_ref, acc_ref):
    del blk_e_ref
    _swiglu_acc(a_ref, w1_ref, w3_ref, w2_ref, acc_ref)

    @pl.when(pl.program_id(1) == pl.num_programs(1) - 1)
    def _():
        o_ref[...] = acc_ref[...]


def _moe_grouped_ffn(blk_e, xb, w1, w3, w2, bm, bf=512):
    cap, d = xb.shape
    dff = w1.shape[2]
    row = pl.BlockSpec((bm, d), lambda i, f, e: (i, 0))
    up = pl.BlockSpec((None, d, bf), lambda i, f, e: (e[i], 0, f))
    grid_spec = pltpu.PrefetchScalarGridSpec(
        num_scalar_prefetch=1, grid=(cap // bm, dff // bf),
        in_specs=[row, up, up, pl.BlockSpec((None, bf, d), lambda i, f, e: (e[i], f, 0))],
        out_specs=row, scratch_shapes=[pltpu.VMEM((bm, d), F32)])
    return pl.pallas_call(
        _moe_ffn_kernel, grid_spec=grid_spec, out_shape=jax.ShapeDtypeStruct((cap, d), F32),
        compiler_params=_params("parallel", "arbitrary"), name="moe_grouped_ffn",
    )(blk_e, xb, w1, w3, w2)


def _moe(h_f32, h_bf16, wr, br, w1, w3, w2, bm=512):
    n, d = h_f32.shape
    logits = jnp.dot(h_f32, wr.astype(F32), precision=lax.Precision.HIGHEST) + br.astype(F32)
    top_val, top_idx = lax.top_k(logits, TOP_K)
    gate = jax.nn.softmax(top_val, axis=-1)
    nk = n * TOP_K
    flat_e = top_idx.reshape(nk)
    order = jnp.argsort(flat_e)
    se = flat_e[order]
    st = (order // TOP_K).astype(jnp.int32)
    counts = jnp.bincount(flat_e, length=N_EXPERTS)
    padded = (counts + bm - 1) // bm * bm
    start = jnp.cumsum(counts) - counts
    pstart = jnp.cumsum(padded) - padded
    dest_sorted = pstart[se] + (jnp.arange(nk, dtype=jnp.int32) - start[se])
    n_blocks = nk // bm + N_EXPERTS
    cap = n_blocks * bm
    slot_tok = jnp.full((cap,), n, jnp.int32).at[dest_sorted].set(st)
    blk_e = jnp.minimum(jnp.searchsorted(jnp.cumsum(padded), jnp.arange(n_blocks) * bm, side='right'),
                        N_EXPERTS - 1).astype(jnp.int32)
    xb = jnp.concatenate([h_bf16, jnp.zeros((1, d), BF16)], 0)[slot_tok]
    yb = _moe_grouped_ffn(blk_e, xb, w1, w3, w2, bm)
    dest = jnp.zeros((nk,), jnp.int32).at[order].set(dest_sorted.astype(jnp.int32)).reshape(n, TOP_K)
    return yb[dest[:, 0]] * gate[:, 0:1] + yb[dest[:, 1]] * gate[:, 1:2]


def _gmlp_kernel(groups, pu_ref, pv_ref, g_ref, b_ref, ws_ref, bs_ref, o_ref):
    u = jax.nn.gelu(pu_ref[...])
    v = _ln_rows(jax.nn.gelu(pv_ref[...]), g_ref[...], b_ref[...])
    for n in range(u.shape[0] // GM_CHUNK):
        rows = slice(n * GM_CHUNK, (n + 1) * GM_CHUNK)
        for grp in range(groups):
            cols = slice(grp * HEAD_DIM, (grp + 1) * HEAD_DIM)
            s = _dot3(ws_ref[grp], v[rows, cols]) + bs_ref[grp]
            o_ref[rows, cols] = (u[rows, cols] * s).astype(o_ref.dtype)


def _gmlp(p, col_u, col_v, gm_w, ln_g, ln_b, ws, bs, bt=512):
    n = p.shape[0]
    groups = ws.shape[0]
    bsb = jnp.broadcast_to(bs[:, :, None], (groups, GM_CHUNK, HEAD_DIM)).astype(F32)
    vec = pl.BlockSpec((1, gm_w), lambda i: (0, 0))
    full3 = pl.BlockSpec((groups, GM_CHUNK, HEAD_DIM), lambda i: (0, 0, 0))
    return pl.pallas_call(
        functools.partial(_gmlp_kernel, groups), grid=(n // bt,),
        in_specs=[pl.BlockSpec((bt, gm_w), lambda i: (i, col_u)), pl.BlockSpec((bt, gm_w), lambda i: (i, col_v)),
                  vec, vec, pl.BlockSpec((groups, GM_CHUNK, GM_CHUNK), lambda i: (0, 0, 0)), full3],
        out_specs=pl.BlockSpec((bt, gm_w), lambda i: (i, 0)),
        out_shape=jax.ShapeDtypeStruct((n, gm_w), BF16),
        compiler_params=_params("parallel"), name="gmlp",
    )(p, p, ln_g.reshape(1, gm_w), ln_b.reshape(1, gm_w), ws.astype(F32), bsb)


def _hgrn2_constants(c, reverse):
    n_lvl = int(math.log2(c))
    t = np.arange(c)
    ms, ws = [], []
    for lvl in range(n_lvl):
        m = c >> (lvl + 1)
        mid = (t // (2 * m)) * (2 * m) + m
        upper = t >= mid
        mat = np.zeros((c, c), np.float32)
        for r in range(c):
            if upper[r]:
                mat[r, mid[r]:r + 1] = 1.0
            else:
                mat[r, r + 1:mid[r]] = 1.0
        same = (t[:, None] // (2 * m)) == (t[None, :] // (2 * m))
        ws.append((same & upper[:, None] & ~upper[None, :]).astype(np.float32))
        ms.append(mat)
    ws.append(np.eye(c, dtype=np.float32))
    ms.append(np.tril(np.ones((c, c), np.float32)))
    ms.append(np.triu(np.ones((c, c), np.float32), 1))
    if reverse:
        ms = [a[::-1, ::-1] for a in ms]
        ws = [a[::-1, ::-1] for a in ws]
    return np.concatenate(ms, 0), np.stack(ws, 0)


def _hgrn2_kernel(heads, n_lvl, carry_row, final, reset_ref, *refs):
    if final:
        (q_ref, z_ref, i_ref, og_ref, oo_ref, loglb_ref, log1mlb_ref, onemlb_ref, ng_ref, m_ref, w_ref,
         out_ref, st_ref, e_ref) = refs
    else:
        (q_ref, z_ref, i_ref, loglb_ref, log1mlb_ref, onemlb_ref, m_ref, w_ref, out_ref, st_ref, e_ref) = refs
    c = q_ref.shape[0]

    @pl.when(reset_ref[pl.program_id(0)] == 1)
    def _():
        st_ref[...] = jnp.zeros_like(st_ref)

    z = z_ref[...]
    e = jnp.exp(-jnp.abs(z))
    r = 1.0 / (1.0 + e)
    log_sig = jnp.minimum(z, 0.0) - jnp.log(1.0 + e)
    sig_neg = jnp.where(z >= 0.0, e * r, r)
    a = loglb_ref[...]
    b = log1mlb_ref[...] + log_sig
    g = jnp.maximum(a, b) + jnp.log(1.0 + jnp.exp(-jnp.abs(a - b)))
    k = onemlb_ref[...] * sig_neg
    q = q_ref[...]
    qs = q * jax.nn.sigmoid(q)
    v = i_ref[...].astype(BF16)

    g_hi, g_lo = _split_bf16(g)
    m_all = m_ref[...]
    e_ref[...] = (jnp.dot(m_all, g_hi, preferred_element_type=F32)
                  + jnp.dot(m_all, g_lo, preferred_element_type=F32))

    for h in range(heads):
        cols = slice(h * HEAD_DIM, (h + 1) * HEAD_DIM)
        qh, kh, vh = qs[:, cols], k[:, cols], v[:, cols]
        amat = w_ref[n_lvl] * lax.dot_general(qh.astype(BF16), kh.astype(BF16), NT_DIMS,
                                              preferred_element_type=F32)
        for lvl in range(n_lvl):
            ex = jnp.exp(e_ref[lvl * c:(lvl + 1) * c, cols])
            amat = amat + w_ref[lvl] * lax.dot_general((qh * ex).astype(BF16), (kh * ex).astype(BF16), NT_DIMS,
                                                       preferred_element_type=F32)
        e_in = e_ref[n_lvl * c:(n_lvl + 1) * c, cols]
        q_in = (qh * jnp.exp(e_in)).astype(BF16)
        k_st = (kh * jnp.exp(e_ref[(n_lvl + 1) * c:(n_lvl + 2) * c, cols])).astype(BF16)
        dec = jnp.exp(e_in[carry_row:carry_row + 1, :])
        st = st_ref[h]
        o = (lax.dot_general(q_in, st.astype(BF16), NT_DIMS, preferred_element_type=F32)
             + jnp.dot(amat.astype(BF16), vh, preferred_element_type=F32))
        st_ref[h] = st * dec + lax.dot_general(vh, k_st, TN_DIMS, preferred_element_type=F32)
        if final:
            o = o + oo_ref[:, cols]
            o = o * lax.rsqrt(jnp.mean(o * o, -1, keepdims=True) + RMS_EPS)
            og = og_ref[:, cols]
            out_ref[:, cols] = (o * ng_ref[:, cols] * (og * jax.nn.sigmoid(og))).astype(out_ref.dtype)
        else:
            out_ref[:, cols] = o


def _hgrn2_pass(p, cols, hg_w, z_col, reverse, reset, lbs, extra, c=HG_CHUNK):
    n = p.shape[0]
    nc = n // c
    heads = hg_w // HEAD_DIM
    n_lvl = int(math.log2(c))
    m_np, w_np = _hgrn2_constants(c, reverse)
    final = extra is not None

    def cmap(i, r):
        return (nc - 1 - i) if reverse else i

    def tok(col):
        return pl.BlockSpec((c, hg_w), lambda i, r: (cmap(i, r), col))

    vec = pl.BlockSpec((1, hg_w), lambda i, r: (0, 0))
    in_specs = [tok(cols["q"]), tok(z_col), tok(cols["i"])]
    args = [p, p, p]
    if final:
        o_other, norm_g = extra
        in_specs += [tok(cols["og"]), pl.BlockSpec((c, hg_w), lambda i, r: (cmap(i, r), 0))]
        args += [p, o_other]
    in_specs += [vec, vec, vec]
    args += list(lbs)
    if final:
        in_specs += [vec]
        args += [norm_g.reshape(1, hg_w).astype(F32)]
    in_specs += [pl.BlockSpec(m_np.shape, lambda i, r: (0, 0)), pl.BlockSpec(w_np.shape, lambda i, r: (0, 0, 0))]
    args += [jnp.asarray(m_np, BF16), jnp.asarray(w_np, F32)]
    grid_spec = pltpu.PrefetchScalarGridSpec(
        num_scalar_prefetch=1, grid=(nc,), in_specs=in_specs,
        out_specs=pl.BlockSpec((c, hg_w), lambda i, r: (cmap(i, r), 0)),
        scratch_shapes=[pltpu.VMEM((heads, HEAD_DIM, HEAD_DIM), F32), pltpu.VMEM((m_np.shape[0], hg_w), F32)])
    order = np.arange(nc)[::-1] if reverse else np.arange(nc)
    reset_steps = jnp.asarray(np.asarray(reset, np.int32)[order])
    return pl.pallas_call(
        functools.partial(_hgrn2_kernel, heads, n_lvl, 0 if reverse else c - 1, final), grid_spec=grid_spec,
        out_shape=jax.ShapeDtypeStruct((n, hg_w), BF16 if final else F32),
        compiler_params=_params("arbitrary"), name="hgrn2_fwd" if final else "hgrn2_bwd",
    )(reset_steps, *args)


def _hgrn2(p, cols, hg_w, lb, norm_g, seq_lens, c=HG_CHUNK):
    lb = lb.astype(F32).reshape(1, hg_w)
    lbs = (jnp.log(lb), jnp.log1p(-lb), 1.0 - lb)
    starts = np.cumsum([0] + list(seq_lens))
    nc = starts[-1] // c
    first = np.zeros(nc, np.int32)
    last = np.zeros(nc, np.int32)
    first[starts[:-1] // c] = 1
    last[starts[1:] // c - 1] = 1
    o_b = _hgrn2_pass(p, cols, hg_w, cols["zb"], True, last, lbs, None, c)
    return _hgrn2_pass(p, cols, hg_w, cols["zf"], False, first, lbs, (o_b, norm_g), c)


def _dft_matrices(t):
    n = 2 * t
    f = jnp.arange(t, dtype=jnp.int32)
    ang = (2.0 * math.pi / n) * ((f[:, None] * f[None, :]) % n).astype(F32)
    cos, sin = jnp.cos(ang), jnp.sin(ang)
    alt = (1 - 2 * (f % 2)).astype(F32)
    row0 = (f == 0)[:, None]
    top = cos
    bot = jnp.where(row0, alt[None, :], -sin)
    fwd = jnp.concatenate([top, bot], 0)
    fwd = jnp.stack([fwd, fwd[:, ::-1]], 0).astype(BF16)
    wgt = jnp.where(f == 0, 1.0, 2.0)[None, :] / n
    inv_top = cos * wgt
    inv_bot = jnp.where((f == 0)[None, :], alt[:, None] / n, -sin * wgt)
    inv = jnp.stack([inv_top, inv_bot], 0).astype(BF16)
    return fwd, inv


def _shortconv_kernel(flags_ref, x_ref, prev_ref, next_ref, w_ref, b_ref, o_ref):
    i = pl.program_id(0)
    x = x_ref[...]
    t = x.shape[0]
    rows = lax.broadcasted_iota(jnp.int32, x.shape, 0)
    has_prev = flags_ref[2 * i].astype(F32)
    has_next = flags_ref[2 * i + 1].astype(F32)
    up = jnp.where(rows == 0, prev_ref[7:8, :] * has_prev, pltpu.roll(x, 1, 0))
    dn = jnp.where(rows == t - 1, next_ref[0:1, :] * has_next, pltpu.roll(x, t - 1, 0))
    o_ref[...] = up * w_ref[0:1, :] + x * w_ref[1:2, :] + dn * w_ref[2:3, :] + b_ref[...]


def _shortconv(p, width, conv_w, conv_b, blk_flags, t=HY_BLOCK, ct=768):
    n = p.shape[0]
    nb = n // t
    r8 = t // 8
    grid_spec = pltpu.PrefetchScalarGridSpec(
        num_scalar_prefetch=1, grid=(nb, width // ct),
        in_specs=[pl.BlockSpec((t, ct), lambda i, j, f: (i, j)),
                  pl.BlockSpec((8, ct), lambda i, j, f: (jnp.maximum(i * r8 - 1, 0), j)),
                  pl.BlockSpec((8, ct), lambda i, j, f: (jnp.minimum((i + 1) * r8, nb * r8 - 1), j)),
                  pl.BlockSpec((3, ct), lambda i, j, f: (0, j)), pl.BlockSpec((1, ct), lambda i, j, f: (0, j))],
        out_specs=pl.BlockSpec((t, ct), lambda i, j, f: (i, j)))
    return pl.pallas_call(
        _shortconv_kernel, grid_spec=grid_spec, out_shape=jax.ShapeDtypeStruct((n, width), F32),
        compiler_params=_params("parallel", "parallel"), name="hyena_shortconv",
    )(blk_flags, p, p, p, conv_w.astype(F32), conv_b.reshape(1, width).astype(F32))


def _filter_kernel(l_total, feats_ref, w1_ref, b1_ref, w2_ref, b2_ref, w3_ref, b3_ref, fr_ref, wo_ref, dl_ref,
                   h_ref, sum_ref):
    i = pl.program_id(0)
    bt = feats_ref.shape[0]
    fr = fr_ref[...]
    h = jnp.sin(fr * (_dot3(feats_ref[...], w1_ref[...]) + b1_ref[...]))
    h = jnp.sin(fr * (_dot3(h, w2_ref[...]) + b2_ref[...]))
    h = jnp.sin(fr * (_dot3(h, w3_ref[...]) + b3_ref[...]))
    out = _dot3(h, wo_ref[...])
    pos = (lax.broadcasted_iota(jnp.int32, (bt, 1), 0) + i * bt).astype(F32)
    tt = pos / max(l_total - 1, 1)
    window = jnp.exp(-tt * dl_ref[...])
    reps = out.shape[1] // window.shape[1]
    out = out * jnp.concatenate([window] * reps, axis=1)
    h_ref[...] = out

    @pl.when(i == 0)
    def _():
        sum_ref[...] = jnp.zeros_like(sum_ref)

    sum_ref[...] += jnp.sum(jnp.abs(out), axis=0, keepdims=True)


def _hyena_filter(l, w1, b1, w2, b2, w3, b3, freq, w_out, hy_w, bt=512):
    hid = w2.shape[0]
    pad = LANES - hid
    pos = jnp.arange(l, dtype=F32)
    tt = pos / max(l - 1, 1)
    bands = jnp.linspace(1e-4, HY_POS_BANDS - 1, HY_POS_BANDS, dtype=F32)
    ang = (2.0 * math.pi / l) * pos[:, None] * bands[None, :]
    feats = jnp.concatenate([tt[:, None], jnp.cos(ang), -jnp.sin(ang)], -1)
    feats = jnp.pad(feats, ((0, 0), (0, LANES - feats.shape[1])))
    w1p = jnp.pad(w1.astype(F32), ((0, LANES - w1.shape[0]), (0, pad)))
    w2p = jnp.pad(w2.astype(F32), ((0, pad), (0, pad)))
    w3p = jnp.pad(w3.astype(F32), ((0, pad), (0, pad)))
    n_out = w_out.shape[1]
    wo = w_out.astype(F32).reshape(hid, HY_ORDER, 2, hy_w).transpose(0, 2, 1, 3).reshape(hid, n_out)
    wop = jnp.pad(wo, ((0, pad), (0, 0)))

    def vecp(a):
        return jnp.pad(a.astype(F32), (0, pad)).reshape(1, LANES)

    deltas = jnp.abs(jnp.linspace(math.log(HY_DECAY_TARGET) / HY_SLOW_DECAY,
                                  math.log(HY_DECAY_TARGET) / HY_FAST_DECAY, hy_w, dtype=F32)).reshape(1, hy_w)
    bt = min(bt, l)
    sq = pl.BlockSpec((LANES, LANES), lambda i: (0, 0))
    vec = pl.BlockSpec((1, LANES), lambda i: (0, 0))
    return pl.pallas_call(
        functools.partial(_filter_kernel, l), grid=(l // bt,),
        in_specs=[pl.BlockSpec((bt, LANES), lambda i: (i, 0)), sq, vec, sq, vec, sq, vec, vec,
                  pl.BlockSpec((LANES, n_out), lambda i: (0, 0)), pl.BlockSpec((1, hy_w), lambda i: (0, 0))],
        out_specs=[pl.BlockSpec((bt, n_out), lambda i: (i, 0)), pl.BlockSpec((1, n_out), lambda i: (0, 0))],
        out_shape=[jax.ShapeDtypeStruct((l, n_out), F32), jax.ShapeDtypeStruct((1, n_out), F32)],
        compiler_params=_params("arbitrary"), name="hyena_filter",
    )(feats, w1p, vecp(b1), w2p, vecp(b2), w3p, vecp(b3), vecp(freq), wop, deltas)


def _dft_kernel(fsel_ref, f_ref, x_ref, o_ref):
    del fsel_ref
    o_ref[...] = jnp.dot(f_ref[...], x_ref[...].astype(BF16), preferred_element_type=F32)


def _block_dft(fwd, fsel, x, col_off, width, t=HY_BLOCK, ct=256):
    nseg = x.shape[0] // t
    ct = min(ct, width)
    grid_spec = pltpu.PrefetchScalarGridSpec(
        num_scalar_prefetch=1, grid=(2, nseg, width // ct),
        in_specs=[pl.BlockSpec((None, t, t), lambda hf, s, j, fs: (fs[s], hf, 0)),
                  pl.BlockSpec((t, ct), lambda hf, s, j, fs: (s, j + col_off))],
        out_specs=pl.BlockSpec((None, t, ct), lambda hf, s, j, fs: (s, hf, j)))
    return pl.pallas_call(
        _dft_kernel, grid_spec=grid_spec, out_shape=jax.ShapeDtypeStruct((nseg, 2 * t, width), F32),
        compiler_params=_params("arbitrary", "arbitrary", "arbitrary"), name="hyena_block_dft",
    )(fsel, fwd, x)


def _spectrum_kernel(idx_ref, a_ref, b_ref, s0_ref, inv_ref, o_ref):
    del idx_ref
    half = pl.program_id(1)
    rows = lax.broadcasted_iota(jnp.int32, a_ref.shape, 0)
    sgn = (1 - 2 * (rows & 1)).astype(F32)
    real_row = jnp.logical_or(half == 0, rows == 0)
    corr = jnp.where(real_row, s0_ref[...], 0.0)
    o_ref[...] = (a_ref[...] + sgn * (b_ref[...] - corr)) * inv_ref[...]


def _filter_spectra(seg_spec, idx, seg0, inv_norm, t=HY_BLOCK, ct=256):
    nslot = seg0.shape[0]
    width = seg_spec.shape[2]
    grid_spec = pltpu.PrefetchScalarGridSpec(
        num_scalar_prefetch=1, grid=(nslot, 2, width // ct),
        in_specs=[pl.BlockSpec((None, t, ct), lambda s, hf, j, ix: (ix[2 * s], hf, j)),
                  pl.BlockSpec((None, t, ct), lambda s, hf, j, ix: (ix[2 * s + 1], hf, j)),
                  pl.BlockSpec((None, 1, ct), lambda s, hf, j, ix: (s, 0, j)),
                  pl.BlockSpec((1, ct), lambda s, hf, j, ix: (0, j))],
        out_specs=pl.BlockSpec((None, t, ct), lambda s, hf, j, ix: (s, hf, j)))
    return pl.pallas_call(
        _spectrum_kernel, grid_spec=grid_spec, out_shape=jax.ShapeDtypeStruct((nslot, 2 * t, width), F32),
        compiler_params=_params("parallel", "arbitrary", "arbitrary"), name="hyena_filter_spectra",
    )(idx, seg_spec, seg_spec, seg0, inv_norm)


def _hyena_spectra_for_group(l, nb, fwd, filt_w, hy_w, t=HY_BLOCK):
    h, abs_sum = _hyena_filter(l, *filt_w, hy_w)
    half = HY_ORDER * hy_w
    hf, hb = h[:, :half], h[:, half:]
    norm = abs_sum[0, :half] + abs_sum[0, half:] - jnp.abs(hb[0])
    inv_norm = (1.0 / norm).reshape(1, half)
    hb_shift = jnp.concatenate([hb[1:], jnp.zeros((1, half), F32)], 0)
    segs = jnp.concatenate([hf, hb_shift], 0)
    fsel = jnp.asarray([0] * nb + [1] * nb, jnp.int32)
    seg_spec = _block_dft(fwd, fsel, segs, 0, half, t)

    def seg_index(d):
        return d if d >= 0 else nb + (-d) - 1

    idx, firsts = [], []
    for d in range(-(nb - 1), nb):
        idx += [seg_index(d), seg_index(d - 1)]
        if d - 1 >= 0:
            firsts.append(hf[(d - 1) * t])
        else:
            w = 1 - d
            firsts.append(hb[w * t] if w * t < l else jnp.zeros((half,), F32))
    seg0 = jnp.stack(firsts, 0).reshape(2 * nb - 1, 1, half)
    return _filter_spectra(seg_spec, jnp.asarray(idx, jnp.int32), seg0, inv_norm, t)


def _mix_kernel(nmax, out_bf16, tab_ref, z_ref, k_ref, fi_ref, zin_ref, gate_ref, bias_ref, o_ref,
                top_ref, bot_ref):
    i, jj = pl.program_id(0), pl.program_id(2)
    t = top_ref.shape[0]

    @pl.when(jj == 0)
    def _():
        top_ref[...] = jnp.zeros_like(top_ref)
        bot_ref[...] = jnp.zeros_like(bot_ref)

    @pl.when(tab_ref[(i * nmax + jj) * 3 + 2] == 1)
    def _():
        zt, zb = z_ref[0:t, :], z_ref[t:2 * t, :]
        kt, kb = k_ref[0:t, :], k_ref[t:2 * t, :]
        row0 = lax.broadcasted_iota(jnp.int32, zt.shape, 0) == 0
        bb = zb * kb
        top_ref[...] += zt * kt - jnp.where(row0, 0.0, bb)
        bot_ref[...] += jnp.where(row0, bb, zt * kb + zb * kt)

    @pl.when(jj == nmax - 1)
    def _():
        y = (jnp.dot(fi_ref[0], top_ref[...].astype(BF16), preferred_element_type=F32)
             + jnp.dot(fi_ref[1], bot_ref[...].astype(BF16), preferred_element_type=F32))
        out = gate_ref[...] * (y + zin_ref[...] * bias_ref[...])
        o_ref[...] = out.astype(o_ref.dtype)


def _hyena_mix(tab, nmax, zspec, kspec, k_col, inv, zin, zin_col, gate, gate_col, bias, out_bf16,
               t=HY_BLOCK, ct=256):
    nblk = zspec.shape[0]
    width = zspec.shape[2]
    nct = width // ct
    grid_spec = pltpu.PrefetchScalarGridSpec(
        num_scalar_prefetch=1, grid=(nblk, nct, nmax),
        in_specs=[pl.BlockSpec((None, 2 * t, ct), lambda i, j, jj, tb: (tb[(i * nmax + jj) * 3], 0, j)),
                  pl.BlockSpec((None, 2 * t, ct),
                               lambda i, j, jj, tb: (tb[(i * nmax + jj) * 3 + 1], 0, j + k_col * nct)),
                  pl.BlockSpec((2, t, t), lambda i, j, jj, tb: (0, 0, 0), pipeline_mode=pl.Buffered(1)),
                  pl.BlockSpec((t, ct), lambda i, j, jj, tb: (i, j + zin_col * nct)),
                  pl.BlockSpec((t, ct), lambda i, j, jj, tb: (i, j + gate_col * nct)),
                  pl.BlockSpec((1, ct), lambda i, j, jj, tb: (0, j))],
        out_specs=pl.BlockSpec((t, ct), lambda i, j, jj, tb: (i, j)),
        scratch_shapes=[pltpu.VMEM((t, ct), F32), pltpu.VMEM((t, ct), F32)])
    return pl.pallas_call(
        functools.partial(_mix_kernel, nmax, out_bf16), grid_spec=grid_spec,
        out_shape=jax.ShapeDtypeStruct((nblk * t, width), BF16 if out_bf16 else F32),
        compiler_params=_params("parallel", "arbitrary", "arbitrary"), name="hyena_mix_inverse",
    )(tab, zspec, kspec, inv, zin, gate, bias)


def _hyena_tables(groups, t=HY_BLOCK):
    nmax = max(l // t for _, l in groups)
    flags, tab = [], []
    blk0, slot0 = 0, 0
    for n_seq, l in groups:
        nb = l // t
        for s in range(n_seq):
            for bi in range(nb):
                flags += [int(bi > 0), int(bi < nb - 1)]
                for jj in range(nmax):
                    j = min(jj, nb - 1)
                    tab += [blk0 + s * nb + j, slot0 + (bi - j) + nb - 1, int(jj < nb)]
        blk0 += n_seq * nb
        slot0 += 2 * nb - 1
    return np.asarray(flags, np.int32), np.asarray(tab, np.int32), nmax


def _hyena(p, hy_w, groups, conv_w, conv_b, filt_w, bias, t=HY_BLOCK):
    flags, tab, nmax = _hyena_tables(groups, t)
    fwd, inv = _dft_matrices(t)
    u = _shortconv(p, 3 * hy_w, conv_w, conv_b, jnp.asarray(flags), t, ct=hy_w)
    kspec = jnp.concatenate([_hyena_spectra_for_group(l, l // t, fwd, filt_w, hy_w, t) for _, l in groups], 0)
    nblk = p.shape[0] // t
    fsel = jnp.zeros((nblk,), jnp.int32)
    tab = jnp.asarray(tab)
    bias = bias.astype(F32)
    ct, ct_mix = 256, 128
    zspec = _block_dft(fwd, fsel, u, 2 * (hy_w // ct), hy_w, t, ct)
    z1 = _hyena_mix(tab, nmax, zspec, kspec, 0, inv, u, 2, u, 0, bias[0:1], False, t, ct_mix)
    zspec = _block_dft(fwd, fsel, z1, 0, hy_w, t, ct)
    return _hyena_mix(tab, nmax, zspec, kspec, 1, inv, z1, 0, u, 1, bias[1:2], True, t, ct_mix)


def kernel(x_prompt, x_sample, ln_in_g, ln_in_b, w_in, hy_conv_w, hy_conv_b, hy_pos_w1, hy_pos_b1, hy_pos_w2, hy_pos_b2, hy_pos_w3, hy_pos_b3, hy_sin_freq, hy_pos_wout, hy_bias, gm_ln_g, gm_ln_b, gm_ws, gm_bs, hg_lb_raw, hg_norm_g, w_out, ln1_g, ln1_b, ln2_g, ln2_b, ffn_w1, ffn_w3, ffn_w2, moe_router_w, moe_router_b, moe_w1, moe_w3, moe_w2):
    depth, d_model, in_w = w_in.shape
    hy_w = hy_bias.shape[-1]
    gm_w = gm_ln_g.shape[-1]
    hg_w = hg_norm_g.shape[-1]
    o1 = 3 * hy_w
    o2 = o1 + 2 * gm_w
    alpha = (2 * depth) ** 0.25
    groups = [(x_prompt.shape[0], x_prompt.shape[1]), (x_sample.shape[0], x_sample.shape[1])]
    seq_lens = [l for n_seq, l in groups for _ in range(n_seq)]
    n_prompt = x_prompt.shape[0] * x_prompt.shape[1]
    hg_off = o1 // hg_w
    hg_cols = {"q": hg_off, "zf": hg_off + 1, "zb": hg_off + 2, "i": hg_off + 3, "og": hg_off + 4}
    gm_off = (o1 + 5 * hg_w) // gm_w

    lb_all = jnp.cumsum(jax.nn.softmax(hg_lb_raw.astype(F32), axis=0), axis=0)
    lb_all = lb_all - lb_all[:1]

    x = jnp.concatenate([x_prompt.reshape(-1, d_model), x_sample.reshape(-1, d_model)], 0)
    x, xb = _layernorm(x, ln_in_g, ln_in_b)
    for l in range(depth):
        w_l = jnp.concatenate([w_in[l][:, :o1], w_in[l][:, o2:], w_in[l][:, o1:o2]], 1).astype(BF16)
        p = _matmul(xb, w_l)
        filt_w = (hy_pos_w1[l], hy_pos_b1[l], hy_pos_w2[l], hy_pos_b2[l], hy_pos_w3[l], hy_pos_b3[l],
                  hy_sin_freq[l], hy_pos_wout[l])
        y_hy = _hyena(p, hy_w, groups, hy_conv_w[l], hy_conv_b[l], filt_w, hy_bias[l])
        y_hg = _hgrn2(p, hg_cols, hg_w, lb_all[l], hg_norm_g[l], seq_lens)
        y_gm = _gmlp(p, gm_off, gm_off + 1, gm_w, gm_ln_g[l], gm_ln_b[l], gm_ws[l], gm_bs[l])
        wo = w_out[l].astype(BF16)
        x, xb = _mix_out_residual_ln(
            [y_hy, y_gm, y_hg], [wo[:hy_w], wo[hy_w:hy_w + gm_w], wo[hy_w + gm_w:]], x, ln1_g[l], ln1_b[l], alpha)
        j = l // 2
        if l % 2 == 0:
            x, xb = _ffn_residual_ln(xb, ffn_w1[j].astype(BF16), ffn_w3[j].astype(BF16), ffn_w2[j].astype(BF16),
                                     x, ln2_g[l], ln2_b[l], alpha)
        else:
            ff = _moe(x, xb, moe_router_w[j], moe_router_b[j], moe_w1[j].astype(BF16), moe_w3[j].astype(BF16),
                      moe_w2[j].astype(BF16))
            x, xb = _residual_layernorm(x, ff, ln2_g[l], ln2_b[l], alpha)
    y_prompt = x[:n_prompt].reshape(x_prompt.shape)
    y_sample = x[n_prompt:].reshape(x_sample.shape)
    return (y_prompt, y_sample)
```

```python
import functools
import math

import numpy as np
import jax
import jax.numpy as jnp
from jax import lax
from jax.experimental import pallas as pl
from jax.experimental.pallas import tpu as pltpu

HEAD_DIM = 128
HY_ORDER = 2
HY_POS_BANDS = 16
HY_FAST_DECAY = 0.3
HY_SLOW_DECAY = 1.5
HY_DECAY_TARGET = 1e-2
GM_CHUNK = 128
N_EXPERTS = 8
TOP_K = 2
LN_EPS = 1e-5
RMS_EPS = 1e-6

HY_BLOCK = 2048
HG_CHUNK = 128
LANES = 128

V7X_VMEM_BYTES = 64 * 1024 * 1024
VMEM_LIMIT = V7X_VMEM_BYTES - 8 * 1024 * 1024

F32 = jnp.float32
BF16 = jnp.bfloat16
NT_DIMS = (((1,), (1,)), ((), ()))
TN_DIMS = (((0,), (0,)), ((), ()))


def _params(*sem):
    return pltpu.CompilerParams(dimension_semantics=sem, vmem_limit_bytes=VMEM_LIMIT)


def _ln_rows(x, g, b):
    mu = jnp.mean(x, -1, keepdims=True)
    xc = x - mu
    var = jnp.mean(xc * xc, -1, keepdims=True)
    return xc * lax.rsqrt(var + LN_EPS) * g + b


def _split_bf16(x):
    hi = x.astype(BF16)
    lo = (x - hi.astype(F32)).astype(BF16)
    return hi, lo


def _dot3(a, b):
    ah, al = _split_bf16(a)
    bh, bl = _split_bf16(b)
    return (jnp.dot(ah, bh, preferred_element_type=F32) + jnp.dot(ah, bl, preferred_element_type=F32)
            + jnp.dot(al, bh, preferred_element_type=F32))


def _ln2_kernel(na_blocks, xa_ref, xb_ref, g_ref, b_ref, of_ref, ob_ref):
    def emit(x_ref):
        y = _ln_rows(x_ref[...], g_ref[...], b_ref[...])
        of_ref[...] = y
        ob_ref[...] = y.astype(BF16)

    first = pl.program_id(0) < na_blocks
    pl.when(first)(lambda: emit(xa_ref))
    pl.when(jnp.logical_not(first))(lambda: emit(xb_ref))


def _layernorm_pair(xa, xb, g, b, bm=512):
    d = xa.shape[1]
    na, nb = xa.shape[0] // bm, xb.shape[0] // bm
    row = pl.BlockSpec((bm, d), lambda i: (i, 0))
    vec = pl.BlockSpec((1, d), lambda i: (0, 0))
    m = (na + nb) * bm
    return pl.pallas_call(
        functools.partial(_ln2_kernel, na), grid=(na + nb,),
        in_specs=[pl.BlockSpec((bm, d), lambda i: (jnp.minimum(i, na - 1), 0)),
                  pl.BlockSpec((bm, d), lambda i: (jnp.maximum(i - na, 0), 0)), vec, vec],
        out_specs=[row, row],
        out_shape=[jax.ShapeDtypeStruct((m, d), F32), jax.ShapeDtypeStruct((m, d), BF16)],
        compiler_params=_params("arbitrary"), name="layernorm",
    )(xa, xb, g.reshape(1, d), b.reshape(1, d))


def _res_ln_split_kernel(alpha, na_blocks, x_ref, y_ref, g_ref, b_ref, oa_ref, ob_ref):
    y = _ln_rows(alpha * x_ref[...] + y_ref[...], g_ref[...], b_ref[...])
    first = pl.program_id(0) < na_blocks

    @pl.when(first)
    def _():
        oa_ref[...] = y

    @pl.when(jnp.logical_not(first))
    def _():
        ob_ref[...] = y


def _residual_layernorm_split(x, y, g, b, alpha, n_first, bm=512):
    m, d = x.shape
    na = n_first // bm
    row = pl.BlockSpec((bm, d), lambda i: (i, 0))
    vec = pl.BlockSpec((1, d), lambda i: (0, 0))
    return pl.pallas_call(
        functools.partial(_res_ln_split_kernel, alpha, na), grid=(m // bm,), in_specs=[row, row, vec, vec],
        out_specs=[pl.BlockSpec((bm, d), lambda i: (jnp.minimum(i, na - 1), 0)),
                   pl.BlockSpec((bm, d), lambda i: (jnp.maximum(i - na, 0), 0))],
        out_shape=[jax.ShapeDtypeStruct((n_first, d), F32), jax.ShapeDtypeStruct((m - n_first, d), F32)],
        compiler_params=_params("arbitrary"), name="residual_layernorm",
    )(x, y, g.reshape(1, d), b.reshape(1, d))


def _mm_kernel(a_ref, w_ref, o_ref):
    o_ref[...] = jnp.dot(a_ref[...], w_ref[...], preferred_element_type=F32).astype(o_ref.dtype)


def _matmul(a, w, out_dtype=F32, bm=1024, bn=1024):
    m, k = a.shape
    n = w.shape[1]
    bm, bn = min(bm, m), min(bn, n)
    return pl.pallas_call(
        _mm_kernel, grid=(m // bm, n // bn),
        in_specs=[pl.BlockSpec((bm, k), lambda i, j: (i, 0)), pl.BlockSpec((k, bn), lambda i, j: (0, j))],
        out_specs=pl.BlockSpec((bm, bn), lambda i, j: (i, j)),
        out_shape=jax.ShapeDtypeStruct((m, n), out_dtype),
        compiler_params=_params("parallel", "arbitrary"), name="matmul",
    )(a, w)


def _mix_out_kernel(alpha, a0_ref, a1_ref, a2_ref, w0_ref, w1_ref, w2_ref, x_ref, g_ref, b_ref, of_ref, ob_ref):
    mix = (jnp.dot(a0_ref[...], w0_ref[...], preferred_element_type=F32)
           + jnp.dot(a1_ref[...], w1_ref[...], preferred_element_type=F32)
           + jnp.dot(a2_ref[...], w2_ref[...], preferred_element_type=F32))
    y = _ln_rows(alpha * x_ref[...] + mix, g_ref[...], b_ref[...])
    of_ref[...] = y
    ob_ref[...] = y.astype(BF16)


def _mix_out_residual_ln(parts, weights, x, g, b, alpha, bm=512):
    m, d = x.shape
    row = pl.BlockSpec((bm, d), lambda i: (i, 0))
    vec = pl.BlockSpec((1, d), lambda i: (0, 0))
    a_specs = [pl.BlockSpec((bm, a.shape[1]), lambda i: (i, 0)) for a in parts]
    w_specs = [pl.BlockSpec(w.shape, lambda i: (0, 0)) for w in weights]
    return pl.pallas_call(
        functools.partial(_mix_out_kernel, alpha), grid=(m // bm,),
        in_specs=a_specs + w_specs + [row, vec, vec], out_specs=[row, row],
        out_shape=[jax.ShapeDtypeStruct((m, d), F32), jax.ShapeDtypeStruct((m, d), BF16)],
        compiler_params=_params("parallel"), name="out_proj_residual_ln",
    )(*parts, *weights, x, g.reshape(1, d), b.reshape(1, d))


def _swiglu_acc(a_ref, w1_ref, w3_ref, w2_ref, acc_ref):
    f = pl.program_id(1)

    @pl.when(f == 0)
    def _():
        acc_ref[...] = jnp.zeros_like(acc_ref)

    a = a_ref[...]
    h1 = jnp.dot(a, w1_ref[...], preferred_element_type=F32)
    h3 = jnp.dot(a, w3_ref[...], preferred_element_type=F32)
    gated = (h1 * jax.nn.sigmoid(h1) * h3).astype(BF16)
    acc_ref[...] += jnp.dot(gated, w2_ref[...], preferred_element_type=F32)


def _ffn_kernel(alpha, a_ref, w1_ref, w3_ref, w2_ref, x_ref, g_ref, b_ref, of_ref, ob_ref, acc_ref):
    _swiglu_acc(a_ref, w1_ref, w3_ref, w2_ref, acc_ref)

    @pl.when(pl.program_id(1) == pl.num_programs(1) - 1)
    def _():
        y = _ln_rows(alpha * x_ref[...] + acc_ref[...], g_ref[...], b_ref[...])
        of_ref[...] = y
        ob_ref[...] = y.astype(BF16)


def _ffn_residual_ln(a, w1, w3, w2, x, g, b, alpha, bm=512, bf=512):
    m, d = a.shape
    dff = w1.shape[1]
    row = pl.BlockSpec((bm, d), lambda i, f: (i, 0))
    vec = pl.BlockSpec((1, d), lambda i, f: (0, 0))
    up = pl.BlockSpec((d, bf), lambda i, f: (0, f))
    return pl.pallas_call(
        functools.partial(_ffn_kernel, alpha), grid=(m // bm, dff // bf),
        in_specs=[row, up, up, pl.BlockSpec((bf, d), lambda i, f: (f, 0)), row, vec, vec],
        out_specs=[row, row],
        out_shape=[jax.ShapeDtypeStruct((m, d), F32), jax.ShapeDtypeStruct((m, d), BF16)],
        scratch_shapes=[pltpu.VMEM((bm, d), F32)],
        compiler_params=_params("parallel", "arbitrary"), name="ffn_residual_ln",
    )(a, w1, w3, w2, x, g.reshape(1, d), b.reshape(1, d))


def _moe_ffn_kernel(blk_e_ref, a_ref, w1_ref, w3_ref, w2_ref, o_ref, acc_ref):
    del blk_e_ref
    _swiglu_acc(a_ref, w1_ref, w3_ref, w2_ref, acc_ref)

    @pl.when(pl.program_id(1) == pl.num_programs(1) - 1)
    def _():
        o_ref[...] = acc_ref[...]


def _moe_grouped_ffn(blk_e, xb, w1, w3, w2, bm, bf=1024):
    cap, d = xb.shape
    dff = w1.shape[2]
    bf = bf if dff % bf == 0 else 512
    row = pl.BlockSpec((bm, d), lambda i, f, e: (i, 0))
    up = pl.BlockSpec((None, d, bf), lambda i, f, e: (e[i], 0, f))
    grid_spec = pltpu.PrefetchScalarGridSpec(
        num_scalar_prefetch=1, grid=(cap // bm, dff // bf),
        in_specs=[row, up, up, pl.BlockSpec((None, bf, d), lambda i, f, e: (e[i], f, 0))],
        out_specs=row, scratch_shapes=[pltpu.VMEM((bm, d), F32)])
    return pl.pallas_call(
        _moe_ffn_kernel, grid_spec=grid_spec, out_shape=jax.ShapeDtypeStruct((cap, d), F32),
        compiler_params=_params("parallel", "arbitrary"), name="moe_grouped_ffn",
    )(blk_e, xb, w1, w3, w2)


def _moe(h_f32, h_bf16, wr, br, w1, w3, w2, bm=1024):
    n, d = h_f32.shape
    logits = jnp.dot(h_f32, wr.astype(F32), precision=lax.Precision.HIGHEST) + br.astype(F32)
    top_val, top_idx = lax.top_k(logits, TOP_K)
    gate = jax.nn.softmax(top_val, axis=-1)
    nk = n * TOP_K
    flat_e = top_idx.reshape(nk)
    order = jnp.argsort(flat_e)
    se = flat_e[order]
    st = (order // TOP_K).astype(jnp.int32)
    counts = jnp.bincount(flat_e, length=N_EXPERTS)
    padded = (counts + bm - 1) // bm * bm
    start = jnp.cumsum(counts) - counts
    pstart = jnp.cumsum(padded) - padded
    dest_sorted = pstart[se] + (jnp.arange(nk, dtype=jnp.int32) - start[se])
    n_blocks = nk // bm + N_EXPERTS
    cap = n_blocks * bm
    slot_tok = jnp.full((cap,), n, jnp.int32).at[dest_sorted].set(st)
    blk_e = jnp.minimum(jnp.searchsorted(jnp.cumsum(padded), jnp.arange(n_blocks) * bm, side='right'),
                        N_EXPERTS - 1).astype(jnp.int32)
    xb = jnp.concatenate([h_bf16, jnp.zeros((1, d), BF16)], 0)[slot_tok]
    yb = _moe_grouped_ffn(blk_e, xb, w1, w3, w2, bm)
    dest = jnp.zeros((nk,), jnp.int32).at[order].set(dest_sorted.astype(jnp.int32)).reshape(n, TOP_K)
    return yb[dest[:, 0]] * gate[:, 0:1] + yb[dest[:, 1]] * gate[:, 1:2]


def _gmlp_kernel(groups, pu_ref, pv_ref, g_ref, b_ref, ws_ref, bs_ref, o_ref):
    u = jax.nn.gelu(pu_ref[...])
    v = _ln_rows(jax.nn.gelu(pv_ref[...]), g_ref[...], b_ref[...])
    for n in range(u.shape[0] // GM_CHUNK):
        rows = slice(n * GM_CHUNK, (n + 1) * GM_CHUNK)
        for grp in range(groups):
            cols = slice(grp * HEAD_DIM, (grp + 1) * HEAD_DIM)
            s = _dot3(ws_ref[grp], v[rows, cols]) + bs_ref[grp]
            o_ref[rows, cols] = (u[rows, cols] * s).astype(o_ref.dtype)


def _gmlp(p, col_u, col_v, gm_w, ln_g, ln_b, ws, bs, bt=512):
    n = p.shape[0]
    groups = ws.shape[0]
    bsb = jnp.broadcast_to(bs[:, :, None], (groups, GM_CHUNK, HEAD_DIM)).astype(F32)
    vec = pl.BlockSpec((1, gm_w), lambda i: (0, 0))
    full3 = pl.BlockSpec((groups, GM_CHUNK, HEAD_DIM), lambda i: (0, 0, 0))
    return pl.pallas_call(
        functools.partial(_gmlp_kernel, groups), grid=(n // bt,),
        in_specs=[pl.BlockSpec((bt, gm_w), lambda i: (i, col_u)), pl.BlockSpec((bt, gm_w), lambda i: (i, col_v)),
                  vec, vec, pl.BlockSpec((groups, GM_CHUNK, GM_CHUNK), lambda i: (0, 0, 0)), full3],
        out_specs=pl.BlockSpec((bt, gm_w), lambda i: (i, 0)),
        out_shape=jax.ShapeDtypeStruct((n, gm_w), BF16),
        compiler_params=_params("parallel"), name="gmlp",
    )(p, p, ln_g.reshape(1, gm_w), ln_b.reshape(1, gm_w), ws.astype(F32), bsb)


def _hgrn2_constants(c, reverse):
    n_lvl = int(math.log2(c))
    t = np.arange(c)
    ms, ws = [], []
    for lvl in range(n_lvl):
        m = c >> (lvl + 1)
        mid = (t // (2 * m)) * (2 * m) + m
        upper = t >= mid
        mat = np.zeros((c, c), np.float32)
        for r in range(c):
            if upper[r]:
                mat[r, mid[r]:r + 1] = 1.0
            else:
                mat[r, r + 1:mid[r]] = 1.0
        same = (t[:, None] // (2 * m)) == (t[None, :] // (2 * m))
        ws.append((same & upper[:, None] & ~upper[None, :]).astype(np.float32))
        ms.append(mat)
    ws.append(np.eye(c, dtype=np.float32))
    ms.append(np.tril(np.ones((c, c), np.float32)))
    ms.append(np.triu(np.ones((c, c), np.float32), 1))
    if reverse:
        ms = [a[::-1, ::-1] for a in ms]
        ws = [a[::-1, ::-1] for a in ws]
    return np.concatenate(ms, 0), np.stack(ws, 0)


def _hgrn2_kernel(heads, n_lvl, carry_row, final, reset_ref, *refs):
    if final:
        (q_ref, z_ref, i_ref, og_ref, oo_ref, loglb_ref, log1mlb_ref, onemlb_ref, ng_ref, m_ref, w_ref,
         out_ref, st_ref, e_ref) = refs
    else:
        (q_ref, z_ref, i_ref, loglb_ref, log1mlb_ref, onemlb_ref, m_ref, w_ref, out_ref, st_ref, e_ref) = refs
    c = q_ref.shape[0]

    @pl.when(reset_ref[pl.program_id(0)] == 1)
    def _():
        st_ref[...] = jnp.zeros_like(st_ref)

    z = z_ref[...]
    e = jnp.exp(-jnp.abs(z))
    r = 1.0 / (1.0 + e)
    log_sig = jnp.minimum(z, 0.0) - jnp.log(1.0 + e)
    sig_neg = jnp.where(z >= 0.0, e * r, r)
    a = loglb_ref[...]
    b = log1mlb_ref[...] + log_sig
    g = jnp.maximum(a, b) + jnp.log(1.0 + jnp.exp(-jnp.abs(a - b)))
    k = onemlb_ref[...] * sig_neg
    q = q_ref[...]
    qs = q * jax.nn.sigmoid(q)
    v = i_ref[...].astype(BF16)

    g_hi, g_lo = _split_bf16(g)
    m_all = m_ref[...]
    e_ref[...] = (jnp.dot(m_all, g_hi, preferred_element_type=F32)
                  + jnp.dot(m_all, g_lo, preferred_element_type=F32))

    for h in range(heads):
        cols = slice(h * HEAD_DIM, (h + 1) * HEAD_DIM)
        qh, kh, vh = qs[:, cols], k[:, cols], v[:, cols]
        amat = w_ref[n_lvl] * lax.dot_general(qh.astype(BF16), kh.astype(BF16), NT_DIMS,
                                              preferred_element_type=F32)
        for lvl in range(n_lvl):
            ex = jnp.exp(e_ref[lvl * c:(lvl + 1) * c, cols])
            amat = amat + w_ref[lvl] * lax.dot_general((qh * ex).astype(BF16), (kh * ex).astype(BF16), NT_DIMS,
                                                       preferred_element_type=F32)
        e_in = e_ref[n_lvl * c:(n_lvl + 1) * c, cols]
        q_in = (qh * jnp.exp(e_in)).astype(BF16)
        k_st = (kh * jnp.exp(e_ref[(n_lvl + 1) * c:(n_lvl + 2) * c, cols])).astype(BF16)
        dec = jnp.exp(e_in[carry_row:carry_row + 1, :])
        st = st_ref[h]
        o = (lax.dot_general(q_in, st.astype(BF16), NT_DIMS, preferred_element_type=F32)
             + jnp.dot(amat.astype(BF16), vh, preferred_element_type=F32))
        st_ref[h] = st * dec + lax.dot_general(vh, k_st, TN_DIMS, preferred_element_type=F32)
        if final:
            o = o + oo_ref[:, cols]
            o = o * lax.rsqrt(jnp.mean(o * o, -1, keepdims=True) + RMS_EPS)
            og = og_ref[:, cols]
            out_ref[:, cols] = (o * ng_ref[:, cols] * (og * jax.nn.sigmoid(og))).astype(out_ref.dtype)
        else:
            out_ref[:, cols] = o


def _hgrn2_pass(p, cols, hg_w, z_col, reverse, reset, lbs, extra, c=HG_CHUNK):
    n = p.shape[0]
    nc = n // c
    heads = hg_w // HEAD_DIM
    n_lvl = int(math.log2(c))
    m_np, w_np = _hgrn2_constants(c, reverse)
    final = extra is not None

    def cmap(i, r):
        return (nc - 1 - i) if reverse else i

    def tok(col):
        return pl.BlockSpec((c, hg_w), lambda i, r: (cmap(i, r), col))

    vec = pl.BlockSpec((1, hg_w), lambda i, r: (0, 0))
    in_specs = [tok(cols["q"]), tok(z_col), tok(cols["i"])]
    args = [p, p, p]
    if final:
        o_other, norm_g = extra
        in_specs += [tok(cols["og"]), pl.BlockSpec((c, hg_w), lambda i, r: (cmap(i, r), 0))]
        args += [p, o_other]
    in_specs += [vec, vec, vec]
    args += list(lbs)
    if final:
        in_specs += [vec]
        args += [norm_g.reshape(1, hg_w).astype(F32)]
    in_specs += [pl.BlockSpec(m_np.shape, lambda i, r: (0, 0)), pl.BlockSpec(w_np.shape, lambda i, r: (0, 0, 0))]
    args += [jnp.asarray(m_np, BF16), jnp.asarray(w_np, F32)]
    grid_spec = pltpu.PrefetchScalarGridSpec(
        num_scalar_prefetch=1, grid=(nc,), in_specs=in_specs,
        out_specs=pl.BlockSpec((c, hg_w), lambda i, r: (cmap(i, r), 0)),
        scratch_shapes=[pltpu.VMEM((heads, HEAD_DIM, HEAD_DIM), F32), pltpu.VMEM((m_np.shape[0], hg_w), F32)])
    order = np.arange(nc)[::-1] if reverse else np.arange(nc)
    reset_steps = jnp.asarray(np.asarray(reset, np.int32)[order])
    return pl.pallas_call(
        functools.partial(_hgrn2_kernel, heads, n_lvl, 0 if reverse else c - 1, final), grid_spec=grid_spec,
        out_shape=jax.ShapeDtypeStruct((n, hg_w), BF16 if final else F32),
        compiler_params=_params("arbitrary"), name="hgrn2_fwd" if final else "hgrn2_bwd",
    )(reset_steps, *args)


def _hgrn2(p, cols, hg_w, lb, norm_g, seq_lens, c=HG_CHUNK):
    lb = lb.astype(F32).reshape(1, hg_w)
    lbs = (jnp.log(lb), jnp.log1p(-lb), 1.0 - lb)
    starts = np.cumsum([0] + list(seq_lens))
    nc = starts[-1] // c
    first = np.zeros(nc, np.int32)
    last = np.zeros(nc, np.int32)
    first[starts[:-1] // c] = 1
    last[starts[1:] // c - 1] = 1
    o_b = _hgrn2_pass(p, cols, hg_w, cols["zb"], True, last, lbs, None, c)
    return _hgrn2_pass(p, cols, hg_w, cols["zf"], False, first, lbs, (o_b, norm_g), c)


def _dft_matrices(t):
    n = 2 * t
    f = jnp.arange(t, dtype=jnp.int32)
    ang = (2.0 * math.pi / n) * ((f[:, None] * f[None, :]) % n).astype(F32)
    cos, sin = jnp.cos(ang), jnp.sin(ang)
    alt = (1 - 2 * (f % 2)).astype(F32)
    first = f == 0
    fwd = jnp.stack([cos, jnp.where(first[:, None], alt[None, :], -sin)], 0).astype(BF16)
    wgt = jnp.where(first, 1.0, 2.0)[None, :] / n
    inv = jnp.stack([cos * wgt, jnp.where(first[None, :], alt[:, None] / n, -sin * wgt)], 0).astype(BF16)
    return fwd, inv


def _shortconv_kernel(flags_ref, x_ref, prev_ref, next_ref, w_ref, b_ref, o_ref):
    i = pl.program_id(0)
    x = x_ref[...]
    t = x.shape[0]
    rows = lax.broadcasted_iota(jnp.int32, x.shape, 0)
    has_prev = flags_ref[2 * i].astype(F32)
    has_next = flags_ref[2 * i + 1].astype(F32)
    up = jnp.where(rows == 0, prev_ref[7:8, :] * has_prev, pltpu.roll(x, 1, 0))
    dn = jnp.where(rows == t - 1, next_ref[0:1, :] * has_next, pltpu.roll(x, t - 1, 0))
    o_ref[...] = up * w_ref[0:1, :] + x * w_ref[1:2, :] + dn * w_ref[2:3, :] + b_ref[...]


def _shortconv(p, width, conv_w, conv_b, blk_flags, t=HY_BLOCK, ct=768):
    n = p.shape[0]
    nb = n // t
    r8 = t // 8
    grid_spec = pltpu.PrefetchScalarGridSpec(
        num_scalar_prefetch=1, grid=(nb, width // ct),
        in_specs=[pl.BlockSpec((t, ct), lambda i, j, f: (i, j)),
                  pl.BlockSpec((8, ct), lambda i, j, f: (jnp.maximum(i * r8 - 1, 0), j)),
                  pl.BlockSpec((8, ct), lambda i, j, f: (jnp.minimum((i + 1) * r8, nb * r8 - 1), j)),
                  pl.BlockSpec((3, ct), lambda i, j, f: (0, j)), pl.BlockSpec((1, ct), lambda i, j, f: (0, j))],
        out_specs=pl.BlockSpec((t, ct), lambda i, j, f: (i, j)))
    return pl.pallas_call(
        _shortconv_kernel, grid_spec=grid_spec, out_shape=jax.ShapeDtypeStruct((n, width), F32),
        compiler_params=_params("parallel", "parallel"), name="hyena_shortconv",
    )(blk_flags, p, p, p, conv_w.astype(F32), conv_b.reshape(1, width).astype(F32))


def _filter_kernel(l_total, feats_ref, w1_ref, b1_ref, w2_ref, b2_ref, w3_ref, b3_ref, fr_ref, wo_ref, dl_ref,
                   h_ref, sum_ref):
    i = pl.program_id(0)
    bt = feats_ref.shape[0]
    fr = fr_ref[...]
    feats = feats_ref[...]
    h = jnp.sin(fr * (_dot3(feats, w1_ref[...]) + b1_ref[...]))
    h = jnp.sin(fr * (_dot3(h, w2_ref[...]) + b2_ref[...]))
    h = jnp.sin(fr * (_dot3(h, w3_ref[...]) + b3_ref[...]))
    out = _dot3(h, wo_ref[...])
    window = jnp.exp(-feats[:, 0:1] * dl_ref[...])
    reps = out.shape[1] // window.shape[1]
    out = out * jnp.concatenate([window] * reps, axis=1)
    row = lax.broadcasted_iota(jnp.int32, (bt, 1), 0) + i * bt
    out = jnp.where(row == l_total, 0.0, out)
    h_ref[...] = out

    @pl.when(i == 0)
    def _():
        sum_ref[...] = jnp.zeros_like(sum_ref)

    sum_ref[...] += jnp.sum(jnp.abs(out), axis=0, keepdims=True)


def _hyena_filter(l, w1, b1, w2, b2, w3, b3, freq, w_out, hy_w, bt=512):
    hid = w2.shape[0]
    pad = LANES - hid
    half = HY_ORDER * hy_w
    pos = jnp.arange(l, dtype=F32)
    pos = jnp.concatenate([pos, l - pos], 0)
    tt = pos / max(l - 1, 1)
    bands = jnp.linspace(1e-4, HY_POS_BANDS - 1, HY_POS_BANDS, dtype=F32)
    ang = (2.0 * math.pi / l) * pos[:, None] * bands[None, :]
    feats = jnp.concatenate([tt[:, None], jnp.cos(ang), -jnp.sin(ang)], -1)
    feats = jnp.pad(feats, ((0, 0), (0, LANES - feats.shape[1])))
    w1p = jnp.pad(w1.astype(F32), ((0, LANES - w1.shape[0]), (0, pad)))
    w2p = jnp.pad(w2.astype(F32), ((0, pad), (0, pad)))
    w3p = jnp.pad(w3.astype(F32), ((0, pad), (0, pad)))
    wo = w_out.astype(F32).reshape(hid, HY_ORDER, 2, hy_w).transpose(2, 0, 1, 3).reshape(2, hid, half)
    wop = jnp.pad(wo, ((0, 0), (0, pad), (0, 0)))

    def vecp(a):
        return jnp.pad(a.astype(F32), (0, pad)).reshape(1, LANES)

    deltas = jnp.abs(jnp.linspace(math.log(HY_DECAY_TARGET) / HY_SLOW_DECAY,
                                  math.log(HY_DECAY_TARGET) / HY_FAST_DECAY, hy_w, dtype=F32)).reshape(1, hy_w)
    bt = min(bt, l)
    nbt = l // bt
    sq = pl.BlockSpec((LANES, LANES), lambda i: (0, 0))
    vec = pl.BlockSpec((1, LANES), lambda i: (0, 0))
    return pl.pallas_call(
        functools.partial(_filter_kernel, l), grid=(2 * nbt,),
        in_specs=[pl.BlockSpec((bt, LANES), lambda i: (i, 0)), sq, vec, sq, vec, sq, vec, vec,
                  pl.BlockSpec((None, LANES, half), lambda i: (i // nbt, 0, 0)),
                  pl.BlockSpec((1, hy_w), lambda i: (0, 0))],
        out_specs=[pl.BlockSpec((bt, half), lambda i: (i, 0)), pl.BlockSpec((1, half), lambda i: (0, 0))],
        out_shape=[jax.ShapeDtypeStruct((2 * l, half), F32), jax.ShapeDtypeStruct((1, half), F32)],
        compiler_params=_params("arbitrary"), name="hyena_filter",
    )(feats, w1p, vecp(b1), w2p, vecp(b2), w3p, vecp(b3), vecp(freq), wop, deltas)


def _dft_kernel(f_ref, x_ref, o_ref):
    o_ref[...] = jnp.dot(f_ref[...], x_ref[...].astype(BF16), preferred_element_type=F32).astype(o_ref.dtype)


def _block_dft(fwd, x, col_off, width, out_dtype, t=HY_BLOCK, ct=256):
    nseg = x.shape[0] // t
    ct = min(ct, width)
    return pl.pallas_call(
        _dft_kernel, grid=(2, nseg, width // ct),
        in_specs=[pl.BlockSpec((None, t, t), lambda hf, s, j: (hf, 0, 0)),
                  pl.BlockSpec((t, ct), lambda hf, s, j: (s, j + col_off))],
        out_specs=pl.BlockSpec((None, None, t, ct), lambda hf, s, j: (s, hf, 0, j)),
        out_shape=jax.ShapeDtypeStruct((nseg, 2, t, width), out_dtype),
        compiler_params=_params("arbitrary", "arbitrary", "arbitrary"), name="hyena_block_dft",
    )(fwd, x)


def _spectrum_kernel(idx_ref, a_ref, b_ref, s0_ref, inv_ref, o_ref):
    del idx_ref
    t, ct = a_ref.shape[1], a_ref.shape[2]
    rows = lax.broadcasted_iota(jnp.int32, (t, ct), 0)
    sgn = (1 - 2 * (rows & 1)).astype(F32)
    s0 = s0_ref[...]
    inv = inv_ref[...]
    o_ref[0] = ((a_ref[0] + sgn * (b_ref[0] - s0)) * inv).astype(o_ref.dtype)
    o_ref[1] = ((a_ref[1] + sgn * (b_ref[1] - jnp.where(rows == 0, s0, 0.0))) * inv).astype(o_ref.dtype)


def _filter_spectra(seg_spec, idx, seg0, inv_norm, ct=256):
    nslot = seg0.shape[0]
    t, width = seg_spec.shape[2], seg_spec.shape[3]
    grid_spec = pltpu.PrefetchScalarGridSpec(
        num_scalar_prefetch=1, grid=(nslot, width // ct),
        in_specs=[pl.BlockSpec((None, 2, t, ct), lambda s, j, ix: (ix[2 * s], 0, 0, j)),
                  pl.BlockSpec((None, 2, t, ct), lambda s, j, ix: (ix[2 * s + 1], 0, 0, j)),
                  pl.BlockSpec((None, 1, ct), lambda s, j, ix: (s, 0, j)),
                  pl.BlockSpec((1, ct), lambda s, j, ix: (0, j))],
        out_specs=pl.BlockSpec((None, 2, t, ct), lambda s, j, ix: (s, 0, 0, j)))
    return pl.pallas_call(
        _spectrum_kernel, grid_spec=grid_spec, out_shape=jax.ShapeDtypeStruct((nslot, 2, t, width), BF16),
        compiler_params=_params("parallel", "arbitrary"), name="hyena_filter_spectra",
    )(idx, seg_spec, seg_spec, seg0, inv_norm)


def _hyena_filter_spectra(l, fwd, filt_w, hy_w, t=HY_BLOCK):
    nb = l // t
    taps, abs_sum = _hyena_filter(l, *filt_w, hy_w)
    half = taps.shape[1]
    seg_spec = _block_dft(fwd, taps, 0, half, F32, t)

    def seg_index(d):
        return d if d >= 0 else 2 * nb + d

    idx = []
    for d in range(-(nb - 1), nb):
        idx += [seg_index(d), seg_index(d - 1)]
    first_rows = taps.reshape(2 * nb, t, half)[:, 0, :]
    seg0 = jnp.stack([first_rows[i] for i in idx[1::2]], 0).reshape(2 * nb - 1, 1, half)
    return _filter_spectra(seg_spec, jnp.asarray(idx, jnp.int32), seg0, 1.0 / abs_sum)


def _mix_kernel(nb, *refs):
    if len(refs) == 5:
        z_ref, k_ref, _, o_ref, acc_ref = refs
    else:
        z_ref, k_ref, o_ref, acc_ref = refs
    dl = pl.program_id(3)
    d = dl - (nb - 1)

    @pl.when(dl == 0)
    def _():
        acc_ref[...] = jnp.zeros_like(acc_ref)

    kt, kb = k_ref[0].astype(F32), k_ref[1].astype(F32)
    rows = lax.broadcasted_iota(jnp.int32, kt.shape, 0)
    row0 = jnp.logical_and(rows == 0, pl.program_id(1) == 0)
    for j in range(nb):
        i = j + d

        @pl.when(jnp.logical_and(i >= 0, i < nb))
        def _():
            zt, zb = z_ref[j, 0].astype(F32), z_ref[j, 1].astype(F32)
            bb = zb * kb
            acc_ref[i, 0] += zt * kt - jnp.where(row0, 0.0, bb)
            acc_ref[i, 1] += jnp.where(row0, bb, zt * kb + zb * kt)

    @pl.when(dl == 2 * nb - 2)
    def _():
        o_ref[...] = acc_ref[...].astype(o_ref.dtype)


def _hyena_mix(zspec, kspec, k_col, n_seq, nb, blk_off, prev, ft=512, ct=256):
    nblk, _, t, width = zspec.shape
    ft = min(ft, t)
    nct = width // ct
    assert blk_off % nb == 0
    s_off = blk_off // nb
    zblk = pl.BlockSpec((nb, 2, ft, ct), lambda s, fi, j, dl: (s + s_off, 0, fi, j))
    in_specs = [zblk, pl.BlockSpec((None, 2, ft, ct), lambda s, fi, j, dl: (dl, 0, fi, j + k_col * nct))]
    args = [zspec, kspec]
    aliases = {}
    if prev is not None:
        in_specs.append(pl.BlockSpec(memory_space=pl.ANY))
        args.append(prev)
        aliases = {2: 0}
    return pl.pallas_call(
        functools.partial(_mix_kernel, nb), grid=(n_seq, t // ft, nct, 2 * nb - 1),
        in_specs=in_specs, out_specs=zblk,
        out_shape=jax.ShapeDtypeStruct(zspec.shape, BF16),
        scratch_shapes=[pltpu.VMEM((nb, 2, ft, ct), F32)], input_output_aliases=aliases,
        compiler_params=_params("parallel", "parallel", "parallel", "arbitrary"), name="hyena_mix",
    )(*args)


def _inverse_kernel(y_ref, fi_ref, zin_ref, gate_ref, bias_ref, o_ref):
    y = (jnp.dot(fi_ref[0], y_ref[0], preferred_element_type=F32)
         + jnp.dot(fi_ref[1], y_ref[1], preferred_element_type=F32))
    o_ref[...] = (gate_ref[...] * (y + zin_ref[...] * bias_ref[...])).astype(o_ref.dtype)


def _hyena_inverse(yspec, inv, zin, zin_col, gate, gate_col, bias, out_dtype, ct=256):
    nblk, _, t, width = yspec.shape
    nct = width // ct
    return pl.pallas_call(
        _inverse_kernel, grid=(nblk, nct),
        in_specs=[pl.BlockSpec((None, 2, t, ct), lambda i, j: (i, 0, 0, j)),
                  pl.BlockSpec((2, t, t), lambda i, j: (0, 0, 0), pipeline_mode=pl.Buffered(1)),
                  pl.BlockSpec((t, ct), lambda i, j: (i, j + zin_col * nct)),
                  pl.BlockSpec((t, ct), lambda i, j: (i, j + gate_col * nct)),
                  pl.BlockSpec((1, ct), lambda i, j: (0, j))],
        out_specs=pl.BlockSpec((t, ct), lambda i, j: (i, j)),
        out_shape=jax.ShapeDtypeStruct((nblk * t, width), out_dtype),
        compiler_params=_params("parallel", "arbitrary"), name="hyena_inverse",
    )(yspec, inv, zin, gate, bias)


def _hyena(p, hy_w, groups, conv_w, conv_b, filt_w, bias, t=HY_BLOCK):
    flags = []
    for n_seq, l in groups:
        nb = l // t
        flags += [int(bi > 0) if side == 0 else int(bi < nb - 1)
                  for _ in range(n_seq) for bi in range(nb) for side in range(2)]
    fwd, inv = _dft_matrices(t)
    u = _shortconv(p, 3 * hy_w, conv_w, conv_b, jnp.asarray(np.asarray(flags, np.int32)), t, ct=hy_w)
    kspecs = [_hyena_filter_spectra(l, fwd, filt_w, hy_w, t) for _, l in groups]
    bias = bias.astype(F32)
    ct = 256
    z, z_col = u, 2
    for o in range(HY_ORDER):
        zspec = _block_dft(fwd, z, z_col * (hy_w // ct), hy_w, BF16, t, ct)
        yspec, blk = None, 0
        for (n_seq, l), kspec in zip(groups, kspecs):
            yspec = _hyena_mix(zspec, kspec, o, n_seq, l // t, blk, yspec, ct=ct)
            blk += n_seq * (l // t)
        last = o == HY_ORDER - 1
        z = _hyena_inverse(yspec, inv, z, z_col, u, o, bias[o:o + 1], BF16 if last else F32, ct)
        z_col = 0
    return z


def kernel(x_prompt, x_sample, ln_in_g, ln_in_b, w_in, hy_conv_w, hy_conv_b, hy_pos_w1, hy_pos_b1, hy_pos_w2, hy_pos_b2, hy_pos_w3, hy_pos_b3, hy_sin_freq, hy_pos_wout, hy_bias, gm_ln_g, gm_ln_b, gm_ws, gm_bs, hg_lb_raw, hg_norm_g, w_out, ln1_g, ln1_b, ln2_g, ln2_b, ffn_w1, ffn_w3, ffn_w2, moe_router_w, moe_router_b, moe_w1, moe_w3, moe_w2):
    depth, d_model, in_w = w_in.shape
    hy_w = hy_bias.shape[-1]
    gm_w = gm_ln_g.shape[-1]
    hg_w = hg_norm_g.shape[-1]
    o1 = 3 * hy_w
    o2 = o1 + 2 * gm_w
    alpha = (2 * depth) ** 0.25
    groups = [(x_prompt.shape[0], x_prompt.shape[1]), (x_sample.shape[0], x_sample.shape[1])]
    seq_lens = [l for n_seq, l in groups for _ in range(n_seq)]
    n_prompt = x_prompt.shape[0] * x_prompt.shape[1]
    hg_off = o1 // hg_w
    hg_cols = {"q": hg_off, "zf": hg_off + 1, "zb": hg_off + 2, "i": hg_off + 3, "og": hg_off + 4}
    gm_off = (o1 + 5 * hg_w) // gm_w

    lb_all = jnp.cumsum(jax.nn.softmax(hg_lb_raw.astype(F32), axis=0), axis=0)
    lb_all = lb_all - lb_all[:1]

    x, xb = _layernorm_pair(x_prompt.reshape(-1, d_model), x_sample.reshape(-1, d_model), ln_in_g, ln_in_b)
    for l in range(depth):
        w_l = jnp.concatenate([w_in[l][:, :o1], w_in[l][:, o2:], w_in[l][:, o1:o2]], 1).astype(BF16)
        p = _matmul(xb, w_l)
        filt_w = (hy_pos_w1[l], hy_pos_b1[l], hy_pos_w2[l], hy_pos_b2[l], hy_pos_w3[l], hy_pos_b3[l],
                  hy_sin_freq[l], hy_pos_wout[l])
        y_hy = _hyena(p, hy_w, groups, hy_conv_w[l], hy_conv_b[l], filt_w, hy_bias[l])
        y_hg = _hgrn2(p, hg_cols, hg_w, lb_all[l], hg_norm_g[l], seq_lens)
        y_gm = _gmlp(p, gm_off, gm_off + 1, gm_w, gm_ln_g[l], gm_ln_b[l], gm_ws[l], gm_bs[l])
        wo = w_out[l].astype(BF16)
        x, xb = _mix_out_residual_ln(
            [y_hy, y_gm, y_hg], [wo[:hy_w], wo[hy_w:hy_w + gm_w], wo[hy_w + gm_w:]], x, ln1_g[l], ln1_b[l], alpha)
        j = l // 2
        if l % 2 == 0:
            x, xb = _ffn_residual_ln(xb, ffn_w1[j].astype(BF16), ffn_w3[j].astype(BF16), ffn_w2[j].astype(BF16),
                                     x, ln2_g[l], ln2_b[l], alpha)
        else:
            ff = _moe(x, xb, moe_router_w[j], moe_router_b[j], moe_w1[j].astype(BF16), moe_w3[j].astype(BF16),
                      moe_w2[j].astype(BF16))
            if l == depth - 1:
                y_prompt, y_sample = _residual_layernorm_split(x, ff, ln2_g[l], ln2_b[l], alpha, n_prompt)
                return (y_prompt.reshape(x_prompt.shape), y_sample.reshape(x_sample.shape))
            x, xb = _residual_layernorm_split(x, ff, ln2_g[l], ln2_b[l], alpha, n_prompt)
            x = jnp.concatenate([x, xb], 0)
            xb = x.astype(BF16)
    return (x[:n_prompt].reshape(x_prompt.shape), x[n_prompt:].reshape(x_sample.shape))
```

```python
import functools
import math

import numpy as np
import jax
import jax.numpy as jnp
from jax import lax
from jax.experimental import pallas as pl
from jax.experimental.pallas import tpu as pltpu

HEAD_DIM = 128
HY_ORDER = 2
HY_POS_BANDS = 16
HY_FAST_DECAY = 0.3
HY_SLOW_DECAY = 1.5
HY_DECAY_TARGET = 1e-2
GM_CHUNK = 128
N_EXPERTS = 8
TOP_K = 2
LN_EPS = 1e-5
RMS_EPS = 1e-6

HY_BLOCK = 2048
HG_CHUNK = 128
LANES = 128

V7X_VMEM_BYTES = 64 * 1024 * 1024
VMEM_LIMIT = V7X_VMEM_BYTES - 8 * 1024 * 1024

F32 = jnp.float32
BF16 = jnp.bfloat16
NT_DIMS = (((1,), (1,)), ((), ()))
TN_DIMS = (((0,), (0,)), ((), ()))


def _params(*sem):
    return pltpu.CompilerParams(dimension_semantics=sem, vmem_limit_bytes=VMEM_LIMIT)


def _ln_rows(x, g, b):
    mu = jnp.mean(x, -1, keepdims=True)
    xc = x - mu
    var = jnp.mean(xc * xc, -1, keepdims=True)
    return xc * lax.rsqrt(var + LN_EPS) * g + b


def _split_bf16(x):
    hi = x.astype(BF16)
    lo = (x - hi.astype(F32)).astype(BF16)
    return hi, lo


def _dot3(a, b):
    ah, al = _split_bf16(a)
    bh, bl = _split_bf16(b)
    return (jnp.dot(ah, bh, preferred_element_type=F32) + jnp.dot(ah, bl, preferred_element_type=F32)
            + jnp.dot(al, bh, preferred_element_type=F32))


def _ln2_kernel(na_blocks, xa_ref, xb_ref, g_ref, b_ref, of_ref, ob_ref):
    def emit(x_ref):
        y = _ln_rows(x_ref[...], g_ref[...], b_ref[...])
        of_ref[...] = y
        ob_ref[...] = y.astype(BF16)

    first = pl.program_id(0) < na_blocks
    pl.when(first)(lambda: emit(xa_ref))
    pl.when(jnp.logical_not(first))(lambda: emit(xb_ref))


def _layernorm_pair(xa, xb, g, b, bm=512):
    d = xa.shape[1]
    na, nb = xa.shape[0] // bm, xb.shape[0] // bm
    row = pl.BlockSpec((bm, d), lambda i: (i, 0))
    vec = pl.BlockSpec((1, d), lambda i: (0, 0))
    m = (na + nb) * bm
    return pl.pallas_call(
        functools.partial(_ln2_kernel, na), grid=(na + nb,),
        in_specs=[pl.BlockSpec((bm, d), lambda i: (jnp.minimum(i, na - 1), 0)),
                  pl.BlockSpec((bm, d), lambda i: (jnp.maximum(i - na, 0), 0)), vec, vec],
        out_specs=[row, row],
        out_shape=[jax.ShapeDtypeStruct((m, d), F32), jax.ShapeDtypeStruct((m, d), BF16)],
        compiler_params=_params("arbitrary"), name="layernorm",
    )(xa, xb, g.reshape(1, d), b.reshape(1, d))


def _combine_ln_kernel(alpha, na_blocks, x_ref, y0_ref, y1_ref, gate_ref, g_ref, b_ref, oa_ref, ob_ref):
    gate = gate_ref[...]
    ff = y0_ref[...].astype(F32) * gate[:, 0:1] + y1_ref[...].astype(F32) * gate[:, 1:2]
    y = _ln_rows(alpha * x_ref[...] + ff, g_ref[...], b_ref[...])
    first = pl.program_id(0) < na_blocks

    @pl.when(first)
    def _():
        oa_ref[...] = y

    @pl.when(jnp.logical_not(first))
    def _():
        ob_ref[...] = y


def _combine_residual_ln_split(x, y0, y1, gate, g, b, alpha, n_first, bm=512):
    m, d = x.shape
    na = n_first // bm
    row = pl.BlockSpec((bm, d), lambda i: (i, 0))
    vec = pl.BlockSpec((1, d), lambda i: (0, 0))
    return pl.pallas_call(
        functools.partial(_combine_ln_kernel, alpha, na), grid=(m // bm,),
        in_specs=[row, row, row, pl.BlockSpec((bm, gate.shape[1]), lambda i: (i, 0)), vec, vec],
        out_specs=[pl.BlockSpec((bm, d), lambda i: (jnp.minimum(i, na - 1), 0)),
                   pl.BlockSpec((bm, d), lambda i: (jnp.maximum(i - na, 0), 0))],
        out_shape=[jax.ShapeDtypeStruct((n_first, d), F32), jax.ShapeDtypeStruct((m - n_first, d), F32)],
        compiler_params=_params("arbitrary"), name="moe_combine_residual_ln",
    )(x, y0, y1, gate, g.reshape(1, d), b.reshape(1, d))


def _mm_kernel(a_ref, w_ref, o_ref):
    o_ref[...] = jnp.dot(a_ref[...], w_ref[...], preferred_element_type=F32).astype(o_ref.dtype)


def _matmul(a, w, out_dtype=F32, bm=1024, bn=1024):
    m, k = a.shape
    n = w.shape[1]
    bm, bn = min(bm, m), min(bn, n)
    return pl.pallas_call(
        _mm_kernel, grid=(m // bm, n // bn),
        in_specs=[pl.BlockSpec((bm, k), lambda i, j: (i, 0)), pl.BlockSpec((k, bn), lambda i, j: (0, j))],
        out_specs=pl.BlockSpec((bm, bn), lambda i, j: (i, j)),
        out_shape=jax.ShapeDtypeStruct((m, n), out_dtype),
        compiler_params=_params("parallel", "arbitrary"), name="matmul",
    )(a, w)


def _mix_out_kernel(alpha, routed, a0_ref, a1_ref, a2_ref, w0_ref, w1_ref, w2_ref, x_ref, g_ref, b_ref, *refs):
    mix = (jnp.dot(a0_ref[...], w0_ref[...], preferred_element_type=F32)
           + jnp.dot(a1_ref[...], w1_ref[...], preferred_element_type=F32)
           + jnp.dot(a2_ref[...], w2_ref[...], preferred_element_type=F32))
    y = _ln_rows(alpha * x_ref[...] + mix, g_ref[...], b_ref[...])
    if routed:
        wr_ref, of_ref, ob_ref, lg_ref = refs
        lg_ref[...] = _dot3(y, wr_ref[...])
    else:
        of_ref, ob_ref = refs
    of_ref[...] = y
    ob_ref[...] = y.astype(BF16)


def _mix_out_residual_ln(parts, weights, x, g, b, alpha, router_w=None, bm=512):
    m, d = x.shape
    row = pl.BlockSpec((bm, d), lambda i: (i, 0))
    vec = pl.BlockSpec((1, d), lambda i: (0, 0))
    in_specs = [pl.BlockSpec((bm, a.shape[1]), lambda i: (i, 0)) for a in parts]
    in_specs += [pl.BlockSpec(w.shape, lambda i: (0, 0)) for w in weights] + [row, vec, vec]
    args = [*parts, *weights, x, g.reshape(1, d), b.reshape(1, d)]
    out_specs = [row, row]
    out_shape = [jax.ShapeDtypeStruct((m, d), F32), jax.ShapeDtypeStruct((m, d), BF16)]
    if router_w is not None:
        in_specs.append(pl.BlockSpec(router_w.shape, lambda i: (0, 0)))
        args.append(router_w)
        out_specs.append(pl.BlockSpec((bm, router_w.shape[1]), lambda i: (i, 0)))
        out_shape.append(jax.ShapeDtypeStruct((m, router_w.shape[1]), F32))
    return pl.pallas_call(
        functools.partial(_mix_out_kernel, alpha, router_w is not None), grid=(m // bm,),
        in_specs=in_specs, out_specs=out_specs, out_shape=out_shape,
        compiler_params=_params("parallel"), name="out_proj_residual_ln",
    )(*args)


def _swiglu_acc(a_ref, w1_ref, w3_ref, w2_ref, acc_ref):
    f = pl.program_id(1)

    @pl.when(f == 0)
    def _():
        acc_ref[...] = jnp.zeros_like(acc_ref)

    a = a_ref[...]
    h1 = jnp.dot(a, w1_ref[...], preferred_element_type=F32)
    h3 = jnp.dot(a, w3_ref[...], preferred_element_type=F32)
    gated = (h1 * jax.nn.sigmoid(h1) * h3).astype(BF16)
    acc_ref[...] += jnp.dot(gated, w2_ref[...], preferred_element_type=F32)


def _ffn_kernel(alpha, a_ref, w1_ref, w3_ref, w2_ref, x_ref, g_ref, b_ref, of_ref, ob_ref, acc_ref):
    _swiglu_acc(a_ref, w1_ref, w3_ref, w2_ref, acc_ref)

    @pl.when(pl.program_id(1) == pl.num_programs(1) - 1)
    def _():
        y = _ln_rows(alpha * x_ref[...] + acc_ref[...], g_ref[...], b_ref[...])
        of_ref[...] = y
        ob_ref[...] = y.astype(BF16)


def _ffn_residual_ln(a, w1, w3, w2, x, g, b, alpha, bm=512, bf=512):
    m, d = a.shape
    dff = w1.shape[1]
    row = pl.BlockSpec((bm, d), lambda i, f: (i, 0))
    vec = pl.BlockSpec((1, d), lambda i, f: (0, 0))
    up = pl.BlockSpec((d, bf), lambda i, f: (0, f))
    return pl.pallas_call(
        functools.partial(_ffn_kernel, alpha), grid=(m // bm, dff // bf),
        in_specs=[row, up, up, pl.BlockSpec((bf, d), lambda i, f: (f, 0)), row, vec, vec],
        out_specs=[row, row],
        out_shape=[jax.ShapeDtypeStruct((m, d), F32), jax.ShapeDtypeStruct((m, d), BF16)],
        scratch_shapes=[pltpu.VMEM((bm, d), F32)],
        compiler_params=_params("parallel", "arbitrary"), name="ffn_residual_ln",
    )(a, w1, w3, w2, x, g.reshape(1, d), b.reshape(1, d))


def _moe_ffn_kernel(blk_e_ref, used_ref, a_ref, w1_ref, w3_ref, w2_ref, o_ref, acc_ref):
    del blk_e_ref
    f = pl.program_id(1)

    @pl.when(pl.program_id(0) < used_ref[0])
    def _():
        @pl.when(f == 0)
        def _():
            acc_ref[...] = jnp.zeros_like(acc_ref)

        a = a_ref[...]
        h1 = jnp.dot(a, w1_ref[...], preferred_element_type=F32)
        h3 = jnp.dot(a, w3_ref[...], preferred_element_type=F32)
        gated = (h1 * jax.nn.sigmoid(h1) * h3).astype(BF16)
        acc_ref[...] += jnp.dot(gated, w2_ref[...], preferred_element_type=F32)

        @pl.when(f == pl.num_programs(1) - 1)
        def _():
            o_ref[...] = acc_ref[...].astype(o_ref.dtype)


def _cast_kernel(x_ref, o_ref):
    o_ref[...] = x_ref[...].astype(o_ref.dtype)


def _cast_bf16(w, rows=512):
    e, r, c = w.shape
    rows = min(rows, r)
    blk = pl.BlockSpec((None, rows, c), lambda i, j: (i, j, 0))
    return pl.pallas_call(
        _cast_kernel, grid=(e, r // rows), in_specs=[blk], out_specs=blk,
        out_shape=jax.ShapeDtypeStruct(w.shape, BF16),
        compiler_params=_params("parallel", "parallel"), name="cast_bf16",
    )(w)


def _moe_grouped_ffn(blk_e, n_used, xb, w1, w3, w2, bm, bf=512):
    cap, d = xb.shape
    dff = w1.shape[2]
    nf = dff // bf

    def blk(i, u):
        return jnp.minimum(i, u[0] - 1)

    def ftile(i, f, u):
        return jnp.where(i < u[0], f, nf - 1)

    row = pl.BlockSpec((bm, d), lambda i, f, e, u: (blk(i, u), 0))
    up = pl.BlockSpec((None, d, bf), lambda i, f, e, u: (e[blk(i, u)], 0, ftile(i, f, u)))
    down = pl.BlockSpec((None, bf, d), lambda i, f, e, u: (e[blk(i, u)], ftile(i, f, u), 0))
    grid_spec = pltpu.PrefetchScalarGridSpec(
        num_scalar_prefetch=2, grid=(cap // bm, nf), in_specs=[row, up, up, down],
        out_specs=row, scratch_shapes=[pltpu.VMEM((bm, d), F32)])
    return pl.pallas_call(
        _moe_ffn_kernel, grid_spec=grid_spec, out_shape=jax.ShapeDtypeStruct((cap, d), BF16),
        compiler_params=_params("arbitrary", "arbitrary"), name="moe_grouped_ffn",
    )(blk_e, n_used, xb, w1, w3, w2)


def _moe_route(logits, bm):
    n = logits.shape[0]
    top_val, top_idx = lax.top_k(logits, TOP_K)
    gate = jax.nn.softmax(top_val, axis=-1)
    nk = n * TOP_K
    flat_e = top_idx.reshape(nk).astype(jnp.int32)
    order = jnp.argsort(flat_e).astype(jnp.int32)
    rank = jnp.argsort(order).astype(jnp.int32)
    counts = jnp.sum((flat_e[:, None] == jnp.arange(N_EXPERTS, dtype=jnp.int32)[None, :]).astype(jnp.int32), 0)
    padded = (counts + bm - 1) // bm * bm
    start = jnp.cumsum(counts) - counts
    pend = jnp.cumsum(padded)
    pstart = pend - padded
    dest = (pstart[flat_e] + rank - start[flat_e]).reshape(n, TOP_K)
    n_blocks = nk // bm + N_EXPERTS
    blk_e = jnp.minimum(jnp.searchsorted(pend, jnp.arange(n_blocks, dtype=jnp.int32) * bm, side='right'),
                        N_EXPERTS - 1).astype(jnp.int32)
    slot = jnp.arange(n_blocks * bm, dtype=jnp.int32)
    slot_e = blk_e[slot // bm]
    off = slot - pstart[slot_e]
    valid = off < counts[slot_e]
    src = jnp.clip(start[slot_e] + off, 0, nk - 1)
    slot_tok = jnp.where(valid, order[src] // TOP_K, n).astype(jnp.int32)
    n_used = (pend[-1] // bm).astype(jnp.int32).reshape(1)
    return gate, dest, slot_tok, blk_e, n_used


def _moe_dispatch_ffn(logits, h_bf16, w1, w3, w2, bm=1024):
    n, d = h_bf16.shape
    gate, dest, slot_tok, blk_e, n_used = _moe_route(logits, bm)
    xb = jnp.concatenate([h_bf16, jnp.zeros((1, d), BF16)], 0)[slot_tok]
    yb = _moe_grouped_ffn(blk_e, n_used, xb, w1, w3, w2, bm)
    return yb[dest[:, 0]], yb[dest[:, 1]], gate


def _gmlp_kernel(groups, pu_ref, pv_ref, g_ref, b_ref, ws_ref, bs_ref, o_ref):
    u = jax.nn.gelu(pu_ref[...])
    v = _ln_rows(jax.nn.gelu(pv_ref[...]), g_ref[...], b_ref[...])
    for n in range(u.shape[0] // GM_CHUNK):
        rows = slice(n * GM_CHUNK, (n + 1) * GM_CHUNK)
        for grp in range(groups):
            cols = slice(grp * HEAD_DIM, (grp + 1) * HEAD_DIM)
            s = _dot3(ws_ref[grp], v[rows, cols]) + bs_ref[grp]
            o_ref[rows, cols] = (u[rows, cols] * s).astype(o_ref.dtype)


def _gmlp(p, col_u, col_v, gm_w, ln_g, ln_b, ws, bs, bt=512):
    n = p.shape[0]
    groups = ws.shape[0]
    bsb = jnp.broadcast_to(bs[:, :, None], (groups, GM_CHUNK, HEAD_DIM)).astype(F32)
    vec = pl.BlockSpec((1, gm_w), lambda i: (0, 0))
    full3 = pl.BlockSpec((groups, GM_CHUNK, HEAD_DIM), lambda i: (0, 0, 0))
    return pl.pallas_call(
        functools.partial(_gmlp_kernel, groups), grid=(n // bt,),
        in_specs=[pl.BlockSpec((bt, gm_w), lambda i: (i, col_u)), pl.BlockSpec((bt, gm_w), lambda i: (i, col_v)),
                  vec, vec, pl.BlockSpec((groups, GM_CHUNK, GM_CHUNK), lambda i: (0, 0, 0)), full3],
        out_specs=pl.BlockSpec((bt, gm_w), lambda i: (i, 0)),
        out_shape=jax.ShapeDtypeStruct((n, gm_w), BF16),
        compiler_params=_params("parallel"), name="gmlp",
    )(p, p, ln_g.reshape(1, gm_w), ln_b.reshape(1, gm_w), ws.astype(F32), bsb)


def _hgrn2_constants(c, reverse):
    n_lvl = int(math.log2(c))
    t = np.arange(c)
    ms, ws = [], []
    for lvl in range(n_lvl):
        m = c >> (lvl + 1)
        mid = (t // (2 * m)) * (2 * m) + m
        upper = t >= mid
        mat = np.zeros((c, c), np.float32)
        for r in range(c):
            if upper[r]:
                mat[r, mid[r]:r + 1] = 1.0
            else:
                mat[r, r + 1:mid[r]] = 1.0
        same = (t[:, None] // (2 * m)) == (t[None, :] // (2 * m))
        ws.append((same & upper[:, None] & ~upper[None, :]).astype(np.float32))
        ms.append(mat)
    ws.append(np.eye(c, dtype=np.float32))
    ms.append(np.tril(np.ones((c, c), np.float32)))
    ms.append(np.triu(np.ones((c, c), np.float32), 1))
    if reverse:
        ms = [a[::-1, ::-1] for a in ms]
        ws = [a[::-1, ::-1] for a in ws]
    return np.concatenate(ms, 0), np.stack(ws, 0)


def _hgrn2_kernel(heads, n_lvl, carry_row, final, reset_ref, *refs):
    if final:
        (q_ref, z_ref, i_ref, og_ref, oo_ref, loglb_ref, log1mlb_ref, onemlb_ref, ng_ref, m_ref, w_ref,
         out_ref, st_ref, e_ref) = refs
    else:
        (q_ref, z_ref, i_ref, loglb_ref, log1mlb_ref, onemlb_ref, m_ref, w_ref, out_ref, st_ref, e_ref) = refs
    n_streams, c = q_ref.shape[0], q_ref.shape[1]
    rows_e = m_ref.shape[0]
    step = pl.program_id(0)

    for s in range(n_streams):
        @pl.when(reset_ref[step * n_streams + s] == 1)
        def _():
            st_ref[s * heads:(s + 1) * heads] = jnp.zeros((heads, HEAD_DIM, HEAD_DIM), F32)

    for s in range(n_streams):
        z = z_ref[s]
        e = jnp.exp(-jnp.abs(z))
        r = 1.0 / (1.0 + e)
        log_sig = jnp.minimum(z, 0.0) - jnp.log(1.0 + e)
        sig_neg = jnp.where(z >= 0.0, e * r, r)
        a = loglb_ref[...]
        b = log1mlb_ref[...] + log_sig
        g = jnp.maximum(a, b) + jnp.log(1.0 + jnp.exp(-jnp.abs(a - b)))
        k = onemlb_ref[...] * sig_neg
        q = q_ref[s]
        qs = q * jax.nn.sigmoid(q)
        v = i_ref[s].astype(BF16)

        g_hi, g_lo = _split_bf16(g)
        m_all = m_ref[...]
        e0 = s * rows_e
        e_ref[e0:e0 + rows_e, :] = (jnp.dot(m_all, g_hi, preferred_element_type=F32)
                                    + jnp.dot(m_all, g_lo, preferred_element_type=F32))

        for h in range(heads):
            cols = slice(h * HEAD_DIM, (h + 1) * HEAD_DIM)
            qh, kh, vh = qs[:, cols], k[:, cols], v[:, cols]
            amat = w_ref[n_lvl] * lax.dot_general(qh.astype(BF16), kh.astype(BF16), NT_DIMS,
                                                  preferred_element_type=F32)
            for lvl in range(n_lvl):
                ex = jnp.exp(e_ref[e0 + lvl * c:e0 + (lvl + 1) * c, cols])
                amat = amat + w_ref[lvl] * lax.dot_general((qh * ex).astype(BF16), (kh * ex).astype(BF16),
                                                           NT_DIMS, preferred_element_type=F32)
            e_in = e_ref[e0 + n_lvl * c:e0 + (n_lvl + 1) * c, cols]
            q_in = (qh * jnp.exp(e_in)).astype(BF16)
            k_st = (kh * jnp.exp(e_ref[e0 + (n_lvl + 1) * c:e0 + (n_lvl + 2) * c, cols])).astype(BF16)
            dec = jnp.exp(e_in[carry_row:carry_row + 1, :])
            st = st_ref[s * heads + h]
            o = (lax.dot_general(q_in, st.astype(BF16), NT_DIMS, preferred_element_type=F32)
                 + jnp.dot(amat.astype(BF16), vh, preferred_element_type=F32))
            st_ref[s * heads + h] = st * dec + lax.dot_general(vh, k_st, TN_DIMS, preferred_element_type=F32)
            if final:
                o = o + oo_ref[s, :, cols]
                o = o * lax.rsqrt(jnp.mean(o * o, -1, keepdims=True) + RMS_EPS)
                og = og_ref[s, :, cols]
                out_ref[s, :, cols] = (o * ng_ref[:, cols] * (og * jax.nn.sigmoid(og))).astype(out_ref.dtype)
            else:
                out_ref[s, :, cols] = o


def _hgrn2_pass(p3, cols, hg_w, z_col, reverse, reset, lbs, extra, c):
    n_streams, n = p3.shape[0], p3.shape[1]
    nc = n // c
    heads = hg_w // HEAD_DIM
    n_lvl = int(math.log2(c))
    m_np, w_np = _hgrn2_constants(c, reverse)
    final = extra is not None

    def cmap(i):
        return (nc - 1 - i) if reverse else i

    def tok(col):
        return pl.BlockSpec((n_streams, c, hg_w), lambda i, r: (0, cmap(i), col))

    vec = pl.BlockSpec((1, hg_w), lambda i, r: (0, 0))
    in_specs = [tok(cols["q"]), tok(z_col), tok(cols["i"])]
    args = [p3, p3, p3]
    if final:
        o_other, norm_g = extra
        in_specs += [tok(cols["og"]), tok(0)]
        args += [p3, o_other]
    in_specs += [vec, vec, vec]
    args += list(lbs)
    if final:
        in_specs += [vec]
        args += [norm_g.reshape(1, hg_w).astype(F32)]
    in_specs += [pl.BlockSpec(m_np.shape, lambda i, r: (0, 0)), pl.BlockSpec(w_np.shape, lambda i, r: (0, 0, 0))]
    args += [jnp.asarray(m_np, BF16), jnp.asarray(w_np, F32)]
    grid_spec = pltpu.PrefetchScalarGridSpec(
        num_scalar_prefetch=1, grid=(nc,), in_specs=in_specs, out_specs=tok(0),
        scratch_shapes=[pltpu.VMEM((n_streams * heads, HEAD_DIM, HEAD_DIM), F32),
                        pltpu.VMEM((n_streams * m_np.shape[0], hg_w), F32)])
    order = np.arange(nc)[::-1] if reverse else np.arange(nc)
    reset_steps = jnp.asarray(np.asarray(reset, np.int32)[order].reshape(-1))
    return pl.pallas_call(
        functools.partial(_hgrn2_kernel, heads, n_lvl, 0 if reverse else c - 1, final), grid_spec=grid_spec,
        out_shape=jax.ShapeDtypeStruct((n_streams, n, hg_w), BF16 if final else F32),
        compiler_params=_params("arbitrary"), name="hgrn2_fwd" if final else "hgrn2_bwd",
    )(reset_steps, *args)


def _hgrn2(p, cols, hg_w, lb, norm_g, seq_lens, n_streams=2, c=HG_CHUNK):
    lb = lb.astype(F32).reshape(1, hg_w)
    lbs = (jnp.log(lb), jnp.log1p(-lb), 1.0 - lb)
    starts = np.cumsum([0] + list(seq_lens))
    n = int(starts[-1])
    per = n // n_streams
    assert n == p.shape[0] and all(s * per in starts for s in range(n_streams)), "a sequence straddles a stream cut"
    first = np.zeros(n // c, np.int32)
    last = np.zeros(n // c, np.int32)
    first[starts[:-1] // c] = 1
    last[starts[1:] // c - 1] = 1
    first = first.reshape(n_streams, per // c).T
    last = last.reshape(n_streams, per // c).T
    p3 = p.reshape(n_streams, per, p.shape[1])
    o_b = _hgrn2_pass(p3, cols, hg_w, cols["zb"], True, last, lbs, None, c)
    return _hgrn2_pass(p3, cols, hg_w, cols["zf"], False, first, lbs, (o_b, norm_g), c).reshape(n, hg_w)


def _dft_matrices(t):
    n = 2 * t
    f = jnp.arange(t, dtype=jnp.int32)
    ang = (2.0 * math.pi / n) * ((f[:, None] * f[None, :]) % n).astype(F32)
    cos, sin = jnp.cos(ang), jnp.sin(ang)
    alt = (1 - 2 * (f % 2)).astype(F32)
    first = f == 0
    fwd = jnp.stack([cos, jnp.where(first[:, None], alt[None, :], -sin)], 0).astype(BF16)
    wgt = jnp.where(first, 1.0, 2.0)[None, :] / n
    inv = jnp.stack([cos * wgt, jnp.where(first[None, :], alt[:, None] / n, -sin * wgt)], 0).astype(BF16)
    return fwd, inv


def _shortconv_kernel(flags_ref, x_ref, prev_ref, next_ref, w_ref, b_ref, o_ref):
    i = pl.program_id(0)
    x = x_ref[...]
    t = x.shape[0]
    rows = lax.broadcasted_iota(jnp.int32, x.shape, 0)
    has_prev = flags_ref[2 * i].astype(F32)
    has_next = flags_ref[2 * i + 1].astype(F32)
    up = jnp.where(rows == 0, prev_ref[7:8, :] * has_prev, pltpu.roll(x, 1, 0))
    dn = jnp.where(rows == t - 1, next_ref[0:1, :] * has_next, pltpu.roll(x, t - 1, 0))
    o_ref[...] = up * w_ref[0:1, :] + x * w_ref[1:2, :] + dn * w_ref[2:3, :] + b_ref[...]


def _shortconv(p, width, conv_w, conv_b, blk_flags, t=HY_BLOCK, ct=768):
    n = p.shape[0]
    nb = n // t
    r8 = t // 8
    grid_spec = pltpu.PrefetchScalarGridSpec(
        num_scalar_prefetch=1, grid=(nb, width // ct),
        in_specs=[pl.BlockSpec((t, ct), lambda i, j, f: (i, j)),
                  pl.BlockSpec((8, ct), lambda i, j, f: (jnp.maximum(i * r8 - 1, 0), j)),
                  pl.BlockSpec((8, ct), lambda i, j, f: (jnp.minimum((i + 1) * r8, nb * r8 - 1), j)),
                  pl.BlockSpec((3, ct), lambda i, j, f: (0, j)), pl.BlockSpec((1, ct), lambda i, j, f: (0, j))],
        out_specs=pl.BlockSpec((t, ct), lambda i, j, f: (i, j)))
    return pl.pallas_call(
        _shortconv_kernel, grid_spec=grid_spec, out_shape=jax.ShapeDtypeStruct((n, width), F32),
        compiler_params=_params("parallel", "parallel"), name="hyena_shortconv",
    )(blk_flags, p, p, p, conv_w.astype(F32), conv_b.reshape(1, width).astype(F32))


def _filter_kernel(l_total, feats_ref, w1_ref, b1_ref, w2_ref, b2_ref, w3_ref, b3_ref, fr_ref, wo_ref, dl_ref,
                   h_ref, sum_ref):
    i = pl.program_id(0)
    bt = feats_ref.shape[0]
    fr = fr_ref[...]
    feats = feats_ref[...]
    h = jnp.sin(fr * (_dot3(feats, w1_ref[...]) + b1_ref[...]))
    h = jnp.sin(fr * (_dot3(h, w2_ref[...]) + b2_ref[...]))
    h = jnp.sin(fr * (_dot3(h, w3_ref[...]) + b3_ref[...]))
    out = _dot3(h, wo_ref[...])
    window = jnp.exp(-feats[:, 0:1] * dl_ref[...])
    reps = out.shape[1] // window.shape[1]
    out = out * jnp.concatenate([window] * reps, axis=1)
    row = lax.broadcasted_iota(jnp.int32, (bt, 1), 0) + i * bt
    out = jnp.where(row == l_total, 0.0, out)
    h_ref[...] = out

    @pl.when(i == 0)
    def _():
        sum_ref[...] = jnp.zeros_like(sum_ref)

    sum_ref[...] += jnp.sum(jnp.abs(out), axis=0, keepdims=True)


def _hyena_filter(l, w1, b1, w2, b2, w3, b3, freq, w_out, hy_w, bt=512):
    hid = w2.shape[0]
    pad = LANES - hid
    half = HY_ORDER * hy_w
    pos = jnp.arange(l, dtype=F32)
    pos = jnp.concatenate([pos, l - pos], 0)
    tt = pos / max(l - 1, 1)
    bands = jnp.linspace(1e-4, HY_POS_BANDS - 1, HY_POS_BANDS, dtype=F32)
    ang = (2.0 * math.pi / l) * pos[:, None] * bands[None, :]
    feats = jnp.concatenate([tt[:, None], jnp.cos(ang), -jnp.sin(ang)], -1)
    feats = jnp.pad(feats, ((0, 0), (0, LANES - feats.shape[1])))
    w1p = jnp.pad(w1.astype(F32), ((0, LANES - w1.shape[0]), (0, pad)))
    w2p = jnp.pad(w2.astype(F32), ((0, pad), (0, pad)))
    w3p = jnp.pad(w3.astype(F32), ((0, pad), (0, pad)))
    wo = w_out.astype(F32).reshape(hid, HY_ORDER, 2, hy_w).transpose(2, 0, 1, 3).reshape(2, hid, half)
    wop = jnp.pad(wo, ((0, 0), (0, pad), (0, 0)))

    def vecp(a):
        return jnp.pad(a.astype(F32), (0, pad)).reshape(1, LANES)

    deltas = jnp.abs(jnp.linspace(math.log(HY_DECAY_TARGET) / HY_SLOW_DECAY,
                                  math.log(HY_DECAY_TARGET) / HY_FAST_DECAY, hy_w, dtype=F32)).reshape(1, hy_w)
    bt = min(bt, l)
    nbt = l // bt
    sq = pl.BlockSpec((LANES, LANES), lambda i: (0, 0))
    vec = pl.BlockSpec((1, LANES), lambda i: (0, 0))
    return pl.pallas_call(
        functools.partial(_filter_kernel, l), grid=(2 * nbt,),
        in_specs=[pl.BlockSpec((bt, LANES), lambda i: (i, 0)), sq, vec, sq, vec, sq, vec, vec,
                  pl.BlockSpec((None, LANES, half), lambda i: (i // nbt, 0, 0)),
                  pl.BlockSpec((1, hy_w), lambda i: (0, 0))],
        out_specs=[pl.BlockSpec((bt, half), lambda i: (i, 0)), pl.BlockSpec((1, half), lambda i: (0, 0))],
        out_shape=[jax.ShapeDtypeStruct((2 * l, half), F32), jax.ShapeDtypeStruct((1, half), F32)],
        compiler_params=_params("arbitrary"), name="hyena_filter",
    )(feats, w1p, vecp(b1), w2p, vecp(b2), w3p, vecp(b3), vecp(freq), wop, deltas)


def _dft_kernel(f_ref, x_ref, o_ref):
    o_ref[...] = jnp.dot(f_ref[...], x_ref[...].astype(BF16), preferred_element_type=F32).astype(o_ref.dtype)


def _block_dft(fwd, x, col_off, width, out_dtype, t=HY_BLOCK, ct=256):
    nseg = x.shape[0] // t
    ct = min(ct, width)
    return pl.pallas_call(
        _dft_kernel, grid=(2, nseg, width // ct),
        in_specs=[pl.BlockSpec((None, t, t), lambda hf, s, j: (hf, 0, 0)),
                  pl.BlockSpec((t, ct), lambda hf, s, j: (s, j + col_off))],
        out_specs=pl.BlockSpec((None, None, t, ct), lambda hf, s, j: (s, hf, 0, j)),
        out_shape=jax.ShapeDtypeStruct((nseg, 2, t, width), out_dtype),
        compiler_params=_params("arbitrary", "arbitrary", "arbitrary"), name="hyena_block_dft",
    )(fwd, x)


def _spectrum_kernel(idx_ref, a_ref, b_ref, s0_ref, inv_ref, o_ref):
    del idx_ref
    t, ct = a_ref.shape[1], a_ref.shape[2]
    rows = lax.broadcasted_iota(jnp.int32, (t, ct), 0)
    sgn = (1 - 2 * (rows & 1)).astype(F32)
    s0 = s0_ref[...]
    inv = inv_ref[...]
    o_ref[0] = ((a_ref[0] + sgn * (b_ref[0] - s0)) * inv).astype(o_ref.dtype)
    o_ref[1] = ((a_ref[1] + sgn * (b_ref[1] - jnp.where(rows == 0, s0, 0.0))) * inv).astype(o_ref.dtype)


def _filter_spectra(seg_spec, idx, seg0, inv_norm, ct=256):
    nslot = seg0.shape[0]
    t, width = seg_spec.shape[2], seg_spec.shape[3]
    grid_spec = pltpu.PrefetchScalarGridSpec(
        num_scalar_prefetch=1, grid=(nslot, width // ct),
        in_specs=[pl.BlockSpec((None, 2, t, ct), lambda s, j, ix: (ix[2 * s], 0, 0, j)),
                  pl.BlockSpec((None, 2, t, ct), lambda s, j, ix: (ix[2 * s + 1], 0, 0, j)),
                  pl.BlockSpec((None, 1, ct), lambda s, j, ix: (s, 0, j)),
                  pl.BlockSpec((1, ct), lambda s, j, ix: (0, j))],
        out_specs=pl.BlockSpec((None, 2, t, ct), lambda s, j, ix: (s, 0, 0, j)))
    return pl.pallas_call(
        _spectrum_kernel, grid_spec=grid_spec, out_shape=jax.ShapeDtypeStruct((nslot, 2, t, width), BF16),
        compiler_params=_params("parallel", "arbitrary"), name="hyena_filter_spectra",
    )(idx, seg_spec, seg_spec, seg0, inv_norm)


def _hyena_filter_spectra(l, fwd, filt_w, hy_w, t=HY_BLOCK):
    nb = l // t
    taps, abs_sum = _hyena_filter(l, *filt_w, hy_w)
    half = taps.shape[1]
    seg_spec = _block_dft(fwd, taps, 0, half, F32, t)

    def seg_index(d):
        return d if d >= 0 else 2 * nb + d

    idx = []
    for d in range(-(nb - 1), nb):
        idx += [seg_index(d), seg_index(d - 1)]
    first_rows = taps.reshape(2 * nb, t, half)[:, 0, :]
    seg0 = jnp.stack([first_rows[i] for i in idx[1::2]], 0).reshape(2 * nb - 1, 1, half)
    return _filter_spectra(seg_spec, jnp.asarray(idx, jnp.int32), seg0, 1.0 / abs_sum)


def _mix_kernel(nb, *refs):
    if len(refs) == 5:
        z_ref, k_ref, _, o_ref, acc_ref = refs
    else:
        z_ref, k_ref, o_ref, acc_ref = refs
    dl = pl.program_id(3)
    d = dl - (nb - 1)

    @pl.when(dl == 0)
    def _():
        acc_ref[...] = jnp.zeros_like(acc_ref)

    kt, kb = k_ref[0].astype(F32), k_ref[1].astype(F32)
    rows = lax.broadcasted_iota(jnp.int32, kt.shape, 0)
    row0 = jnp.logical_and(rows == 0, pl.program_id(1) == 0)
    for j in range(nb):
        i = j + d

        @pl.when(jnp.logical_and(i >= 0, i < nb))
        def _():
            zt, zb = z_ref[j, 0].astype(F32), z_ref[j, 1].astype(F32)
            bb = zb * kb
            acc_ref[i, 0] += zt * kt - jnp.where(row0, 0.0, bb)
            acc_ref[i, 1] += jnp.where(row0, bb, zt * kb + zb * kt)

    @pl.when(dl == 2 * nb - 2)
    def _():
        o_ref[...] = acc_ref[...].astype(o_ref.dtype)


def _hyena_mix(zspec, kspec, k_col, n_seq, nb, blk_off, prev, ft=512, ct=256):
    nblk, _, t, width = zspec.shape
    ft = min(ft, t)
    nct = width // ct
    assert blk_off % nb == 0
    s_off = blk_off // nb
    zblk = pl.BlockSpec((nb, 2, ft, ct), lambda s, fi, j, dl: (s + s_off, 0, fi, j))
    in_specs = [zblk, pl.BlockSpec((None, 2, ft, ct), lambda s, fi, j, dl: (dl, 0, fi, j + k_col * nct))]
    args = [zspec, kspec]
    aliases = {}
    if prev is not None:
        in_specs.append(pl.BlockSpec(memory_space=pl.ANY))
        args.append(prev)
        aliases = {2: 0}
    return pl.pallas_call(
        functools.partial(_mix_kernel, nb), grid=(n_seq, t // ft, nct, 2 * nb - 1),
        in_specs=in_specs, out_specs=zblk,
        out_shape=jax.ShapeDtypeStruct(zspec.shape, BF16),
        scratch_shapes=[pltpu.VMEM((nb, 2, ft, ct), F32)], input_output_aliases=aliases,
        compiler_params=_params("parallel", "parallel", "parallel", "arbitrary"), name="hyena_mix",
    )(*args)


def _inverse_kernel(y_ref, fi_ref, zin_ref, gate_ref, bias_ref, o_ref):
    y = (jnp.dot(fi_ref[0], y_ref[0], preferred_element_type=F32)
         + jnp.dot(fi_ref[1], y_ref[1], preferred_element_type=F32))
    o_ref[...] = (gate_ref[...] * (y + zin_ref[...] * bias_ref[...])).astype(o_ref.dtype)


def _hyena_inverse(yspec, inv, zin, zin_col, gate, gate_col, bias, out_dtype, ct=256):
    nblk, _, t, width = yspec.shape
    nct = width // ct
    return pl.pallas_call(
        _inverse_kernel, grid=(nblk, nct),
        in_specs=[pl.BlockSpec((None, 2, t, ct), lambda i, j: (i, 0, 0, j)),
                  pl.BlockSpec((2, t, t), lambda i, j: (0, 0, 0), pipeline_mode=pl.Buffered(1)),
                  pl.BlockSpec((t, ct), lambda i, j: (i, j + zin_col * nct)),
                  pl.BlockSpec((t, ct), lambda i, j: (i, j + gate_col * nct)),
                  pl.BlockSpec((1, ct), lambda i, j: (0, j))],
        out_specs=pl.BlockSpec((t, ct), lambda i, j: (i, j)),
        out_shape=jax.ShapeDtypeStruct((nblk * t, width), out_dtype),
        compiler_params=_params("parallel", "arbitrary"), name="hyena_inverse",
    )(yspec, inv, zin, gate, bias)


def _hyena(p, hy_w, groups, conv_w, conv_b, filt_w, bias, t=HY_BLOCK):
    flags = []
    for n_seq, l in groups:
        nb = l // t
        flags += [int(bi > 0) if side == 0 else int(bi < nb - 1)
                  for _ in range(n_seq) for bi in range(nb) for side in range(2)]
    fwd, inv = _dft_matrices(t)
    u = _shortconv(p, 3 * hy_w, conv_w, conv_b, jnp.asarray(np.asarray(flags, np.int32)), t, ct=hy_w)
    kspecs = [_hyena_filter_spectra(l, fwd, filt_w, hy_w, t) for _, l in groups]
    bias = bias.astype(F32)
    ct = 256
    z, z_col = u, 2
    for o in range(HY_ORDER):
        zspec = _block_dft(fwd, z, z_col * (hy_w // ct), hy_w, BF16, t, ct)
        yspec, blk = None, 0
        for (n_seq, l), kspec in zip(groups, kspecs):
            yspec = _hyena_mix(zspec, kspec, o, n_seq, l // t, blk, yspec, ct=ct)
            blk += n_seq * (l // t)
        last = o == HY_ORDER - 1
        z = _hyena_inverse(yspec, inv, z, z_col, u, o, bias[o:o + 1], BF16 if last else F32, ct)
        z_col = 0
    return z


def kernel(x_prompt, x_sample, ln_in_g, ln_in_b, w_in, hy_conv_w, hy_conv_b, hy_pos_w1, hy_pos_b1, hy_pos_w2, hy_pos_b2, hy_pos_w3, hy_pos_b3, hy_sin_freq, hy_pos_wout, hy_bias, gm_ln_g, gm_ln_b, gm_ws, gm_bs, hg_lb_raw, hg_norm_g, w_out, ln1_g, ln1_b, ln2_g, ln2_b, ffn_w1, ffn_w3, ffn_w2, moe_router_w, moe_router_b, moe_w1, moe_w3, moe_w2):
    depth, d_model, in_w = w_in.shape
    hy_w = hy_bias.shape[-1]
    gm_w = gm_ln_g.shape[-1]
    hg_w = hg_norm_g.shape[-1]
    o1 = 3 * hy_w
    o2 = o1 + 2 * gm_w
    alpha = (2 * depth) ** 0.25
    groups = [(x_prompt.shape[0], x_prompt.shape[1]), (x_sample.shape[0], x_sample.shape[1])]
    seq_lens = [l for n_seq, l in groups for _ in range(n_seq)]
    n_prompt = x_prompt.shape[0] * x_prompt.shape[1]
    hg_off = o1 // hg_w
    hg_cols = {"q": hg_off, "zf": hg_off + 1, "zb": hg_off + 2, "i": hg_off + 3, "og": hg_off + 4}
    gm_off = (o1 + 5 * hg_w) // gm_w

    lb_all = jnp.cumsum(jax.nn.softmax(hg_lb_raw.astype(F32), axis=0), axis=0)
    lb_all = lb_all - lb_all[:1]

    x, xb = _layernorm_pair(x_prompt.reshape(-1, d_model), x_sample.reshape(-1, d_model), ln_in_g, ln_in_b)
    for l in range(depth):
        w_l = jnp.concatenate([w_in[l][:, :o1], w_in[l][:, o2:], w_in[l][:, o1:o2]], 1).astype(BF16)
        p = _matmul(xb, w_l)
        filt_w = (hy_pos_w1[l], hy_pos_b1[l], hy_pos_w2[l], hy_pos_b2[l], hy_pos_w3[l], hy_pos_b3[l],
                  hy_sin_freq[l], hy_pos_wout[l])
        y_hy = _hyena(p, hy_w, groups, hy_conv_w[l], hy_conv_b[l], filt_w, hy_bias[l])
        y_hg = _hgrn2(p, hg_cols, hg_w, lb_all[l], hg_norm_g[l], seq_lens)
        y_gm = _gmlp(p, gm_off, gm_off + 1, gm_w, gm_ln_g[l], gm_ln_b[l], gm_ws[l], gm_bs[l])
        wo = w_out[l].astype(BF16)
        parts = [y_hy, y_gm, y_hg]
        wo_parts = [wo[:hy_w], wo[hy_w:hy_w + gm_w], wo[hy_w + gm_w:]]
        j = l // 2
        if l % 2 == 0:
            x, xb = _mix_out_residual_ln(parts, wo_parts, x, ln1_g[l], ln1_b[l], alpha)
            x, xb = _ffn_residual_ln(xb, ffn_w1[j].astype(BF16), ffn_w3[j].astype(BF16), ffn_w2[j].astype(BF16),
                                     x, ln2_g[l], ln2_b[l], alpha)
        else:
            n_exp = moe_router_w.shape[-1]
            wr = jnp.pad(moe_router_w[j].astype(F32), ((0, 0), (0, LANES - n_exp)))
            x, xb, logits = _mix_out_residual_ln(parts, wo_parts, x, ln1_g[l], ln1_b[l], alpha, wr)
            logits = logits[:, :n_exp] + moe_router_b[j].astype(F32)
            y0, y1, gate = _moe_dispatch_ffn(logits, xb, _cast_bf16(moe_w1[j]), _cast_bf16(moe_w3[j]),
                                             _cast_bf16(moe_w2[j]))
            xa, xs = _combine_residual_ln_split(x, y0, y1, gate, ln2_g[l], ln2_b[l], alpha, n_prompt)
            if l == depth - 1:
                return (xa.reshape(x_prompt.shape), xs.reshape(x_sample.shape))
            x = jnp.concatenate([xa, xs], 0)
            xb = x.astype(BF16)
    return (x[:n_prompt].reshape(x_prompt.shape), x[n_prompt:].reshape(x_sample.shape))
```

```python
import functools
import math

import numpy as np
import jax
import jax.numpy as jnp
from jax import lax
from jax.experimental import pallas as pl
from jax.experimental.pallas import tpu as pltpu

HEAD_DIM = 128
HY_ORDER = 2
HY_POS_BANDS = 16
HY_FAST_DECAY = 0.3
HY_SLOW_DECAY = 1.5
HY_DECAY_TARGET = 1e-2
GM_CHUNK = 128
N_EXPERTS = 8
TOP_K = 2
LN_EPS = 1e-5
RMS_EPS = 1e-6

HY_BLOCK = 1024
HG_CHUNK = 128
LANES = 128

V7X_VMEM_BYTES = 64 * 1024 * 1024
VMEM_LIMIT = V7X_VMEM_BYTES - 8 * 1024 * 1024

F32 = jnp.float32
BF16 = jnp.bfloat16
NT_DIMS = (((1,), (1,)), ((), ()))
TN_DIMS = (((0,), (0,)), ((), ()))


def _params(*sem):
    return pltpu.CompilerParams(dimension_semantics=sem, vmem_limit_bytes=VMEM_LIMIT)


def _ln_rows(x, g, b):
    mu = jnp.mean(x, -1, keepdims=True)
    xc = x - mu
    var = jnp.mean(xc * xc, -1, keepdims=True)
    return xc * lax.rsqrt(var + LN_EPS) * g + b


def _split_bf16(x):
    hi = x.astype(BF16)
    lo = (x - hi.astype(F32)).astype(BF16)
    return hi, lo


def _dot3(a, b):
    ah, al = _split_bf16(a)
    bh, bl = _split_bf16(b)
    return (jnp.dot(ah, bh, preferred_element_type=F32) + jnp.dot(ah, bl, preferred_element_type=F32)
            + jnp.dot(al, bh, preferred_element_type=F32))


def _ln2_kernel(na_blocks, xa_ref, xb_ref, g_ref, b_ref, of_ref, ob_ref):
    def emit(x_ref):
        y = _ln_rows(x_ref[...], g_ref[...], b_ref[...])
        of_ref[...] = y
        ob_ref[...] = y.astype(BF16)

    first = pl.program_id(0) < na_blocks
    pl.when(first)(lambda: emit(xa_ref))
    pl.when(jnp.logical_not(first))(lambda: emit(xb_ref))


def _layernorm_pair(xa, xb, g, b, bm=512):
    d = xa.shape[1]
    na, nb = xa.shape[0] // bm, xb.shape[0] // bm
    row = pl.BlockSpec((bm, d), lambda i: (i, 0))
    vec = pl.BlockSpec((1, d), lambda i: (0, 0))
    m = (na + nb) * bm
    return pl.pallas_call(
        functools.partial(_ln2_kernel, na), grid=(na + nb,),
        in_specs=[pl.BlockSpec((bm, d), lambda i: (jnp.minimum(i, na - 1), 0)),
                  pl.BlockSpec((bm, d), lambda i: (jnp.maximum(i - na, 0), 0)), vec, vec],
        out_specs=[row, row],
        out_shape=[jax.ShapeDtypeStruct((m, d), F32), jax.ShapeDtypeStruct((m, d), BF16)],
        compiler_params=_params("arbitrary"), name="layernorm",
    )(xa, xb, g.reshape(1, d), b.reshape(1, d))


def _combine_ln_kernel(alpha, na_blocks, x_ref, y0_ref, y1_ref, gate_ref, g_ref, b_ref, oa_ref, ob_ref):
    gate = gate_ref[...]
    ff = y0_ref[...].astype(F32) * gate[:, 0:1] + y1_ref[...].astype(F32) * gate[:, 1:2]
    y = _ln_rows(alpha * x_ref[...] + ff, g_ref[...], b_ref[...])
    first = pl.program_id(0) < na_blocks

    @pl.when(first)
    def _():
        oa_ref[...] = y

    @pl.when(jnp.logical_not(first))
    def _():
        ob_ref[...] = y


def _combine_residual_ln_split(x, y0, y1, gate, g, b, alpha, n_first, bm=512):
    m, d = x.shape
    na = n_first // bm
    row = pl.BlockSpec((bm, d), lambda i: (i, 0))
    vec = pl.BlockSpec((1, d), lambda i: (0, 0))
    return pl.pallas_call(
        functools.partial(_combine_ln_kernel, alpha, na), grid=(m // bm,),
        in_specs=[row, row, row, pl.BlockSpec((bm, gate.shape[1]), lambda i: (i, 0)), vec, vec],
        out_specs=[pl.BlockSpec((bm, d), lambda i: (jnp.minimum(i, na - 1), 0)),
                   pl.BlockSpec((bm, d), lambda i: (jnp.maximum(i - na, 0), 0))],
        out_shape=[jax.ShapeDtypeStruct((n_first, d), F32), jax.ShapeDtypeStruct((m - n_first, d), F32)],
        compiler_params=_params("arbitrary"), name="moe_combine_residual_ln",
    )(x, y0, y1, gate, g.reshape(1, d), b.reshape(1, d))


def _mm_kernel(a_ref, w_ref, o_ref):
    o_ref[...] = jnp.dot(a_ref[...], w_ref[...], preferred_element_type=F32).astype(o_ref.dtype)


def _matmul(a, w, out_dtype=F32, bm=1024, bn=1024):
    m, k = a.shape
    n = w.shape[1]
    bm, bn = min(bm, m), min(bn, n)
    return pl.pallas_call(
        _mm_kernel, grid=(m // bm, n // bn),
        in_specs=[pl.BlockSpec((bm, k), lambda i, j: (i, 0)), pl.BlockSpec((k, bn), lambda i, j: (0, j))],
        out_specs=pl.BlockSpec((bm, bn), lambda i, j: (i, j)),
        out_shape=jax.ShapeDtypeStruct((m, n), out_dtype),
        compiler_params=_params("parallel", "arbitrary"), name="matmul",
    )(a, w)


def _mix_out_kernel(alpha, routed, a0_ref, a1_ref, a2_ref, w0_ref, w1_ref, w2_ref, x_ref, g_ref, b_ref, *refs):
    mix = (jnp.dot(a0_ref[...], w0_ref[...], preferred_element_type=F32)
           + jnp.dot(a1_ref[...], w1_ref[...], preferred_element_type=F32)
           + jnp.dot(a2_ref[...], w2_ref[...], preferred_element_type=F32))
    y = _ln_rows(alpha * x_ref[...] + mix, g_ref[...], b_ref[...])
    if routed:
        wr_ref, of_ref, ob_ref, lg_ref = refs
        y_hi, y_lo = _split_bf16(y)
        lg_ref[...] = (jnp.dot(y_hi, wr_ref[...], preferred_element_type=F32)
                       + jnp.dot(y_lo, wr_ref[...], preferred_element_type=F32))
    else:
        of_ref, ob_ref = refs
    of_ref[...] = y
    ob_ref[...] = y.astype(BF16)


def _mix_out_residual_ln(parts, weights, x, g, b, alpha, router_w=None, bm=512):
    m, d = x.shape
    row = pl.BlockSpec((bm, d), lambda i: (i, 0))
    vec = pl.BlockSpec((1, d), lambda i: (0, 0))
    in_specs = [pl.BlockSpec((bm, a.shape[1]), lambda i: (i, 0)) for a in parts]
    in_specs += [pl.BlockSpec(w.shape, lambda i: (0, 0)) for w in weights] + [row, vec, vec]
    args = [*parts, *weights, x, g.reshape(1, d), b.reshape(1, d)]
    out_specs = [row, row]
    out_shape = [jax.ShapeDtypeStruct((m, d), F32), jax.ShapeDtypeStruct((m, d), BF16)]
    if router_w is not None:
        in_specs.append(pl.BlockSpec(router_w.shape, lambda i: (0, 0)))
        args.append(router_w)
        out_specs.append(pl.BlockSpec((bm, router_w.shape[1]), lambda i: (i, 0)))
        out_shape.append(jax.ShapeDtypeStruct((m, router_w.shape[1]), F32))
    return pl.pallas_call(
        functools.partial(_mix_out_kernel, alpha, router_w is not None), grid=(m // bm,),
        in_specs=in_specs, out_specs=out_specs, out_shape=out_shape,
        compiler_params=_params("parallel"), name="out_proj_residual_ln",
    )(*args)


def _swiglu_acc(a_ref, w1_ref, w3_ref, w2_ref, acc_ref):
    f = pl.program_id(1)

    @pl.when(f == 0)
    def _():
        acc_ref[...] = jnp.zeros_like(acc_ref)

    a = a_ref[...]
    h1 = jnp.dot(a, w1_ref[...], preferred_element_type=F32)
    h3 = jnp.dot(a, w3_ref[...], preferred_element_type=F32)
    gated = (h1 * jax.nn.sigmoid(h1) * h3).astype(BF16)
    acc_ref[...] += jnp.dot(gated, w2_ref[...], preferred_element_type=F32)


def _ffn_kernel(alpha, a_ref, w1_ref, w3_ref, w2_ref, x_ref, g_ref, b_ref, of_ref, ob_ref, acc_ref):
    _swiglu_acc(a_ref, w1_ref, w3_ref, w2_ref, acc_ref)

    @pl.when(pl.program_id(1) == pl.num_programs(1) - 1)
    def _():
        y = _ln_rows(alpha * x_ref[...] + acc_ref[...], g_ref[...], b_ref[...])
        of_ref[...] = y
        ob_ref[...] = y.astype(BF16)


def _ffn_residual_ln(a, w1, w3, w2, x, g, b, alpha, bm=512, bf=512):
    m, d = a.shape
    dff = w1.shape[1]
    row = pl.BlockSpec((bm, d), lambda i, f: (i, 0))
    vec = pl.BlockSpec((1, d), lambda i, f: (0, 0))
    up = pl.BlockSpec((d, bf), lambda i, f: (0, f))
    return pl.pallas_call(
        functools.partial(_ffn_kernel, alpha), grid=(m // bm, dff // bf),
        in_specs=[row, up, up, pl.BlockSpec((bf, d), lambda i, f: (f, 0)), row, vec, vec],
        out_specs=[row, row],
        out_shape=[jax.ShapeDtypeStruct((m, d), F32), jax.ShapeDtypeStruct((m, d), BF16)],
        scratch_shapes=[pltpu.VMEM((bm, d), F32)],
        compiler_params=_params("parallel", "arbitrary"), name="ffn_residual_ln",
    )(a, w1, w3, w2, x, g.reshape(1, d), b.reshape(1, d))


def _moe_ffn_kernel(blk_e_ref, used_ref, a_ref, w1_ref, w3_ref, w2_ref, o_ref, acc_ref):
    del blk_e_ref
    f = pl.program_id(1)

    @pl.when(pl.program_id(0) < used_ref[0])
    def _():
        @pl.when(f == 0)
        def _():
            acc_ref[...] = jnp.zeros_like(acc_ref)

        a = a_ref[...]
        h1 = jnp.dot(a, w1_ref[...], preferred_element_type=F32)
        h3 = jnp.dot(a, w3_ref[...], preferred_element_type=F32)
        gated = (h1 * jax.nn.sigmoid(h1) * h3).astype(BF16)
        acc_ref[...] += jnp.dot(gated, w2_ref[...], preferred_element_type=F32)

        @pl.when(f == pl.num_programs(1) - 1)
        def _():
            o_ref[...] = acc_ref[...].astype(o_ref.dtype)


def _cast_kernel(x_ref, o_ref):
    o_ref[...] = x_ref[...].astype(o_ref.dtype)


def _cast_bf16(w, rows=128):
    e, r, c = w.shape
    rows = min(rows, r)
    blk = pl.BlockSpec((None, rows, c), lambda i, j: (i, j, 0))
    return pl.pallas_call(
        _cast_kernel, grid=(e, r // rows), in_specs=[blk], out_specs=blk,
        out_shape=jax.ShapeDtypeStruct(w.shape, BF16),
        compiler_params=_params("parallel", "parallel"), name="cast_bf16",
    )(w)


def _moe_grouped_ffn(blk_e, n_used, xb, w1, w3, w2, bm, bf=512):
    cap, d = xb.shape
    dff = w1.shape[2]
    nf = dff // bf

    def blk(i, u):
        return jnp.minimum(i, u[0] - 1)

    def ftile(i, f, u):
        return jnp.where(i < u[0], f, nf - 1)

    row = pl.BlockSpec((bm, d), lambda i, f, e, u: (blk(i, u), 0))
    up = pl.BlockSpec((None, d, bf), lambda i, f, e, u: (e[blk(i, u)], 0, ftile(i, f, u)))
    down = pl.BlockSpec((None, bf, d), lambda i, f, e, u: (e[blk(i, u)], ftile(i, f, u), 0))
    grid_spec = pltpu.PrefetchScalarGridSpec(
        num_scalar_prefetch=2, grid=(cap // bm, nf), in_specs=[row, up, up, down],
        out_specs=row, scratch_shapes=[pltpu.VMEM((bm, d), F32)])
    return pl.pallas_call(
        _moe_ffn_kernel, grid_spec=grid_spec, out_shape=jax.ShapeDtypeStruct((cap, d), BF16),
        compiler_params=_params("arbitrary", "arbitrary"), name="moe_grouped_ffn",
    )(blk_e, n_used, xb, w1, w3, w2)


def _moe_route(logits, bm):
    n = logits.shape[0]
    top_val, top_idx = lax.top_k(logits, TOP_K)
    gate = jax.nn.softmax(top_val, axis=-1)
    nk = n * TOP_K
    assert nk % bm == 0
    experts = jnp.arange(N_EXPERTS, dtype=jnp.int32)
    flat_e = top_idx.reshape(nk).astype(jnp.int32)
    one_hot = flat_e[:, None] == experts[None, :]
    order = jnp.argsort(flat_e, stable=True).astype(jnp.int32)
    rank = jnp.argsort(order).astype(jnp.int32)
    counts = jnp.sum(one_hot.astype(jnp.int32), 0)
    padded = (counts + bm - 1) // bm * bm
    start = jnp.cumsum(counts) - counts
    pend = jnp.cumsum(padded)
    shift = jnp.sum(jnp.where(one_hot, (pend - padded - start)[None, :], 0), 1)
    dest = (rank + shift).reshape(n, TOP_K)
    n_blocks = nk // bm + N_EXPERTS
    blk_e = jnp.minimum(jnp.searchsorted(pend, jnp.arange(n_blocks, dtype=jnp.int32) * bm, side='right'),
                        N_EXPERTS - 1).astype(jnp.int32)
    pad_rank = jnp.arange(bm, dtype=jnp.int32)[None, :]
    pad_key = jnp.where(pad_rank < (padded - counts)[:, None], 2 * experts[:, None] + 1, 2 * N_EXPERTS)
    keys = jnp.concatenate([2 * flat_e, pad_key.reshape(-1)])
    toks = jnp.concatenate([jnp.arange(nk, dtype=jnp.int32) // TOP_K, jnp.zeros((N_EXPERTS * bm,), jnp.int32)])
    _, slot_tok = lax.sort((keys, toks), num_keys=1, is_stable=True)
    n_used = (pend[-1] // bm).astype(jnp.int32).reshape(1)
    return gate, dest, slot_tok, blk_e, n_used


def _moe_dispatch_ffn(logits, h_bf16, w1, w3, w2, bm=1024):
    gate, dest, slot_tok, blk_e, n_used = _moe_route(logits, bm)
    yb = _moe_grouped_ffn(blk_e, n_used, h_bf16[slot_tok], w1, w3, w2, bm)
    return yb[dest[:, 0]], yb[dest[:, 1]], gate


def _gmlp_kernel(groups, pu_ref, pv_ref, g_ref, b_ref, ws_ref, bs_ref, o_ref):
    u = jax.nn.gelu(pu_ref[...])
    v = _ln_rows(jax.nn.gelu(pv_ref[...]), g_ref[...], b_ref[...])
    for n in range(u.shape[0] // GM_CHUNK):
        rows = slice(n * GM_CHUNK, (n + 1) * GM_CHUNK)
        for grp in range(groups):
            cols = slice(grp * HEAD_DIM, (grp + 1) * HEAD_DIM)
            s = _dot3(ws_ref[grp], v[rows, cols]) + bs_ref[grp]
            o_ref[rows, cols] = (u[rows, cols] * s).astype(o_ref.dtype)


def _gmlp(p, col_u, col_v, gm_w, ln_g, ln_b, ws, bs, bt=512):
    n = p.shape[0]
    groups = ws.shape[0]
    bsb = jnp.broadcast_to(bs[:, :, None], (groups, GM_CHUNK, HEAD_DIM)).astype(F32)
    vec = pl.BlockSpec((1, gm_w), lambda i: (0, 0))
    full3 = pl.BlockSpec((groups, GM_CHUNK, HEAD_DIM), lambda i: (0, 0, 0))
    return pl.pallas_call(
        functools.partial(_gmlp_kernel, groups), grid=(n // bt,),
        in_specs=[pl.BlockSpec((bt, gm_w), lambda i: (i, col_u)), pl.BlockSpec((bt, gm_w), lambda i: (i, col_v)),
                  vec, vec, pl.BlockSpec((groups, GM_CHUNK, GM_CHUNK), lambda i: (0, 0, 0)), full3],
        out_specs=pl.BlockSpec((bt, gm_w), lambda i: (i, 0)),
        out_shape=jax.ShapeDtypeStruct((n, gm_w), BF16),
        compiler_params=_params("parallel"), name="gmlp",
    )(p, p, ln_g.reshape(1, gm_w), ln_b.reshape(1, gm_w), ws.astype(F32), bsb)


def _hgrn2_constants(c, reverse):
    n_lvl = int(math.log2(c))
    t = np.arange(c)
    ms, ws = [], []
    for lvl in range(n_lvl):
        m = c >> (lvl + 1)
        mid = (t // (2 * m)) * (2 * m) + m
        upper = t >= mid
        mat = np.zeros((c, c), np.float32)
        for r in range(c):
            if upper[r]:
                mat[r, mid[r]:r + 1] = 1.0
            else:
                mat[r, r + 1:mid[r]] = 1.0
        same = (t[:, None] // (2 * m)) == (t[None, :] // (2 * m))
        ws.append((same & upper[:, None] & ~upper[None, :]).astype(np.float32))
        ms.append(mat)
    ws.append(np.eye(c, dtype=np.float32))
    ms.append(np.tril(np.ones((c, c), np.float32)))
    ms.append(np.triu(np.ones((c, c), np.float32), 1))
    if reverse:
        ms = [a[::-1, ::-1] for a in ms]
        ws = [a[::-1, ::-1] for a in ws]
    return np.concatenate(ms, 0), np.stack(ws, 0)


def _hgrn2_kernel(heads, n_lvl, carry_row, final, reset_ref, *refs):
    if final:
        (q_ref, z_ref, i_ref, og_ref, oo_ref, loglb_ref, log1mlb_ref, onemlb_ref, ng_ref, m_ref, w_ref,
         out_ref, st_ref, e_ref) = refs
    else:
        (q_ref, z_ref, i_ref, loglb_ref, log1mlb_ref, onemlb_ref, m_ref, w_ref, out_ref, st_ref, e_ref) = refs
    n_streams, c = q_ref.shape[0], q_ref.shape[1]
    rows_e = m_ref.shape[0]
    step = pl.program_id(0)

    for s in range(n_streams):
        @pl.when(reset_ref[step * n_streams + s] == 1)
        def _():
            st_ref[s * heads:(s + 1) * heads] = jnp.zeros((heads, HEAD_DIM, HEAD_DIM), F32)

    for s in range(n_streams):
        z = z_ref[s]
        e = jnp.exp(-jnp.abs(z))
        r = 1.0 / (1.0 + e)
        log_sig = jnp.minimum(z, 0.0) - jnp.log(1.0 + e)
        sig_neg = jnp.where(z >= 0.0, e * r, r)
        a = loglb_ref[...]
        b = log1mlb_ref[...] + log_sig
        g = jnp.maximum(a, b) + jnp.log(1.0 + jnp.exp(-jnp.abs(a - b)))
        k = onemlb_ref[...] * sig_neg
        q = q_ref[s]
        qs = q * jax.nn.sigmoid(q)
        v = i_ref[s].astype(BF16)

        g_hi, g_lo = _split_bf16(g)
        m_all = m_ref[...]
        e0 = s * rows_e
        e_ref[e0:e0 + rows_e, :] = (jnp.dot(m_all, g_hi, preferred_element_type=F32)
                                    + jnp.dot(m_all, g_lo, preferred_element_type=F32))

        for h in range(heads):
            cols = slice(h * HEAD_DIM, (h + 1) * HEAD_DIM)
            qh, kh, vh = qs[:, cols], k[:, cols], v[:, cols]
            amat = w_ref[n_lvl] * lax.dot_general(qh.astype(BF16), kh.astype(BF16), NT_DIMS,
                                                  preferred_element_type=F32)
            for lvl in range(n_lvl):
                ex = jnp.exp(e_ref[e0 + lvl * c:e0 + (lvl + 1) * c, cols])
                amat = amat + w_ref[lvl] * lax.dot_general((qh * ex).astype(BF16), (kh * ex).astype(BF16),
                                                           NT_DIMS, preferred_element_type=F32)
            e_in = e_ref[e0 + n_lvl * c:e0 + (n_lvl + 1) * c, cols]
            q_in = (qh * jnp.exp(e_in)).astype(BF16)
            k_st = (kh * jnp.exp(e_ref[e0 + (n_lvl + 1) * c:e0 + (n_lvl + 2) * c, cols])).astype(BF16)
            dec = jnp.exp(e_in[carry_row:carry_row + 1, :])
            st = st_ref[s * heads + h]
            o = (lax.dot_general(q_in, st.astype(BF16), NT_DIMS, preferred_element_type=F32)
                 + jnp.dot(amat.astype(BF16), vh, preferred_element_type=F32))
            st_ref[s * heads + h] = st * dec + lax.dot_general(vh, k_st, TN_DIMS, preferred_element_type=F32)
            if final:
                o = o + oo_ref[s, :, cols]
                o = o * lax.rsqrt(jnp.mean(o * o, -1, keepdims=True) + RMS_EPS)
                og = og_ref[s, :, cols]
                out_ref[s, :, cols] = (o * ng_ref[:, cols] * (og * jax.nn.sigmoid(og))).astype(out_ref.dtype)
            else:
                out_ref[s, :, cols] = o


def _hgrn2_pass(p3, cols, hg_w, z_col, reverse, reset, lbs, extra, c):
    n_streams, n = p3.shape[0], p3.shape[1]
    nc = n // c
    heads = hg_w // HEAD_DIM
    n_lvl = int(math.log2(c))
    m_np, w_np = _hgrn2_constants(c, reverse)
    final = extra is not None

    def cmap(i):
        return (nc - 1 - i) if reverse else i

    def tok(col):
        return pl.BlockSpec((n_streams, c, hg_w), lambda i, r: (0, cmap(i), col))

    vec = pl.BlockSpec((1, hg_w), lambda i, r: (0, 0))
    in_specs = [tok(cols["q"]), tok(z_col), tok(cols["i"])]
    args = [p3, p3, p3]
    if final:
        o_other, norm_g = extra
        in_specs += [tok(cols["og"]), tok(0)]
        args += [p3, o_other]
    in_specs += [vec, vec, vec]
    args += list(lbs)
    if final:
        in_specs += [vec]
        args += [norm_g.reshape(1, hg_w).astype(F32)]
    in_specs += [pl.BlockSpec(m_np.shape, lambda i, r: (0, 0)), pl.BlockSpec(w_np.shape, lambda i, r: (0, 0, 0))]
    args += [jnp.asarray(m_np, BF16), jnp.asarray(w_np, F32)]
    grid_spec = pltpu.PrefetchScalarGridSpec(
        num_scalar_prefetch=1, grid=(nc,), in_specs=in_specs, out_specs=tok(0),
        scratch_shapes=[pltpu.VMEM((n_streams * heads, HEAD_DIM, HEAD_DIM), F32),
                        pltpu.VMEM((n_streams * m_np.shape[0], hg_w), F32)])
    order = np.arange(nc)[::-1] if reverse else np.arange(nc)
    reset_steps = jnp.asarray(np.asarray(reset, np.int32)[order].reshape(-1))
    return pl.pallas_call(
        functools.partial(_hgrn2_kernel, heads, n_lvl, 0 if reverse else c - 1, final), grid_spec=grid_spec,
        out_shape=jax.ShapeDtypeStruct((n_streams, n, hg_w), BF16 if final else F32),
        compiler_params=_params("arbitrary"), name="hgrn2_fwd" if final else "hgrn2_bwd",
    )(reset_steps, *args)


def _hgrn2(p, cols, hg_w, lb, norm_g, seq_lens, n_streams=2, c=HG_CHUNK):
    lb = lb.astype(F32).reshape(1, hg_w)
    lbs = (jnp.log(lb), jnp.log1p(-lb), 1.0 - lb)
    starts = np.cumsum([0] + list(seq_lens))
    n = int(starts[-1])
    per = n // n_streams
    assert n == p.shape[0] and all(s * per in starts for s in range(n_streams)), "a sequence straddles a stream cut"
    first = np.zeros(n // c, np.int32)
    last = np.zeros(n // c, np.int32)
    first[starts[:-1] // c] = 1
    last[starts[1:] // c - 1] = 1
    first = first.reshape(n_streams, per // c).T
    last = last.reshape(n_streams, per // c).T
    p3 = p.reshape(n_streams, per, p.shape[1])
    o_b = _hgrn2_pass(p3, cols, hg_w, cols["zb"], True, last, lbs, None, c)
    return _hgrn2_pass(p3, cols, hg_w, cols["zf"], False, first, lbs, (o_b, norm_g), c).reshape(n, hg_w)


def _dft_matrices(t):
    n = 2 * t
    f = jnp.arange(t, dtype=jnp.int32)
    ang = (2.0 * math.pi / n) * ((f[:, None] * f[None, :]) % n).astype(F32)
    cos, sin = jnp.cos(ang), jnp.sin(ang)
    alt = (1 - 2 * (f % 2)).astype(F32)
    first = f == 0
    fwd = jnp.stack([cos, jnp.where(first[:, None], alt[None, :], -sin)], 0).astype(BF16)
    wgt = jnp.where(first, 1.0, 2.0)[None, :] / n
    inv = jnp.stack([cos * wgt, jnp.where(first[None, :], alt[:, None] / n, -sin * wgt)], 0).astype(BF16)
    return fwd, inv


def _shortconv_kernel(flags_ref, x_ref, prev_ref, next_ref, w_ref, b_ref, o_ref):
    i = pl.program_id(0)
    x = x_ref[...]
    t = x.shape[0]
    rows = lax.broadcasted_iota(jnp.int32, x.shape, 0)
    has_prev = flags_ref[2 * i].astype(F32)
    has_next = flags_ref[2 * i + 1].astype(F32)
    up = jnp.where(rows == 0, prev_ref[7:8, :] * has_prev, pltpu.roll(x, 1, 0))
    dn = jnp.where(rows == t - 1, next_ref[0:1, :] * has_next, pltpu.roll(x, t - 1, 0))
    o_ref[...] = up * w_ref[0:1, :] + x * w_ref[1:2, :] + dn * w_ref[2:3, :] + b_ref[...]


def _shortconv(p, width, conv_w, conv_b, blk_flags, t=HY_BLOCK, ct=768):
    n = p.shape[0]
    nb = n // t
    r8 = t // 8
    grid_spec = pltpu.PrefetchScalarGridSpec(
        num_scalar_prefetch=1, grid=(nb, width // ct),
        in_specs=[pl.BlockSpec((t, ct), lambda i, j, f: (i, j)),
                  pl.BlockSpec((8, ct), lambda i, j, f: (jnp.maximum(i * r8 - 1, 0), j)),
                  pl.BlockSpec((8, ct), lambda i, j, f: (jnp.minimum((i + 1) * r8, nb * r8 - 1), j)),
                  pl.BlockSpec((3, ct), lambda i, j, f: (0, j)), pl.BlockSpec((1, ct), lambda i, j, f: (0, j))],
        out_specs=pl.BlockSpec((t, ct), lambda i, j, f: (i, j)))
    return pl.pallas_call(
        _shortconv_kernel, grid_spec=grid_spec, out_shape=jax.ShapeDtypeStruct((n, width), F32),
        compiler_params=_params("parallel", "parallel"), name="hyena_shortconv",
    )(blk_flags, p, p, p, conv_w.astype(F32), conv_b.reshape(1, width).astype(F32))


def _filter_kernel(l_total, feats_ref, w1_ref, b1_ref, w2_ref, b2_ref, w3_ref, b3_ref, fr_ref, wo_ref, dl_ref,
                   h_ref, sum_ref):
    del l_total
    i = pl.program_id(0)
    bt = feats_ref.shape[0]
    half_lanes = LANES // 2
    fr = fr_ref[...]
    feats = feats_ref[...]
    h = jnp.sin(fr * (_dot3(feats, w1_ref[...]) + b1_ref[...]))
    h = jnp.sin(fr * (_dot3(h, w2_ref[...]) + b2_ref[...]))
    h = jnp.sin(fr * (_dot3(h, w3_ref[...]) + b3_ref[...]))
    reps = wo_ref.shape[2] // dl_ref.shape[1]
    row = lax.broadcasted_iota(jnp.int32, (bt, 1), 0) + i * bt
    total = jnp.zeros((1, wo_ref.shape[2]), F32)
    for side in range(2):
        window = jnp.exp(-feats[:, side * half_lanes:side * half_lanes + 1] * dl_ref[...])
        out = _dot3(h, wo_ref[side]) * jnp.concatenate([window] * reps, axis=1)
        if side == 1:
            out = jnp.where(row == 0, 0.0, out)
        h_ref[side] = out
        total = total + jnp.sum(jnp.abs(out), axis=0, keepdims=True)

    @pl.when(i == 0)
    def _():
        sum_ref[...] = jnp.zeros_like(sum_ref)

    sum_ref[...] += total


def _hyena_filter(l, w1, b1, w2, b2, w3, b3, freq, w_out, hy_w, bt=512):
    hid = w2.shape[0]
    half_lanes = LANES // 2
    assert hid <= half_lanes and w1.shape[0] <= half_lanes
    half = HY_ORDER * hy_w
    bands = jnp.linspace(1e-4, HY_POS_BANDS - 1, HY_POS_BANDS, dtype=F32)

    def features(pos):
        tt = pos / max(l - 1, 1)
        ang = (2.0 * math.pi / l) * pos[:, None] * bands[None, :]
        f = jnp.concatenate([tt[:, None], jnp.cos(ang), -jnp.sin(ang)], -1)
        return jnp.pad(f, ((0, 0), (0, half_lanes - f.shape[1])))

    def twice_diag(w):
        wp = jnp.pad(w.astype(F32), ((0, half_lanes - w.shape[0]), (0, half_lanes - w.shape[1])))
        z = jnp.zeros_like(wp)
        return jnp.concatenate([jnp.concatenate([wp, z], 1), jnp.concatenate([z, wp], 1)], 0)

    def twice_vec(a):
        ap = jnp.pad(a.astype(F32), (0, half_lanes - a.shape[0]))
        return jnp.concatenate([ap, ap]).reshape(1, LANES)

    pos = jnp.arange(l, dtype=F32)
    feats = jnp.concatenate([features(pos), features(l - pos)], 1)
    wo = w_out.astype(F32).reshape(hid, HY_ORDER, 2, hy_w).transpose(2, 0, 1, 3).reshape(2, hid, half)
    wo = jnp.pad(wo, ((0, 0), (0, half_lanes - hid), (0, 0)))
    zero = jnp.zeros_like(wo[0])
    wop = jnp.stack([jnp.concatenate([wo[0], zero], 0), jnp.concatenate([zero, wo[1]], 0)], 0)
    deltas = jnp.abs(jnp.linspace(math.log(HY_DECAY_TARGET) / HY_SLOW_DECAY,
                                  math.log(HY_DECAY_TARGET) / HY_FAST_DECAY, hy_w, dtype=F32)).reshape(1, hy_w)
    bt = min(bt, l)
    sq = pl.BlockSpec((LANES, LANES), lambda i: (0, 0))
    vec = pl.BlockSpec((1, LANES), lambda i: (0, 0))
    taps, abs_sum = pl.pallas_call(
        functools.partial(_filter_kernel, l), grid=(l // bt,),
        in_specs=[pl.BlockSpec((bt, LANES), lambda i: (i, 0)), sq, vec, sq, vec, sq, vec, vec,
                  pl.BlockSpec((2, LANES, half), lambda i: (0, 0, 0)),
                  pl.BlockSpec((1, hy_w), lambda i: (0, 0))],
        out_specs=[pl.BlockSpec((2, bt, half), lambda i: (0, i, 0)), pl.BlockSpec((1, half), lambda i: (0, 0))],
        out_shape=[jax.ShapeDtypeStruct((2, l, half), F32), jax.ShapeDtypeStruct((1, half), F32)],
        compiler_params=_params("arbitrary"), name="hyena_filter",
    )(feats, twice_diag(w1), twice_vec(b1), twice_diag(w2), twice_vec(b2), twice_diag(w3), twice_vec(b3),
      twice_vec(freq), wop, deltas)
    return taps.reshape(2 * l, half), abs_sum


def _dft_cols(t, width):
    return width if t <= 1024 else 256


def _dft_kernel(f_ref, x_ref, o_ref):
    o_ref[...] = jnp.dot(f_ref[...], x_ref[...].astype(BF16), preferred_element_type=F32).astype(o_ref.dtype)


def _block_dft(fwd, x, col_off, width, out_dtype, t=HY_BLOCK, ct=256):
    nseg = x.shape[0] // t
    ct = min(ct, width)
    return pl.pallas_call(
        _dft_kernel, grid=(2, nseg, width // ct),
        in_specs=[pl.BlockSpec((None, t, t), lambda hf, s, j: (hf, 0, 0)),
                  pl.BlockSpec((t, ct), lambda hf, s, j: (s, j + col_off))],
        out_specs=pl.BlockSpec((None, None, t, ct), lambda hf, s, j: (s, hf, 0, j)),
        out_shape=jax.ShapeDtypeStruct((nseg, 2, t, width), out_dtype),
        compiler_params=_params("arbitrary", "arbitrary", "arbitrary"), name="hyena_block_dft",
    )(fwd, x)


def _spectrum_kernel(idx_ref, a_ref, b_ref, s0_ref, inv_ref, o_ref):
    del idx_ref
    t, ct = a_ref.shape[1], a_ref.shape[2]
    rows = lax.broadcasted_iota(jnp.int32, (t, ct), 0)
    sgn = (1 - 2 * (rows & 1)).astype(F32)
    s0 = s0_ref[...]
    inv = inv_ref[...]
    o_ref[0] = ((a_ref[0] + sgn * (b_ref[0] - s0)) * inv).astype(o_ref.dtype)
    o_ref[1] = ((a_ref[1] + sgn * (b_ref[1] - jnp.where(rows == 0, s0, 0.0))) * inv).astype(o_ref.dtype)


def _filter_spectra(seg_spec, idx, seg0, inv_norm, ct=256):
    nslot = seg0.shape[0]
    t, width = seg_spec.shape[2], seg_spec.shape[3]
    grid_spec = pltpu.PrefetchScalarGridSpec(
        num_scalar_prefetch=1, grid=(nslot, width // ct),
        in_specs=[pl.BlockSpec((None, 2, t, ct), lambda s, j, ix: (ix[2 * s], 0, 0, j)),
                  pl.BlockSpec((None, 2, t, ct), lambda s, j, ix: (ix[2 * s + 1], 0, 0, j)),
                  pl.BlockSpec((None, 1, ct), lambda s, j, ix: (s, 0, j)),
                  pl.BlockSpec((1, ct), lambda s, j, ix: (0, j))],
        out_specs=pl.BlockSpec((None, 2, t, ct), lambda s, j, ix: (s, 0, 0, j)))
    return pl.pallas_call(
        _spectrum_kernel, grid_spec=grid_spec, out_shape=jax.ShapeDtypeStruct((nslot, 2, t, width), F32),
        compiler_params=_params("parallel", "arbitrary"), name="hyena_filter_spectra",
    )(idx, seg_spec, seg_spec, seg0, inv_norm)


def _hyena_filter_spectra(l, fwd, filt_w, hy_w, t=HY_BLOCK):
    nb = l // t
    taps, abs_sum = _hyena_filter(l, *filt_w, hy_w)
    half = taps.shape[1]
    seg_spec = _block_dft(fwd, taps, 0, half, F32, t, _dft_cols(t, hy_w))

    def seg_index(d):
        return d if d >= 0 else 2 * nb + d

    idx = []
    for d in range(-(nb - 1), nb):
        idx += [seg_index(d), seg_index(d - 1)]
    first_rows = taps.reshape(2 * nb, t, half)[:, 0, :]
    seg0 = jnp.stack([first_rows[i] for i in idx[1::2]], 0).reshape(2 * nb - 1, 1, half)
    return _filter_spectra(seg_spec, jnp.asarray(idx, jnp.int32), seg0, 1.0 / abs_sum)


MIX_ROWS = 16


def _mix_kernel(nb, *refs):
    z_ref, k_ref, o_ref = refs[0], refs[1], refs[-1]
    ft, ct = z_ref.shape[2], z_ref.shape[3]
    first_tile = pl.program_id(1) == 0

    def out_rows(i, r0, masked):
        rows = pl.ds(r0, MIX_ROWS)
        acc_t = jnp.zeros((MIX_ROWS, ct), F32)
        acc_b = jnp.zeros((MIX_ROWS, ct), F32)
        if masked:
            row0 = jnp.logical_and(lax.broadcasted_iota(jnp.int32, (MIX_ROWS, ct), 0) == 0, first_tile)
        for j in range(nb):
            d = i - j + (nb - 1)
            zt, zb = z_ref[j, 0, rows, :], z_ref[j, 1, rows, :]
            kt, kb = k_ref[d, 0, rows, :], k_ref[d, 1, rows, :]
            bb = zb * kb
            if masked:
                acc_t = acc_t + (zt * kt - jnp.where(row0, 0.0, bb))
                acc_b = acc_b + jnp.where(row0, bb, zt * kb + zb * kt)
            else:
                acc_t = acc_t + (zt * kt - bb)
                acc_b = acc_b + (zt * kb + zb * kt)
        o_ref[i, 0, rows, :] = acc_t.astype(o_ref.dtype)
        o_ref[i, 1, rows, :] = acc_b.astype(o_ref.dtype)

    def all_outputs(r0, masked):
        def body(i, carry):
            out_rows(i, r0, masked)
            return carry
        lax.fori_loop(0, nb, body, 0)

    all_outputs(0, True)

    def chunk(r, carry):
        all_outputs(pl.multiple_of(r * MIX_ROWS, MIX_ROWS), False)
        return carry

    lax.fori_loop(1, ft // MIX_ROWS, chunk, 0)


def _hyena_mix(zspec, kspec, k_col, n_seq, nb, blk_off, prev, ct=256):
    nblk, _, t, width = zspec.shape
    ft = max(MIX_ROWS, min(t, 2048 // nb))
    nct = width // ct
    assert blk_off % nb == 0 and t % ft == 0
    s_off = blk_off // nb
    zblk = pl.BlockSpec((nb, 2, ft, ct), lambda s, fi, j: (s + s_off, 0, fi, j))
    in_specs = [zblk, pl.BlockSpec((2 * nb - 1, 2, ft, ct), lambda s, fi, j: (0, 0, fi, j + k_col * nct))]
    args = [zspec, kspec]
    aliases = {}
    if prev is not None:
        in_specs.append(pl.BlockSpec(memory_space=pl.ANY))
        args.append(prev)
        aliases = {2: 0}
    return pl.pallas_call(
        functools.partial(_mix_kernel, nb), grid=(n_seq, t // ft, nct),
        in_specs=in_specs, out_specs=zblk,
        out_shape=jax.ShapeDtypeStruct(zspec.shape, BF16), input_output_aliases=aliases,
        compiler_params=_params("parallel", "parallel", "parallel"), name="hyena_mix",
    )(*args)


def _inverse_kernel(y_ref, fi_ref, zin_ref, gate_ref, bias_ref, o_ref):
    y = (jnp.dot(fi_ref[0], y_ref[0], preferred_element_type=F32)
         + jnp.dot(fi_ref[1], y_ref[1], preferred_element_type=F32))
    o_ref[...] = (gate_ref[...] * (y + zin_ref[...] * bias_ref[...])).astype(o_ref.dtype)


def _hyena_inverse(yspec, inv, zin, zin_col, gate, gate_col, bias, out_dtype, ct=256):
    nblk, _, t, width = yspec.shape
    nct = width // ct
    return pl.pallas_call(
        _inverse_kernel, grid=(nblk, nct),
        in_specs=[pl.BlockSpec((None, 2, t, ct), lambda i, j: (i, 0, 0, j)),
                  pl.BlockSpec((2, t, t), lambda i, j: (0, 0, 0), pipeline_mode=pl.Buffered(1)),
                  pl.BlockSpec((t, ct), lambda i, j: (i, j + zin_col * nct)),
                  pl.BlockSpec((t, ct), lambda i, j: (i, j + gate_col * nct)),
                  pl.BlockSpec((1, ct), lambda i, j: (0, j))],
        out_specs=pl.BlockSpec((t, ct), lambda i, j: (i, j)),
        out_shape=jax.ShapeDtypeStruct((nblk * t, width), out_dtype),
        compiler_params=_params("parallel", "arbitrary"), name="hyena_inverse",
    )(yspec, inv, zin, gate, bias)


def _hyena(p, hy_w, groups, conv_w, conv_b, filt_w, bias, t=HY_BLOCK):
    flags = []
    for n_seq, l in groups:
        nb = l // t
        flags += [int(bi > 0) if side == 0 else int(bi < nb - 1)
                  for _ in range(n_seq) for bi in range(nb) for side in range(2)]
    fwd, inv = _dft_matrices(t)
    u = _shortconv(p, 3 * hy_w, conv_w, conv_b, jnp.asarray(np.asarray(flags, np.int32)), t, ct=hy_w)
    kspecs = [_hyena_filter_spectra(l, fwd, filt_w, hy_w, t) for _, l in groups]
    bias = bias.astype(F32)
    ct = _dft_cols(t, hy_w)
    z, z_col = u, 2
    for o in range(HY_ORDER):
        zspec = _block_dft(fwd, z, z_col * (hy_w // ct), hy_w, F32, t, ct)
        yspec, blk = None, 0
        for (n_seq, l), kspec in zip(groups, kspecs):
            yspec = _hyena_mix(zspec, kspec, o, n_seq, l // t, blk, yspec)
            blk += n_seq * (l // t)
        last = o == HY_ORDER - 1
        z = _hyena_inverse(yspec, inv, z, z_col, u, o, bias[o:o + 1], BF16 if last else F32, ct)
        z_col = 0
    return z


def kernel(x_prompt, x_sample, ln_in_g, ln_in_b, w_in, hy_conv_w, hy_conv_b, hy_pos_w1, hy_pos_b1, hy_pos_w2, hy_pos_b2, hy_pos_w3, hy_pos_b3, hy_sin_freq, hy_pos_wout, hy_bias, gm_ln_g, gm_ln_b, gm_ws, gm_bs, hg_lb_raw, hg_norm_g, w_out, ln1_g, ln1_b, ln2_g, ln2_b, ffn_w1, ffn_w3, ffn_w2, moe_router_w, moe_router_b, moe_w1, moe_w3, moe_w2):
    depth, d_model, in_w = w_in.shape
    hy_w = hy_bias.shape[-1]
    gm_w = gm_ln_g.shape[-1]
    hg_w = hg_norm_g.shape[-1]
    o1 = 3 * hy_w
    o2 = o1 + 2 * gm_w
    alpha = (2 * depth) ** 0.25
    groups = [(x_prompt.shape[0], x_prompt.shape[1]), (x_sample.shape[0], x_sample.shape[1])]
    seq_lens = [l for n_seq, l in groups for _ in range(n_seq)]
    n_prompt = x_prompt.shape[0] * x_prompt.shape[1]
    hg_off = o1 // hg_w
    hg_cols = {"q": hg_off, "zf": hg_off + 1, "zb": hg_off + 2, "i": hg_off + 3, "og": hg_off + 4}
    gm_off = (o1 + 5 * hg_w) // gm_w

    lb_all = jnp.cumsum(jax.nn.softmax(hg_lb_raw.astype(F32), axis=0), axis=0)
    lb_all = lb_all - lb_all[:1]

    x, xb = _layernorm_pair(x_prompt.reshape(-1, d_model), x_sample.reshape(-1, d_model), ln_in_g, ln_in_b)
    for l in range(depth):
        w_l = jnp.concatenate([w_in[l][:, :o1], w_in[l][:, o2:], w_in[l][:, o1:o2]], 1).astype(BF16)
        p = _matmul(xb, w_l)
        filt_w = (hy_pos_w1[l], hy_pos_b1[l], hy_pos_w2[l], hy_pos_b2[l], hy_pos_w3[l], hy_pos_b3[l],
                  hy_sin_freq[l], hy_pos_wout[l])
        y_hy = _hyena(p, hy_w, groups, hy_conv_w[l], hy_conv_b[l], filt_w, hy_bias[l])
        y_hg = _hgrn2(p, hg_cols, hg_w, lb_all[l], hg_norm_g[l], seq_lens)
        y_gm = _gmlp(p, gm_off, gm_off + 1, gm_w, gm_ln_g[l], gm_ln_b[l], gm_ws[l], gm_bs[l])
        wo = w_out[l].astype(BF16)
        parts = [y_hy, y_gm, y_hg]
        wo_parts = [wo[:hy_w], wo[hy_w:hy_w + gm_w], wo[hy_w + gm_w:]]
        j = l // 2
        if l % 2 == 0:
            x, xb = _mix_out_residual_ln(parts, wo_parts, x, ln1_g[l], ln1_b[l], alpha)
            x, xb = _ffn_residual_ln(xb, ffn_w1[j].astype(BF16), ffn_w3[j].astype(BF16), ffn_w2[j].astype(BF16),
                                     x, ln2_g[l], ln2_b[l], alpha)
        else:
            n_exp = moe_router_w.shape[-1]
            wr = jnp.pad(moe_router_w[j].astype(BF16), ((0, 0), (0, LANES - n_exp)))
            x, xb, logits = _mix_out_residual_ln(parts, wo_parts, x, ln1_g[l], ln1_b[l], alpha, wr)
            logits = logits[:, :n_exp] + moe_router_b[j].astype(F32)
            y0, y1, gate = _moe_dispatch_ffn(logits, xb, _cast_bf16(moe_w1[j]), _cast_bf16(moe_w3[j]),
                                             _cast_bf16(moe_w2[j]))
            xa, xs = _combine_residual_ln_split(x, y0, y1, gate, ln2_g[l], ln2_b[l], alpha, n_prompt)
            if l == depth - 1:
                return (xa.reshape(x_prompt.shape), xs.reshape(x_sample.shape))
            x = jnp.concatenate([xa, xs], 0)
            xb = x.astype(BF16)
    return (x[:n_prompt].reshape(x_prompt.shape), x[n_prompt:].reshape(x_sample.shape))
```

```python
import functools
import math

import numpy as np
import jax
import jax.numpy as jnp
from jax import lax
from jax.experimental import pallas as pl
from jax.experimental.pallas import tpu as pltpu

HEAD_DIM = 128
HY_ORDER = 2
HY_POS_BANDS = 16
HY_FAST_DECAY = 0.3
HY_SLOW_DECAY = 1.5
HY_DECAY_TARGET = 1e-2
GM_CHUNK = 128
N_EXPERTS = 8
TOP_K = 2
LN_EPS = 1e-5
RMS_EPS = 1e-6

HY_BLOCK = 1024
HG_CHUNK = 128
LANES = 128

V7X_VMEM_BYTES = 64 * 1024 * 1024
VMEM_LIMIT = V7X_VMEM_BYTES - 8 * 1024 * 1024

F32 = jnp.float32
BF16 = jnp.bfloat16
NT_DIMS = (((1,), (1,)), ((), ()))
TN_DIMS = (((0,), (0,)), ((), ()))


def _params(*sem):
    return pltpu.CompilerParams(dimension_semantics=sem, vmem_limit_bytes=VMEM_LIMIT)


def _ln_rows(x, g, b):
    mu = jnp.mean(x, -1, keepdims=True)
    xc = x - mu
    var = jnp.mean(xc * xc, -1, keepdims=True)
    return xc * lax.rsqrt(var + LN_EPS) * g + b


def _split_bf16(x):
    hi = x.astype(BF16)
    lo = (x - hi.astype(F32)).astype(BF16)
    return hi, lo


def _dot3(a, b):
    ah, al = _split_bf16(a)
    bh, bl = _split_bf16(b)
    return (jnp.dot(ah, bh, preferred_element_type=F32) + jnp.dot(ah, bl, preferred_element_type=F32)
            + jnp.dot(al, bh, preferred_element_type=F32))


def _ln2_kernel(na_blocks, xa_ref, xb_ref, g_ref, b_ref, of_ref, ob_ref):
    def emit(x_ref):
        y = _ln_rows(x_ref[...], g_ref[...], b_ref[...])
        of_ref[...] = y
        ob_ref[...] = y.astype(BF16)

    first = pl.program_id(0) < na_blocks
    pl.when(first)(lambda: emit(xa_ref))
    pl.when(jnp.logical_not(first))(lambda: emit(xb_ref))


def _layernorm_pair(xa, xb, g, b, bm=512):
    d = xa.shape[1]
    na, nb = xa.shape[0] // bm, xb.shape[0] // bm
    row = pl.BlockSpec((bm, d), lambda i: (i, 0))
    vec = pl.BlockSpec((1, d), lambda i: (0, 0))
    m = (na + nb) * bm
    return pl.pallas_call(
        functools.partial(_ln2_kernel, na), grid=(na + nb,),
        in_specs=[pl.BlockSpec((bm, d), lambda i: (jnp.minimum(i, na - 1), 0)),
                  pl.BlockSpec((bm, d), lambda i: (jnp.maximum(i - na, 0), 0)), vec, vec],
        out_specs=[row, row],
        out_shape=[jax.ShapeDtypeStruct((m, d), F32), jax.ShapeDtypeStruct((m, d), BF16)],
        compiler_params=_params("arbitrary"), name="layernorm",
    )(xa, xb, g.reshape(1, d), b.reshape(1, d))


def _combine_ln_kernel(alpha, na_blocks, x_ref, y0_ref, y1_ref, gate_ref, g_ref, b_ref, oa_ref, ob_ref):
    gate = gate_ref[...]
    ff = y0_ref[...].astype(F32) * gate[:, 0:1] + y1_ref[...].astype(F32) * gate[:, 1:2]
    y = _ln_rows(alpha * x_ref[...] + ff, g_ref[...], b_ref[...])
    first = pl.program_id(0) < na_blocks

    @pl.when(first)
    def _():
        oa_ref[...] = y

    @pl.when(jnp.logical_not(first))
    def _():
        ob_ref[...] = y


def _combine_residual_ln_split(x, y0, y1, gate, g, b, alpha, n_first, bm=512):
    m, d = x.shape
    na = n_first // bm
    row = pl.BlockSpec((bm, d), lambda i: (i, 0))
    vec = pl.BlockSpec((1, d), lambda i: (0, 0))
    return pl.pallas_call(
        functools.partial(_combine_ln_kernel, alpha, na), grid=(m // bm,),
        in_specs=[row, row, row, pl.BlockSpec((bm, gate.shape[1]), lambda i: (i, 0)), vec, vec],
        out_specs=[pl.BlockSpec((bm, d), lambda i: (jnp.minimum(i, na - 1), 0)),
                   pl.BlockSpec((bm, d), lambda i: (jnp.maximum(i - na, 0), 0))],
        out_shape=[jax.ShapeDtypeStruct((n_first, d), F32), jax.ShapeDtypeStruct((m - n_first, d), F32)],
        compiler_params=_params("arbitrary"), name="moe_combine_residual_ln",
    )(x, y0, y1, gate, g.reshape(1, d), b.reshape(1, d))


def _mm_kernel(a_ref, w_ref, o_ref):
    o_ref[...] = jnp.dot(a_ref[...], w_ref[...], preferred_element_type=F32).astype(o_ref.dtype)


def _matmul(a, w, out_dtype=F32, bm=1024, bn=1024):
    m, k = a.shape
    n = w.shape[1]
    bm, bn = min(bm, m), min(bn, n)
    return pl.pallas_call(
        _mm_kernel, grid=(m // bm, n // bn),
        in_specs=[pl.BlockSpec((bm, k), lambda i, j: (i, 0)), pl.BlockSpec((k, bn), lambda i, j: (0, j))],
        out_specs=pl.BlockSpec((bm, bn), lambda i, j: (i, j)),
        out_shape=jax.ShapeDtypeStruct((m, n), out_dtype),
        compiler_params=_params("parallel", "arbitrary"), name="matmul",
    )(a, w)


def _mix_out_kernel(alpha, routed, a0_ref, a1_ref, a2_ref, w0_ref, w1_ref, w2_ref, x_ref, g_ref, b_ref, *refs):
    mix = (jnp.dot(a0_ref[...], w0_ref[...], preferred_element_type=F32)
           + jnp.dot(a1_ref[...], w1_ref[...], preferred_element_type=F32)
           + jnp.dot(a2_ref[...], w2_ref[...], preferred_element_type=F32))
    y = _ln_rows(alpha * x_ref[...] + mix, g_ref[...], b_ref[...])
    if routed:
        wr_ref, of_ref, ob_ref, lg_ref = refs
        y_hi, y_lo = _split_bf16(y)
        lg_ref[...] = (jnp.dot(y_hi, wr_ref[...], preferred_element_type=F32)
                       + jnp.dot(y_lo, wr_ref[...], preferred_element_type=F32))
    else:
        of_ref, ob_ref = refs
    of_ref[...] = y
    ob_ref[...] = y.astype(BF16)


def _mix_out_residual_ln(parts, weights, x, g, b, alpha, router_w=None, bm=512):
    m, d = x.shape
    row = pl.BlockSpec((bm, d), lambda i: (i, 0))
    vec = pl.BlockSpec((1, d), lambda i: (0, 0))
    in_specs = [pl.BlockSpec((bm, a.shape[1]), lambda i: (i, 0)) for a in parts]
    in_specs += [pl.BlockSpec(w.shape, lambda i: (0, 0)) for w in weights] + [row, vec, vec]
    args = [*parts, *weights, x, g.reshape(1, d), b.reshape(1, d)]
    out_specs = [row, row]
    out_shape = [jax.ShapeDtypeStruct((m, d), F32), jax.ShapeDtypeStruct((m, d), BF16)]
    if router_w is not None:
        in_specs.append(pl.BlockSpec(router_w.shape, lambda i: (0, 0)))
        args.append(router_w)
        out_specs.append(pl.BlockSpec((bm, router_w.shape[1]), lambda i: (i, 0)))
        out_shape.append(jax.ShapeDtypeStruct((m, router_w.shape[1]), F32))
    return pl.pallas_call(
        functools.partial(_mix_out_kernel, alpha, router_w is not None), grid=(m // bm,),
        in_specs=in_specs, out_specs=out_specs, out_shape=out_shape,
        compiler_params=_params("parallel"), name="out_proj_residual_ln",
    )(*args)


def _swiglu_acc(a_ref, w1_ref, w3_ref, w2_ref, acc_ref):
    f = pl.program_id(1)

    @pl.when(f == 0)
    def _():
        acc_ref[...] = jnp.zeros_like(acc_ref)

    a = a_ref[...]
    h1 = jnp.dot(a, w1_ref[...], preferred_element_type=F32)
    h3 = jnp.dot(a, w3_ref[...], preferred_element_type=F32)
    gated = (h1 * jax.nn.sigmoid(h1) * h3).astype(BF16)
    acc_ref[...] += jnp.dot(gated, w2_ref[...], preferred_element_type=F32)


def _ffn_kernel(alpha, a_ref, w1_ref, w3_ref, w2_ref, x_ref, g_ref, b_ref, of_ref, ob_ref, acc_ref):
    _swiglu_acc(a_ref, w1_ref, w3_ref, w2_ref, acc_ref)

    @pl.when(pl.program_id(1) == pl.num_programs(1) - 1)
    def _():
        y = _ln_rows(alpha * x_ref[...] + acc_ref[...], g_ref[...], b_ref[...])
        of_ref[...] = y
        ob_ref[...] = y.astype(BF16)


def _ffn_residual_ln(a, w1, w3, w2, x, g, b, alpha, bm=512, bf=512):
    m, d = a.shape
    dff = w1.shape[1]
    row = pl.BlockSpec((bm, d), lambda i, f: (i, 0))
    vec = pl.BlockSpec((1, d), lambda i, f: (0, 0))
    up = pl.BlockSpec((d, bf), lambda i, f: (0, f))
    return pl.pallas_call(
        functools.partial(_ffn_kernel, alpha), grid=(m // bm, dff // bf),
        in_specs=[row, up, up, pl.BlockSpec((bf, d), lambda i, f: (f, 0)), row, vec, vec],
        out_specs=[row, row],
        out_shape=[jax.ShapeDtypeStruct((m, d), F32), jax.ShapeDtypeStruct((m, d), BF16)],
        scratch_shapes=[pltpu.VMEM((bm, d), F32)],
        compiler_params=_params("parallel", "arbitrary"), name="ffn_residual_ln",
    )(a, w1, w3, w2, x, g.reshape(1, d), b.reshape(1, d))


def _moe_ffn_kernel(blk_e_ref, used_ref, a_ref, w1_ref, w3_ref, w2_ref, o_ref, acc_ref):
    del blk_e_ref
    f = pl.program_id(1)

    @pl.when(pl.program_id(0) < used_ref[0])
    def _():
        @pl.when(f == 0)
        def _():
            acc_ref[...] = jnp.zeros_like(acc_ref)

        a = a_ref[...]
        h1 = jnp.dot(a, w1_ref[...], preferred_element_type=F32)
        h3 = jnp.dot(a, w3_ref[...], preferred_element_type=F32)
        gated = (h1 * jax.nn.sigmoid(h1) * h3).astype(BF16)
        acc_ref[...] += jnp.dot(gated, w2_ref[...], preferred_element_type=F32)

        @pl.when(f == pl.num_programs(1) - 1)
        def _():
            o_ref[...] = acc_ref[...].astype(o_ref.dtype)


def _cast_kernel(x_ref, o_ref):
    o_ref[...] = x_ref[...].astype(o_ref.dtype)


def _cast_bf16(w, rows=128):
    e, r, c = w.shape
    rows = min(rows, r)
    blk = pl.BlockSpec((None, rows, c), lambda i, j: (i, j, 0))
    return pl.pallas_call(
        _cast_kernel, grid=(e, r // rows), in_specs=[blk], out_specs=blk,
        out_shape=jax.ShapeDtypeStruct(w.shape, BF16),
        compiler_params=_params("parallel", "parallel"), name="cast_bf16",
    )(w)


def _moe_grouped_ffn(blk_e, n_used, xb, w1, w3, w2, bm, bf=512):
    cap, d = xb.shape
    dff = w1.shape[2]
    nf = dff // bf

    def blk(i, u):
        return jnp.minimum(i, u[0] - 1)

    def ftile(i, f, u):
        return jnp.where(i < u[0], f, nf - 1)

    row = pl.BlockSpec((bm, d), lambda i, f, e, u: (blk(i, u), 0))
    up = pl.BlockSpec((None, d, bf), lambda i, f, e, u: (e[blk(i, u)], 0, ftile(i, f, u)))
    down = pl.BlockSpec((None, bf, d), lambda i, f, e, u: (e[blk(i, u)], ftile(i, f, u), 0))
    grid_spec = pltpu.PrefetchScalarGridSpec(
        num_scalar_prefetch=2, grid=(cap // bm, nf), in_specs=[row, up, up, down],
        out_specs=row, scratch_shapes=[pltpu.VMEM((bm, d), F32)])
    return pl.pallas_call(
        _moe_ffn_kernel, grid_spec=grid_spec, out_shape=jax.ShapeDtypeStruct((cap, d), BF16),
        compiler_params=_params("arbitrary", "arbitrary"), name="moe_grouped_ffn",
    )(blk_e, n_used, xb, w1, w3, w2)


def _moe_route(logits, bm):
    n = logits.shape[0]
    top_val, top_idx = lax.top_k(logits, TOP_K)
    gate = jax.nn.softmax(top_val, axis=-1)
    nk = n * TOP_K
    assert nk % bm == 0
    experts = jnp.arange(N_EXPERTS, dtype=jnp.int32)
    flat_e = top_idx.reshape(nk).astype(jnp.int32)
    one_hot = flat_e[:, None] == experts[None, :]
    order = jnp.argsort(flat_e, stable=True).astype(jnp.int32)
    rank = jnp.argsort(order).astype(jnp.int32)
    counts = jnp.sum(one_hot.astype(jnp.int32), 0)
    padded = (counts + bm - 1) // bm * bm
    start = jnp.cumsum(counts) - counts
    pend = jnp.cumsum(padded)
    shift = jnp.sum(jnp.where(one_hot, (pend - padded - start)[None, :], 0), 1)
    dest = (rank + shift).reshape(n, TOP_K)
    n_blocks = nk // bm + N_EXPERTS
    blk_e = jnp.minimum(jnp.searchsorted(pend, jnp.arange(n_blocks, dtype=jnp.int32) * bm, side='right'),
                        N_EXPERTS - 1).astype(jnp.int32)
    pad_rank = jnp.arange(bm, dtype=jnp.int32)[None, :]
    pad_key = jnp.where(pad_rank < (padded - counts)[:, None], 2 * experts[:, None] + 1, 2 * N_EXPERTS)
    keys = jnp.concatenate([2 * flat_e, pad_key.reshape(-1)])
    toks = jnp.concatenate([jnp.arange(nk, dtype=jnp.int32) // TOP_K, jnp.zeros((N_EXPERTS * bm,), jnp.int32)])
    _, slot_tok = lax.sort((keys, toks), num_keys=1, is_stable=True)
    n_used = (pend[-1] // bm).astype(jnp.int32).reshape(1)
    return gate, dest, slot_tok, blk_e, n_used


def _moe_dispatch_ffn(logits, h_bf16, w1, w3, w2, bm=1024):
    gate, dest, slot_tok, blk_e, n_used = _moe_route(logits, bm)
    yb = _moe_grouped_ffn(blk_e, n_used, h_bf16[slot_tok], w1, w3, w2, bm)
    return yb[dest[:, 0]], yb[dest[:, 1]], gate


def _gmlp_kernel(groups, pu_ref, pv_ref, g_ref, b_ref, ws_ref, bs_ref, o_ref):
    u = jax.nn.gelu(pu_ref[...])
    v = _ln_rows(jax.nn.gelu(pv_ref[...]), g_ref[...], b_ref[...])
    for n in range(u.shape[0] // GM_CHUNK):
        rows = slice(n * GM_CHUNK, (n + 1) * GM_CHUNK)
        for grp in range(groups):
            cols = slice(grp * HEAD_DIM, (grp + 1) * HEAD_DIM)
            s = _dot3(ws_ref[grp], v[rows, cols]) + bs_ref[grp]
            o_ref[rows, cols] = (u[rows, cols] * s).astype(o_ref.dtype)


def _gmlp(p, col_u, col_v, gm_w, ln_g, ln_b, ws, bs, bt=512):
    n = p.shape[0]
    groups = ws.shape[0]
    bsb = jnp.broadcast_to(bs[:, :, None], (groups, GM_CHUNK, HEAD_DIM)).astype(F32)
    vec = pl.BlockSpec((1, gm_w), lambda i: (0, 0))
    full3 = pl.BlockSpec((groups, GM_CHUNK, HEAD_DIM), lambda i: (0, 0, 0))
    return pl.pallas_call(
        functools.partial(_gmlp_kernel, groups), grid=(n // bt,),
        in_specs=[pl.BlockSpec((bt, gm_w), lambda i: (i, col_u)), pl.BlockSpec((bt, gm_w), lambda i: (i, col_v)),
                  vec, vec, pl.BlockSpec((groups, GM_CHUNK, GM_CHUNK), lambda i: (0, 0, 0)), full3],
        out_specs=pl.BlockSpec((bt, gm_w), lambda i: (i, 0)),
        out_shape=jax.ShapeDtypeStruct((n, gm_w), BF16),
        compiler_params=_params("parallel"), name="gmlp",
    )(p, p, ln_g.reshape(1, gm_w), ln_b.reshape(1, gm_w), ws.astype(F32), bsb)


def _hgrn2_constants(c, reverse):
    n_lvl = int(math.log2(c))
    t = np.arange(c)
    ms, ws = [], []
    for lvl in range(n_lvl):
        m = c >> (lvl + 1)
        mid = (t // (2 * m)) * (2 * m) + m
        upper = t >= mid
        mat = np.zeros((c, c), np.float32)
        for r in range(c):
            if upper[r]:
                mat[r, mid[r]:r + 1] = 1.0
            else:
                mat[r, r + 1:mid[r]] = 1.0
        same = (t[:, None] // (2 * m)) == (t[None, :] // (2 * m))
        ws.append((same & upper[:, None] & ~upper[None, :]).astype(np.float32))
        ms.append(mat)
    ws.append(np.eye(c, dtype=np.float32))
    ms.append(np.tril(np.ones((c, c), np.float32)))
    ms.append(np.triu(np.ones((c, c), np.float32), 1))
    if reverse:
        ms = [a[::-1, ::-1] for a in ms]
        ws = [a[::-1, ::-1] for a in ws]
    return np.concatenate(ms, 0), np.stack(ws, 0)


def _hgrn2_kernel(heads, n_lvl, carry_row, final, reset_ref, *refs):
    if final:
        (q_ref, z_ref, i_ref, og_ref, oo_ref, loglb_ref, log1mlb_ref, onemlb_ref, ng_ref, m_ref, w_ref,
         out_ref, st_ref, e_ref) = refs
    else:
        (q_ref, z_ref, i_ref, loglb_ref, log1mlb_ref, onemlb_ref, m_ref, w_ref, out_ref, st_ref, e_ref) = refs
    n_streams, c = q_ref.shape[0], q_ref.shape[1]
    rows_e = m_ref.shape[0]
    step = pl.program_id(0)

    for s in range(n_streams):
        @pl.when(reset_ref[step * n_streams + s] == 1)
        def _():
            st_ref[s * heads:(s + 1) * heads] = jnp.zeros((heads, HEAD_DIM, HEAD_DIM), F32)

    for s in range(n_streams):
        z = z_ref[s]
        e = jnp.exp(-jnp.abs(z))
        r = 1.0 / (1.0 + e)
        log_sig = jnp.minimum(z, 0.0) - jnp.log(1.0 + e)
        sig_neg = jnp.where(z >= 0.0, e * r, r)
        a = loglb_ref[...]
        b = log1mlb_ref[...] + log_sig
        g = jnp.maximum(a, b) + jnp.log(1.0 + jnp.exp(-jnp.abs(a - b)))
        k = onemlb_ref[...] * sig_neg
        q = q_ref[s]
        qs = q * jax.nn.sigmoid(q)
        v = i_ref[s].astype(BF16)

        g_hi, g_lo = _split_bf16(g)
        m_all = m_ref[...]
        e0 = s * rows_e
        e_ref[e0:e0 + rows_e, :] = (jnp.dot(m_all, g_hi, preferred_element_type=F32)
                                    + jnp.dot(m_all, g_lo, preferred_element_type=F32))

        for h in range(heads):
            cols = slice(h * HEAD_DIM, (h + 1) * HEAD_DIM)
            qh, kh, vh = qs[:, cols], k[:, cols], v[:, cols]
            amat = w_ref[n_lvl] * lax.dot_general(qh.astype(BF16), kh.astype(BF16), NT_DIMS,
                                                  preferred_element_type=F32)
            for lvl in range(n_lvl):
                ex = jnp.exp(e_ref[e0 + lvl * c:e0 + (lvl + 1) * c, cols])
                amat = amat + w_ref[lvl] * lax.dot_general((qh * ex).astype(BF16), (kh * ex).astype(BF16),
                                                           NT_DIMS, preferred_element_type=F32)
            e_in = e_ref[e0 + n_lvl * c:e0 + (n_lvl + 1) * c, cols]
            q_in = (qh * jnp.exp(e_in)).astype(BF16)
            k_st = (kh * jnp.exp(e_ref[e0 + (n_lvl + 1) * c:e0 + (n_lvl + 2) * c, cols])).astype(BF16)
            dec = jnp.exp(e_in[carry_row:carry_row + 1, :])
            st = st_ref[s * heads + h]
            o = (lax.dot_general(q_in, st.astype(BF16), NT_DIMS, preferred_element_type=F32)
                 + jnp.dot(amat.astype(BF16), vh, preferred_element_type=F32))
            st_ref[s * heads + h] = st * dec + lax.dot_general(vh, k_st, TN_DIMS, preferred_element_type=F32)
            if final:
                o = o + oo_ref[s, :, cols]
                o = o * lax.rsqrt(jnp.mean(o * o, -1, keepdims=True) + RMS_EPS)
                og = og_ref[s, :, cols]
                out_ref[s, :, cols] = (o * ng_ref[:, cols] * (og * jax.nn.sigmoid(og))).astype(out_ref.dtype)
            else:
                out_ref[s, :, cols] = o


def _hgrn2_pass(p3, cols, hg_w, z_col, reverse, reset, lbs, extra, c):
    n_streams, n = p3.shape[0], p3.shape[1]
    nc = n // c
    heads = hg_w // HEAD_DIM
    n_lvl = int(math.log2(c))
    m_np, w_np = _hgrn2_constants(c, reverse)
    final = extra is not None

    def cmap(i):
        return (nc - 1 - i) if reverse else i

    def tok(col):
        return pl.BlockSpec((n_streams, c, hg_w), lambda i, r: (0, cmap(i), col))

    vec = pl.BlockSpec((1, hg_w), lambda i, r: (0, 0))
    in_specs = [tok(cols["q"]), tok(z_col), tok(cols["i"])]
    args = [p3, p3, p3]
    if final:
        o_other, norm_g = extra
        in_specs += [tok(cols["og"]), tok(0)]
        args += [p3, o_other]
    in_specs += [vec, vec, vec]
    args += list(lbs)
    if final:
        in_specs += [vec]
        args += [norm_g.reshape(1, hg_w).astype(F32)]
    in_specs += [pl.BlockSpec(m_np.shape, lambda i, r: (0, 0)), pl.BlockSpec(w_np.shape, lambda i, r: (0, 0, 0))]
    args += [jnp.asarray(m_np, BF16), jnp.asarray(w_np, F32)]
    grid_spec = pltpu.PrefetchScalarGridSpec(
        num_scalar_prefetch=1, grid=(nc,), in_specs=in_specs, out_specs=tok(0),
        scratch_shapes=[pltpu.VMEM((n_streams * heads, HEAD_DIM, HEAD_DIM), F32),
                        pltpu.VMEM((n_streams * m_np.shape[0], hg_w), F32)])
    order = np.arange(nc)[::-1] if reverse else np.arange(nc)
    reset_steps = jnp.asarray(np.asarray(reset, np.int32)[order].reshape(-1))
    return pl.pallas_call(
        functools.partial(_hgrn2_kernel, heads, n_lvl, 0 if reverse else c - 1, final), grid_spec=grid_spec,
        out_shape=jax.ShapeDtypeStruct((n_streams, n, hg_w), BF16 if final else F32),
        compiler_params=_params("arbitrary"), name="hgrn2_fwd" if final else "hgrn2_bwd",
    )(reset_steps, *args)


def _hgrn2(p, cols, hg_w, lb, norm_g, seq_lens, n_streams=2, c=HG_CHUNK):
    lb = lb.astype(F32).reshape(1, hg_w)
    lbs = (jnp.log(lb), jnp.log1p(-lb), 1.0 - lb)
    starts = np.cumsum([0] + list(seq_lens))
    n = int(starts[-1])
    per = n // n_streams
    assert n == p.shape[0] and all(s * per in starts for s in range(n_streams)), "a sequence straddles a stream cut"
    first = np.zeros(n // c, np.int32)
    last = np.zeros(n // c, np.int32)
    first[starts[:-1] // c] = 1
    last[starts[1:] // c - 1] = 1
    first = first.reshape(n_streams, per // c).T
    last = last.reshape(n_streams, per // c).T
    p3 = p.reshape(n_streams, per, p.shape[1])
    o_b = _hgrn2_pass(p3, cols, hg_w, cols["zb"], True, last, lbs, None, c)
    return _hgrn2_pass(p3, cols, hg_w, cols["zf"], False, first, lbs, (o_b, norm_g), c).reshape(n, hg_w)


def _dft_matrices(t):
    n = 2 * t
    f = jnp.arange(t, dtype=jnp.int32)
    ang = (2.0 * math.pi / n) * ((f[:, None] * f[None, :]) % n).astype(F32)
    cos, sin = jnp.cos(ang), jnp.sin(ang)
    alt = (1 - 2 * (f % 2)).astype(F32)
    first = f == 0
    fwd = jnp.stack([cos, jnp.where(first[:, None], alt[None, :], -sin)], 0).astype(BF16)
    wgt = jnp.where(first, 1.0, 2.0)[None, :] / n
    inv = jnp.stack([cos * wgt, jnp.where(first[None, :], alt[:, None] / n, -sin * wgt)], 0).astype(BF16)
    return fwd, inv


def _shortconv_kernel(flags_ref, x_ref, prev_ref, next_ref, w_ref, b_ref, o_ref):
    i = pl.program_id(0)
    x = x_ref[...]
    t = x.shape[0]
    rows = lax.broadcasted_iota(jnp.int32, x.shape, 0)
    has_prev = flags_ref[2 * i].astype(F32)
    has_next = flags_ref[2 * i + 1].astype(F32)
    up = jnp.where(rows == 0, prev_ref[7:8, :] * has_prev, pltpu.roll(x, 1, 0))
    dn = jnp.where(rows == t - 1, next_ref[0:1, :] * has_next, pltpu.roll(x, t - 1, 0))
    o_ref[...] = up * w_ref[0:1, :] + x * w_ref[1:2, :] + dn * w_ref[2:3, :] + b_ref[...]


def _shortconv(p, width, conv_w, conv_b, blk_flags, t=HY_BLOCK, ct=768):
    n = p.shape[0]
    nb = n // t
    r8 = t // 8
    grid_spec = pltpu.PrefetchScalarGridSpec(
        num_scalar_prefetch=1, grid=(nb, width // ct),
        in_specs=[pl.BlockSpec((t, ct), lambda i, j, f: (i, j)),
                  pl.BlockSpec((8, ct), lambda i, j, f: (jnp.maximum(i * r8 - 1, 0), j)),
                  pl.BlockSpec((8, ct), lambda i, j, f: (jnp.minimum((i + 1) * r8, nb * r8 - 1), j)),
                  pl.BlockSpec((3, ct), lambda i, j, f: (0, j)), pl.BlockSpec((1, ct), lambda i, j, f: (0, j))],
        out_specs=pl.BlockSpec((t, ct), lambda i, j, f: (i, j)))
    return pl.pallas_call(
        _shortconv_kernel, grid_spec=grid_spec, out_shape=jax.ShapeDtypeStruct((n, width), F32),
        compiler_params=_params("parallel", "parallel"), name="hyena_shortconv",
    )(blk_flags, p, p, p, conv_w.astype(F32), conv_b.reshape(1, width).astype(F32))


def _filter_kernel(l_total, feats_ref, w1_ref, b1_ref, w2_ref, b2_ref, w3_ref, b3_ref, fr_ref, wo_ref, dl_ref,
                   h_ref, sum_ref):
    del l_total
    i = pl.program_id(0)
    bt = feats_ref.shape[0]
    half_lanes = LANES // 2
    fr = fr_ref[...]
    feats = feats_ref[...]
    h = jnp.sin(fr * (_dot3(feats, w1_ref[...]) + b1_ref[...]))
    h = jnp.sin(fr * (_dot3(h, w2_ref[...]) + b2_ref[...]))
    h = jnp.sin(fr * (_dot3(h, w3_ref[...]) + b3_ref[...]))
    reps = wo_ref.shape[2] // dl_ref.shape[1]
    row = lax.broadcasted_iota(jnp.int32, (bt, 1), 0) + i * bt
    total = jnp.zeros((1, wo_ref.shape[2]), F32)
    for side in range(2):
        window = jnp.exp(-feats[:, side * half_lanes:side * half_lanes + 1] * dl_ref[...])
        out = _dot3(h, wo_ref[side]) * jnp.concatenate([window] * reps, axis=1)
        if side == 1:
            out = jnp.where(row == 0, 0.0, out)
        h_ref[side] = out
        total = total + jnp.sum(jnp.abs(out), axis=0, keepdims=True)

    @pl.when(i == 0)
    def _():
        sum_ref[...] = jnp.zeros_like(sum_ref)

    sum_ref[...] += total


def _hyena_filter(l, w1, b1, w2, b2, w3, b3, freq, w_out, hy_w, bt=512):
    hid = w2.shape[0]
    half_lanes = LANES // 2
    assert hid <= half_lanes and w1.shape[0] <= half_lanes
    half = HY_ORDER * hy_w
    bands = jnp.linspace(1e-4, HY_POS_BANDS - 1, HY_POS_BANDS, dtype=F32)

    def features(pos):
        tt = pos / max(l - 1, 1)
        ang = (2.0 * math.pi / l) * pos[:, None] * bands[None, :]
        f = jnp.concatenate([tt[:, None], jnp.cos(ang), -jnp.sin(ang)], -1)
        return jnp.pad(f, ((0, 0), (0, half_lanes - f.shape[1])))

    def twice_diag(w):
        wp = jnp.pad(w.astype(F32), ((0, half_lanes - w.shape[0]), (0, half_lanes - w.shape[1])))
        z = jnp.zeros_like(wp)
        return jnp.concatenate([jnp.concatenate([wp, z], 1), jnp.concatenate([z, wp], 1)], 0)

    def twice_vec(a):
        ap = jnp.pad(a.astype(F32), (0, half_lanes - a.shape[0]))
        return jnp.concatenate([ap, ap]).reshape(1, LANES)

    pos = jnp.arange(l, dtype=F32)
    feats = jnp.concatenate([features(pos), features(l - pos)], 1)
    wo = w_out.astype(F32).reshape(hid, HY_ORDER, 2, hy_w).transpose(2, 0, 1, 3).reshape(2, hid, half)
    wo = jnp.pad(wo, ((0, 0), (0, half_lanes - hid), (0, 0)))
    zero = jnp.zeros_like(wo[0])
    wop = jnp.stack([jnp.concatenate([wo[0], zero], 0), jnp.concatenate([zero, wo[1]], 0)], 0)
    deltas = jnp.abs(jnp.linspace(math.log(HY_DECAY_TARGET) / HY_SLOW_DECAY,
                                  math.log(HY_DECAY_TARGET) / HY_FAST_DECAY, hy_w, dtype=F32)).reshape(1, hy_w)
    bt = min(bt, l)
    sq = pl.BlockSpec((LANES, LANES), lambda i: (0, 0))
    vec = pl.BlockSpec((1, LANES), lambda i: (0, 0))
    taps, abs_sum = pl.pallas_call(
        functools.partial(_filter_kernel, l), grid=(l // bt,),
        in_specs=[pl.BlockSpec((bt, LANES), lambda i: (i, 0)), sq, vec, sq, vec, sq, vec, vec,
                  pl.BlockSpec((2, LANES, half), lambda i: (0, 0, 0)),
                  pl.BlockSpec((1, hy_w), lambda i: (0, 0))],
        out_specs=[pl.BlockSpec((2, bt, half), lambda i: (0, i, 0)), pl.BlockSpec((1, half), lambda i: (0, 0))],
        out_shape=[jax.ShapeDtypeStruct((2, l, half), F32), jax.ShapeDtypeStruct((1, half), F32)],
        compiler_params=_params("arbitrary"), name="hyena_filter",
    )(feats, twice_diag(w1), twice_vec(b1), twice_diag(w2), twice_vec(b2), twice_diag(w3), twice_vec(b3),
      twice_vec(freq), wop, deltas)
    return taps.reshape(2 * l, half), abs_sum


def _dft_cols(t, width):
    return width if t <= 1024 else 256


def _dft_kernel(f_ref, x_ref, o_ref):
    x = x_ref[...].astype(BF16)
    o_ref[0] = jnp.dot(f_ref[0], x, preferred_element_type=F32).astype(o_ref.dtype)
    o_ref[1] = jnp.dot(f_ref[1], x, preferred_element_type=F32).astype(o_ref.dtype)


def _block_dft(fwd, x, col_off, width, out_dtype, t=HY_BLOCK, ct=256):
    nseg = x.shape[0] // t
    ct = min(ct, width)
    return pl.pallas_call(
        _dft_kernel, grid=(nseg, width // ct),
        in_specs=[pl.BlockSpec((2, t, t), lambda s, j: (0, 0, 0)),
                  pl.BlockSpec((t, ct), lambda s, j: (s, j + col_off))],
        out_specs=pl.BlockSpec((None, 2, t, ct), lambda s, j: (s, 0, 0, j)),
        out_shape=jax.ShapeDtypeStruct((nseg, 2, t, width), out_dtype),
        compiler_params=_params("parallel", "parallel"), name="hyena_block_dft",
    )(fwd, x)


def _segment_index(d, nb):
    return d if d >= 0 else 2 * nb + d


def _hyena_filter_spectra(l, fwd, filt_w, hy_w, t=HY_BLOCK):
    nb = l // t
    taps, abs_sum = _hyena_filter(l, *filt_w, hy_w)
    half = taps.shape[1]
    seg_spec = _block_dft(fwd, taps, 0, half, F32, t, _dft_cols(t, hy_w))
    first_rows = taps.reshape(2 * nb, t, half)[:, 0, :]
    seg0 = jnp.stack([first_rows[_segment_index(d - 1, nb)] for d in range(-(nb - 1), nb)], 0)
    return seg_spec, seg0.reshape(2 * nb - 1, 1, half), 1.0 / abs_sum


MIX_ROWS = 16


def _mix_kernel(nb, z_ref, a_ref, s0_ref, inv_ref, *refs):
    o_ref, k_ref = refs[-2], refs[-1]
    ft, ct = z_ref.shape[2], z_ref.shape[3]
    first_tile = pl.program_id(1) == 0

    tile_rows = lax.broadcasted_iota(jnp.int32, (ft, ct), 0)
    sgn = (1 - 2 * (tile_rows & 1)).astype(F32)
    real_row = jnp.logical_and(tile_rows == 0, first_tile)
    inv = inv_ref[...]
    for slot in range(2 * nb - 1):
        d = slot - (nb - 1)
        ia, ib = _segment_index(d, nb), _segment_index(d - 1, nb)
        s0 = s0_ref[slot]
        k_ref[slot, 0] = (a_ref[ia, 0] + sgn * (a_ref[ib, 0] - s0)) * inv
        k_ref[slot, 1] = (a_ref[ia, 1] + sgn * (a_ref[ib, 1] - jnp.where(real_row, s0, 0.0))) * inv

    n_out = 2 if nb % 2 == 0 else 1

    def out_rows(i, r0, masked):
        rows = pl.ds(r0, MIX_ROWS)
        acc = [[jnp.zeros((MIX_ROWS, ct), F32) for _ in range(2)] for _ in range(n_out)]
        if masked:
            row0 = jnp.logical_and(lax.broadcasted_iota(jnp.int32, (MIX_ROWS, ct), 0) == 0, first_tile)
        d_hi = i + (n_out - 1) + (nb - 1)
        k_prev = (k_ref[d_hi, 0, rows, :], k_ref[d_hi, 1, rows, :]) if n_out == 2 else None
        for j in range(nb):
            d = i - j + (nb - 1)
            zt, zb = z_ref[j, 0, rows, :], z_ref[j, 1, rows, :]
            k_cur = (k_ref[d, 0, rows, :], k_ref[d, 1, rows, :])
            for o, (kt, kb) in enumerate((k_cur, k_prev)[:n_out]):
                bb = zb * kb
                if masked:
                    acc[o][0] = acc[o][0] + (zt * kt - jnp.where(row0, 0.0, bb))
                    acc[o][1] = acc[o][1] + jnp.where(row0, bb, zt * kb + zb * kt)
                else:
                    acc[o][0] = acc[o][0] + (zt * kt - bb)
                    acc[o][1] = acc[o][1] + (zt * kb + zb * kt)
            k_prev = k_cur
        for o in range(n_out):
            o_ref[i + o, 0, rows, :] = acc[o][0].astype(o_ref.dtype)
            o_ref[i + o, 1, rows, :] = acc[o][1].astype(o_ref.dtype)

    def all_outputs(r0, masked):
        def body(ip, carry):
            out_rows(ip * n_out, r0, masked)
            return carry
        lax.fori_loop(0, nb // n_out, body, 0)

    all_outputs(0, True)

    def chunk(r, carry):
        all_outputs(pl.multiple_of(r * MIX_ROWS, MIX_ROWS), False)
        return carry

    lax.fori_loop(1, ft // MIX_ROWS, chunk, 0)


def _hyena_mix(zspec, kspec, k_col, n_seq, nb, blk_off, prev, ct=256):
    seg_spec, seg0, inv_norm = kspec
    nblk, _, t, width = zspec.shape
    ft = max(MIX_ROWS, min(t, 2048 // nb))
    nct = width // ct
    assert blk_off % nb == 0 and t % ft == 0
    s_off = blk_off // nb
    zblk = pl.BlockSpec((nb, 2, ft, ct), lambda s, fi, j: (s + s_off, 0, fi, j))
    in_specs = [zblk,
                pl.BlockSpec((2 * nb, 2, ft, ct), lambda s, fi, j: (0, 0, fi, j + k_col * nct)),
                pl.BlockSpec((2 * nb - 1, 1, ct), lambda s, fi, j: (0, 0, j + k_col * nct)),
                pl.BlockSpec((1, ct), lambda s, fi, j: (0, j + k_col * nct))]
    args = [zspec, seg_spec, seg0, inv_norm]
    aliases = {}
    if prev is not None:
        in_specs.append(pl.BlockSpec(memory_space=pl.ANY))
        args.append(prev)
        aliases = {4: 0}
    return pl.pallas_call(
        functools.partial(_mix_kernel, nb), grid=(n_seq, t // ft, nct),
        in_specs=in_specs, out_specs=zblk,
        out_shape=jax.ShapeDtypeStruct(zspec.shape, BF16), input_output_aliases=aliases,
        scratch_shapes=[pltpu.VMEM((2 * nb - 1, 2, ft, ct), F32)],
        compiler_params=_params("parallel", "parallel", "parallel"), name="hyena_mix",
    )(*args)


def _inverse_kernel(y_ref, fi_ref, zin_ref, gate_ref, bias_ref, o_ref):
    y = (jnp.dot(fi_ref[0], y_ref[0], preferred_element_type=F32)
         + jnp.dot(fi_ref[1], y_ref[1], preferred_element_type=F32))
    o_ref[...] = (gate_ref[...] * (y + zin_ref[...] * bias_ref[...])).astype(o_ref.dtype)


def _hyena_inverse(yspec, inv, zin, zin_col, gate, gate_col, bias, out_dtype, ct=256):
    nblk, _, t, width = yspec.shape
    nct = width // ct
    return pl.pallas_call(
        _inverse_kernel, grid=(nblk, nct),
        in_specs=[pl.BlockSpec((None, 2, t, ct), lambda i, j: (i, 0, 0, j)),
                  pl.BlockSpec((2, t, t), lambda i, j: (0, 0, 0), pipeline_mode=pl.Buffered(1)),
                  pl.BlockSpec((t, ct), lambda i, j: (i, j + zin_col * nct)),
                  pl.BlockSpec((t, ct), lambda i, j: (i, j + gate_col * nct)),
                  pl.BlockSpec((1, ct), lambda i, j: (0, j))],
        out_specs=pl.BlockSpec((t, ct), lambda i, j: (i, j)),
        out_shape=jax.ShapeDtypeStruct((nblk * t, width), out_dtype),
        compiler_params=_params("parallel", "arbitrary"), name="hyena_inverse",
    )(yspec, inv, zin, gate, bias)


def _hyena(p, hy_w, groups, conv_w, conv_b, filt_w, bias, t=HY_BLOCK):
    flags = []
    for n_seq, l in groups:
        nb = l // t
        flags += [int(bi > 0) if side == 0 else int(bi < nb - 1)
                  for _ in range(n_seq) for bi in range(nb) for side in range(2)]
    fwd, inv = _dft_matrices(t)
    u = _shortconv(p, 3 * hy_w, conv_w, conv_b, jnp.asarray(np.asarray(flags, np.int32)), t, ct=hy_w)
    kspecs = [_hyena_filter_spectra(l, fwd, filt_w, hy_w, t) for _, l in groups]
    bias = bias.astype(F32)
    ct = _dft_cols(t, hy_w)
    z, z_col = u, 2
    for o in range(HY_ORDER):
        zspec = _block_dft(fwd, z, z_col * (hy_w // ct), hy_w, F32, t, ct)
        yspec, blk = None, 0
        for (n_seq, l), kspec in zip(groups, kspecs):
            yspec = _hyena_mix(zspec, kspec, o, n_seq, l // t, blk, yspec)
            blk += n_seq * (l // t)
        last = o == HY_ORDER - 1
        z = _hyena_inverse(yspec, inv, z, z_col, u, o, bias[o:o + 1], BF16 if last else F32, ct)
        z_col = 0
    return z


def kernel(x_prompt, x_sample, ln_in_g, ln_in_b, w_in, hy_conv_w, hy_conv_b, hy_pos_w1, hy_pos_b1, hy_pos_w2, hy_pos_b2, hy_pos_w3, hy_pos_b3, hy_sin_freq, hy_pos_wout, hy_bias, gm_ln_g, gm_ln_b, gm_ws, gm_bs, hg_lb_raw, hg_norm_g, w_out, ln1_g, ln1_b, ln2_g, ln2_b, ffn_w1, ffn_w3, ffn_w2, moe_router_w, moe_router_b, moe_w1, moe_w3, moe_w2):
    depth, d_model, in_w = w_in.shape
    hy_w = hy_bias.shape[-1]
    gm_w = gm_ln_g.shape[-1]
    hg_w = hg_norm_g.shape[-1]
    o1 = 3 * hy_w
    o2 = o1 + 2 * gm_w
    alpha = (2 * depth) ** 0.25
    groups = [(x_prompt.shape[0], x_prompt.shape[1]), (x_sample.shape[0], x_sample.shape[1])]
    seq_lens = [l for n_seq, l in groups for _ in range(n_seq)]
    n_prompt = x_prompt.shape[0] * x_prompt.shape[1]
    hg_off = o1 // hg_w
    hg_cols = {"q": hg_off, "zf": hg_off + 1, "zb": hg_off + 2, "i": hg_off + 3, "og": hg_off + 4}
    gm_off = (o1 + 5 * hg_w) // gm_w

    lb_all = jnp.cumsum(jax.nn.softmax(hg_lb_raw.astype(F32), axis=0), axis=0)
    lb_all = lb_all - lb_all[:1]

    x, xb = _layernorm_pair(x_prompt.reshape(-1, d_model), x_sample.reshape(-1, d_model), ln_in_g, ln_in_b)
    for l in range(depth):
        w_l = jnp.concatenate([w_in[l][:, :o1], w_in[l][:, o2:], w_in[l][:, o1:o2]], 1).astype(BF16)
        p = _matmul(xb, w_l)
        filt_w = (hy_pos_w1[l], hy_pos_b1[l], hy_pos_w2[l], hy_pos_b2[l], hy_pos_w3[l], hy_pos_b3[l],
                  hy_sin_freq[l], hy_pos_wout[l])
        y_hy = _hyena(p, hy_w, groups, hy_conv_w[l], hy_conv_b[l], filt_w, hy_bias[l])
        y_hg = _hgrn2(p, hg_cols, hg_w, lb_all[l], hg_norm_g[l], seq_lens)
        y_gm = _gmlp(p, gm_off, gm_off + 1, gm_w, gm_ln_g[l], gm_ln_b[l], gm_ws[l], gm_bs[l])
        wo = w_out[l].astype(BF16)
        parts = [y_hy, y_gm, y_hg]
        wo_parts = [wo[:hy_w], wo[hy_w:hy_w + gm_w], wo[hy_w + gm_w:]]
        j = l // 2
        if l % 2 == 0:
            x, xb = _mix_out_residual_ln(parts, wo_parts, x, ln1_g[l], ln1_b[l], alpha)
            x, xb = _ffn_residual_ln(xb, ffn_w1[j].astype(BF16), ffn_w3[j].astype(BF16), ffn_w2[j].astype(BF16),
                                     x, ln2_g[l], ln2_b[l], alpha)
        else:
            n_exp = moe_router_w.shape[-1]
            wr = jnp.pad(moe_router_w[j].astype(BF16), ((0, 0), (0, LANES - n_exp)))
            x, xb, logits = _mix_out_residual_ln(parts, wo_parts, x, ln1_g[l], ln1_b[l], alpha, wr)
            logits = logits[:, :n_exp] + moe_router_b[j].astype(F32)
            y0, y1, gate = _moe_dispatch_ffn(logits, xb, _cast_bf16(moe_w1[j]), _cast_bf16(moe_w3[j]),
                                             _cast_bf16(moe_w2[j]))
            xa, xs = _combine_residual_ln_split(x, y0, y1, gate, ln2_g[l], ln2_b[l], alpha, n_prompt)
            if l == depth - 1:
                return (xa.reshape(x_prompt.shape), xs.reshape(x_sample.shape))
            x = jnp.concatenate([xa, xs], 0)
            xb = x.astype(BF16)
    return (x[:n_prompt].reshape(x_prompt.shape), x[n_prompt:].reshape(x_sample.shape))
```

```python
import functools
import math

import numpy as np
import jax
import jax.numpy as jnp
from jax import lax
from jax.experimental import pallas as pl
from jax.experimental.pallas import tpu as pltpu

HEAD_DIM = 128
HY_ORDER = 2
HY_POS_BANDS = 16
HY_FAST_DECAY = 0.3
HY_SLOW_DECAY = 1.5
HY_DECAY_TARGET = 1e-2
GM_CHUNK = 128
N_EXPERTS = 8
TOP_K = 2
LN_EPS = 1e-5
RMS_EPS = 1e-6

HY_BLOCK = 1024
HG_CHUNK = 128
LANES = 128

V7X_VMEM_BYTES = 64 * 1024 * 1024
VMEM_LIMIT = V7X_VMEM_BYTES - 8 * 1024 * 1024

F32 = jnp.float32
BF16 = jnp.bfloat16
NT_DIMS = (((1,), (1,)), ((), ()))
TN_DIMS = (((0,), (0,)), ((), ()))


def _params(*sem):
    return pltpu.CompilerParams(dimension_semantics=sem, vmem_limit_bytes=VMEM_LIMIT)


def _ln_rows(x, g, b):
    mu = jnp.mean(x, -1, keepdims=True)
    xc = x - mu
    var = jnp.mean(xc * xc, -1, keepdims=True)
    return xc * lax.rsqrt(var + LN_EPS) * g + b


def _split_bf16(x):
    hi = x.astype(BF16)
    lo = (x - hi.astype(F32)).astype(BF16)
    return hi, lo


def _dot3(a, b):
    ah, al = _split_bf16(a)
    bh, bl = _split_bf16(b)
    return (jnp.dot(ah, bh, preferred_element_type=F32) + jnp.dot(ah, bl, preferred_element_type=F32)
            + jnp.dot(al, bh, preferred_element_type=F32))


def _ln2_kernel(na_blocks, xa_ref, xb_ref, g_ref, b_ref, of_ref, ob_ref):
    def emit(x_ref):
        y = _ln_rows(x_ref[...], g_ref[...], b_ref[...])
        of_ref[...] = y
        ob_ref[...] = y.astype(BF16)

    first = pl.program_id(0) < na_blocks
    pl.when(first)(lambda: emit(xa_ref))
    pl.when(jnp.logical_not(first))(lambda: emit(xb_ref))


def _layernorm_pair(xa, xb, g, b, bm=512):
    d = xa.shape[1]
    na, nb = xa.shape[0] // bm, xb.shape[0] // bm
    row = pl.BlockSpec((bm, d), lambda i: (i, 0))
    vec = pl.BlockSpec((1, d), lambda i: (0, 0))
    m = (na + nb) * bm
    return pl.pallas_call(
        functools.partial(_ln2_kernel, na), grid=(na + nb,),
        in_specs=[pl.BlockSpec((bm, d), lambda i: (jnp.minimum(i, na - 1), 0)),
                  pl.BlockSpec((bm, d), lambda i: (jnp.maximum(i - na, 0), 0)), vec, vec],
        out_specs=[row, row],
        out_shape=[jax.ShapeDtypeStruct((m, d), F32), jax.ShapeDtypeStruct((m, d), BF16)],
        compiler_params=_params("arbitrary"), name="layernorm",
    )(xa, xb, g.reshape(1, d), b.reshape(1, d))


def _combine_ln_kernel(alpha, na_blocks, x_ref, y0_ref, y1_ref, gate_ref, g_ref, b_ref, oa_ref, ob_ref):
    gate = gate_ref[...]
    ff = y0_ref[...].astype(F32) * gate[:, 0:1] + y1_ref[...].astype(F32) * gate[:, 1:2]
    y = _ln_rows(alpha * x_ref[...] + ff, g_ref[...], b_ref[...])
    first = pl.program_id(0) < na_blocks

    @pl.when(first)
    def _():
        oa_ref[...] = y

    @pl.when(jnp.logical_not(first))
    def _():
        ob_ref[...] = y


def _combine_residual_ln_split(x, y0, y1, gate, g, b, alpha, n_first, bm=512):
    m, d = x.shape
    na = n_first // bm
    row = pl.BlockSpec((bm, d), lambda i: (i, 0))
    vec = pl.BlockSpec((1, d), lambda i: (0, 0))
    return pl.pallas_call(
        functools.partial(_combine_ln_kernel, alpha, na), grid=(m // bm,),
        in_specs=[row, row, row, pl.BlockSpec((bm, gate.shape[1]), lambda i: (i, 0)), vec, vec],
        out_specs=[pl.BlockSpec((bm, d), lambda i: (jnp.minimum(i, na - 1), 0)),
                   pl.BlockSpec((bm, d), lambda i: (jnp.maximum(i - na, 0), 0))],
        out_shape=[jax.ShapeDtypeStruct((n_first, d), F32), jax.ShapeDtypeStruct((m - n_first, d), F32)],
        compiler_params=_params("arbitrary"), name="moe_combine_residual_ln",
    )(x, y0, y1, gate, g.reshape(1, d), b.reshape(1, d))


def _mm_kernel(a_ref, w_ref, o_ref):
    o_ref[...] = jnp.dot(a_ref[...], w_ref[...], preferred_element_type=F32).astype(o_ref.dtype)


def _matmul(a, w, out_dtype=F32, bm=1024, bn=1024):
    m, k = a.shape
    n = w.shape[1]
    bm, bn = min(bm, m), min(bn, n)
    return pl.pallas_call(
        _mm_kernel, grid=(m // bm, n // bn),
        in_specs=[pl.BlockSpec((bm, k), lambda i, j: (i, 0)), pl.BlockSpec((k, bn), lambda i, j: (0, j))],
        out_specs=pl.BlockSpec((bm, bn), lambda i, j: (i, j)),
        out_shape=jax.ShapeDtypeStruct((m, n), out_dtype),
        compiler_params=_params("parallel", "arbitrary"), name="matmul",
    )(a, w)


def _mix_out_kernel(alpha, routed, a0_ref, a1_ref, a2_ref, w0_ref, w1_ref, w2_ref, x_ref, g_ref, b_ref, *refs):
    mix = (jnp.dot(a0_ref[...], w0_ref[...], preferred_element_type=F32)
           + jnp.dot(a1_ref[...], w1_ref[...], preferred_element_type=F32)
           + jnp.dot(a2_ref[...], w2_ref[...], preferred_element_type=F32))
    y = _ln_rows(alpha * x_ref[...] + mix, g_ref[...], b_ref[...])
    if routed:
        wr_ref, of_ref, ob_ref, lg_ref = refs
        y_hi, y_lo = _split_bf16(y)
        lg_ref[...] = (jnp.dot(y_hi, wr_ref[...], preferred_element_type=F32)
                       + jnp.dot(y_lo, wr_ref[...], preferred_element_type=F32))
    else:
        of_ref, ob_ref = refs
    of_ref[...] = y
    ob_ref[...] = y.astype(BF16)


def _mix_out_residual_ln(parts, weights, x, g, b, alpha, router_w=None, bm=512):
    m, d = x.shape
    row = pl.BlockSpec((bm, d), lambda i: (i, 0))
    vec = pl.BlockSpec((1, d), lambda i: (0, 0))
    in_specs = [pl.BlockSpec((bm, a.shape[1]), lambda i: (i, 0)) for a in parts]
    in_specs += [pl.BlockSpec(w.shape, lambda i: (0, 0)) for w in weights] + [row, vec, vec]
    args = [*parts, *weights, x, g.reshape(1, d), b.reshape(1, d)]
    out_specs = [row, row]
    out_shape = [jax.ShapeDtypeStruct((m, d), F32), jax.ShapeDtypeStruct((m, d), BF16)]
    if router_w is not None:
        in_specs.append(pl.BlockSpec(router_w.shape, lambda i: (0, 0)))
        args.append(router_w)
        out_specs.append(pl.BlockSpec((bm, router_w.shape[1]), lambda i: (i, 0)))
        out_shape.append(jax.ShapeDtypeStruct((m, router_w.shape[1]), F32))
    return pl.pallas_call(
        functools.partial(_mix_out_kernel, alpha, router_w is not None), grid=(m // bm,),
        in_specs=in_specs, out_specs=out_specs, out_shape=out_shape,
        compiler_params=_params("parallel"), name="out_proj_residual_ln",
    )(*args)


def _swiglu_acc(a_ref, w1_ref, w3_ref, w2_ref, acc_ref):
    f = pl.program_id(1)

    @pl.when(f == 0)
    def _():
        acc_ref[...] = jnp.zeros_like(acc_ref)

    a = a_ref[...]
    h1 = jnp.dot(a, w1_ref[...], preferred_element_type=F32)
    h3 = jnp.dot(a, w3_ref[...], preferred_element_type=F32)
    gated = (h1 * jax.nn.sigmoid(h1) * h3).astype(BF16)
    acc_ref[...] += jnp.dot(gated, w2_ref[...], preferred_element_type=F32)


def _ffn_kernel(alpha, a_ref, w1_ref, w3_ref, w2_ref, x_ref, g_ref, b_ref, of_ref, ob_ref, acc_ref):
    _swiglu_acc(a_ref, w1_ref, w3_ref, w2_ref, acc_ref)

    @pl.when(pl.program_id(1) == pl.num_programs(1) - 1)
    def _():
        y = _ln_rows(alpha * x_ref[...] + acc_ref[...], g_ref[...], b_ref[...])
        of_ref[...] = y
        ob_ref[...] = y.astype(BF16)


def _ffn_residual_ln(a, w1, w3, w2, x, g, b, alpha, bm=512, bf=512):
    m, d = a.shape
    dff = w1.shape[1]
    row = pl.BlockSpec((bm, d), lambda i, f: (i, 0))
    vec = pl.BlockSpec((1, d), lambda i, f: (0, 0))
    up = pl.BlockSpec((d, bf), lambda i, f: (0, f))
    return pl.pallas_call(
        functools.partial(_ffn_kernel, alpha), grid=(m // bm, dff // bf),
        in_specs=[row, up, up, pl.BlockSpec((bf, d), lambda i, f: (f, 0)), row, vec, vec],
        out_specs=[row, row],
        out_shape=[jax.ShapeDtypeStruct((m, d), F32), jax.ShapeDtypeStruct((m, d), BF16)],
        scratch_shapes=[pltpu.VMEM((bm, d), F32)],
        compiler_params=_params("parallel", "arbitrary"), name="ffn_residual_ln",
    )(a, w1, w3, w2, x, g.reshape(1, d), b.reshape(1, d))


def _moe_ffn_kernel(blk_e_ref, used_ref, a_ref, w1_ref, w3_ref, w2_ref, o_ref, acc_ref):
    del blk_e_ref
    f = pl.program_id(1)

    @pl.when(pl.program_id(0) < used_ref[0])
    def _():
        @pl.when(f == 0)
        def _():
            acc_ref[...] = jnp.zeros_like(acc_ref)

        a = a_ref[...]
        h1 = jnp.dot(a, w1_ref[...].astype(BF16), preferred_element_type=F32)
        h3 = jnp.dot(a, w3_ref[...].astype(BF16), preferred_element_type=F32)
        gated = (h1 * jax.nn.sigmoid(h1) * h3).astype(BF16)
        acc_ref[...] += jnp.dot(gated, w2_ref[...], preferred_element_type=F32)

        @pl.when(f == pl.num_programs(1) - 1)
        def _():
            o_ref[...] = acc_ref[...].astype(o_ref.dtype)


def _cast_kernel(x_ref, o_ref):
    o_ref[...] = x_ref[...].astype(o_ref.dtype)


def _cast_bf16(w, rows=128):
    e, r, c = w.shape
    rows = min(rows, r)
    blk = pl.BlockSpec((None, rows, c), lambda i, j: (i, j, 0))
    return pl.pallas_call(
        _cast_kernel, grid=(e, r // rows), in_specs=[blk], out_specs=blk,
        out_shape=jax.ShapeDtypeStruct(w.shape, BF16),
        compiler_params=_params("parallel", "parallel"), name="cast_bf16",
    )(w)


def _moe_grouped_ffn(blk_e, n_used, xb, w1, w3, w2, bm, bf=512):
    cap, d = xb.shape
    dff = w1.shape[2]
    nf = dff // bf

    def blk(i, u):
        return jnp.minimum(i, u[0] - 1)

    def ftile(i, f, u):
        return jnp.where(i < u[0], f, nf - 1)

    row = pl.BlockSpec((bm, d), lambda i, f, e, u: (blk(i, u), 0))
    up = pl.BlockSpec((None, d, bf), lambda i, f, e, u: (e[blk(i, u)], 0, ftile(i, f, u)))
    down = pl.BlockSpec((None, bf, d), lambda i, f, e, u: (e[blk(i, u)], ftile(i, f, u), 0))
    grid_spec = pltpu.PrefetchScalarGridSpec(
        num_scalar_prefetch=2, grid=(cap // bm, nf), in_specs=[row, up, up, down],
        out_specs=row, scratch_shapes=[pltpu.VMEM((bm, d), F32)])
    return pl.pallas_call(
        _moe_ffn_kernel, grid_spec=grid_spec, out_shape=jax.ShapeDtypeStruct((cap, d), BF16),
        compiler_params=_params("arbitrary", "arbitrary"), name="moe_grouped_ffn",
    )(blk_e, n_used, xb, w1, w3, w2)


def _moe_route(logits, bm):
    n = logits.shape[0]
    top_val, top_idx = lax.top_k(logits, TOP_K)
    gate = jax.nn.softmax(top_val, axis=-1)
    nk = n * TOP_K
    assert nk % bm == 0
    experts = jnp.arange(N_EXPERTS, dtype=jnp.int32)
    flat_e = top_idx.reshape(nk).astype(jnp.int32)
    one_hot = flat_e[:, None] == experts[None, :]
    order = jnp.argsort(flat_e, stable=True).astype(jnp.int32)
    rank = jnp.argsort(order).astype(jnp.int32)
    counts = jnp.sum(one_hot.astype(jnp.int32), 0)
    padded = (counts + bm - 1) // bm * bm
    start = jnp.cumsum(counts) - counts
    pend = jnp.cumsum(padded)
    shift = jnp.sum(jnp.where(one_hot, (pend - padded - start)[None, :], 0), 1)
    dest = (rank + shift).reshape(n, TOP_K)
    n_blocks = nk // bm + N_EXPERTS
    blk_e = jnp.minimum(jnp.searchsorted(pend, jnp.arange(n_blocks, dtype=jnp.int32) * bm, side='right'),
                        N_EXPERTS - 1).astype(jnp.int32)
    pad_rank = jnp.arange(bm, dtype=jnp.int32)[None, :]
    pad_key = jnp.where(pad_rank < (padded - counts)[:, None], 2 * experts[:, None] + 1, 2 * N_EXPERTS)
    keys = jnp.concatenate([2 * flat_e, pad_key.reshape(-1)])
    toks = jnp.concatenate([jnp.arange(nk, dtype=jnp.int32) // TOP_K, jnp.zeros((N_EXPERTS * bm,), jnp.int32)])
    _, slot_tok = lax.sort((keys, toks), num_keys=1, is_stable=True)
    n_used = (pend[-1] // bm).astype(jnp.int32).reshape(1)
    return gate, dest, slot_tok, blk_e, n_used


def _moe_dispatch_ffn(logits, h_bf16, w1, w3, w2, bm=1024):
    gate, dest, slot_tok, blk_e, n_used = _moe_route(logits, bm)
    yb = _moe_grouped_ffn(blk_e, n_used, h_bf16[slot_tok], w1, w3, w2, bm)
    return yb[dest[:, 0]], yb[dest[:, 1]], gate


def _gmlp_kernel(groups, pu_ref, pv_ref, g_ref, b_ref, ws_ref, bs_ref, o_ref):
    u = jax.nn.gelu(pu_ref[...])
    v = _ln_rows(jax.nn.gelu(pv_ref[...]), g_ref[...], b_ref[...])
    for n in range(u.shape[0] // GM_CHUNK):
        rows = slice(n * GM_CHUNK, (n + 1) * GM_CHUNK)
        for grp in range(groups):
            cols = slice(grp * HEAD_DIM, (grp + 1) * HEAD_DIM)
            s = _dot3(ws_ref[grp], v[rows, cols]) + bs_ref[grp]
            o_ref[rows, cols] = (u[rows, cols] * s).astype(o_ref.dtype)


def _gmlp(p, col_u, col_v, gm_w, ln_g, ln_b, ws, bs, bt=512):
    n = p.shape[0]
    groups = ws.shape[0]
    bsb = jnp.broadcast_to(bs[:, :, None], (groups, GM_CHUNK, HEAD_DIM)).astype(F32)
    vec = pl.BlockSpec((1, gm_w), lambda i: (0, 0))
    full3 = pl.BlockSpec((groups, GM_CHUNK, HEAD_DIM), lambda i: (0, 0, 0))
    return pl.pallas_call(
        functools.partial(_gmlp_kernel, groups), grid=(n // bt,),
        in_specs=[pl.BlockSpec((bt, gm_w), lambda i: (i, col_u)), pl.BlockSpec((bt, gm_w), lambda i: (i, col_v)),
                  vec, vec, pl.BlockSpec((groups, GM_CHUNK, GM_CHUNK), lambda i: (0, 0, 0)), full3],
        out_specs=pl.BlockSpec((bt, gm_w), lambda i: (i, 0)),
        out_shape=jax.ShapeDtypeStruct((n, gm_w), BF16),
        compiler_params=_params("parallel"), name="gmlp",
    )(p, p, ln_g.reshape(1, gm_w), ln_b.reshape(1, gm_w), ws.astype(F32), bsb)


def _hgrn2_constants(c, reverse):
    n_lvl = int(math.log2(c))
    t = np.arange(c)
    ms, ws = [], []
    for lvl in range(n_lvl):
        m = c >> (lvl + 1)
        mid = (t // (2 * m)) * (2 * m) + m
        upper = t >= mid
        mat = np.zeros((c, c), np.float32)
        for r in range(c):
            if upper[r]:
                mat[r, mid[r]:r + 1] = 1.0
            else:
                mat[r, r + 1:mid[r]] = 1.0
        same = (t[:, None] // (2 * m)) == (t[None, :] // (2 * m))
        ws.append((same & upper[:, None] & ~upper[None, :]).astype(np.float32))
        ms.append(mat)
    ws.append(np.eye(c, dtype=np.float32))
    ms.append(np.tril(np.ones((c, c), np.float32)))
    ms.append(np.triu(np.ones((c, c), np.float32), 1))
    if reverse:
        ms = [a[::-1, ::-1] for a in ms]
        ws = [a[::-1, ::-1] for a in ws]
    return np.concatenate(ms, 0), np.stack(ws, 0)


def _hgrn2_kernel(heads, n_lvl, carry_row, final, reset_ref, *refs):
    if final:
        (q_ref, z_ref, i_ref, og_ref, oo_ref, loglb_ref, log1mlb_ref, onemlb_ref, ng_ref, m_ref, w_ref,
         out_ref, st_ref, e_ref) = refs
    else:
        (q_ref, z_ref, i_ref, loglb_ref, log1mlb_ref, onemlb_ref, m_ref, w_ref, out_ref, st_ref, e_ref) = refs
    n_streams, c = q_ref.shape[0], q_ref.shape[1]
    rows_e = m_ref.shape[0]
    step = pl.program_id(0)

    for s in range(n_streams):
        @pl.when(reset_ref[step * n_streams + s] == 1)
        def _():
            st_ref[s * heads:(s + 1) * heads] = jnp.zeros((heads, HEAD_DIM, HEAD_DIM), F32)

    for s in range(n_streams):
        z = z_ref[s]
        e = jnp.exp(-jnp.abs(z))
        r = 1.0 / (1.0 + e)
        log_sig = jnp.minimum(z, 0.0) - jnp.log(1.0 + e)
        sig_neg = jnp.where(z >= 0.0, e * r, r)
        a = loglb_ref[...]
        b = log1mlb_ref[...] + log_sig
        g = jnp.maximum(a, b) + jnp.log(1.0 + jnp.exp(-jnp.abs(a - b)))
        k = onemlb_ref[...] * sig_neg
        q = q_ref[s]
        qs = q * jax.nn.sigmoid(q)
        v = i_ref[s].astype(BF16)

        g_hi, g_lo = _split_bf16(g)
        m_all = m_ref[...]
        e0 = s * rows_e
        e_ref[e0:e0 + rows_e, :] = (jnp.dot(m_all, g_hi, preferred_element_type=F32)
                                    + jnp.dot(m_all, g_lo, preferred_element_type=F32))

        for h in range(heads):
            cols = slice(h * HEAD_DIM, (h + 1) * HEAD_DIM)
            qh, kh, vh = qs[:, cols], k[:, cols], v[:, cols]
            amat = w_ref[n_lvl] * lax.dot_general(qh.astype(BF16), kh.astype(BF16), NT_DIMS,
                                                  preferred_element_type=F32)
            for lvl in range(n_lvl):
                ex = jnp.exp(e_ref[e0 + lvl * c:e0 + (lvl + 1) * c, cols])
                amat = amat + w_ref[lvl] * lax.dot_general((qh * ex).astype(BF16), (kh * ex).astype(BF16),
                                                           NT_DIMS, preferred_element_type=F32)
            e_in = e_ref[e0 + n_lvl * c:e0 + (n_lvl + 1) * c, cols]
            q_in = (qh * jnp.exp(e_in)).astype(BF16)
            k_st = (kh * jnp.exp(e_ref[e0 + (n_lvl + 1) * c:e0 + (n_lvl + 2) * c, cols])).astype(BF16)
            dec = jnp.exp(e_in[carry_row:carry_row + 1, :])
            st = st_ref[s * heads + h]
            o = (lax.dot_general(q_in, st.astype(BF16), NT_DIMS, preferred_element_type=F32)
                 + jnp.dot(amat.astype(BF16), vh, preferred_element_type=F32))
            st_ref[s * heads + h] = st * dec + lax.dot_general(vh, k_st, TN_DIMS, preferred_element_type=F32)
            if final:
                o = o + oo_ref[s, :, cols]
                o = o * lax.rsqrt(jnp.mean(o * o, -1, keepdims=True) + RMS_EPS)
                og = og_ref[s, :, cols]
                out_ref[s, :, cols] = (o * ng_ref[:, cols] * (og * jax.nn.sigmoid(og))).astype(out_ref.dtype)
            else:
                out_ref[s, :, cols] = o


def _hgrn2_pass(p3, cols, hg_w, z_col, reverse, reset, lbs, extra, c):
    n_streams, n = p3.shape[0], p3.shape[1]
    nc = n // c
    heads = hg_w // HEAD_DIM
    n_lvl = int(math.log2(c))
    m_np, w_np = _hgrn2_constants(c, reverse)
    final = extra is not None

    def cmap(i):
        return (nc - 1 - i) if reverse else i

    def tok(col):
        return pl.BlockSpec((n_streams, c, hg_w), lambda i, r: (0, cmap(i), col))

    vec = pl.BlockSpec((1, hg_w), lambda i, r: (0, 0))
    in_specs = [tok(cols["q"]), tok(z_col), tok(cols["i"])]
    args = [p3, p3, p3]
    if final:
        o_other, norm_g = extra
        in_specs += [tok(cols["og"]), tok(0)]
        args += [p3, o_other]
    in_specs += [vec, vec, vec]
    args += list(lbs)
    if final:
        in_specs += [vec]
        args += [norm_g.reshape(1, hg_w).astype(F32)]
    in_specs += [pl.BlockSpec(m_np.shape, lambda i, r: (0, 0)), pl.BlockSpec(w_np.shape, lambda i, r: (0, 0, 0))]
    args += [jnp.asarray(m_np, BF16), jnp.asarray(w_np, F32)]
    grid_spec = pltpu.PrefetchScalarGridSpec(
        num_scalar_prefetch=1, grid=(nc,), in_specs=in_specs, out_specs=tok(0),
        scratch_shapes=[pltpu.VMEM((n_streams * heads, HEAD_DIM, HEAD_DIM), F32),
                        pltpu.VMEM((n_streams * m_np.shape[0], hg_w), F32)])
    order = np.arange(nc)[::-1] if reverse else np.arange(nc)
    reset_steps = jnp.asarray(np.asarray(reset, np.int32)[order].reshape(-1))
    return pl.pallas_call(
        functools.partial(_hgrn2_kernel, heads, n_lvl, 0 if reverse else c - 1, final), grid_spec=grid_spec,
        out_shape=jax.ShapeDtypeStruct((n_streams, n, hg_w), BF16 if final else F32),
        compiler_params=_params("arbitrary"), name="hgrn2_fwd" if final else "hgrn2_bwd",
    )(reset_steps, *args)


def _hgrn2(p, cols, hg_w, lb, norm_g, seq_lens, n_streams=2, c=HG_CHUNK):
    lb = lb.astype(F32).reshape(1, hg_w)
    lbs = (jnp.log(lb), jnp.log1p(-lb), 1.0 - lb)
    starts = np.cumsum([0] + list(seq_lens))
    n = int(starts[-1])
    per = n // n_streams
    assert n == p.shape[0] and all(s * per in starts for s in range(n_streams)), "a sequence straddles a stream cut"
    first = np.zeros(n // c, np.int32)
    last = np.zeros(n // c, np.int32)
    first[starts[:-1] // c] = 1
    last[starts[1:] // c - 1] = 1
    first = first.reshape(n_streams, per // c).T
    last = last.reshape(n_streams, per // c).T
    p3 = p.reshape(n_streams, per, p.shape[1])
    o_b = _hgrn2_pass(p3, cols, hg_w, cols["zb"], True, last, lbs, None, c)
    return _hgrn2_pass(p3, cols, hg_w, cols["zf"], False, first, lbs, (o_b, norm_g), c).reshape(n, hg_w)


def _dft_matrices(t):
    n = 2 * t
    f = jnp.arange(t, dtype=jnp.int32)
    ang = (2.0 * math.pi / n) * ((f[:, None] * f[None, :]) % n).astype(F32)
    cos, sin = jnp.cos(ang), jnp.sin(ang)
    alt = (1 - 2 * (f % 2)).astype(F32)
    first = f == 0
    fwd = jnp.stack([cos, jnp.where(first[:, None], alt[None, :], -sin)], 0).astype(BF16)
    wgt = jnp.where(first, 1.0, 2.0)[None, :] / n
    inv = jnp.stack([cos * wgt, jnp.where(first[None, :], alt[:, None] / n, -sin * wgt)], 0).astype(BF16)
    return fwd, inv


def _shortconv_kernel(flags_ref, x_ref, prev_ref, next_ref, w_ref, b_ref, o_ref):
    i = pl.program_id(0)
    x = x_ref[...]
    t = x.shape[0]
    rows = lax.broadcasted_iota(jnp.int32, x.shape, 0)
    has_prev = flags_ref[2 * i].astype(F32)
    has_next = flags_ref[2 * i + 1].astype(F32)
    up = jnp.where(rows == 0, prev_ref[7:8, :] * has_prev, pltpu.roll(x, 1, 0))
    dn = jnp.where(rows == t - 1, next_ref[0:1, :] * has_next, pltpu.roll(x, t - 1, 0))
    o_ref[...] = up * w_ref[0:1, :] + x * w_ref[1:2, :] + dn * w_ref[2:3, :] + b_ref[...]


def _shortconv(p, width, conv_w, conv_b, blk_flags, t=HY_BLOCK, ct=768):
    n = p.shape[0]
    nb = n // t
    r8 = t // 8
    grid_spec = pltpu.PrefetchScalarGridSpec(
        num_scalar_prefetch=1, grid=(nb, width // ct),
        in_specs=[pl.BlockSpec((t, ct), lambda i, j, f: (i, j)),
                  pl.BlockSpec((8, ct), lambda i, j, f: (jnp.maximum(i * r8 - 1, 0), j)),
                  pl.BlockSpec((8, ct), lambda i, j, f: (jnp.minimum((i + 1) * r8, nb * r8 - 1), j)),
                  pl.BlockSpec((3, ct), lambda i, j, f: (0, j)), pl.BlockSpec((1, ct), lambda i, j, f: (0, j))],
        out_specs=pl.BlockSpec((t, ct), lambda i, j, f: (i, j)))
    return pl.pallas_call(
        _shortconv_kernel, grid_spec=grid_spec, out_shape=jax.ShapeDtypeStruct((n, width), F32),
        compiler_params=_params("parallel", "parallel"), name="hyena_shortconv",
    )(blk_flags, p, p, p, conv_w.astype(F32), conv_b.reshape(1, width).astype(F32))


def _filter_kernel(l_total, feats_ref, w1_ref, b1_ref, w2_ref, b2_ref, w3_ref, b3_ref, fr_ref, wo_ref, dl_ref,
                   h_ref, sum_ref):
    del l_total
    i = pl.program_id(0)
    bt = feats_ref.shape[0]
    half_lanes = LANES // 2
    fr = fr_ref[...]
    feats = feats_ref[...]
    h = jnp.sin(fr * (_dot3(feats, w1_ref[...]) + b1_ref[...]))
    h = jnp.sin(fr * (_dot3(h, w2_ref[...]) + b2_ref[...]))
    h = jnp.sin(fr * (_dot3(h, w3_ref[...]) + b3_ref[...]))
    reps = wo_ref.shape[2] // dl_ref.shape[1]
    row = lax.broadcasted_iota(jnp.int32, (bt, 1), 0) + i * bt
    total = jnp.zeros((1, wo_ref.shape[2]), F32)
    for side in range(2):
        window = jnp.exp(-feats[:, side * half_lanes:side * half_lanes + 1] * dl_ref[...])
        out = _dot3(h, wo_ref[side]) * jnp.concatenate([window] * reps, axis=1)
        if side == 1:
            out = jnp.where(row == 0, 0.0, out)
        h_ref[side] = out
        total = total + jnp.sum(jnp.abs(out), axis=0, keepdims=True)

    @pl.when(i == 0)
    def _():
        sum_ref[...] = jnp.zeros_like(sum_ref)

    sum_ref[...] += total


def _hyena_filter(l, w1, b1, w2, b2, w3, b3, freq, w_out, hy_w, bt=512):
    hid = w2.shape[0]
    half_lanes = LANES // 2
    assert hid <= half_lanes and w1.shape[0] <= half_lanes
    half = HY_ORDER * hy_w
    bands = jnp.linspace(1e-4, HY_POS_BANDS - 1, HY_POS_BANDS, dtype=F32)

    def features(pos):
        tt = pos / max(l - 1, 1)
        ang = (2.0 * math.pi / l) * pos[:, None] * bands[None, :]
        f = jnp.concatenate([tt[:, None], jnp.cos(ang), -jnp.sin(ang)], -1)
        return jnp.pad(f, ((0, 0), (0, half_lanes - f.shape[1])))

    def twice_diag(w):
        wp = jnp.pad(w.astype(F32), ((0, half_lanes - w.shape[0]), (0, half_lanes - w.shape[1])))
        z = jnp.zeros_like(wp)
        return jnp.concatenate([jnp.concatenate([wp, z], 1), jnp.concatenate([z, wp], 1)], 0)

    def twice_vec(a):
        ap = jnp.pad(a.astype(F32), (0, half_lanes - a.shape[0]))
        return jnp.concatenate([ap, ap]).reshape(1, LANES)

    pos = jnp.arange(l, dtype=F32)
    feats = jnp.concatenate([features(pos), features(l - pos)], 1)
    wo = w_out.astype(F32).reshape(hid, HY_ORDER, 2, hy_w).transpose(2, 0, 1, 3).reshape(2, hid, half)
    wo = jnp.pad(wo, ((0, 0), (0, half_lanes - hid), (0, 0)))
    zero = jnp.zeros_like(wo[0])
    wop = jnp.stack([jnp.concatenate([wo[0], zero], 0), jnp.concatenate([zero, wo[1]], 0)], 0)
    deltas = jnp.abs(jnp.linspace(math.log(HY_DECAY_TARGET) / HY_SLOW_DECAY,
                                  math.log(HY_DECAY_TARGET) / HY_FAST_DECAY, hy_w, dtype=F32)).reshape(1, hy_w)
    bt = min(bt, l)
    sq = pl.BlockSpec((LANES, LANES), lambda i: (0, 0))
    vec = pl.BlockSpec((1, LANES), lambda i: (0, 0))
    taps, abs_sum = pl.pallas_call(
        functools.partial(_filter_kernel, l), grid=(l // bt,),
        in_specs=[pl.BlockSpec((bt, LANES), lambda i: (i, 0)), sq, vec, sq, vec, sq, vec, vec,
                  pl.BlockSpec((2, LANES, half), lambda i: (0, 0, 0)),
                  pl.BlockSpec((1, hy_w), lambda i: (0, 0))],
        out_specs=[pl.BlockSpec((2, bt, half), lambda i: (0, i, 0)), pl.BlockSpec((1, half), lambda i: (0, 0))],
        out_shape=[jax.ShapeDtypeStruct((2, l, half), F32), jax.ShapeDtypeStruct((1, half), F32)],
        compiler_params=_params("arbitrary"), name="hyena_filter",
    )(feats, twice_diag(w1), twice_vec(b1), twice_diag(w2), twice_vec(b2), twice_diag(w3), twice_vec(b3),
      twice_vec(freq), wop, deltas)
    return taps.reshape(2 * l, half), abs_sum


def _dft_cols(t, width):
    return width if t <= 1024 else 256


def _dft_kernel(f_ref, x_ref, o_ref):
    x = x_ref[...].astype(BF16)
    o_ref[0] = jnp.dot(f_ref[0], x, preferred_element_type=F32).astype(o_ref.dtype)
    o_ref[1] = jnp.dot(f_ref[1], x, preferred_element_type=F32).astype(o_ref.dtype)


def _block_dft(fwd, x, col_off, width, out_dtype, t=HY_BLOCK, ct=256):
    nseg = x.shape[0] // t
    ct = min(ct, width)
    return pl.pallas_call(
        _dft_kernel, grid=(nseg, width // ct),
        in_specs=[pl.BlockSpec((2, t, t), lambda s, j: (0, 0, 0)),
                  pl.BlockSpec((t, ct), lambda s, j: (s, j + col_off))],
        out_specs=pl.BlockSpec((None, 2, t, ct), lambda s, j: (s, 0, 0, j)),
        out_shape=jax.ShapeDtypeStruct((nseg, 2, t, width), out_dtype),
        compiler_params=_params("parallel", "parallel"), name="hyena_block_dft",
    )(fwd, x)


def _segment_index(d, nb):
    return d if d >= 0 else 2 * nb + d


def _hyena_filter_spectra(l, fwd, filt_w, hy_w, t=HY_BLOCK):
    nb = l // t
    taps, abs_sum = _hyena_filter(l, *filt_w, hy_w)
    half = taps.shape[1]
    seg_spec = _block_dft(fwd, taps, 0, half, F32, t, _dft_cols(t, hy_w))
    first_rows = taps.reshape(2 * nb, t, half)[:, 0, :]
    seg0 = jnp.stack([first_rows[_segment_index(d - 1, nb)] for d in range(-(nb - 1), nb)], 0)
    return seg_spec, seg0.reshape(2 * nb - 1, 1, half), 1.0 / abs_sum


MIX_ROWS = 8


def _mix_kernel(nb, z_ref, a_ref, s0_ref, inv_ref, *refs):
    o_ref, k_ref = refs[-2], refs[-1]
    ft, ct = z_ref.shape[2], z_ref.shape[3]
    first_tile = pl.program_id(1) == 0

    tile_rows = lax.broadcasted_iota(jnp.int32, (ft, ct), 0)
    sgn = (1 - 2 * (tile_rows & 1)).astype(F32)
    real_row = jnp.logical_and(tile_rows == 0, first_tile)
    inv = inv_ref[...]
    for slot in range(2 * nb - 1):
        d = slot - (nb - 1)
        ia, ib = _segment_index(d, nb), _segment_index(d - 1, nb)
        s0 = s0_ref[slot]
        k_ref[slot, 0] = (a_ref[ia, 0] + sgn * (a_ref[ib, 0] - s0)) * inv
        k_ref[slot, 1] = (a_ref[ia, 1] + sgn * (a_ref[ib, 1] - jnp.where(real_row, s0, 0.0))) * inv

    n_out = 2 if nb % 2 == 0 else 1

    def out_rows(i, r0, masked):
        rows = pl.ds(r0, MIX_ROWS)
        acc = [[jnp.zeros((MIX_ROWS, ct), F32) for _ in range(2)] for _ in range(n_out)]
        if masked:
            row0 = jnp.logical_and(lax.broadcasted_iota(jnp.int32, (MIX_ROWS, ct), 0) == 0, first_tile)
        d_hi = i + (n_out - 1) + (nb - 1)
        k_prev = (k_ref[d_hi, 0, rows, :], k_ref[d_hi, 1, rows, :]) if n_out == 2 else None
        for j in range(nb):
            d = i - j + (nb - 1)
            zt, zb = z_ref[j, 0, rows, :], z_ref[j, 1, rows, :]
            k_cur = (k_ref[d, 0, rows, :], k_ref[d, 1, rows, :])
            for o, (kt, kb) in enumerate((k_cur, k_prev)[:n_out]):
                bb = zb * kb
                if masked:
                    acc[o][0] = acc[o][0] + (zt * kt - jnp.where(row0, 0.0, bb))
                    acc[o][1] = acc[o][1] + jnp.where(row0, bb, zt * kb + zb * kt)
                else:
                    acc[o][0] = acc[o][0] + (zt * kt - bb)
                    acc[o][1] = acc[o][1] + (zt * kb + zb * kt)
            k_prev = k_cur
        for o in range(n_out):
            o_ref[i + o, 0, rows, :] = acc[o][0].astype(o_ref.dtype)
            o_ref[i + o, 1, rows, :] = acc[o][1].astype(o_ref.dtype)

    def all_outputs(r0, masked):
        def body(ip, carry):
            out_rows(ip * n_out, r0, masked)
            return carry
        lax.fori_loop(0, nb // n_out, body, 0)

    all_outputs(0, True)

    def chunk(r, carry):
        all_outputs(pl.multiple_of(r * MIX_ROWS, MIX_ROWS), False)
        return carry

    lax.fori_loop(1, ft // MIX_ROWS, chunk, 0)


def _hyena_mix(zspec, kspec, k_col, n_seq, nb, blk_off, prev, ct=256):
    seg_spec, seg0, inv_norm = kspec
    nblk, _, t, width = zspec.shape
    ft = max(MIX_ROWS, min(t, 2048 // nb))
    nct = width // ct
    assert blk_off % nb == 0 and t % ft == 0
    s_off = blk_off // nb
    zblk = pl.BlockSpec((nb, 2, ft, ct), lambda s, fi, j: (s + s_off, 0, fi, j))
    in_specs = [zblk,
                pl.BlockSpec((2 * nb, 2, ft, ct), lambda s, fi, j: (0, 0, fi, j + k_col * nct)),
                pl.BlockSpec((2 * nb - 1, 1, ct), lambda s, fi, j: (0, 0, j + k_col * nct)),
                pl.BlockSpec((1, ct), lambda s, fi, j: (0, j + k_col * nct))]
    args = [zspec, seg_spec, seg0, inv_norm]
    aliases = {}
    if prev is not None:
        in_specs.append(pl.BlockSpec(memory_space=pl.ANY))
        args.append(prev)
        aliases = {4: 0}
    return pl.pallas_call(
        functools.partial(_mix_kernel, nb), grid=(n_seq, t // ft, nct),
        in_specs=in_specs, out_specs=zblk,
        out_shape=jax.ShapeDtypeStruct(zspec.shape, BF16), input_output_aliases=aliases,
        scratch_shapes=[pltpu.VMEM((2 * nb - 1, 2, ft, ct), F32)],
        compiler_params=_params("parallel", "parallel", "parallel"), name="hyena_mix",
    )(*args)


def _inverse_kernel(y_ref, fi_ref, zin_ref, gate_ref, bias_ref, o_ref):
    y = (jnp.dot(fi_ref[0], y_ref[0], preferred_element_type=F32)
         + jnp.dot(fi_ref[1], y_ref[1], preferred_element_type=F32))
    o_ref[...] = (gate_ref[...] * (y + zin_ref[...] * bias_ref[...])).astype(o_ref.dtype)


def _hyena_inverse(yspec, inv, zin, zin_col, gate, gate_col, bias, out_dtype, ct=256):
    nblk, _, t, width = yspec.shape
    nct = width // ct
    return pl.pallas_call(
        _inverse_kernel, grid=(nblk, nct),
        in_specs=[pl.BlockSpec((None, 2, t, ct), lambda i, j: (i, 0, 0, j)),
                  pl.BlockSpec((2, t, t), lambda i, j: (0, 0, 0), pipeline_mode=pl.Buffered(1)),
                  pl.BlockSpec((t, ct), lambda i, j: (i, j + zin_col * nct)),
                  pl.BlockSpec((t, ct), lambda i, j: (i, j + gate_col * nct)),
                  pl.BlockSpec((1, ct), lambda i, j: (0, j))],
        out_specs=pl.BlockSpec((t, ct), lambda i, j: (i, j)),
        out_shape=jax.ShapeDtypeStruct((nblk * t, width), out_dtype),
        compiler_params=_params("parallel", "arbitrary"), name="hyena_inverse",
    )(yspec, inv, zin, gate, bias)


def _hyena(p, hy_w, groups, conv_w, conv_b, filt_w, bias, t=HY_BLOCK):
    flags = []
    for n_seq, l in groups:
        nb = l // t
        flags += [int(bi > 0) if side == 0 else int(bi < nb - 1)
                  for _ in range(n_seq) for bi in range(nb) for side in range(2)]
    fwd, inv = _dft_matrices(t)
    u = _shortconv(p, 3 * hy_w, conv_w, conv_b, jnp.asarray(np.asarray(flags, np.int32)), t, ct=hy_w)
    kspecs = [_hyena_filter_spectra(l, fwd, filt_w, hy_w, t) for _, l in groups]
    bias = bias.astype(F32)
    ct = _dft_cols(t, hy_w)
    z, z_col = u, 2
    for o in range(HY_ORDER):
        zspec = _block_dft(fwd, z, z_col * (hy_w // ct), hy_w, F32, t, ct)
        yspec, blk = None, 0
        for (n_seq, l), kspec in zip(groups, kspecs):
            yspec = _hyena_mix(zspec, kspec, o, n_seq, l // t, blk, yspec)
            blk += n_seq * (l // t)
        last = o == HY_ORDER - 1
        z = _hyena_inverse(yspec, inv, z, z_col, u, o, bias[o:o + 1], BF16 if last else F32, ct)
        z_col = 0
    return z


def kernel(x_prompt, x_sample, ln_in_g, ln_in_b, w_in, hy_conv_w, hy_conv_b, hy_pos_w1, hy_pos_b1, hy_pos_w2, hy_pos_b2, hy_pos_w3, hy_pos_b3, hy_sin_freq, hy_pos_wout, hy_bias, gm_ln_g, gm_ln_b, gm_ws, gm_bs, hg_lb_raw, hg_norm_g, w_out, ln1_g, ln1_b, ln2_g, ln2_b, ffn_w1, ffn_w3, ffn_w2, moe_router_w, moe_router_b, moe_w1, moe_w3, moe_w2):
    depth, d_model, in_w = w_in.shape
    hy_w = hy_bias.shape[-1]
    gm_w = gm_ln_g.shape[-1]
    hg_w = hg_norm_g.shape[-1]
    o1 = 3 * hy_w
    o2 = o1 + 2 * gm_w
    alpha = (2 * depth) ** 0.25
    groups = [(x_prompt.shape[0], x_prompt.shape[1]), (x_sample.shape[0], x_sample.shape[1])]
    seq_lens = [l for n_seq, l in groups for _ in range(n_seq)]
    n_prompt = x_prompt.shape[0] * x_prompt.shape[1]
    hg_off = o1 // hg_w
    hg_cols = {"q": hg_off, "zf": hg_off + 1, "zb": hg_off + 2, "i": hg_off + 3, "og": hg_off + 4}
    gm_off = (o1 + 5 * hg_w) // gm_w

    lb_all = jnp.cumsum(jax.nn.softmax(hg_lb_raw.astype(F32), axis=0), axis=0)
    lb_all = lb_all - lb_all[:1]

    x, xb = _layernorm_pair(x_prompt.reshape(-1, d_model), x_sample.reshape(-1, d_model), ln_in_g, ln_in_b)
    for l in range(depth):
        w_l = jnp.concatenate([w_in[l][:, :o1], w_in[l][:, o2:], w_in[l][:, o1:o2]], 1).astype(BF16)
        p = _matmul(xb, w_l)
        filt_w = (hy_pos_w1[l], hy_pos_b1[l], hy_pos_w2[l], hy_pos_b2[l], hy_pos_w3[l], hy_pos_b3[l],
                  hy_sin_freq[l], hy_pos_wout[l])
        y_hy = _hyena(p, hy_w, groups, hy_conv_w[l], hy_conv_b[l], filt_w, hy_bias[l])
        y_hg = _hgrn2(p, hg_cols, hg_w, lb_all[l], hg_norm_g[l], seq_lens)
        y_gm = _gmlp(p, gm_off, gm_off + 1, gm_w, gm_ln_g[l], gm_ln_b[l], gm_ws[l], gm_bs[l])
        wo = w_out[l].astype(BF16)
        parts = [y_hy, y_gm, y_hg]
        wo_parts = [wo[:hy_w], wo[hy_w:hy_w + gm_w], wo[hy_w + gm_w:]]
        j = l // 2
        if l % 2 == 0:
            x, xb = _mix_out_residual_ln(parts, wo_parts, x, ln1_g[l], ln1_b[l], alpha)
            x, xb = _ffn_residual_ln(xb, ffn_w1[j].astype(BF16), ffn_w3[j].astype(BF16), ffn_w2[j].astype(BF16),
                                     x, ln2_g[l], ln2_b[l], alpha)
        else:
            n_exp = moe_router_w.shape[-1]
            wr = jnp.pad(moe_router_w[j].astype(BF16), ((0, 0), (0, LANES - n_exp)))
            x, xb, logits = _mix_out_residual_ln(parts, wo_parts, x, ln1_g[l], ln1_b[l], alpha, wr)
            logits = logits[:, :n_exp] + moe_router_b[j].astype(F32)
            y0, y1, gate = _moe_dispatch_ffn(logits, xb, moe_w1[j], moe_w3[j], _cast_bf16(moe_w2[j]))
            xa, xs = _combine_residual_ln_split(x, y0, y1, gate, ln2_g[l], ln2_b[l], alpha, n_prompt)
            if l == depth - 1:
                return (xa.reshape(x_prompt.shape), xs.reshape(x_sample.shape))
            x = jnp.concatenate([xa, xs], 0)
            xb = x.astype(BF16)
    return (x[:n_prompt].reshape(x_prompt.shape), x[n_prompt:].reshape(x_sample.shape))
```

```python
import functools
import math

import numpy as np
import jax
import jax.numpy as jnp
from jax import lax
from jax.experimental import pallas as pl
from jax.experimental.pallas import tpu as pltpu

HEAD_DIM = 128
HY_ORDER = 2
HY_POS_BANDS = 16
HY_FAST_DECAY = 0.3
HY_SLOW_DECAY = 1.5
HY_DECAY_TARGET = 1e-2
GM_CHUNK = 128
N_EXPERTS = 8
TOP_K = 2
LN_EPS = 1e-5
RMS_EPS = 1e-6

HY_BLOCK = 1024
HG_CHUNK = 128
LANES = 128

V7X_VMEM_BYTES = 64 * 1024 * 1024
VMEM_LIMIT = V7X_VMEM_BYTES - 8 * 1024 * 1024

F32 = jnp.float32
BF16 = jnp.bfloat16
NT_DIMS = (((1,), (1,)), ((), ()))
TN_DIMS = (((0,), (0,)), ((), ()))


def _params(*sem):
    return pltpu.CompilerParams(dimension_semantics=sem, vmem_limit_bytes=VMEM_LIMIT)


def _ln_rows(x, g, b):
    mu = jnp.mean(x, -1, keepdims=True)
    xc = x - mu
    var = jnp.mean(xc * xc, -1, keepdims=True)
    return xc * lax.rsqrt(var + LN_EPS) * g + b


def _split_bf16(x):
    hi = x.astype(BF16)
    lo = (x - hi.astype(F32)).astype(BF16)
    return hi, lo


def _dot3(a, b):
    ah, al = _split_bf16(a)
    bh, bl = _split_bf16(b)
    return (jnp.dot(ah, bh, preferred_element_type=F32) + jnp.dot(ah, bl, preferred_element_type=F32)
            + jnp.dot(al, bh, preferred_element_type=F32))


def _ln2_kernel(na_blocks, xa_ref, xb_ref, g_ref, b_ref, of_ref, ob_ref):
    def emit(x_ref):
        y = _ln_rows(x_ref[...], g_ref[...], b_ref[...])
        of_ref[...] = y
        ob_ref[...] = y.astype(BF16)

    first = pl.program_id(0) < na_blocks
    pl.when(first)(lambda: emit(xa_ref))
    pl.when(jnp.logical_not(first))(lambda: emit(xb_ref))


def _layernorm_pair(xa, xb, g, b, bm=512):
    d = xa.shape[1]
    na, nb = xa.shape[0] // bm, xb.shape[0] // bm
    row = pl.BlockSpec((bm, d), lambda i: (i, 0))
    vec = pl.BlockSpec((1, d), lambda i: (0, 0))
    m = (na + nb) * bm
    return pl.pallas_call(
        functools.partial(_ln2_kernel, na), grid=(na + nb,),
        in_specs=[pl.BlockSpec((bm, d), lambda i: (jnp.minimum(i, na - 1), 0)),
                  pl.BlockSpec((bm, d), lambda i: (jnp.maximum(i - na, 0), 0)), vec, vec],
        out_specs=[row, row],
        out_shape=[jax.ShapeDtypeStruct((m, d), F32), jax.ShapeDtypeStruct((m, d), BF16)],
        compiler_params=_params("arbitrary"), name="layernorm",
    )(xa, xb, g.reshape(1, d), b.reshape(1, d))


def _combine_ln_kernel(alpha, na_blocks, x_ref, y0_ref, y1_ref, gate_ref, g_ref, b_ref, oa_ref, ob_ref):
    gate = gate_ref[...]
    ff = y0_ref[...].astype(F32) * gate[:, 0:1] + y1_ref[...].astype(F32) * gate[:, 1:2]
    y = _ln_rows(alpha * x_ref[...] + ff, g_ref[...], b_ref[...])
    first = pl.program_id(0) < na_blocks

    @pl.when(first)
    def _():
        oa_ref[...] = y

    @pl.when(jnp.logical_not(first))
    def _():
        ob_ref[...] = y


def _combine_residual_ln_split(x, y0, y1, gate, g, b, alpha, n_first, bm=512):
    m, d = x.shape
    na = n_first // bm
    row = pl.BlockSpec((bm, d), lambda i: (i, 0))
    vec = pl.BlockSpec((1, d), lambda i: (0, 0))
    return pl.pallas_call(
        functools.partial(_combine_ln_kernel, alpha, na), grid=(m // bm,),
        in_specs=[row, row, row, pl.BlockSpec((bm, gate.shape[1]), lambda i: (i, 0)), vec, vec],
        out_specs=[pl.BlockSpec((bm, d), lambda i: (jnp.minimum(i, na - 1), 0)),
                   pl.BlockSpec((bm, d), lambda i: (jnp.maximum(i - na, 0), 0))],
        out_shape=[jax.ShapeDtypeStruct((n_first, d), F32), jax.ShapeDtypeStruct((m - n_first, d), F32)],
        compiler_params=_params("arbitrary"), name="moe_combine_residual_ln",
    )(x, y0, y1, gate, g.reshape(1, d), b.reshape(1, d))


def _mm_kernel(a_ref, w_ref, o_ref):
    o_ref[...] = jnp.dot(a_ref[...], w_ref[...], preferred_element_type=F32).astype(o_ref.dtype)


def _matmul(a, w, out_dtype=F32, bm=1024, bn=1024):
    m, k = a.shape
    n = w.shape[1]
    bm, bn = min(bm, m), min(bn, n)
    return pl.pallas_call(
        _mm_kernel, grid=(m // bm, n // bn),
        in_specs=[pl.BlockSpec((bm, k), lambda i, j: (i, 0)), pl.BlockSpec((k, bn), lambda i, j: (0, j))],
        out_specs=pl.BlockSpec((bm, bn), lambda i, j: (i, j)),
        out_shape=jax.ShapeDtypeStruct((m, n), out_dtype),
        compiler_params=_params("parallel", "arbitrary"), name="matmul",
    )(a, w)


def _mix_out_kernel(alpha, routed, a0_ref, a1_ref, a2_ref, w0_ref, w1_ref, w2_ref, x_ref, g_ref, b_ref, *refs):
    mix = (jnp.dot(a0_ref[...], w0_ref[...], preferred_element_type=F32)
           + jnp.dot(a1_ref[...], w1_ref[...], preferred_element_type=F32)
           + jnp.dot(a2_ref[...], w2_ref[...], preferred_element_type=F32))
    y = _ln_rows(alpha * x_ref[...] + mix, g_ref[...], b_ref[...])
    if routed:
        wr_ref, of_ref, ob_ref, lg_ref = refs
        y_hi, y_lo = _split_bf16(y)
        lg_ref[...] = (jnp.dot(y_hi, wr_ref[...], preferred_element_type=F32)
                       + jnp.dot(y_lo, wr_ref[...], preferred_element_type=F32))
    else:
        of_ref, ob_ref = refs
    of_ref[...] = y
    ob_ref[...] = y.astype(BF16)


def _mix_out_residual_ln(parts, weights, x, g, b, alpha, router_w=None, bm=512):
    m, d = x.shape
    row = pl.BlockSpec((bm, d), lambda i: (i, 0))
    vec = pl.BlockSpec((1, d), lambda i: (0, 0))
    in_specs = [pl.BlockSpec((bm, a.shape[1]), lambda i: (i, 0)) for a in parts]
    in_specs += [pl.BlockSpec(w.shape, lambda i: (0, 0)) for w in weights] + [row, vec, vec]
    args = [*parts, *weights, x, g.reshape(1, d), b.reshape(1, d)]
    out_specs = [row, row]
    out_shape = [jax.ShapeDtypeStruct((m, d), F32), jax.ShapeDtypeStruct((m, d), BF16)]
    if router_w is not None:
        in_specs.append(pl.BlockSpec(router_w.shape, lambda i: (0, 0)))
        args.append(router_w)
        out_specs.append(pl.BlockSpec((bm, router_w.shape[1]), lambda i: (i, 0)))
        out_shape.append(jax.ShapeDtypeStruct((m, router_w.shape[1]), F32))
    return pl.pallas_call(
        functools.partial(_mix_out_kernel, alpha, router_w is not None), grid=(m // bm,),
        in_specs=in_specs, out_specs=out_specs, out_shape=out_shape,
        compiler_params=_params("parallel"), name="out_proj_residual_ln",
    )(*args)


def _swiglu_acc(a_ref, w1_ref, w3_ref, w2_ref, acc_ref):
    f = pl.program_id(1)

    @pl.when(f == 0)
    def _():
        acc_ref[...] = jnp.zeros_like(acc_ref)

    a = a_ref[...]
    h1 = jnp.dot(a, w1_ref[...], preferred_element_type=F32)
    h3 = jnp.dot(a, w3_ref[...], preferred_element_type=F32)
    gated = (h1 * jax.nn.sigmoid(h1) * h3).astype(BF16)
    acc_ref[...] += jnp.dot(gated, w2_ref[...], preferred_element_type=F32)


def _ffn_kernel(alpha, a_ref, w1_ref, w3_ref, w2_ref, x_ref, g_ref, b_ref, of_ref, ob_ref, acc_ref):
    _swiglu_acc(a_ref, w1_ref, w3_ref, w2_ref, acc_ref)

    @pl.when(pl.program_id(1) == pl.num_programs(1) - 1)
    def _():
        y = _ln_rows(alpha * x_ref[...] + acc_ref[...], g_ref[...], b_ref[...])
        of_ref[...] = y
        ob_ref[...] = y.astype(BF16)


def _ffn_residual_ln(a, w1, w3, w2, x, g, b, alpha, bm=512, bf=512):
    m, d = a.shape
    dff = w1.shape[1]
    row = pl.BlockSpec((bm, d), lambda i, f: (i, 0))
    vec = pl.BlockSpec((1, d), lambda i, f: (0, 0))
    up = pl.BlockSpec((d, bf), lambda i, f: (0, f))
    return pl.pallas_call(
        functools.partial(_ffn_kernel, alpha), grid=(m // bm, dff // bf),
        in_specs=[row, up, up, pl.BlockSpec((bf, d), lambda i, f: (f, 0)), row, vec, vec],
        out_specs=[row, row],
        out_shape=[jax.ShapeDtypeStruct((m, d), F32), jax.ShapeDtypeStruct((m, d), BF16)],
        scratch_shapes=[pltpu.VMEM((bm, d), F32)],
        compiler_params=_params("parallel", "arbitrary"), name="ffn_residual_ln",
    )(a, w1, w3, w2, x, g.reshape(1, d), b.reshape(1, d))


def _moe_ffn_kernel(blk_e_ref, used_ref, a_ref, w1_ref, w3_ref, w2_ref, o_ref, acc_ref):
    del blk_e_ref
    f = pl.program_id(1)

    @pl.when(pl.program_id(0) < used_ref[0])
    def _():
        @pl.when(f == 0)
        def _():
            acc_ref[...] = jnp.zeros_like(acc_ref)

        a = a_ref[...]
        h1 = jnp.dot(a, w1_ref[...].astype(BF16), preferred_element_type=F32)
        h3 = jnp.dot(a, w3_ref[...].astype(BF16), preferred_element_type=F32)
        gated = (h1 * jax.nn.sigmoid(h1) * h3).astype(BF16)
        acc_ref[...] += jnp.dot(gated, w2_ref[...].astype(BF16), preferred_element_type=F32)

        @pl.when(f == pl.num_programs(1) - 1)
        def _():
            o_ref[...] = acc_ref[...].astype(o_ref.dtype)


def _moe_grouped_ffn(blk_e, n_used, xb, w1, w3, w2, bm, bf=512):
    cap, d = xb.shape
    dff = w1.shape[2]
    nf = dff // bf

    def blk(i, u):
        return jnp.minimum(i, u[0] - 1)

    def ftile(i, f, u):
        return jnp.where(i < u[0], f, nf - 1)

    row = pl.BlockSpec((bm, d), lambda i, f, e, u: (blk(i, u), 0), pipeline_mode=pl.Buffered(1))
    up = pl.BlockSpec((None, d, bf), lambda i, f, e, u: (e[blk(i, u)], 0, ftile(i, f, u)))
    down = pl.BlockSpec((None, bf, d), lambda i, f, e, u: (e[blk(i, u)], ftile(i, f, u), 0))
    grid_spec = pltpu.PrefetchScalarGridSpec(
        num_scalar_prefetch=2, grid=(cap // bm, nf), in_specs=[row, up, up, down],
        out_specs=row, scratch_shapes=[pltpu.VMEM((bm, d), F32)])
    return pl.pallas_call(
        _moe_ffn_kernel, grid_spec=grid_spec, out_shape=jax.ShapeDtypeStruct((cap, d), BF16),
        compiler_params=_params("arbitrary", "arbitrary"), name="moe_grouped_ffn",
    )(blk_e, n_used, xb, w1, w3, w2)


def _moe_route(logits, bm):
    n = logits.shape[0]
    top_val, top_idx = lax.top_k(logits, TOP_K)
    gate = jax.nn.softmax(top_val, axis=-1)
    nk = n * TOP_K
    assert nk % bm == 0
    experts = jnp.arange(N_EXPERTS, dtype=jnp.int32)
    flat_e = top_idx.reshape(nk).astype(jnp.int32)
    one_hot = flat_e[:, None] == experts[None, :]
    order = jnp.argsort(flat_e, stable=True).astype(jnp.int32)
    rank = jnp.argsort(order).astype(jnp.int32)
    counts = jnp.sum(one_hot.astype(jnp.int32), 0)
    padded = (counts + bm - 1) // bm * bm
    start = jnp.cumsum(counts) - counts
    pend = jnp.cumsum(padded)
    shift = jnp.sum(jnp.where(one_hot, (pend - padded - start)[None, :], 0), 1)
    dest = (rank + shift).reshape(n, TOP_K)
    n_blocks = nk // bm + N_EXPERTS
    blk_e = jnp.minimum(jnp.searchsorted(pend, jnp.arange(n_blocks, dtype=jnp.int32) * bm, side='right'),
                        N_EXPERTS - 1).astype(jnp.int32)
    pad_rank = jnp.arange(bm, dtype=jnp.int32)[None, :]
    pad_key = jnp.where(pad_rank < (padded - counts)[:, None], 2 * experts[:, None] + 1, 2 * N_EXPERTS)
    keys = jnp.concatenate([2 * flat_e, pad_key.reshape(-1)])
    toks = jnp.concatenate([jnp.arange(nk, dtype=jnp.int32) // TOP_K, jnp.zeros((N_EXPERTS * bm,), jnp.int32)])
    _, slot_tok = lax.sort((keys, toks), num_keys=1, is_stable=True)
    n_used = (pend[-1] // bm).astype(jnp.int32).reshape(1)
    return gate, dest, slot_tok, blk_e, n_used


def _moe_dispatch_ffn(logits, h_bf16, w1, w3, w2, bm=1024):
    gate, dest, slot_tok, blk_e, n_used = _moe_route(logits, bm)
    yb = _moe_grouped_ffn(blk_e, n_used, h_bf16[slot_tok], w1, w3, w2, bm)
    return yb[dest[:, 0]], yb[dest[:, 1]], gate


def _gmlp_kernel(groups, pu_ref, pv_ref, g_ref, b_ref, ws_ref, bs_ref, o_ref):
    u = jax.nn.gelu(pu_ref[...])
    v = _ln_rows(jax.nn.gelu(pv_ref[...]), g_ref[...], b_ref[...])
    for n in range(u.shape[0] // GM_CHUNK):
        rows = slice(n * GM_CHUNK, (n + 1) * GM_CHUNK)
        for grp in range(groups):
            cols = slice(grp * HEAD_DIM, (grp + 1) * HEAD_DIM)
            s = _dot3(ws_ref[grp], v[rows, cols]) + bs_ref[grp]
            o_ref[rows, cols] = (u[rows, cols] * s).astype(o_ref.dtype)


def _gmlp(p, col_u, col_v, gm_w, ln_g, ln_b, ws, bs, bt=512):
    n = p.shape[0]
    groups = ws.shape[0]
    bsb = jnp.broadcast_to(bs[:, :, None], (groups, GM_CHUNK, HEAD_DIM)).astype(F32)
    vec = pl.BlockSpec((1, gm_w), lambda i: (0, 0))
    full3 = pl.BlockSpec((groups, GM_CHUNK, HEAD_DIM), lambda i: (0, 0, 0))
    return pl.pallas_call(
        functools.partial(_gmlp_kernel, groups), grid=(n // bt,),
        in_specs=[pl.BlockSpec((bt, gm_w), lambda i: (i, col_u)), pl.BlockSpec((bt, gm_w), lambda i: (i, col_v)),
                  vec, vec, pl.BlockSpec((groups, GM_CHUNK, GM_CHUNK), lambda i: (0, 0, 0)), full3],
        out_specs=pl.BlockSpec((bt, gm_w), lambda i: (i, 0)),
        out_shape=jax.ShapeDtypeStruct((n, gm_w), BF16),
        compiler_params=_params("parallel"), name="gmlp",
    )(p, p, ln_g.reshape(1, gm_w), ln_b.reshape(1, gm_w), ws.astype(F32), bsb)


def _hgrn2_constants(c, reverse):
    n_lvl = int(math.log2(c))
    t = np.arange(c)
    ms, ws = [], []
    for lvl in range(n_lvl):
        m = c >> (lvl + 1)
        mid = (t // (2 * m)) * (2 * m) + m
        upper = t >= mid
        mat = np.zeros((c, c), np.float32)
        for r in range(c):
            if upper[r]:
                mat[r, mid[r]:r + 1] = 1.0
            else:
                mat[r, r + 1:mid[r]] = 1.0
        same = (t[:, None] // (2 * m)) == (t[None, :] // (2 * m))
        ws.append((same & upper[:, None] & ~upper[None, :]).astype(np.float32))
        ms.append(mat)
    ws.append(np.eye(c, dtype=np.float32))
    ms.append(np.tril(np.ones((c, c), np.float32)))
    ms.append(np.triu(np.ones((c, c), np.float32), 1))
    if reverse:
        ms = [a[::-1, ::-1] for a in ms]
        ws = [a[::-1, ::-1] for a in ws]
    return np.concatenate(ms, 0), np.stack(ws, 0)


def _hgrn2_kernel(heads, n_lvl, carry_row, final, reset_ref, *refs):
    if final:
        (q_ref, z_ref, i_ref, og_ref, oo_ref, loglb_ref, log1mlb_ref, onemlb_ref, ng_ref, m_ref, w_ref,
         out_ref, st_ref, e_ref) = refs
    else:
        (q_ref, z_ref, i_ref, loglb_ref, log1mlb_ref, onemlb_ref, m_ref, w_ref, out_ref, st_ref, e_ref) = refs
    n_streams, c = q_ref.shape[0], q_ref.shape[1]
    rows_e = m_ref.shape[0]
    step = pl.program_id(0)

    for s in range(n_streams):
        @pl.when(reset_ref[step * n_streams + s] == 1)
        def _():
            st_ref[s * heads:(s + 1) * heads] = jnp.zeros((heads, HEAD_DIM, HEAD_DIM), F32)

    for s in range(n_streams):
        z = z_ref[s]
        e = jnp.exp(-jnp.abs(z))
        r = 1.0 / (1.0 + e)
        log_sig = jnp.minimum(z, 0.0) - jnp.log(1.0 + e)
        sig_neg = jnp.where(z >= 0.0, e * r, r)
        a = loglb_ref[...]
        b = log1mlb_ref[...] + log_sig
        g = jnp.maximum(a, b) + jnp.log(1.0 + jnp.exp(-jnp.abs(a - b)))
        k = onemlb_ref[...] * sig_neg
        q = q_ref[s]
        qs = q * jax.nn.sigmoid(q)
        v = i_ref[s].astype(BF16)

        g_hi, g_lo = _split_bf16(g)
        m_all = m_ref[...]
        e0 = s * rows_e
        e_ref[e0:e0 + rows_e, :] = (jnp.dot(m_all, g_hi, preferred_element_type=F32)
                                    + jnp.dot(m_all, g_lo, preferred_element_type=F32))

        for h in range(heads):
            cols = slice(h * HEAD_DIM, (h + 1) * HEAD_DIM)
            qh, kh, vh = qs[:, cols], k[:, cols], v[:, cols]
            amat = w_ref[n_lvl] * lax.dot_general(qh.astype(BF16), kh.astype(BF16), NT_DIMS,
                                                  preferred_element_type=F32)
            for lvl in range(n_lvl):
                ex = jnp.exp(e_ref[e0 + lvl * c:e0 + (lvl + 1) * c, cols])
                amat = amat + w_ref[lvl] * lax.dot_general((qh * ex).astype(BF16), (kh * ex).astype(BF16),
                                                           NT_DIMS, preferred_element_type=F32)
            e_in = e_ref[e0 + n_lvl * c:e0 + (n_lvl + 1) * c, cols]
            q_in = (qh * jnp.exp(e_in)).astype(BF16)
            k_st = (kh * jnp.exp(e_ref[e0 + (n_lvl + 1) * c:e0 + (n_lvl + 2) * c, cols])).astype(BF16)
            dec = jnp.exp(e_in[carry_row:carry_row + 1, :])
            st = st_ref[s * heads + h]
            o = (lax.dot_general(q_in, st.astype(BF16), NT_DIMS, preferred_element_type=F32)
                 + jnp.dot(amat.astype(BF16), vh, preferred_element_type=F32))
            st_ref[s * heads + h] = st * dec + lax.dot_general(vh, k_st, TN_DIMS, preferred_element_type=F32)
            if final:
                o = o + oo_ref[s, :, cols]
                o = o * lax.rsqrt(jnp.mean(o * o, -1, keepdims=True) + RMS_EPS)
                og = og_ref[s, :, cols]
                out_ref[s, :, cols] = (o * ng_ref[:, cols] * (og * jax.nn.sigmoid(og))).astype(out_ref.dtype)
            else:
                out_ref[s, :, cols] = o


def _hgrn2_pass(p3, cols, hg_w, z_col, reverse, reset, lbs, extra, c):
    n_streams, n = p3.shape[0], p3.shape[1]
    nc = n // c
    heads = hg_w // HEAD_DIM
    n_lvl = int(math.log2(c))
    m_np, w_np = _hgrn2_constants(c, reverse)
    final = extra is not None

    def cmap(i):
        return (nc - 1 - i) if reverse else i

    def tok(col):
        return pl.BlockSpec((n_streams, c, hg_w), lambda i, r: (0, cmap(i), col))

    vec = pl.BlockSpec((1, hg_w), lambda i, r: (0, 0))
    in_specs = [tok(cols["q"]), tok(z_col), tok(cols["i"])]
    args = [p3, p3, p3]
    if final:
        o_other, norm_g = extra
        in_specs += [tok(cols["og"]), tok(0)]
        args += [p3, o_other]
    in_specs += [vec, vec, vec]
    args += list(lbs)
    if final:
        in_specs += [vec]
        args += [norm_g.reshape(1, hg_w).astype(F32)]
    in_specs += [pl.BlockSpec(m_np.shape, lambda i, r: (0, 0)), pl.BlockSpec(w_np.shape, lambda i, r: (0, 0, 0))]
    args += [jnp.asarray(m_np, BF16), jnp.asarray(w_np, F32)]
    grid_spec = pltpu.PrefetchScalarGridSpec(
        num_scalar_prefetch=1, grid=(nc,), in_specs=in_specs, out_specs=tok(0),
        scratch_shapes=[pltpu.VMEM((n_streams * heads, HEAD_DIM, HEAD_DIM), F32),
                        pltpu.VMEM((n_streams * m_np.shape[0], hg_w), F32)])
    order = np.arange(nc)[::-1] if reverse else np.arange(nc)
    reset_steps = jnp.asarray(np.asarray(reset, np.int32)[order].reshape(-1))
    return pl.pallas_call(
        functools.partial(_hgrn2_kernel, heads, n_lvl, 0 if reverse else c - 1, final), grid_spec=grid_spec,
        out_shape=jax.ShapeDtypeStruct((n_streams, n, hg_w), BF16 if final else F32),
        compiler_params=_params("arbitrary"), name="hgrn2_fwd" if final else "hgrn2_bwd",
    )(reset_steps, *args)


def _hgrn2(p, cols, hg_w, lb, norm_g, seq_lens, n_streams=2, c=HG_CHUNK):
    lb = lb.astype(F32).reshape(1, hg_w)
    lbs = (jnp.log(lb), jnp.log1p(-lb), 1.0 - lb)
    starts = np.cumsum([0] + list(seq_lens))
    n = int(starts[-1])
    per = n // n_streams
    assert n == p.shape[0] and all(s * per in starts for s in range(n_streams)), "a sequence straddles a stream cut"
    first = np.zeros(n // c, np.int32)
    last = np.zeros(n // c, np.int32)
    first[starts[:-1] // c] = 1
    last[starts[1:] // c - 1] = 1
    first = first.reshape(n_streams, per // c).T
    last = last.reshape(n_streams, per // c).T
    p3 = p.reshape(n_streams, per, p.shape[1])
    o_b = _hgrn2_pass(p3, cols, hg_w, cols["zb"], True, last, lbs, None, c)
    return _hgrn2_pass(p3, cols, hg_w, cols["zf"], False, first, lbs, (o_b, norm_g), c).reshape(n, hg_w)


def _dft_matrices(t):
    n = 2 * t
    f = jnp.arange(t, dtype=jnp.int32)
    ang = (2.0 * math.pi / n) * ((f[:, None] * f[None, :]) % n).astype(F32)
    cos, sin = jnp.cos(ang), jnp.sin(ang)
    alt = (1 - 2 * (f % 2)).astype(F32)
    first = f == 0
    fwd = jnp.stack([cos, jnp.where(first[:, None], alt[None, :], -sin)], 0).astype(BF16)
    wgt = jnp.where(first, 1.0, 2.0)[None, :] / n
    inv = jnp.stack([cos * wgt, jnp.where(first[None, :], alt[:, None] / n, -sin * wgt)], 0).astype(BF16)
    return fwd, inv


def _shortconv_kernel(flags_ref, x_ref, prev_ref, next_ref, w_ref, b_ref, o_ref):
    i = pl.program_id(0)
    x = x_ref[...]
    t = x.shape[0]
    rows = lax.broadcasted_iota(jnp.int32, x.shape, 0)
    has_prev = flags_ref[2 * i].astype(F32)
    has_next = flags_ref[2 * i + 1].astype(F32)
    up = jnp.where(rows == 0, prev_ref[7:8, :] * has_prev, pltpu.roll(x, 1, 0))
    dn = jnp.where(rows == t - 1, next_ref[0:1, :] * has_next, pltpu.roll(x, t - 1, 0))
    o_ref[...] = up * w_ref[0:1, :] + x * w_ref[1:2, :] + dn * w_ref[2:3, :] + b_ref[...]


def _shortconv(p, width, conv_w, conv_b, blk_flags, t=HY_BLOCK, ct=768):
    n = p.shape[0]
    nb = n // t
    r8 = t // 8
    grid_spec = pltpu.PrefetchScalarGridSpec(
        num_scalar_prefetch=1, grid=(nb, width // ct),
        in_specs=[pl.BlockSpec((t, ct), lambda i, j, f: (i, j)),
                  pl.BlockSpec((8, ct), lambda i, j, f: (jnp.maximum(i * r8 - 1, 0), j)),
                  pl.BlockSpec((8, ct), lambda i, j, f: (jnp.minimum((i + 1) * r8, nb * r8 - 1), j)),
                  pl.BlockSpec((3, ct), lambda i, j, f: (0, j)), pl.BlockSpec((1, ct), lambda i, j, f: (0, j))],
        out_specs=pl.BlockSpec((t, ct), lambda i, j, f: (i, j)))
    return pl.pallas_call(
        _shortconv_kernel, grid_spec=grid_spec, out_shape=jax.ShapeDtypeStruct((n, width), F32),
        compiler_params=_params("parallel", "parallel"), name="hyena_shortconv",
    )(blk_flags, p, p, p, conv_w.astype(F32), conv_b.reshape(1, width).astype(F32))


def _filter_kernel(l_total, feats_ref, w1_ref, b1_ref, w2_ref, b2_ref, w3_ref, b3_ref, fr_ref, wo_ref, dl_ref,
                   h_ref, sum_ref):
    del l_total
    i = pl.program_id(0)
    bt = feats_ref.shape[0]
    half_lanes = LANES // 2
    fr = fr_ref[...]
    feats = feats_ref[...]
    h = jnp.sin(fr * (_dot3(feats, w1_ref[...]) + b1_ref[...]))
    h = jnp.sin(fr * (_dot3(h, w2_ref[...]) + b2_ref[...]))
    h = jnp.sin(fr * (_dot3(h, w3_ref[...]) + b3_ref[...]))
    reps = wo_ref.shape[2] // dl_ref.shape[1]
    row = lax.broadcasted_iota(jnp.int32, (bt, 1), 0) + i * bt
    total = jnp.zeros((1, wo_ref.shape[2]), F32)
    for side in range(2):
        window = jnp.exp(-feats[:, side * half_lanes:side * half_lanes + 1] * dl_ref[...])
        out = _dot3(h, wo_ref[side]) * jnp.concatenate([window] * reps, axis=1)
        if side == 1:
            out = jnp.where(row == 0, 0.0, out)
        h_ref[side] = out
        total = total + jnp.sum(jnp.abs(out), axis=0, keepdims=True)

    @pl.when(i == 0)
    def _():
        sum_ref[...] = jnp.zeros_like(sum_ref)

    sum_ref[...] += total


def _hyena_filter(l, w1, b1, w2, b2, w3, b3, freq, w_out, hy_w, bt=512):
    hid = w2.shape[0]
    half_lanes = LANES // 2
    assert hid <= half_lanes and w1.shape[0] <= half_lanes
    half = HY_ORDER * hy_w
    bands = jnp.linspace(1e-4, HY_POS_BANDS - 1, HY_POS_BANDS, dtype=F32)

    def features(pos):
        tt = pos / max(l - 1, 1)
        ang = (2.0 * math.pi / l) * pos[:, None] * bands[None, :]
        f = jnp.concatenate([tt[:, None], jnp.cos(ang), -jnp.sin(ang)], -1)
        return jnp.pad(f, ((0, 0), (0, half_lanes - f.shape[1])))

    def twice_diag(w):
        wp = jnp.pad(w.astype(F32), ((0, half_lanes - w.shape[0]), (0, half_lanes - w.shape[1])))
        z = jnp.zeros_like(wp)
        return jnp.concatenate([jnp.concatenate([wp, z], 1), jnp.concatenate([z, wp], 1)], 0)

    def twice_vec(a):
        ap = jnp.pad(a.astype(F32), (0, half_lanes - a.shape[0]))
        return jnp.concatenate([ap, ap]).reshape(1, LANES)

    pos = jnp.arange(l, dtype=F32)
    feats = jnp.concatenate([features(pos), features(l - pos)], 1)
    wo = w_out.astype(F32).reshape(hid, HY_ORDER, 2, hy_w).transpose(2, 0, 1, 3).reshape(2, hid, half)
    wo = jnp.pad(wo, ((0, 0), (0, half_lanes - hid), (0, 0)))
    zero = jnp.zeros_like(wo[0])
    wop = jnp.stack([jnp.concatenate([wo[0], zero], 0), jnp.concatenate([zero, wo[1]], 0)], 0)
    deltas = jnp.abs(jnp.linspace(math.log(HY_DECAY_TARGET) / HY_SLOW_DECAY,
                                  math.log(HY_DECAY_TARGET) / HY_FAST_DECAY, hy_w, dtype=F32)).reshape(1, hy_w)
    bt = min(bt, l)
    sq = pl.BlockSpec((LANES, LANES), lambda i: (0, 0))
    vec = pl.BlockSpec((1, LANES), lambda i: (0, 0))
    taps, abs_sum = pl.pallas_call(
        functools.partial(_filter_kernel, l), grid=(l // bt,),
        in_specs=[pl.BlockSpec((bt, LANES), lambda i: (i, 0)), sq, vec, sq, vec, sq, vec, vec,
                  pl.BlockSpec((2, LANES, half), lambda i: (0, 0, 0)),
                  pl.BlockSpec((1, hy_w), lambda i: (0, 0))],
        out_specs=[pl.BlockSpec((2, bt, half), lambda i: (0, i, 0)), pl.BlockSpec((1, half), lambda i: (0, 0))],
        out_shape=[jax.ShapeDtypeStruct((2, l, half), F32), jax.ShapeDtypeStruct((1, half), F32)],
        compiler_params=_params("arbitrary"), name="hyena_filter",
    )(feats, twice_diag(w1), twice_vec(b1), twice_diag(w2), twice_vec(b2), twice_diag(w3), twice_vec(b3),
      twice_vec(freq), wop, deltas)
    return taps.reshape(2 * l, half), abs_sum


def _dft_cols(t, width):
    return width if t <= 1024 else 256


def _dft_kernel(f_ref, x_ref, o_ref):
    x = x_ref[...].astype(BF16)
    o_ref[0] = jnp.dot(f_ref[0], x, preferred_element_type=F32).astype(o_ref.dtype)
    o_ref[1] = jnp.dot(f_ref[1], x, preferred_element_type=F32).astype(o_ref.dtype)


def _block_dft(fwd, x, col_off, width, out_dtype, t=HY_BLOCK, ct=256):
    nseg = x.shape[0] // t
    ct = min(ct, width)
    return pl.pallas_call(
        _dft_kernel, grid=(nseg, width // ct),
        in_specs=[pl.BlockSpec((2, t, t), lambda s, j: (0, 0, 0)),
                  pl.BlockSpec((t, ct), lambda s, j: (s, j + col_off))],
        out_specs=pl.BlockSpec((None, 2, t, ct), lambda s, j: (s, 0, 0, j)),
        out_shape=jax.ShapeDtypeStruct((nseg, 2, t, width), out_dtype),
        compiler_params=_params("parallel", "parallel"), name="hyena_block_dft",
    )(fwd, x)


def _segment_index(d, nb):
    return d if d >= 0 else 2 * nb + d


def _hyena_filter_spectra(l, fwd, filt_w, hy_w, t=HY_BLOCK):
    nb = l // t
    taps, abs_sum = _hyena_filter(l, *filt_w, hy_w)
    half = taps.shape[1]
    seg_spec = _block_dft(fwd, taps, 0, half, F32, t, _dft_cols(t, hy_w))
    first_rows = taps.reshape(2 * nb, t, half)[:, 0, :]
    seg0 = jnp.stack([first_rows[_segment_index(d - 1, nb)] for d in range(-(nb - 1), nb)], 0)
    return seg_spec, seg0.reshape(2 * nb - 1, 1, half), 1.0 / abs_sum


MIX_ROWS = 8


def _mix_kernel(nb, z_ref, a_ref, s0_ref, inv_ref, *refs):
    o_ref, k_ref = refs[-2], refs[-1]
    ft, ct = z_ref.shape[2], z_ref.shape[3]
    first_tile = pl.program_id(1) == 0

    tile_rows = lax.broadcasted_iota(jnp.int32, (ft, ct), 0)
    sgn = (1 - 2 * (tile_rows & 1)).astype(F32)
    real_row = jnp.logical_and(tile_rows == 0, first_tile)
    inv = inv_ref[...]
    for slot in range(2 * nb - 1):
        d = slot - (nb - 1)
        ia, ib = _segment_index(d, nb), _segment_index(d - 1, nb)
        s0 = s0_ref[slot]
        k_ref[slot, 0] = (a_ref[ia, 0] + sgn * (a_ref[ib, 0] - s0)) * inv
        k_ref[slot, 1] = (a_ref[ia, 1] + sgn * (a_ref[ib, 1] - jnp.where(real_row, s0, 0.0))) * inv

    n_out = 2 if nb % 2 == 0 else 1

    def out_rows(i, r0, masked):
        rows = pl.ds(r0, MIX_ROWS)
        acc = [[jnp.zeros((MIX_ROWS, ct), F32) for _ in range(2)] for _ in range(n_out)]
        if masked:
            row0 = jnp.logical_and(lax.broadcasted_iota(jnp.int32, (MIX_ROWS, ct), 0) == 0, first_tile)
        d_hi = i + (n_out - 1) + (nb - 1)
        k_prev = (k_ref[d_hi, 0, rows, :], k_ref[d_hi, 1, rows, :]) if n_out == 2 else None
        for j in range(nb):
            d = i - j + (nb - 1)
            zt, zb = z_ref[j, 0, rows, :], z_ref[j, 1, rows, :]
            k_cur = (k_ref[d, 0, rows, :], k_ref[d, 1, rows, :])
            for o, (kt, kb) in enumerate((k_cur, k_prev)[:n_out]):
                bb = zb * kb
                if masked:
                    acc[o][0] = acc[o][0] + (zt * kt - jnp.where(row0, 0.0, bb))
                    acc[o][1] = acc[o][1] + jnp.where(row0, bb, zt * kb + zb * kt)
                else:
                    acc[o][0] = acc[o][0] + (zt * kt - bb)
                    acc[o][1] = acc[o][1] + (zt * kb + zb * kt)
            k_prev = k_cur
        for o in range(n_out):
            o_ref[i + o, 0, rows, :] = acc[o][0].astype(o_ref.dtype)
            o_ref[i + o, 1, rows, :] = acc[o][1].astype(o_ref.dtype)

    def all_outputs(r0, masked):
        def body(ip, carry):
            out_rows(ip * n_out, r0, masked)
            return carry
        lax.fori_loop(0, nb // n_out, body, 0)

    all_outputs(0, True)

    def chunk(r, carry):
        all_outputs(pl.multiple_of(r * MIX_ROWS, MIX_ROWS), False)
        return carry

    lax.fori_loop(1, ft // MIX_ROWS, chunk, 0)


def _hyena_mix(zspec, kspec, k_col, n_seq, nb, blk_off, prev, ct=256):
    seg_spec, seg0, inv_norm = kspec
    nblk, _, t, width = zspec.shape
    ft = max(MIX_ROWS, min(t, 2048 // nb))
    nct = width // ct
    assert blk_off % nb == 0 and t % ft == 0
    s_off = blk_off // nb
    zblk = pl.BlockSpec((nb, 2, ft, ct), lambda s, fi, j: (s + s_off, 0, fi, j))
    in_specs = [zblk,
                pl.BlockSpec((2 * nb, 2, ft, ct), lambda s, fi, j: (0, 0, fi, j + k_col * nct)),
                pl.BlockSpec((2 * nb - 1, 1, ct), lambda s, fi, j: (0, 0, j + k_col * nct)),
                pl.BlockSpec((1, ct), lambda s, fi, j: (0, j + k_col * nct))]
    args = [zspec, seg_spec, seg0, inv_norm]
    aliases = {}
    if prev is not None:
        in_specs.append(pl.BlockSpec(memory_space=pl.ANY))
        args.append(prev)
        aliases = {4: 0}
    return pl.pallas_call(
        functools.partial(_mix_kernel, nb), grid=(n_seq, t // ft, nct),
        in_specs=in_specs, out_specs=zblk,
        out_shape=jax.ShapeDtypeStruct(zspec.shape, BF16), input_output_aliases=aliases,
        scratch_shapes=[pltpu.VMEM((2 * nb - 1, 2, ft, ct), F32)],
        compiler_params=_params("parallel", "parallel", "parallel"), name="hyena_mix",
    )(*args)


def _inverse_kernel(y_ref, fi_ref, zin_ref, gate_ref, bias_ref, o_ref):
    y = (jnp.dot(fi_ref[0], y_ref[0], preferred_element_type=F32)
         + jnp.dot(fi_ref[1], y_ref[1], preferred_element_type=F32))
    o_ref[...] = (gate_ref[...] * (y + zin_ref[...] * bias_ref[...])).astype(o_ref.dtype)


def _hyena_inverse(yspec, inv, zin, zin_col, gate, gate_col, bias, out_dtype, ct=256):
    nblk, _, t, width = yspec.shape
    nct = width // ct
    return pl.pallas_call(
        _inverse_kernel, grid=(nblk, nct),
        in_specs=[pl.BlockSpec((None, 2, t, ct), lambda i, j: (i, 0, 0, j)),
                  pl.BlockSpec((2, t, t), lambda i, j: (0, 0, 0), pipeline_mode=pl.Buffered(1)),
                  pl.BlockSpec((t, ct), lambda i, j: (i, j + zin_col * nct)),
                  pl.BlockSpec((t, ct), lambda i, j: (i, j + gate_col * nct)),
                  pl.BlockSpec((1, ct), lambda i, j: (0, j))],
        out_specs=pl.BlockSpec((t, ct), lambda i, j: (i, j)),
        out_shape=jax.ShapeDtypeStruct((nblk * t, width), out_dtype),
        compiler_params=_params("parallel", "arbitrary"), name="hyena_inverse",
    )(yspec, inv, zin, gate, bias)


def _hyena(p, hy_w, groups, conv_w, conv_b, filt_w, bias, t=HY_BLOCK):
    flags = []
    for n_seq, l in groups:
        nb = l // t
        flags += [int(bi > 0) if side == 0 else int(bi < nb - 1)
                  for _ in range(n_seq) for bi in range(nb) for side in range(2)]
    fwd, inv = _dft_matrices(t)
    u = _shortconv(p, 3 * hy_w, conv_w, conv_b, jnp.asarray(np.asarray(flags, np.int32)), t, ct=hy_w)
    kspecs = [_hyena_filter_spectra(l, fwd, filt_w, hy_w, t) for _, l in groups]
    bias = bias.astype(F32)
    ct = _dft_cols(t, hy_w)
    z, z_col = u, 2
    for o in range(HY_ORDER):
        zspec = _block_dft(fwd, z, z_col * (hy_w // ct), hy_w, F32, t, ct)
        yspec, blk = None, 0
        for (n_seq, l), kspec in zip(groups, kspecs):
            yspec = _hyena_mix(zspec, kspec, o, n_seq, l // t, blk, yspec)
            blk += n_seq * (l // t)
        last = o == HY_ORDER - 1
        z = _hyena_inverse(yspec, inv, z, z_col, u, o, bias[o:o + 1], BF16 if last else F32, ct)
        z_col = 0
    return z


def kernel(x_prompt, x_sample, ln_in_g, ln_in_b, w_in, hy_conv_w, hy_conv_b, hy_pos_w1, hy_pos_b1, hy_pos_w2, hy_pos_b2, hy_pos_w3, hy_pos_b3, hy_sin_freq, hy_pos_wout, hy_bias, gm_ln_g, gm_ln_b, gm_ws, gm_bs, hg_lb_raw, hg_norm_g, w_out, ln1_g, ln1_b, ln2_g, ln2_b, ffn_w1, ffn_w3, ffn_w2, moe_router_w, moe_router_b, moe_w1, moe_w3, moe_w2):
    depth, d_model, in_w = w_in.shape
    hy_w = hy_bias.shape[-1]
    gm_w = gm_ln_g.shape[-1]
    hg_w = hg_norm_g.shape[-1]
    o1 = 3 * hy_w
    o2 = o1 + 2 * gm_w
    alpha = (2 * depth) ** 0.25
    groups = [(x_prompt.shape[0], x_prompt.shape[1]), (x_sample.shape[0], x_sample.shape[1])]
    seq_lens = [l for n_seq, l in groups for _ in range(n_seq)]
    n_prompt = x_prompt.shape[0] * x_prompt.shape[1]
    hg_off = o1 // hg_w
    hg_cols = {"q": hg_off, "zf": hg_off + 1, "zb": hg_off + 2, "i": hg_off + 3, "og": hg_off + 4}
    gm_off = (o1 + 5 * hg_w) // gm_w

    lb_all = jnp.cumsum(jax.nn.softmax(hg_lb_raw.astype(F32), axis=0), axis=0)
    lb_all = lb_all - lb_all[:1]

    x, xb = _layernorm_pair(x_prompt.reshape(-1, d_model), x_sample.reshape(-1, d_model), ln_in_g, ln_in_b)
    for l in range(depth):
        w_l = jnp.concatenate([w_in[l][:, :o1], w_in[l][:, o2:], w_in[l][:, o1:o2]], 1).astype(BF16)
        p = _matmul(xb, w_l)
        filt_w = (hy_pos_w1[l], hy_pos_b1[l], hy_pos_w2[l], hy_pos_b2[l], hy_pos_w3[l], hy_pos_b3[l],
                  hy_sin_freq[l], hy_pos_wout[l])
        y_hy = _hyena(p, hy_w, groups, hy_conv_w[l], hy_conv_b[l], filt_w, hy_bias[l])
        y_hg = _hgrn2(p, hg_cols, hg_w, lb_all[l], hg_norm_g[l], seq_lens)
        y_gm = _gmlp(p, gm_off, gm_off + 1, gm_w, gm_ln_g[l], gm_ln_b[l], gm_ws[l], gm_bs[l])
        wo = w_out[l].astype(BF16)
        parts = [y_hy, y_gm, y_hg]
        wo_parts = [wo[:hy_w], wo[hy_w:hy_w + gm_w], wo[hy_w + gm_w:]]
        j = l // 2
        if l % 2 == 0:
            x, xb = _mix_out_residual_ln(parts, wo_parts, x, ln1_g[l], ln1_b[l], alpha)
            x, xb = _ffn_residual_ln(xb, ffn_w1[j].astype(BF16), ffn_w3[j].astype(BF16), ffn_w2[j].astype(BF16),
                                     x, ln2_g[l], ln2_b[l], alpha)
        else:
            n_exp = moe_router_w.shape[-1]
            wr = jnp.pad(moe_router_w[j].astype(BF16), ((0, 0), (0, LANES - n_exp)))
            x, xb, logits = _mix_out_residual_ln(parts, wo_parts, x, ln1_g[l], ln1_b[l], alpha, wr)
            logits = logits[:, :n_exp] + moe_router_b[j].astype(F32)
            y0, y1, gate = _moe_dispatch_ffn(logits, xb, moe_w1[j], moe_w3[j], moe_w2[j])
            xa, xs = _combine_residual_ln_split(x, y0, y1, gate, ln2_g[l], ln2_b[l], alpha, n_prompt)
            if l == depth - 1:
                return (xa.reshape(x_prompt.shape), xs.reshape(x_sample.shape))
            x = jnp.concatenate([xa, xs], 0)
            xb = x.astype(BF16)
    return (x[:n_prompt].reshape(x_prompt.shape), x[n_prompt:].reshape(x_sample.shape))
```

```python
import functools
import math

import numpy as np
import jax
import jax.numpy as jnp
from jax import lax
from jax.experimental import pallas as pl
from jax.experimental.pallas import tpu as pltpu
from jax.experimental.pallas import tpu_sc as plsc

HEAD_DIM = 128
HY_ORDER = 2
HY_POS_BANDS = 16
HY_FAST_DECAY = 0.3
HY_SLOW_DECAY = 1.5
HY_DECAY_TARGET = 1e-2
GM_CHUNK = 128
N_EXPERTS = 8
TOP_K = 2
LN_EPS = 1e-5
RMS_EPS = 1e-6

HY_BLOCK = 1024
HG_CHUNK = 128
LANES = 128

V7X_VMEM_BYTES = 64 * 1024 * 1024
VMEM_LIMIT = V7X_VMEM_BYTES - 8 * 1024 * 1024

F32 = jnp.float32
BF16 = jnp.bfloat16
NT_DIMS = (((1,), (1,)), ((), ()))
TN_DIMS = (((0,), (0,)), ((), ()))


def _params(*sem):
    return pltpu.CompilerParams(dimension_semantics=sem, vmem_limit_bytes=VMEM_LIMIT)


def _ln_rows(x, g, b):
    mu = jnp.mean(x, -1, keepdims=True)
    xc = x - mu
    var = jnp.mean(xc * xc, -1, keepdims=True)
    return xc * lax.rsqrt(var + LN_EPS) * g + b


def _split_bf16(x):
    hi = x.astype(BF16)
    lo = (x - hi.astype(F32)).astype(BF16)
    return hi, lo


def _dot3(a, b):
    ah, al = _split_bf16(a)
    bh, bl = _split_bf16(b)
    return (jnp.dot(ah, bh, preferred_element_type=F32) + jnp.dot(ah, bl, preferred_element_type=F32)
            + jnp.dot(al, bh, preferred_element_type=F32))


def _pack_bf16_pairs(y):
    w = y.shape[1] // 2
    bits = lax.bitcast_convert_type(y.astype(BF16).astype(F32), jnp.uint32)
    word = (bits[:, :w] >> 16) | (bits[:, w:] & jnp.uint32(0xFFFF0000))
    return lax.bitcast_convert_type(word, jnp.int32)


def _unpack_bf16_pairs(word):
    u = lax.bitcast_convert_type(word, jnp.uint32)
    lo = lax.bitcast_convert_type(u << 16, F32)
    hi = lax.bitcast_convert_type(u & jnp.uint32(0xFFFF0000), F32)
    return jnp.concatenate([lo, hi], axis=1)


SC_GATHER_WINDOW = 128
SC_GATHER_WORDS = 256


def _gather_rows(xw, idx):
    n, words = xw.shape
    m = idx.shape[0]
    parts = words // SC_GATHER_WORDS
    steps = m * parts // SC_GATHER_WINDOW
    assert words % SC_GATHER_WORDS == 0 and (m * parts) % (2 * SC_GATHER_WINDOW) == 0
    half = steps // 2
    piece_idx = (idx.astype(jnp.int32)[:, None] * parts + jnp.arange(parts, dtype=jnp.int32)[None, :])
    mesh = plsc.VectorSubcoreMesh(core_axis_name="core", subcore_axis_name="subcore")

    @pl.kernel(out_type=jax.ShapeDtypeStruct((m * parts, SC_GATHER_WORDS), xw.dtype), mesh=mesh)
    def gather(x_hbm, i_hbm, o_hbm):
        def body(i_vmem, o_vmem):
            pltpu.sync_copy(x_hbm.at[i_vmem.at[0]], o_vmem)

        pltpu.emit_pipeline(
            body, grid=(2, half),
            in_specs=[pl.BlockSpec((1, SC_GATHER_WINDOW), index_map=lambda c, i: (0, c * half + i))],
            out_specs=[pl.BlockSpec((SC_GATHER_WINDOW, SC_GATHER_WORDS), index_map=lambda c, i: (c * half + i, 0))],
            core_axis_name=("core", "subcore"), dimension_semantics=(pltpu.PARALLEL, pltpu.PARALLEL),
        )(i_hbm, o_hbm)

    out = gather(xw.reshape(n * parts, SC_GATHER_WORDS), piece_idx.reshape(1, m * parts))
    return out.reshape(m, words)


def _ln2_kernel(na_blocks, xa_ref, xb_ref, g_ref, b_ref, of_ref, ob_ref):
    def emit(x_ref):
        y = _ln_rows(x_ref[...], g_ref[...], b_ref[...])
        of_ref[...] = y
        ob_ref[...] = y.astype(BF16)

    first = pl.program_id(0) < na_blocks
    pl.when(first)(lambda: emit(xa_ref))
    pl.when(jnp.logical_not(first))(lambda: emit(xb_ref))


def _layernorm_pair(xa, xb, g, b, bm=512):
    d = xa.shape[1]
    na, nb = xa.shape[0] // bm, xb.shape[0] // bm
    row = pl.BlockSpec((bm, d), lambda i: (i, 0))
    vec = pl.BlockSpec((1, d), lambda i: (0, 0))
    m = (na + nb) * bm
    return pl.pallas_call(
        functools.partial(_ln2_kernel, na), grid=(na + nb,),
        in_specs=[pl.BlockSpec((bm, d), lambda i: (jnp.minimum(i, na - 1), 0)),
                  pl.BlockSpec((bm, d), lambda i: (jnp.maximum(i - na, 0), 0)), vec, vec],
        out_specs=[row, row],
        out_shape=[jax.ShapeDtypeStruct((m, d), F32), jax.ShapeDtypeStruct((m, d), BF16)],
        compiler_params=_params("arbitrary"), name="layernorm",
    )(xa, xb, g.reshape(1, d), b.reshape(1, d))


def _combine_ln_kernel(alpha, na_blocks, x_ref, y_ref, gate_ref, g_ref, b_ref, oa_ref, ob_ref):
    gate = gate_ref[...]
    words = y_ref.shape[1] // 2
    ff = (_unpack_bf16_pairs(y_ref[:, :words]) * gate[:, 0:1]
          + _unpack_bf16_pairs(y_ref[:, words:]) * gate[:, 1:2])
    y = _ln_rows(alpha * x_ref[...] + ff, g_ref[...], b_ref[...])
    first = pl.program_id(0) < na_blocks

    @pl.when(first)
    def _():
        oa_ref[...] = y

    @pl.when(jnp.logical_not(first))
    def _():
        ob_ref[...] = y


def _combine_residual_ln_split(x, y01, gate, g, b, alpha, n_first, bm=512):
    m, d = x.shape
    na = n_first // bm
    row = pl.BlockSpec((bm, d), lambda i: (i, 0))
    vec = pl.BlockSpec((1, d), lambda i: (0, 0))
    return pl.pallas_call(
        functools.partial(_combine_ln_kernel, alpha, na), grid=(m // bm,),
        in_specs=[row, row, pl.BlockSpec((bm, gate.shape[1]), lambda i: (i, 0)), vec, vec],
        out_specs=[pl.BlockSpec((bm, d), lambda i: (jnp.minimum(i, na - 1), 0)),
                   pl.BlockSpec((bm, d), lambda i: (jnp.maximum(i - na, 0), 0))],
        out_shape=[jax.ShapeDtypeStruct((n_first, d), F32), jax.ShapeDtypeStruct((m - n_first, d), F32)],
        compiler_params=_params("arbitrary"), name="moe_combine_residual_ln",
    )(x, y01, gate, g.reshape(1, d), b.reshape(1, d))


def _mm_kernel(a_ref, w_ref, o_ref):
    o_ref[...] = jnp.dot(a_ref[...], w_ref[...], preferred_element_type=F32).astype(o_ref.dtype)


def _matmul(a, w, out_dtype=F32, bm=1024, bn=1024):
    m, k = a.shape
    n = w.shape[1]
    bm, bn = min(bm, m), min(bn, n)
    return pl.pallas_call(
        _mm_kernel, grid=(m // bm, n // bn),
        in_specs=[pl.BlockSpec((bm, k), lambda i, j: (i, 0)), pl.BlockSpec((k, bn), lambda i, j: (0, j))],
        out_specs=pl.BlockSpec((bm, bn), lambda i, j: (i, j)),
        out_shape=jax.ShapeDtypeStruct((m, n), out_dtype),
        compiler_params=_params("parallel", "arbitrary"), name="matmul",
    )(a, w)


def _mix_out_kernel(alpha, routed, a0_ref, a1_ref, a2_ref, w0_ref, w1_ref, w2_ref, x_ref, g_ref, b_ref, *refs):
    mix = (jnp.dot(a0_ref[...], w0_ref[...], preferred_element_type=F32)
           + jnp.dot(a1_ref[...], w1_ref[...], preferred_element_type=F32)
           + jnp.dot(a2_ref[...], w2_ref[...], preferred_element_type=F32))
    y = _ln_rows(alpha * x_ref[...] + mix, g_ref[...], b_ref[...])
    if routed:
        wr_ref, of_ref, ob_ref, lg_ref = refs
        y_hi, y_lo = _split_bf16(y)
        lg_ref[...] = (jnp.dot(y_hi, wr_ref[...], preferred_element_type=F32)
                       + jnp.dot(y_lo, wr_ref[...], preferred_element_type=F32))
        ob_ref[...] = _pack_bf16_pairs(y)
    else:
        of_ref, ob_ref = refs
        ob_ref[...] = y.astype(BF16)
    of_ref[...] = y


def _mix_out_residual_ln(parts, weights, x, g, b, alpha, router_w=None, bm=512):
    m, d = x.shape
    row = pl.BlockSpec((bm, d), lambda i: (i, 0))
    vec = pl.BlockSpec((1, d), lambda i: (0, 0))
    in_specs = [pl.BlockSpec((bm, a.shape[1]), lambda i: (i, 0)) for a in parts]
    in_specs += [pl.BlockSpec(w.shape, lambda i: (0, 0)) for w in weights] + [row, vec, vec]
    args = [*parts, *weights, x, g.reshape(1, d), b.reshape(1, d)]
    out_specs = [row, row]
    out_shape = [jax.ShapeDtypeStruct((m, d), F32), jax.ShapeDtypeStruct((m, d), BF16)]
    if router_w is not None:
        in_specs.append(pl.BlockSpec(router_w.shape, lambda i: (0, 0)))
        args.append(router_w)
        out_specs[1] = pl.BlockSpec((bm, d // 2), lambda i: (i, 0))
        out_shape[1] = jax.ShapeDtypeStruct((m, d // 2), jnp.int32)
        out_specs.append(pl.BlockSpec((bm, router_w.shape[1]), lambda i: (i, 0)))
        out_shape.append(jax.ShapeDtypeStruct((m, router_w.shape[1]), F32))
    return pl.pallas_call(
        functools.partial(_mix_out_kernel, alpha, router_w is not None), grid=(m // bm,),
        in_specs=in_specs, out_specs=out_specs, out_shape=out_shape,
        compiler_params=_params("parallel"), name="out_proj_residual_ln",
    )(*args)


def _swiglu_acc(a_ref, w1_ref, w3_ref, w2_ref, acc_ref):
    f = pl.program_id(1)

    @pl.when(f == 0)
    def _():
        acc_ref[...] = jnp.zeros_like(acc_ref)

    a = a_ref[...]
    h1 = jnp.dot(a, w1_ref[...], preferred_element_type=F32)
    h3 = jnp.dot(a, w3_ref[...], preferred_element_type=F32)
    gated = (h1 * jax.nn.sigmoid(h1) * h3).astype(BF16)
    acc_ref[...] += jnp.dot(gated, w2_ref[...], preferred_element_type=F32)


def _ffn_kernel(alpha, a_ref, w1_ref, w3_ref, w2_ref, x_ref, g_ref, b_ref, of_ref, ob_ref, acc_ref):
    _swiglu_acc(a_ref, w1_ref, w3_ref, w2_ref, acc_ref)

    @pl.when(pl.program_id(1) == pl.num_programs(1) - 1)
    def _():
        y = _ln_rows(alpha * x_ref[...] + acc_ref[...], g_ref[...], b_ref[...])
        of_ref[...] = y
        ob_ref[...] = y.astype(BF16)


def _ffn_residual_ln(a, w1, w3, w2, x, g, b, alpha, bm=512, bf=512):
    m, d = a.shape
    dff = w1.shape[1]
    row = pl.BlockSpec((bm, d), lambda i, f: (i, 0))
    vec = pl.BlockSpec((1, d), lambda i, f: (0, 0))
    up = pl.BlockSpec((d, bf), lambda i, f: (0, f))
    return pl.pallas_call(
        functools.partial(_ffn_kernel, alpha), grid=(m // bm, dff // bf),
        in_specs=[row, up, up, pl.BlockSpec((bf, d), lambda i, f: (f, 0)), row, vec, vec],
        out_specs=[row, row],
        out_shape=[jax.ShapeDtypeStruct((m, d), F32), jax.ShapeDtypeStruct((m, d), BF16)],
        scratch_shapes=[pltpu.VMEM((bm, d), F32)],
        compiler_params=_params("parallel", "arbitrary"), name="ffn_residual_ln",
    )(a, w1, w3, w2, x, g.reshape(1, d), b.reshape(1, d))


def _moe_ffn_kernel(blk_e_ref, used_ref, a_ref, w1_ref, w3_ref, w2_ref, o_ref, acc_ref):
    del blk_e_ref
    f = pl.program_id(1)

    @pl.when(pl.program_id(0) < used_ref[0])
    def _():
        @pl.when(f == 0)
        def _():
            acc_ref[...] = jnp.zeros_like(acc_ref)

        a = _unpack_bf16_pairs(a_ref[...]).astype(BF16)
        h1 = jnp.dot(a, w1_ref[...].astype(BF16), preferred_element_type=F32)
        h3 = jnp.dot(a, w3_ref[...].astype(BF16), preferred_element_type=F32)
        gated = (h1 * jax.nn.sigmoid(h1) * h3).astype(BF16)
        acc_ref[...] += jnp.dot(gated, w2_ref[...], preferred_element_type=F32)

        @pl.when(f == pl.num_programs(1) - 1)
        def _():
            o_ref[...] = _pack_bf16_pairs(acc_ref[...])


def _cast_kernel(x_ref, o_ref):
    o_ref[...] = x_ref[...].astype(o_ref.dtype)


def _cast_bf16(w, rows=128):
    e, r, c = w.shape
    rows = min(rows, r)
    blk = pl.BlockSpec((None, rows, c), lambda i, j: (i, j, 0))
    return pl.pallas_call(
        _cast_kernel, grid=(e, r // rows), in_specs=[blk], out_specs=blk,
        out_shape=jax.ShapeDtypeStruct(w.shape, BF16),
        compiler_params=_params("parallel", "parallel"), name="cast_bf16",
    )(w)


def _moe_grouped_ffn(blk_e, n_used, xw, w1, w3, w2, bm, bf=512):
    cap, words = xw.shape
    d = 2 * words
    dff = w1.shape[2]
    nf = dff // bf

    def blk(i, u):
        return jnp.minimum(i, u[0] - 1)

    def ftile(i, f, u):
        return jnp.where(i < u[0], f, nf - 1)

    row = pl.BlockSpec((bm, words), lambda i, f, e, u: (blk(i, u), 0))
    up = pl.BlockSpec((None, d, bf), lambda i, f, e, u: (e[blk(i, u)], 0, ftile(i, f, u)))
    down = pl.BlockSpec((None, bf, d), lambda i, f, e, u: (e[blk(i, u)], ftile(i, f, u), 0))
    grid_spec = pltpu.PrefetchScalarGridSpec(
        num_scalar_prefetch=2, grid=(cap // bm, nf), in_specs=[row, up, up, down],
        out_specs=row, scratch_shapes=[pltpu.VMEM((bm, d), F32)])
    return pl.pallas_call(
        _moe_ffn_kernel, grid_spec=grid_spec, out_shape=jax.ShapeDtypeStruct((cap, words), jnp.int32),
        compiler_params=_params("arbitrary", "arbitrary"), name="moe_grouped_ffn",
    )(blk_e, n_used, xw, w1, w3, w2)


def _moe_route(logits, bm):
    n = logits.shape[0]
    top_val, top_idx = lax.top_k(logits, TOP_K)
    gate = jax.nn.softmax(top_val, axis=-1)
    nk = n * TOP_K
    assert nk % bm == 0
    experts = jnp.arange(N_EXPERTS, dtype=jnp.int32)
    flat_e = top_idx.reshape(nk).astype(jnp.int32)
    one_hot = flat_e[:, None] == experts[None, :]
    order = jnp.argsort(flat_e, stable=True).astype(jnp.int32)
    rank = jnp.argsort(order).astype(jnp.int32)
    counts = jnp.sum(one_hot.astype(jnp.int32), 0)
    padded = (counts + bm - 1) // bm * bm
    start = jnp.cumsum(counts) - counts
    pend = jnp.cumsum(padded)
    shift = jnp.sum(jnp.where(one_hot, (pend - padded - start)[None, :], 0), 1)
    dest = (rank + shift).reshape(n, TOP_K)
    n_blocks = nk // bm + N_EXPERTS
    blk_e = jnp.minimum(jnp.searchsorted(pend, jnp.arange(n_blocks, dtype=jnp.int32) * bm, side='right'),
                        N_EXPERTS - 1).astype(jnp.int32)
    pad_rank = jnp.arange(bm, dtype=jnp.int32)[None, :]
    pad_key = jnp.where(pad_rank < (padded - counts)[:, None], 2 * experts[:, None] + 1, 2 * N_EXPERTS)
    keys = jnp.concatenate([2 * flat_e, pad_key.reshape(-1)])
    toks = jnp.concatenate([jnp.arange(nk, dtype=jnp.int32) // TOP_K, jnp.zeros((N_EXPERTS * bm,), jnp.int32)])
    _, slot_tok = lax.sort((keys, toks), num_keys=1, is_stable=True)
    n_used = (pend[-1] // bm).astype(jnp.int32).reshape(1)
    return gate, dest, slot_tok, blk_e, n_used


def _moe_dispatch_ffn(logits, xw, w1, w3, w2, bm=1024):
    n, words = xw.shape
    gate, dest, slot_tok, blk_e, n_used = _moe_route(logits, bm)
    yw = _moe_grouped_ffn(blk_e, n_used, _gather_rows(xw, slot_tok), w1, w3, w2, bm)
    return _gather_rows(yw, dest.reshape(n * TOP_K)).reshape(n, TOP_K * words), gate


def _gmlp_kernel(groups, pu_ref, pv_ref, g_ref, b_ref, ws_ref, bs_ref, o_ref):
    u = jax.nn.gelu(pu_ref[...])
    v = _ln_rows(jax.nn.gelu(pv_ref[...]), g_ref[...], b_ref[...])
    for n in range(u.shape[0] // GM_CHUNK):
        rows = slice(n * GM_CHUNK, (n + 1) * GM_CHUNK)
        for grp in range(groups):
            cols = slice(grp * HEAD_DIM, (grp + 1) * HEAD_DIM)
            s = _dot3(ws_ref[grp], v[rows, cols]) + bs_ref[grp]
            o_ref[rows, cols] = (u[rows, cols] * s).astype(o_ref.dtype)


def _gmlp(p, col_u, col_v, gm_w, ln_g, ln_b, ws, bs, bt=512):
    n = p.shape[0]
    groups = ws.shape[0]
    bsb = jnp.broadcast_to(bs[:, :, None], (groups, GM_CHUNK, HEAD_DIM)).astype(F32)
    vec = pl.BlockSpec((1, gm_w), lambda i: (0, 0))
    full3 = pl.BlockSpec((groups, GM_CHUNK, HEAD_DIM), lambda i: (0, 0, 0))
    return pl.pallas_call(
        functools.partial(_gmlp_kernel, groups), grid=(n // bt,),
        in_specs=[pl.BlockSpec((bt, gm_w), lambda i: (i, col_u)), pl.BlockSpec((bt, gm_w), lambda i: (i, col_v)),
                  vec, vec, pl.BlockSpec((groups, GM_CHUNK, GM_CHUNK), lambda i: (0, 0, 0)), full3],
        out_specs=pl.BlockSpec((bt, gm_w), lambda i: (i, 0)),
        out_shape=jax.ShapeDtypeStruct((n, gm_w), BF16),
        compiler_params=_params("parallel"), name="gmlp",
    )(p, p, ln_g.reshape(1, gm_w), ln_b.reshape(1, gm_w), ws.astype(F32), bsb)


def _hgrn2_constants(c, reverse):
    n_lvl = int(math.log2(c))
    t = np.arange(c)
    ms, ws = [], []
    for lvl in range(n_lvl):
        m = c >> (lvl + 1)
        mid = (t // (2 * m)) * (2 * m) + m
        upper = t >= mid
        mat = np.zeros((c, c), np.float32)
        for r in range(c):
            if upper[r]:
                mat[r, mid[r]:r + 1] = 1.0
            else:
                mat[r, r + 1:mid[r]] = 1.0
        same = (t[:, None] // (2 * m)) == (t[None, :] // (2 * m))
        ws.append((same & upper[:, None] & ~upper[None, :]).astype(np.float32))
        ms.append(mat)
    ws.append(np.eye(c, dtype=np.float32))
    ms.append(np.tril(np.ones((c, c), np.float32)))
    ms.append(np.triu(np.ones((c, c), np.float32), 1))
    if reverse:
        ms = [a[::-1, ::-1] for a in ms]
        ws = [a[::-1, ::-1] for a in ws]
    return np.concatenate(ms, 0), np.stack(ws, 0)


def _hgrn2_kernel(heads, n_lvl, carry_row, final, reset_ref, *refs):
    if final:
        (q_ref, z_ref, i_ref, og_ref, oo_ref, loglb_ref, log1mlb_ref, onemlb_ref, ng_ref, m_ref, w_ref,
         out_ref, st_ref, e_ref) = refs
    else:
        (q_ref, z_ref, i_ref, loglb_ref, log1mlb_ref, onemlb_ref, m_ref, w_ref, out_ref, st_ref, e_ref) = refs
    n_streams, c = q_ref.shape[0], q_ref.shape[1]
    rows_e = m_ref.shape[0]
    step = pl.program_id(0)

    for s in range(n_streams):
        @pl.when(reset_ref[step * n_streams + s] == 1)
        def _():
            st_ref[s * heads:(s + 1) * heads] = jnp.zeros((heads, HEAD_DIM, HEAD_DIM), F32)

    for s in range(n_streams):
        z = z_ref[s]
        e = jnp.exp(-jnp.abs(z))
        r = 1.0 / (1.0 + e)
        log_sig = jnp.minimum(z, 0.0) - jnp.log(1.0 + e)
        sig_neg = jnp.where(z >= 0.0, e * r, r)
        a = loglb_ref[...]
        b = log1mlb_ref[...] + log_sig
        g = jnp.maximum(a, b) + jnp.log(1.0 + jnp.exp(-jnp.abs(a - b)))
        k = onemlb_ref[...] * sig_neg
        q = q_ref[s]
        qs = q * jax.nn.sigmoid(q)
        v = i_ref[s].astype(BF16)

        g_hi, g_lo = _split_bf16(g)
        m_all = m_ref[...]
        e0 = s * rows_e
        e_ref[e0:e0 + rows_e, :] = (jnp.dot(m_all, g_hi, preferred_element_type=F32)
                                    + jnp.dot(m_all, g_lo, preferred_element_type=F32))

        for h in range(heads):
            cols = slice(h * HEAD_DIM, (h + 1) * HEAD_DIM)
            qh, kh, vh = qs[:, cols], k[:, cols], v[:, cols]
            amat = w_ref[n_lvl] * lax.dot_general(qh.astype(BF16), kh.astype(BF16), NT_DIMS,
                                                  preferred_element_type=F32)
            for lvl in range(n_lvl):
                ex = jnp.exp(e_ref[e0 + lvl * c:e0 + (lvl + 1) * c, cols])
                amat = amat + w_ref[lvl] * lax.dot_general((qh * ex).astype(BF16), (kh * ex).astype(BF16),
                                                           NT_DIMS, preferred_element_type=F32)
            e_in = e_ref[e0 + n_lvl * c:e0 + (n_lvl + 1) * c, cols]
            q_in = (qh * jnp.exp(e_in)).astype(BF16)
            k_st = (kh * jnp.exp(e_ref[e0 + (n_lvl + 1) * c:e0 + (n_lvl + 2) * c, cols])).astype(BF16)
            dec = jnp.exp(e_in[carry_row:carry_row + 1, :])
            st = st_ref[s * heads + h]
            o = (lax.dot_general(q_in, st.astype(BF16), NT_DIMS, preferred_element_type=F32)
                 + jnp.dot(amat.astype(BF16), vh, preferred_element_type=F32))
            st_ref[s * heads + h] = st * dec + lax.dot_general(vh, k_st, TN_DIMS, preferred_element_type=F32)
            if final:
                o = o + oo_ref[s, :, cols]
                o = o * lax.rsqrt(jnp.mean(o * o, -1, keepdims=True) + RMS_EPS)
                og = og_ref[s, :, cols]
                out_ref[s, :, cols] = (o * ng_ref[:, cols] * (og * jax.nn.sigmoid(og))).astype(out_ref.dtype)
            else:
                out_ref[s, :, cols] = o


def _hgrn2_pass(p3, cols, hg_w, z_col, reverse, reset, lbs, extra, c):
    n_streams, n = p3.shape[0], p3.shape[1]
    nc = n // c
    heads = hg_w // HEAD_DIM
    n_lvl = int(math.log2(c))
    m_np, w_np = _hgrn2_constants(c, reverse)
    final = extra is not None

    def cmap(i):
        return (nc - 1 - i) if reverse else i

    def tok(col):
        return pl.BlockSpec((n_streams, c, hg_w), lambda i, r: (0, cmap(i), col))

    vec = pl.BlockSpec((1, hg_w), lambda i, r: (0, 0))
    in_specs = [tok(cols["q"]), tok(z_col), tok(cols["i"])]
    args = [p3, p3, p3]
    if final:
        o_other, norm_g = extra
        in_specs += [tok(cols["og"]), tok(0)]
        args += [p3, o_other]
    in_specs += [vec, vec, vec]
    args += list(lbs)
    if final:
        in_specs += [vec]
        args += [norm_g.reshape(1, hg_w).astype(F32)]
    in_specs += [pl.BlockSpec(m_np.shape, lambda i, r: (0, 0)), pl.BlockSpec(w_np.shape, lambda i, r: (0, 0, 0))]
    args += [jnp.asarray(m_np, BF16), jnp.asarray(w_np, F32)]
    grid_spec = pltpu.PrefetchScalarGridSpec(
        num_scalar_prefetch=1, grid=(nc,), in_specs=in_specs, out_specs=tok(0),
        scratch_shapes=[pltpu.VMEM((n_streams * heads, HEAD_DIM, HEAD_DIM), F32),
                        pltpu.VMEM((n_streams * m_np.shape[0], hg_w), F32)])
    order = np.arange(nc)[::-1] if reverse else np.arange(nc)
    reset_steps = jnp.asarray(np.asarray(reset, np.int32)[order].reshape(-1))
    return pl.pallas_call(
        functools.partial(_hgrn2_kernel, heads, n_lvl, 0 if reverse else c - 1, final), grid_spec=grid_spec,
        out_shape=jax.ShapeDtypeStruct((n_streams, n, hg_w), BF16 if final else F32),
        compiler_params=_params("arbitrary"), name="hgrn2_fwd" if final else "hgrn2_bwd",
    )(reset_steps, *args)


def _hgrn2(p, cols, hg_w, lb, norm_g, seq_lens, n_streams=2, c=HG_CHUNK):
    lb = lb.astype(F32).reshape(1, hg_w)
    lbs = (jnp.log(lb), jnp.log1p(-lb), 1.0 - lb)
    starts = np.cumsum([0] + list(seq_lens))
    n = int(starts[-1])
    per = n // n_streams
    assert n == p.shape[0] and all(s * per in starts for s in range(n_streams)), "a sequence straddles a stream cut"
    first = np.zeros(n // c, np.int32)
    last = np.zeros(n // c, np.int32)
    first[starts[:-1] // c] = 1
    last[starts[1:] // c - 1] = 1
    first = first.reshape(n_streams, per // c).T
    last = last.reshape(n_streams, per // c).T
    p3 = p.reshape(n_streams, per, p.shape[1])
    o_b = _hgrn2_pass(p3, cols, hg_w, cols["zb"], True, last, lbs, None, c)
    return _hgrn2_pass(p3, cols, hg_w, cols["zf"], False, first, lbs, (o_b, norm_g), c).reshape(n, hg_w)


def _dft_matrices(t):
    n = 2 * t
    f = jnp.arange(t, dtype=jnp.int32)
    ang = (2.0 * math.pi / n) * ((f[:, None] * f[None, :]) % n).astype(F32)
    cos, sin = jnp.cos(ang), jnp.sin(ang)
    alt = (1 - 2 * (f % 2)).astype(F32)
    first = f == 0
    fwd = jnp.stack([cos, jnp.where(first[:, None], alt[None, :], -sin)], 0).astype(BF16)
    wgt = jnp.where(first, 1.0, 2.0)[None, :] / n
    inv = jnp.stack([cos * wgt, jnp.where(first[None, :], alt[:, None] / n, -sin * wgt)], 0).astype(BF16)
    return fwd, inv


def _shortconv_kernel(flags_ref, x_ref, prev_ref, next_ref, w_ref, b_ref, o_ref):
    i = pl.program_id(0)
    x = x_ref[...]
    t = x.shape[0]
    rows = lax.broadcasted_iota(jnp.int32, x.shape, 0)
    has_prev = flags_ref[2 * i].astype(F32)
    has_next = flags_ref[2 * i + 1].astype(F32)
    up = jnp.where(rows == 0, prev_ref[7:8, :] * has_prev, pltpu.roll(x, 1, 0))
    dn = jnp.where(rows == t - 1, next_ref[0:1, :] * has_next, pltpu.roll(x, t - 1, 0))
    o_ref[...] = up * w_ref[0:1, :] + x * w_ref[1:2, :] + dn * w_ref[2:3, :] + b_ref[...]


def _shortconv(p, width, conv_w, conv_b, blk_flags, t=HY_BLOCK, ct=768):
    n = p.shape[0]
    nb = n // t
    r8 = t // 8
    grid_spec = pltpu.PrefetchScalarGridSpec(
        num_scalar_prefetch=1, grid=(nb, width // ct),
        in_specs=[pl.BlockSpec((t, ct), lambda i, j, f: (i, j)),
                  pl.BlockSpec((8, ct), lambda i, j, f: (jnp.maximum(i * r8 - 1, 0), j)),
                  pl.BlockSpec((8, ct), lambda i, j, f: (jnp.minimum((i + 1) * r8, nb * r8 - 1), j)),
                  pl.BlockSpec((3, ct), lambda i, j, f: (0, j)), pl.BlockSpec((1, ct), lambda i, j, f: (0, j))],
        out_specs=pl.BlockSpec((t, ct), lambda i, j, f: (i, j)))
    return pl.pallas_call(
        _shortconv_kernel, grid_spec=grid_spec, out_shape=jax.ShapeDtypeStruct((n, width), F32),
        compiler_params=_params("parallel", "parallel"), name="hyena_shortconv",
    )(blk_flags, p, p, p, conv_w.astype(F32), conv_b.reshape(1, width).astype(F32))


def _filter_kernel(l_total, feats_ref, w1_ref, b1_ref, w2_ref, b2_ref, w3_ref, b3_ref, fr_ref, wo_ref, dl_ref,
                   h_ref, sum_ref):
    del l_total
    i = pl.program_id(0)
    bt = feats_ref.shape[0]
    half_lanes = LANES // 2
    fr = fr_ref[...]
    feats = feats_ref[...]
    h = jnp.sin(fr * (_dot3(feats, w1_ref[...]) + b1_ref[...]))
    h = jnp.sin(fr * (_dot3(h, w2_ref[...]) + b2_ref[...]))
    h = jnp.sin(fr * (_dot3(h, w3_ref[...]) + b3_ref[...]))
    reps = wo_ref.shape[2] // dl_ref.shape[1]
    row = lax.broadcasted_iota(jnp.int32, (bt, 1), 0) + i * bt
    total = jnp.zeros((1, wo_ref.shape[2]), F32)
    for side in range(2):
        window = jnp.exp(-feats[:, side * half_lanes:side * half_lanes + 1] * dl_ref[...])
        out = _dot3(h, wo_ref[side]) * jnp.concatenate([window] * reps, axis=1)
        if side == 1:
            out = jnp.where(row == 0, 0.0, out)
        h_ref[side] = out
        total = total + jnp.sum(jnp.abs(out), axis=0, keepdims=True)

    @pl.when(i == 0)
    def _():
        sum_ref[...] = jnp.zeros_like(sum_ref)

    sum_ref[...] += total


def _hyena_filter(l, w1, b1, w2, b2, w3, b3, freq, w_out, hy_w, bt=512):
    hid = w2.shape[0]
    half_lanes = LANES // 2
    assert hid <= half_lanes and w1.shape[0] <= half_lanes
    half = HY_ORDER * hy_w
    bands = jnp.linspace(1e-4, HY_POS_BANDS - 1, HY_POS_BANDS, dtype=F32)

    def features(pos):
        tt = pos / max(l - 1, 1)
        ang = (2.0 * math.pi / l) * pos[:, None] * bands[None, :]
        f = jnp.concatenate([tt[:, None], jnp.cos(ang), -jnp.sin(ang)], -1)
        return jnp.pad(f, ((0, 0), (0, half_lanes - f.shape[1])))

    def twice_diag(w):
        wp = jnp.pad(w.astype(F32), ((0, half_lanes - w.shape[0]), (0, half_lanes - w.shape[1])))
        z = jnp.zeros_like(wp)
        return jnp.concatenate([jnp.concatenate([wp, z], 1), jnp.concatenate([z, wp], 1)], 0)

    def twice_vec(a):
        ap = jnp.pad(a.astype(F32), (0, half_lanes - a.shape[0]))
        return jnp.concatenate([ap, ap]).reshape(1, LANES)

    pos = jnp.arange(l, dtype=F32)
    feats = jnp.concatenate([features(pos), features(l - pos)], 1)
    wo = w_out.astype(F32).reshape(hid, HY_ORDER, 2, hy_w).transpose(2, 0, 1, 3).reshape(2, hid, half)
    wo = jnp.pad(wo, ((0, 0), (0, half_lanes - hid), (0, 0)))
    zero = jnp.zeros_like(wo[0])
    wop = jnp.stack([jnp.concatenate([wo[0], zero], 0), jnp.concatenate([zero, wo[1]], 0)], 0)
    deltas = jnp.abs(jnp.linspace(math.log(HY_DECAY_TARGET) / HY_SLOW_DECAY,
                                  math.log(HY_DECAY_TARGET) / HY_FAST_DECAY, hy_w, dtype=F32)).reshape(1, hy_w)
    bt = min(bt, l)
    sq = pl.BlockSpec((LANES, LANES), lambda i: (0, 0))
    vec = pl.BlockSpec((1, LANES), lambda i: (0, 0))
    taps, abs_sum = pl.pallas_call(
        functools.partial(_filter_kernel, l), grid=(l // bt,),
        in_specs=[pl.BlockSpec((bt, LANES), lambda i: (i, 0)), sq, vec, sq, vec, sq, vec, vec,
                  pl.BlockSpec((2, LANES, half), lambda i: (0, 0, 0)),
                  pl.BlockSpec((1, hy_w), lambda i: (0, 0))],
        out_specs=[pl.BlockSpec((2, bt, half), lambda i: (0, i, 0)), pl.BlockSpec((1, half), lambda i: (0, 0))],
        out_shape=[jax.ShapeDtypeStruct((2, l, half), F32), jax.ShapeDtypeStruct((1, half), F32)],
        compiler_params=_params("arbitrary"), name="hyena_filter",
    )(feats, twice_diag(w1), twice_vec(b1), twice_diag(w2), twice_vec(b2), twice_diag(w3), twice_vec(b3),
      twice_vec(freq), wop, deltas)
    return taps.reshape(2 * l, half), abs_sum


def _dft_cols(t, width):
    return width if t <= 1024 else 256


def _dft_kernel(f_ref, x_ref, o_ref):
    x = x_ref[...].astype(BF16)
    o_ref[0] = jnp.dot(f_ref[0], x, preferred_element_type=F32).astype(o_ref.dtype)
    o_ref[1] = jnp.dot(f_ref[1], x, preferred_element_type=F32).astype(o_ref.dtype)


def _block_dft(fwd, x, col_off, width, out_dtype, t=HY_BLOCK, ct=256):
    nseg = x.shape[0] // t
    ct = min(ct, width)
    return pl.pallas_call(
        _dft_kernel, grid=(nseg, width // ct),
        in_specs=[pl.BlockSpec((2, t, t), lambda s, j: (0, 0, 0)),
                  pl.BlockSpec((t, ct), lambda s, j: (s, j + col_off))],
        out_specs=pl.BlockSpec((None, 2, t, ct), lambda s, j: (s, 0, 0, j)),
        out_shape=jax.ShapeDtypeStruct((nseg, 2, t, width), out_dtype),
        compiler_params=_params("parallel", "parallel"), name="hyena_block_dft",
    )(fwd, x)


def _segment_index(d, nb):
    return d if d >= 0 else 2 * nb + d


def _hyena_filter_spectra(l, fwd, filt_w, hy_w, t=HY_BLOCK):
    nb = l // t
    taps, abs_sum = _hyena_filter(l, *filt_w, hy_w)
    half = taps.shape[1]
    seg_spec = _block_dft(fwd, taps, 0, half, F32, t, _dft_cols(t, hy_w))
    first_rows = taps.reshape(2 * nb, t, half)[:, 0, :]
    seg0 = jnp.stack([first_rows[_segment_index(d - 1, nb)] for d in range(-(nb - 1), nb)], 0)
    return seg_spec, seg0.reshape(2 * nb - 1, 1, half), 1.0 / abs_sum


MIX_ROWS = 8


def _mix_kernel(nb, z_ref, a_ref, s0_ref, inv_ref, *refs):
    o_ref, k_ref = refs[-2], refs[-1]
    ft, ct = z_ref.shape[2], z_ref.shape[3]
    first_tile = pl.program_id(1) == 0

    tile_rows = lax.broadcasted_iota(jnp.int32, (ft, ct), 0)
    sgn = (1 - 2 * (tile_rows & 1)).astype(F32)
    real_row = jnp.logical_and(tile_rows == 0, first_tile)
    inv = inv_ref[...]
    for slot in range(2 * nb - 1):
        d = slot - (nb - 1)
        ia, ib = _segment_index(d, nb), _segment_index(d - 1, nb)
        s0 = s0_ref[slot]
        k_ref[slot, 0] = (a_ref[ia, 0] + sgn * (a_ref[ib, 0] - s0)) * inv
        k_ref[slot, 1] = (a_ref[ia, 1] + sgn * (a_ref[ib, 1] - jnp.where(real_row, s0, 0.0))) * inv

    n_out = 2 if nb % 2 == 0 else 1

    def out_rows(i, r0, masked):
        rows = pl.ds(r0, MIX_ROWS)
        acc = [[jnp.zeros((MIX_ROWS, ct), F32) for _ in range(2)] for _ in range(n_out)]
        if masked:
            row0 = jnp.logical_and(lax.broadcasted_iota(jnp.int32, (MIX_ROWS, ct), 0) == 0, first_tile)
        d_hi = i + (n_out - 1) + (nb - 1)
        k_prev = (k_ref[d_hi, 0, rows, :], k_ref[d_hi, 1, rows, :]) if n_out == 2 else None
        for j in range(nb):
            d = i - j + (nb - 1)
            zt, zb = z_ref[j, 0, rows, :], z_ref[j, 1, rows, :]
            k_cur = (k_ref[d, 0, rows, :], k_ref[d, 1, rows, :])
            for o, (kt, kb) in enumerate((k_cur, k_prev)[:n_out]):
                bb = zb * kb
                if masked:
                    acc[o][0] = acc[o][0] + (zt * kt - jnp.where(row0, 0.0, bb))
                    acc[o][1] = acc[o][1] + jnp.where(row0, bb, zt * kb + zb * kt)
                else:
                    acc[o][0] = acc[o][0] + (zt * kt - bb)
                    acc[o][1] = acc[o][1] + (zt * kb + zb * kt)
            k_prev = k_cur
        for o in range(n_out):
            o_ref[i + o, 0, rows, :] = acc[o][0].astype(o_ref.dtype)
            o_ref[i + o, 1, rows, :] = acc[o][1].astype(o_ref.dtype)

    def all_outputs(r0, masked):
        def body(ip, carry):
            out_rows(ip * n_out, r0, masked)
            return carry
        lax.fori_loop(0, nb // n_out, body, 0)

    all_outputs(0, True)

    def chunk(r, carry):
        all_outputs(pl.multiple_of(r * MIX_ROWS, MIX_ROWS), False)
        return carry

    lax.fori_loop(1, ft // MIX_ROWS, chunk, 0)


def _hyena_mix(zspec, kspec, k_col, n_seq, nb, blk_off, ct=256):
    seg_spec, seg0, inv_norm = kspec
    _, _, t, width = zspec.shape
    ft = max(MIX_ROWS, min(t, 2048 // nb))
    nct = width // ct
    assert blk_off % nb == 0 and t % ft == 0
    s_off = blk_off // nb
    return pl.pallas_call(
        functools.partial(_mix_kernel, nb), grid=(n_seq, t // ft, nct),
        in_specs=[pl.BlockSpec((nb, 2, ft, ct), lambda s, fi, j: (s + s_off, 0, fi, j)),
                  pl.BlockSpec((2 * nb, 2, ft, ct), lambda s, fi, j: (0, 0, fi, j + k_col * nct)),
                  pl.BlockSpec((2 * nb - 1, 1, ct), lambda s, fi, j: (0, 0, j + k_col * nct)),
                  pl.BlockSpec((1, ct), lambda s, fi, j: (0, j + k_col * nct))],
        out_specs=pl.BlockSpec((nb, 2, ft, ct), lambda s, fi, j: (s, 0, fi, j)),
        out_shape=jax.ShapeDtypeStruct((n_seq * nb, 2, t, width), BF16),
        scratch_shapes=[pltpu.VMEM((2 * nb - 1, 2, ft, ct), F32)],
        compiler_params=_params("parallel", "parallel", "parallel"), name="hyena_mix",
    )(zspec, seg_spec, seg0, inv_norm)


def _inverse_kernel(bounds, *refs):
    n_groups = len(bounds)
    y_refs = refs[:n_groups]
    fi_ref, zin_ref, gate_ref, bias_ref, o_ref = refs[n_groups:]
    i = pl.program_id(0)
    for (lo, hi), y_ref in zip(bounds, y_refs):
        @pl.when(jnp.logical_and(i >= lo, i < hi))
        def _():
            y = (jnp.dot(fi_ref[0], y_ref[0], preferred_element_type=F32)
                 + jnp.dot(fi_ref[1], y_ref[1], preferred_element_type=F32))
            o_ref[...] = (gate_ref[...] * (y + zin_ref[...] * bias_ref[...])).astype(o_ref.dtype)


def _hyena_inverse(yspecs, inv, zin, zin_col, gate, gate_col, bias, out_dtype, ct=256):
    _, _, t, width = yspecs[0].shape
    nct = width // ct
    bounds, lo = [], 0
    for y in yspecs:
        bounds.append((lo, lo + y.shape[0]))
        lo += y.shape[0]

    def group_spec(lo, hi):
        return pl.BlockSpec((None, 2, t, ct), lambda i, j: (jnp.clip(i - lo, 0, hi - lo - 1), 0, 0, j))

    return pl.pallas_call(
        functools.partial(_inverse_kernel, tuple(bounds)), grid=(lo, nct),
        in_specs=[group_spec(*b) for b in bounds] + [
            pl.BlockSpec((2, t, t), lambda i, j: (0, 0, 0), pipeline_mode=pl.Buffered(1)),
            pl.BlockSpec((t, ct), lambda i, j: (i, j + zin_col * nct)),
            pl.BlockSpec((t, ct), lambda i, j: (i, j + gate_col * nct)),
            pl.BlockSpec((1, ct), lambda i, j: (0, j))],
        out_specs=pl.BlockSpec((t, ct), lambda i, j: (i, j)),
        out_shape=jax.ShapeDtypeStruct((lo * t, width), out_dtype),
        compiler_params=_params("arbitrary", "arbitrary"), name="hyena_inverse",
    )(*yspecs, inv, zin, gate, bias)


def _hyena(p, hy_w, groups, conv_w, conv_b, filt_w, bias, t=HY_BLOCK):
    flags = []
    for n_seq, l in groups:
        nb = l // t
        flags += [int(bi > 0) if side == 0 else int(bi < nb - 1)
                  for _ in range(n_seq) for bi in range(nb) for side in range(2)]
    fwd, inv = _dft_matrices(t)
    u = _shortconv(p, 3 * hy_w, conv_w, conv_b, jnp.asarray(np.asarray(flags, np.int32)), t, ct=hy_w)
    kspecs = [_hyena_filter_spectra(l, fwd, filt_w, hy_w, t) for _, l in groups]
    bias = bias.astype(F32)
    ct = _dft_cols(t, hy_w)
    z, z_col = u, 2
    for o in range(HY_ORDER):
        zspec = _block_dft(fwd, z, z_col * (hy_w // ct), hy_w, F32, t, ct)
        yspecs, blk = [], 0
        for (n_seq, l), kspec in zip(groups, kspecs):
            yspecs.append(_hyena_mix(zspec, kspec, o, n_seq, l // t, blk))
            blk += n_seq * (l // t)
        last = o == HY_ORDER - 1
        z = _hyena_inverse(yspecs, inv, z, z_col, u, o, bias[o:o + 1], BF16 if last else F32, ct)
        z_col = 0
    return z


def kernel(x_prompt, x_sample, ln_in_g, ln_in_b, w_in, hy_conv_w, hy_conv_b, hy_pos_w1, hy_pos_b1, hy_pos_w2, hy_pos_b2, hy_pos_w3, hy_pos_b3, hy_sin_freq, hy_pos_wout, hy_bias, gm_ln_g, gm_ln_b, gm_ws, gm_bs, hg_lb_raw, hg_norm_g, w_out, ln1_g, ln1_b, ln2_g, ln2_b, ffn_w1, ffn_w3, ffn_w2, moe_router_w, moe_router_b, moe_w1, moe_w3, moe_w2):
    depth, d_model, in_w = w_in.shape
    hy_w = hy_bias.shape[-1]
    gm_w = gm_ln_g.shape[-1]
    hg_w = hg_norm_g.shape[-1]
    o1 = 3 * hy_w
    o2 = o1 + 2 * gm_w
    alpha = (2 * depth) ** 0.25
    groups = [(x_prompt.shape[0], x_prompt.shape[1]), (x_sample.shape[0], x_sample.shape[1])]
    seq_lens = [l for n_seq, l in groups for _ in range(n_seq)]
    n_prompt = x_prompt.shape[0] * x_prompt.shape[1]
    hg_off = o1 // hg_w
    hg_cols = {"q": hg_off, "zf": hg_off + 1, "zb": hg_off + 2, "i": hg_off + 3, "og": hg_off + 4}
    gm_off = (o1 + 5 * hg_w) // gm_w

    lb_all = jnp.cumsum(jax.nn.softmax(hg_lb_raw.astype(F32), axis=0), axis=0)
    lb_all = lb_all - lb_all[:1]

    x, xb = _layernorm_pair(x_prompt.reshape(-1, d_model), x_sample.reshape(-1, d_model), ln_in_g, ln_in_b)
    for l in range(depth):
        w_l = jnp.concatenate([w_in[l][:, :o1], w_in[l][:, o2:], w_in[l][:, o1:o2]], 1).astype(BF16)
        p = _matmul(xb, w_l)
        filt_w = (hy_pos_w1[l], hy_pos_b1[l], hy_pos_w2[l], hy_pos_b2[l], hy_pos_w3[l], hy_pos_b3[l],
                  hy_sin_freq[l], hy_pos_wout[l])
        y_hy = _hyena(p, hy_w, groups, hy_conv_w[l], hy_conv_b[l], filt_w, hy_bias[l])
        y_hg = _hgrn2(p, hg_cols, hg_w, lb_all[l], hg_norm_g[l], seq_lens)
        y_gm = _gmlp(p, gm_off, gm_off + 1, gm_w, gm_ln_g[l], gm_ln_b[l], gm_ws[l], gm_bs[l])
        wo = w_out[l].astype(BF16)
        parts = [y_hy, y_gm, y_hg]
        wo_parts = [wo[:hy_w], wo[hy_w:hy_w + gm_w], wo[hy_w + gm_w:]]
        j = l // 2
        if l % 2 == 0:
            x, xb = _mix_out_residual_ln(parts, wo_parts, x, ln1_g[l], ln1_b[l], alpha)
            x, xb = _ffn_residual_ln(xb, ffn_w1[j].astype(BF16), ffn_w3[j].astype(BF16), ffn_w2[j].astype(BF16),
                                     x, ln2_g[l], ln2_b[l], alpha)
        else:
            n_exp = moe_router_w.shape[-1]
            wr = jnp.pad(moe_router_w[j].astype(BF16), ((0, 0), (0, LANES - n_exp)))
            x, xw, logits = _mix_out_residual_ln(parts, wo_parts, x, ln1_g[l], ln1_b[l], alpha, wr)
            logits = logits[:, :n_exp] + moe_router_b[j].astype(F32)
            y01, gate = _moe_dispatch_ffn(logits, xw, moe_w1[j], moe_w3[j], _cast_bf16(moe_w2[j]))
            xa, xs = _combine_residual_ln_split(x, y01, gate, ln2_g[l], ln2_b[l], alpha, n_prompt)
            if l == depth - 1:
                return (xa.reshape(x_prompt.shape), xs.reshape(x_sample.shape))
            x = jnp.concatenate([xa, xs], 0)
            xb = x.astype(BF16)
    return (x[:n_prompt].reshape(x_prompt.shape), x[n_prompt:].reshape(x_sample.shape))
```

```python
import functools
import math

import numpy as np
import jax
import jax.numpy as jnp
from jax import lax
from jax.experimental import pallas as pl
from jax.experimental.pallas import tpu as pltpu
from jax.experimental.pallas import tpu_sc as plsc

HEAD_DIM = 128
HY_ORDER = 2
HY_POS_BANDS = 16
HY_FAST_DECAY = 0.3
HY_SLOW_DECAY = 1.5
HY_DECAY_TARGET = 1e-2
GM_CHUNK = 128
N_EXPERTS = 8
TOP_K = 2
LN_EPS = 1e-5
RMS_EPS = 1e-6

HY_BLOCK = 1024
HG_CHUNK = 128
LANES = 128

V7X_VMEM_BYTES = 64 * 1024 * 1024
VMEM_LIMIT = V7X_VMEM_BYTES - 8 * 1024 * 1024

F32 = jnp.float32
BF16 = jnp.bfloat16
NT_DIMS = (((1,), (1,)), ((), ()))
TN_DIMS = (((0,), (0,)), ((), ()))


def _params(*sem):
    return pltpu.CompilerParams(dimension_semantics=sem, vmem_limit_bytes=VMEM_LIMIT)


def _ln_rows(x, g, b):
    mu = jnp.mean(x, -1, keepdims=True)
    xc = x - mu
    var = jnp.mean(xc * xc, -1, keepdims=True)
    return xc * lax.rsqrt(var + LN_EPS) * g + b


def _split_bf16(x):
    hi = x.astype(BF16)
    lo = (x - hi.astype(F32)).astype(BF16)
    return hi, lo


def _dot3(a, b):
    ah, al = _split_bf16(a)
    bh, bl = _split_bf16(b)
    return (jnp.dot(ah, bh, preferred_element_type=F32) + jnp.dot(ah, bl, preferred_element_type=F32)
            + jnp.dot(al, bh, preferred_element_type=F32))


def _pack_bf16_pairs(y):
    w = y.shape[1] // 2
    bits = lax.bitcast_convert_type(y.astype(BF16).astype(F32), jnp.uint32)
    word = (bits[:, :w] >> 16) | (bits[:, w:] & jnp.uint32(0xFFFF0000))
    return lax.bitcast_convert_type(word, jnp.int32)


def _unpack_bf16_pairs(word):
    u = lax.bitcast_convert_type(word, jnp.uint32)
    lo = lax.bitcast_convert_type(u << 16, F32)
    hi = lax.bitcast_convert_type(u & jnp.uint32(0xFFFF0000), F32)
    return jnp.concatenate([lo, hi], axis=1)


SC_GATHER_WINDOW = 128
SC_GATHER_WORDS = 256


def _gather_rows(xw, idx):
    n, words = xw.shape
    m = idx.shape[0]
    parts = words // SC_GATHER_WORDS
    assert words % SC_GATHER_WORDS == 0 and m % (2 * SC_GATHER_WINDOW) == 0
    half = m // SC_GATHER_WINDOW // 2
    mesh = plsc.VectorSubcoreMesh(core_axis_name="core", subcore_axis_name="subcore")

    @pl.kernel(out_type=jax.ShapeDtypeStruct((m, words), xw.dtype), mesh=mesh)
    def gather(x_hbm, i_hbm, o_hbm):
        def body(i_vmem, o_vmem):
            piece = pl.program_id(2)
            pltpu.sync_copy(x_hbm.at[i_vmem.at[0], pl.ds(piece * SC_GATHER_WORDS, SC_GATHER_WORDS)], o_vmem)

        pltpu.emit_pipeline(
            body, grid=(2, half, parts),
            in_specs=[pl.BlockSpec((1, SC_GATHER_WINDOW), index_map=lambda c, i, p: (0, c * half + i))],
            out_specs=[pl.BlockSpec((SC_GATHER_WINDOW, SC_GATHER_WORDS),
                                    index_map=lambda c, i, p: (c * half + i, p))],
            core_axis_name=("core", "subcore"),
            dimension_semantics=(pltpu.PARALLEL, pltpu.PARALLEL, pltpu.ARBITRARY),
        )(i_hbm, o_hbm)

    return gather(xw, idx.astype(jnp.int32).reshape(1, m))


def _ln2_kernel(na_blocks, xa_ref, xb_ref, g_ref, b_ref, of_ref, ob_ref):
    def emit(x_ref):
        y = _ln_rows(x_ref[...], g_ref[...], b_ref[...])
        of_ref[...] = y
        ob_ref[...] = y.astype(BF16)

    first = pl.program_id(0) < na_blocks
    pl.when(first)(lambda: emit(xa_ref))
    pl.when(jnp.logical_not(first))(lambda: emit(xb_ref))


def _layernorm_pair(xa, xb, g, b, bm=512):
    d = xa.shape[1]
    na, nb = xa.shape[0] // bm, xb.shape[0] // bm
    row = pl.BlockSpec((bm, d), lambda i: (i, 0))
    vec = pl.BlockSpec((1, d), lambda i: (0, 0))
    m = (na + nb) * bm
    return pl.pallas_call(
        functools.partial(_ln2_kernel, na), grid=(na + nb,),
        in_specs=[pl.BlockSpec((bm, d), lambda i: (jnp.minimum(i, na - 1), 0)),
                  pl.BlockSpec((bm, d), lambda i: (jnp.maximum(i - na, 0), 0)), vec, vec],
        out_specs=[row, row],
        out_shape=[jax.ShapeDtypeStruct((m, d), F32), jax.ShapeDtypeStruct((m, d), BF16)],
        compiler_params=_params("arbitrary"), name="layernorm",
    )(xa, xb, g.reshape(1, d), b.reshape(1, d))


def _combine_ln_kernel(alpha, na_blocks, x_ref, y0_ref, y1_ref, gate_ref, g_ref, b_ref, oa_ref, ob_ref):
    gate = gate_ref[...]
    ff = _unpack_bf16_pairs(y0_ref[...]) * gate[:, 0:1] + _unpack_bf16_pairs(y1_ref[...]) * gate[:, 1:2]
    y = _ln_rows(alpha * x_ref[...] + ff, g_ref[...], b_ref[...])
    first = pl.program_id(0) < na_blocks

    @pl.when(first)
    def _():
        oa_ref[...] = y

    @pl.when(jnp.logical_not(first))
    def _():
        ob_ref[...] = y


def _combine_residual_ln_split(x, y0, y1, gate, g, b, alpha, n_first, bm=512):
    m, d = x.shape
    na = n_first // bm
    row = pl.BlockSpec((bm, d), lambda i: (i, 0))
    packed = pl.BlockSpec((bm, d // 2), lambda i: (i, 0))
    vec = pl.BlockSpec((1, d), lambda i: (0, 0))
    return pl.pallas_call(
        functools.partial(_combine_ln_kernel, alpha, na), grid=(m // bm,),
        in_specs=[row, packed, packed, pl.BlockSpec((bm, gate.shape[1]), lambda i: (i, 0)), vec, vec],
        out_specs=[pl.BlockSpec((bm, d), lambda i: (jnp.minimum(i, na - 1), 0)),
                   pl.BlockSpec((bm, d), lambda i: (jnp.maximum(i - na, 0), 0))],
        out_shape=[jax.ShapeDtypeStruct((n_first, d), F32), jax.ShapeDtypeStruct((m - n_first, d), F32)],
        compiler_params=_params("arbitrary"), name="moe_combine_residual_ln",
    )(x, y0, y1, gate, g.reshape(1, d), b.reshape(1, d))


def _mm_kernel(a_ref, w_ref, o_ref):
    o_ref[...] = jnp.dot(a_ref[...], w_ref[...], preferred_element_type=F32).astype(o_ref.dtype)


def _matmul(a, w, out_dtype=F32, bm=1024, bn=1024):
    m, k = a.shape
    n = w.shape[1]
    bm, bn = min(bm, m), min(bn, n)
    return pl.pallas_call(
        _mm_kernel, grid=(m // bm, n // bn),
        in_specs=[pl.BlockSpec((bm, k), lambda i, j: (i, 0)), pl.BlockSpec((k, bn), lambda i, j: (0, j))],
        out_specs=pl.BlockSpec((bm, bn), lambda i, j: (i, j)),
        out_shape=jax.ShapeDtypeStruct((m, n), out_dtype),
        compiler_params=_params("parallel", "arbitrary"), name="matmul",
    )(a, w)


def _mix_out_kernel(alpha, routed, a0_ref, a1_ref, a2_ref, w0_ref, w1_ref, w2_ref, x_ref, g_ref, b_ref, *refs):
    mix = (jnp.dot(a0_ref[...], w0_ref[...], preferred_element_type=F32)
           + jnp.dot(a1_ref[...], w1_ref[...], preferred_element_type=F32)
           + jnp.dot(a2_ref[...], w2_ref[...], preferred_element_type=F32))
    y = _ln_rows(alpha * x_ref[...] + mix, g_ref[...], b_ref[...])
    if routed:
        wr_ref, of_ref, ob_ref, lg_ref = refs
        y_hi, y_lo = _split_bf16(y)
        lg_ref[...] = (jnp.dot(y_hi, wr_ref[...], preferred_element_type=F32)
                       + jnp.dot(y_lo, wr_ref[...], preferred_element_type=F32))
        ob_ref[...] = _pack_bf16_pairs(y)
    else:
        of_ref, ob_ref = refs
        ob_ref[...] = y.astype(BF16)
    of_ref[...] = y


def _mix_out_residual_ln(parts, weights, x, g, b, alpha, router_w=None, bm=512):
    m, d = x.shape
    row = pl.BlockSpec((bm, d), lambda i: (i, 0))
    vec = pl.BlockSpec((1, d), lambda i: (0, 0))
    in_specs = [pl.BlockSpec((bm, a.shape[1]), lambda i: (i, 0)) for a in parts]
    in_specs += [pl.BlockSpec(w.shape, lambda i: (0, 0)) for w in weights] + [row, vec, vec]
    args = [*parts, *weights, x, g.reshape(1, d), b.reshape(1, d)]
    out_specs = [row, row]
    out_shape = [jax.ShapeDtypeStruct((m, d), F32), jax.ShapeDtypeStruct((m, d), BF16)]
    if router_w is not None:
        in_specs.append(pl.BlockSpec(router_w.shape, lambda i: (0, 0)))
        args.append(router_w)
        out_specs[1] = pl.BlockSpec((bm, d // 2), lambda i: (i, 0))
        out_shape[1] = jax.ShapeDtypeStruct((m, d // 2), jnp.int32)
        out_specs.append(pl.BlockSpec((bm, router_w.shape[1]), lambda i: (i, 0)))
        out_shape.append(jax.ShapeDtypeStruct((m, router_w.shape[1]), F32))
    return pl.pallas_call(
        functools.partial(_mix_out_kernel, alpha, router_w is not None), grid=(m // bm,),
        in_specs=in_specs, out_specs=out_specs, out_shape=out_shape,
        compiler_params=_params("parallel"), name="out_proj_residual_ln",
    )(*args)


def _swiglu_acc(a_ref, w1_ref, w3_ref, w2_ref, acc_ref):
    f = pl.program_id(1)

    @pl.when(f == 0)
    def _():
        acc_ref[...] = jnp.zeros_like(acc_ref)

    a = a_ref[...]
    h1 = jnp.dot(a, w1_ref[...], preferred_element_type=F32)
    h3 = jnp.dot(a, w3_ref[...], preferred_element_type=F32)
    gated = (h1 * jax.nn.sigmoid(h1) * h3).astype(BF16)
    acc_ref[...] += jnp.dot(gated, w2_ref[...], preferred_element_type=F32)


def _ffn_kernel(alpha, a_ref, w1_ref, w3_ref, w2_ref, x_ref, g_ref, b_ref, of_ref, ob_ref, acc_ref):
    _swiglu_acc(a_ref, w1_ref, w3_ref, w2_ref, acc_ref)

    @pl.when(pl.program_id(1) == pl.num_programs(1) - 1)
    def _():
        y = _ln_rows(alpha * x_ref[...] + acc_ref[...], g_ref[...], b_ref[...])
        of_ref[...] = y
        ob_ref[...] = y.astype(BF16)


def _ffn_residual_ln(a, w1, w3, w2, x, g, b, alpha, bm=512, bf=512):
    m, d = a.shape
    dff = w1.shape[1]
    row = pl.BlockSpec((bm, d), lambda i, f: (i, 0))
    vec = pl.BlockSpec((1, d), lambda i, f: (0, 0))
    up = pl.BlockSpec((d, bf), lambda i, f: (0, f))
    return pl.pallas_call(
        functools.partial(_ffn_kernel, alpha), grid=(m // bm, dff // bf),
        in_specs=[row, up, up, pl.BlockSpec((bf, d), lambda i, f: (f, 0)), row, vec, vec],
        out_specs=[row, row],
        out_shape=[jax.ShapeDtypeStruct((m, d), F32), jax.ShapeDtypeStruct((m, d), BF16)],
        scratch_shapes=[pltpu.VMEM((bm, d), F32)],
        compiler_params=_params("parallel", "arbitrary"), name="ffn_residual_ln",
    )(a, w1, w3, w2, x, g.reshape(1, d), b.reshape(1, d))


def _moe_ffn_kernel(blk_e_ref, used_ref, a_ref, w1_ref, w3_ref, w2_ref, o_ref, acc_ref):
    del blk_e_ref
    f = pl.program_id(1)

    @pl.when(pl.program_id(0) < used_ref[0])
    def _():
        @pl.when(f == 0)
        def _():
            acc_ref[...] = jnp.zeros_like(acc_ref)

        a = _unpack_bf16_pairs(a_ref[...]).astype(BF16)
        h1 = jnp.dot(a, w1_ref[...].astype(BF16), preferred_element_type=F32)
        h3 = jnp.dot(a, w3_ref[...].astype(BF16), preferred_element_type=F32)
        gated = (h1 * jax.nn.sigmoid(h1) * h3).astype(BF16)
        acc_ref[...] += jnp.dot(gated, w2_ref[...], preferred_element_type=F32)

        @pl.when(f == pl.num_programs(1) - 1)
        def _():
            o_ref[...] = _pack_bf16_pairs(acc_ref[...])


def _cast_kernel(x_ref, o_ref):
    o_ref[...] = x_ref[...].astype(o_ref.dtype)


def _cast_bf16(w, rows=128):
    e, r, c = w.shape
    rows = min(rows, r)
    blk = pl.BlockSpec((None, rows, c), lambda i, j: (i, j, 0))
    return pl.pallas_call(
        _cast_kernel, grid=(e, r // rows), in_specs=[blk], out_specs=blk,
        out_shape=jax.ShapeDtypeStruct(w.shape, BF16),
        compiler_params=_params("parallel", "parallel"), name="cast_bf16",
    )(w)


def _moe_grouped_ffn(blk_e, n_used, xw, w1, w3, w2, bm, bf=512):
    cap, words = xw.shape
    d = 2 * words
    dff = w1.shape[2]
    nf = dff // bf

    def blk(i, u):
        return jnp.minimum(i, u[0] - 1)

    def ftile(i, f, u):
        return jnp.where(i < u[0], f, nf - 1)

    row = pl.BlockSpec((bm, words), lambda i, f, e, u: (blk(i, u), 0))
    up = pl.BlockSpec((None, d, bf), lambda i, f, e, u: (e[blk(i, u)], 0, ftile(i, f, u)))
    down = pl.BlockSpec((None, bf, d), lambda i, f, e, u: (e[blk(i, u)], ftile(i, f, u), 0))
    grid_spec = pltpu.PrefetchScalarGridSpec(
        num_scalar_prefetch=2, grid=(cap // bm, nf), in_specs=[row, up, up, down],
        out_specs=row, scratch_shapes=[pltpu.VMEM((bm, d), F32)])
    return pl.pallas_call(
        _moe_ffn_kernel, grid_spec=grid_spec, out_shape=jax.ShapeDtypeStruct((cap, words), jnp.int32),
        compiler_params=_params("arbitrary", "arbitrary"), name="moe_grouped_ffn",
    )(blk_e, n_used, xw, w1, w3, w2)


def _moe_route(logits, bm):
    n = logits.shape[0]
    top_val, top_idx = lax.top_k(logits, TOP_K)
    gate = jax.nn.softmax(top_val, axis=-1)
    nk = n * TOP_K
    assert nk % bm == 0
    experts = jnp.arange(N_EXPERTS, dtype=jnp.int32)
    flat_e = top_idx.reshape(nk).astype(jnp.int32)
    one_hot = flat_e[:, None] == experts[None, :]
    order = jnp.argsort(flat_e, stable=True).astype(jnp.int32)
    rank = jnp.argsort(order).astype(jnp.int32)
    counts = jnp.sum(one_hot.astype(jnp.int32), 0)
    padded = (counts + bm - 1) // bm * bm
    start = jnp.cumsum(counts) - counts
    pend = jnp.cumsum(padded)
    shift = jnp.sum(jnp.where(one_hot, (pend - padded - start)[None, :], 0), 1)
    dest = (rank + shift).reshape(n, TOP_K)
    n_blocks = nk // bm + N_EXPERTS
    blk_e = jnp.minimum(jnp.searchsorted(pend, jnp.arange(n_blocks, dtype=jnp.int32) * bm, side='right'),
                        N_EXPERTS - 1).astype(jnp.int32)
    pad_rank = jnp.arange(bm, dtype=jnp.int32)[None, :]
    pad_key = jnp.where(pad_rank < (padded - counts)[:, None], 2 * experts[:, None] + 1, 2 * N_EXPERTS)
    keys = jnp.concatenate([2 * flat_e, pad_key.reshape(-1)])
    toks = jnp.concatenate([jnp.arange(nk, dtype=jnp.int32) // TOP_K, jnp.zeros((N_EXPERTS * bm,), jnp.int32)])
    _, slot_tok = lax.sort((keys, toks), num_keys=1, is_stable=True)
    n_used = (pend[-1] // bm).astype(jnp.int32).reshape(1)
    return gate, dest, slot_tok, blk_e, n_used


def _moe_dispatch_ffn(logits, xw, w1, w3, w2, bm=1024):
    gate, dest, slot_tok, blk_e, n_used = _moe_route(logits, bm)
    yw = _moe_grouped_ffn(blk_e, n_used, _gather_rows(xw, slot_tok), w1, w3, w2, bm)
    return _gather_rows(yw, dest[:, 0]), _gather_rows(yw, dest[:, 1]), gate


def _gmlp_kernel(groups, pu_ref, pv_ref, g_ref, b_ref, ws_ref, bs_ref, o_ref):
    u = jax.nn.gelu(pu_ref[...])
    v = _ln_rows(jax.nn.gelu(pv_ref[...]), g_ref[...], b_ref[...])
    for n in range(u.shape[0] // GM_CHUNK):
        rows = slice(n * GM_CHUNK, (n + 1) * GM_CHUNK)
        for grp in range(groups):
            cols = slice(grp * HEAD_DIM, (grp + 1) * HEAD_DIM)
            s = _dot3(ws_ref[grp], v[rows, cols]) + bs_ref[grp]
            o_ref[rows, cols] = (u[rows, cols] * s).astype(o_ref.dtype)


def _gmlp(p, col_u, col_v, gm_w, ln_g, ln_b, ws, bs, bt=512):
    n = p.shape[0]
    groups = ws.shape[0]
    bsb = jnp.broadcast_to(bs[:, :, None], (groups, GM_CHUNK, HEAD_DIM)).astype(F32)
    vec = pl.BlockSpec((1, gm_w), lambda i: (0, 0))
    full3 = pl.BlockSpec((groups, GM_CHUNK, HEAD_DIM), lambda i: (0, 0, 0))
    return pl.pallas_call(
        functools.partial(_gmlp_kernel, groups), grid=(n // bt,),
        in_specs=[pl.BlockSpec((bt, gm_w), lambda i: (i, col_u)), pl.BlockSpec((bt, gm_w), lambda i: (i, col_v)),
                  vec, vec, pl.BlockSpec((groups, GM_CHUNK, GM_CHUNK), lambda i: (0, 0, 0)), full3],
        out_specs=pl.BlockSpec((bt, gm_w), lambda i: (i, 0)),
        out_shape=jax.ShapeDtypeStruct((n, gm_w), BF16),
        compiler_params=_params("parallel"), name="gmlp",
    )(p, p, ln_g.reshape(1, gm_w), ln_b.reshape(1, gm_w), ws.astype(F32), bsb)


def _hgrn2_constants(c, reverse):
    n_lvl = int(math.log2(c))
    t = np.arange(c)
    ms, ws = [], []
    for lvl in range(n_lvl):
        m = c >> (lvl + 1)
        mid = (t // (2 * m)) * (2 * m) + m
        upper = t >= mid
        mat = np.zeros((c, c), np.float32)
        for r in range(c):
            if upper[r]:
                mat[r, mid[r]:r + 1] = 1.0
            else:
                mat[r, r + 1:mid[r]] = 1.0
        same = (t[:, None] // (2 * m)) == (t[None, :] // (2 * m))
        ws.append((same & upper[:, None] & ~upper[None, :]).astype(np.float32))
        ms.append(mat)
    ws.append(np.eye(c, dtype=np.float32))
    ms.append(np.tril(np.ones((c, c), np.float32)))
    ms.append(np.triu(np.ones((c, c), np.float32), 1))
    if reverse:
        ms = [a[::-1, ::-1] for a in ms]
        ws = [a[::-1, ::-1] for a in ws]
    return np.concatenate(ms, 0), np.stack(ws, 0)


def _hgrn2_kernel(heads, n_lvl, carry_row, final, reset_ref, *refs):
    if final:
        (q_ref, z_ref, i_ref, og_ref, oo_ref, loglb_ref, log1mlb_ref, onemlb_ref, ng_ref, m_ref, w_ref,
         out_ref, st_ref, e_ref) = refs
    else:
        (q_ref, z_ref, i_ref, loglb_ref, log1mlb_ref, onemlb_ref, m_ref, w_ref, out_ref, st_ref, e_ref) = refs
    n_streams, c = q_ref.shape[0], q_ref.shape[1]
    rows_e = m_ref.shape[0]
    step = pl.program_id(0)

    for s in range(n_streams):
        @pl.when(reset_ref[step * n_streams + s] == 1)
        def _():
            st_ref[s * heads:(s + 1) * heads] = jnp.zeros((heads, HEAD_DIM, HEAD_DIM), F32)

    for s in range(n_streams):
        z = z_ref[s]
        e = jnp.exp(-jnp.abs(z))
        r = 1.0 / (1.0 + e)
        log_sig = jnp.minimum(z, 0.0) - jnp.log(1.0 + e)
        sig_neg = jnp.where(z >= 0.0, e * r, r)
        a = loglb_ref[...]
        b = log1mlb_ref[...] + log_sig
        g = jnp.maximum(a, b) + jnp.log(1.0 + jnp.exp(-jnp.abs(a - b)))
        k = onemlb_ref[...] * sig_neg
        q = q_ref[s]
        qs = q * jax.nn.sigmoid(q)
        v = i_ref[s].astype(BF16)

        g_hi, g_lo = _split_bf16(g)
        m_all = m_ref[...]
        e0 = s * rows_e
        e_ref[e0:e0 + rows_e, :] = (jnp.dot(m_all, g_hi, preferred_element_type=F32)
                                    + jnp.dot(m_all, g_lo, preferred_element_type=F32))

        for h in range(heads):
            cols = slice(h * HEAD_DIM, (h + 1) * HEAD_DIM)
            qh, kh, vh = qs[:, cols], k[:, cols], v[:, cols]
            amat = w_ref[n_lvl] * lax.dot_general(qh.astype(BF16), kh.astype(BF16), NT_DIMS,
                                                  preferred_element_type=F32)
            for lvl in range(n_lvl):
                ex = jnp.exp(e_ref[e0 + lvl * c:e0 + (lvl + 1) * c, cols])
                amat = amat + w_ref[lvl] * lax.dot_general((qh * ex).astype(BF16), (kh * ex).astype(BF16),
                                                           NT_DIMS, preferred_element_type=F32)
            e_in = e_ref[e0 + n_lvl * c:e0 + (n_lvl + 1) * c, cols]
            q_in = (qh * jnp.exp(e_in)).astype(BF16)
            k_st = (kh * jnp.exp(e_ref[e0 + (n_lvl + 1) * c:e0 + (n_lvl + 2) * c, cols])).astype(BF16)
            dec = jnp.exp(e_in[carry_row:carry_row + 1, :])
            st = st_ref[s * heads + h]
            o = (lax.dot_general(q_in, st.astype(BF16), NT_DIMS, preferred_element_type=F32)
                 + jnp.dot(amat.astype(BF16), vh, preferred_element_type=F32))
            st_ref[s * heads + h] = st * dec + lax.dot_general(vh, k_st, TN_DIMS, preferred_element_type=F32)
            if final:
                o = o + oo_ref[s, :, cols]
                o = o * lax.rsqrt(jnp.mean(o * o, -1, keepdims=True) + RMS_EPS)
                og = og_ref[s, :, cols]
                out_ref[s, :, cols] = (o * ng_ref[:, cols] * (og * jax.nn.sigmoid(og))).astype(out_ref.dtype)
            else:
                out_ref[s, :, cols] = o


def _hgrn2_pass(p3, cols, hg_w, z_col, reverse, reset, lbs, extra, c):
    n_streams, n = p3.shape[0], p3.shape[1]
    nc = n // c
    heads = hg_w // HEAD_DIM
    n_lvl = int(math.log2(c))
    m_np, w_np = _hgrn2_constants(c, reverse)
    final = extra is not None

    def cmap(i):
        return (nc - 1 - i) if reverse else i

    def tok(col):
        return pl.BlockSpec((n_streams, c, hg_w), lambda i, r: (0, cmap(i), col))

    vec = pl.BlockSpec((1, hg_w), lambda i, r: (0, 0))
    in_specs = [tok(cols["q"]), tok(z_col), tok(cols["i"])]
    args = [p3, p3, p3]
    if final:
        o_other, norm_g = extra
        in_specs += [tok(cols["og"]), tok(0)]
        args += [p3, o_other]
    in_specs += [vec, vec, vec]
    args += list(lbs)
    if final:
        in_specs += [vec]
        args += [norm_g.reshape(1, hg_w).astype(F32)]
    in_specs += [pl.BlockSpec(m_np.shape, lambda i, r: (0, 0)), pl.BlockSpec(w_np.shape, lambda i, r: (0, 0, 0))]
    args += [jnp.asarray(m_np, BF16), jnp.asarray(w_np, F32)]
    grid_spec = pltpu.PrefetchScalarGridSpec(
        num_scalar_prefetch=1, grid=(nc,), in_specs=in_specs, out_specs=tok(0),
        scratch_shapes=[pltpu.VMEM((n_streams * heads, HEAD_DIM, HEAD_DIM), F32),
                        pltpu.VMEM((n_streams * m_np.shape[0], hg_w), F32)])
    order = np.arange(nc)[::-1] if reverse else np.arange(nc)
    reset_steps = jnp.asarray(np.asarray(reset, np.int32)[order].reshape(-1))
    return pl.pallas_call(
        functools.partial(_hgrn2_kernel, heads, n_lvl, 0 if reverse else c - 1, final), grid_spec=grid_spec,
        out_shape=jax.ShapeDtypeStruct((n_streams, n, hg_w), BF16 if final else F32),
        compiler_params=_params("arbitrary"), name="hgrn2_fwd" if final else "hgrn2_bwd",
    )(reset_steps, *args)


def _hgrn2(p, cols, hg_w, lb, norm_g, seq_lens, n_streams=2, c=HG_CHUNK):
    lb = lb.astype(F32).reshape(1, hg_w)
    lbs = (jnp.log(lb), jnp.log1p(-lb), 1.0 - lb)
    starts = np.cumsum([0] + list(seq_lens))
    n = int(starts[-1])
    per = n // n_streams
    assert n == p.shape[0] and all(s * per in starts for s in range(n_streams)), "a sequence straddles a stream cut"
    first = np.zeros(n // c, np.int32)
    last = np.zeros(n // c, np.int32)
    first[starts[:-1] // c] = 1
    last[starts[1:] // c - 1] = 1
    first = first.reshape(n_streams, per // c).T
    last = last.reshape(n_streams, per // c).T
    p3 = p.reshape(n_streams, per, p.shape[1])
    o_b = _hgrn2_pass(p3, cols, hg_w, cols["zb"], True, last, lbs, None, c)
    return _hgrn2_pass(p3, cols, hg_w, cols["zf"], False, first, lbs, (o_b, norm_g), c).reshape(n, hg_w)


def _dft_matrices(t):
    n = 2 * t
    f = jnp.arange(t, dtype=jnp.int32)
    ang = (2.0 * math.pi / n) * ((f[:, None] * f[None, :]) % n).astype(F32)
    cos, sin = jnp.cos(ang), jnp.sin(ang)
    alt = (1 - 2 * (f % 2)).astype(F32)
    first = f == 0
    fwd = jnp.stack([cos, jnp.where(first[:, None], alt[None, :], -sin)], 0).astype(BF16)
    wgt = jnp.where(first, 1.0, 2.0)[None, :] / n
    inv = jnp.stack([cos * wgt, jnp.where(first[None, :], alt[:, None] / n, -sin * wgt)], 0).astype(BF16)
    return fwd, inv


def _shortconv_kernel(flags_ref, x_ref, prev_ref, next_ref, w_ref, b_ref, o_ref):
    i = pl.program_id(0)
    x = x_ref[...]
    t = x.shape[0]
    rows = lax.broadcasted_iota(jnp.int32, x.shape, 0)
    has_prev = flags_ref[2 * i].astype(F32)
    has_next = flags_ref[2 * i + 1].astype(F32)
    up = jnp.where(rows == 0, prev_ref[7:8, :] * has_prev, pltpu.roll(x, 1, 0))
    dn = jnp.where(rows == t - 1, next_ref[0:1, :] * has_next, pltpu.roll(x, t - 1, 0))
    o_ref[...] = up * w_ref[0:1, :] + x * w_ref[1:2, :] + dn * w_ref[2:3, :] + b_ref[...]


def _shortconv(p, width, conv_w, conv_b, blk_flags, t=HY_BLOCK, ct=768):
    n = p.shape[0]
    nb = n // t
    r8 = t // 8
    grid_spec = pltpu.PrefetchScalarGridSpec(
        num_scalar_prefetch=1, grid=(nb, width // ct),
        in_specs=[pl.BlockSpec((t, ct), lambda i, j, f: (i, j)),
                  pl.BlockSpec((8, ct), lambda i, j, f: (jnp.maximum(i * r8 - 1, 0), j)),
                  pl.BlockSpec((8, ct), lambda i, j, f: (jnp.minimum((i + 1) * r8, nb * r8 - 1), j)),
                  pl.BlockSpec((3, ct), lambda i, j, f: (0, j)), pl.BlockSpec((1, ct), lambda i, j, f: (0, j))],
        out_specs=pl.BlockSpec((t, ct), lambda i, j, f: (i, j)))
    return pl.pallas_call(
        _shortconv_kernel, grid_spec=grid_spec, out_shape=jax.ShapeDtypeStruct((n, width), F32),
        compiler_params=_params("parallel", "parallel"), name="hyena_shortconv",
    )(blk_flags, p, p, p, conv_w.astype(F32), conv_b.reshape(1, width).astype(F32))


def _filter_kernel(l_total, feats_ref, w1_ref, b1_ref, w2_ref, b2_ref, w3_ref, b3_ref, fr_ref, wo_ref, dl_ref,
                   h_ref, sum_ref):
    del l_total
    i = pl.program_id(0)
    bt = feats_ref.shape[0]
    half_lanes = LANES // 2
    fr = fr_ref[...]
    feats = feats_ref[...]
    h = jnp.sin(fr * (_dot3(feats, w1_ref[...]) + b1_ref[...]))
    h = jnp.sin(fr * (_dot3(h, w2_ref[...]) + b2_ref[...]))
    h = jnp.sin(fr * (_dot3(h, w3_ref[...]) + b3_ref[...]))
    reps = wo_ref.shape[2] // dl_ref.shape[1]
    row = lax.broadcasted_iota(jnp.int32, (bt, 1), 0) + i * bt
    total = jnp.zeros((1, wo_ref.shape[2]), F32)
    for side in range(2):
        window = jnp.exp(-feats[:, side * half_lanes:side * half_lanes + 1] * dl_ref[...])
        out = _dot3(h, wo_ref[side]) * jnp.concatenate([window] * reps, axis=1)
        if side == 1:
            out = jnp.where(row == 0, 0.0, out)
        h_ref[side] = out
        total = total + jnp.sum(jnp.abs(out), axis=0, keepdims=True)

    @pl.when(i == 0)
    def _():
        sum_ref[...] = jnp.zeros_like(sum_ref)

    sum_ref[...] += total


def _hyena_filter(l, w1, b1, w2, b2, w3, b3, freq, w_out, hy_w, bt=512):
    hid = w2.shape[0]
    half_lanes = LANES // 2
    assert hid <= half_lanes and w1.shape[0] <= half_lanes
    half = HY_ORDER * hy_w
    bands = jnp.linspace(1e-4, HY_POS_BANDS - 1, HY_POS_BANDS, dtype=F32)

    def features(pos):
        tt = pos / max(l - 1, 1)
        ang = (2.0 * math.pi / l) * pos[:, None] * bands[None, :]
        f = jnp.concatenate([tt[:, None], jnp.cos(ang), -jnp.sin(ang)], -1)
        return jnp.pad(f, ((0, 0), (0, half_lanes - f.shape[1])))

    def twice_diag(w):
        wp = jnp.pad(w.astype(F32), ((0, half_lanes - w.shape[0]), (0, half_lanes - w.shape[1])))
        z = jnp.zeros_like(wp)
        return jnp.concatenate([jnp.concatenate([wp, z], 1), jnp.concatenate([z, wp], 1)], 0)

    def twice_vec(a):
        ap = jnp.pad(a.astype(F32), (0, half_lanes - a.shape[0]))
        return jnp.concatenate([ap, ap]).reshape(1, LANES)

    pos = jnp.arange(l, dtype=F32)
    feats = jnp.concatenate([features(pos), features(l - pos)], 1)
    wo = w_out.astype(F32).reshape(hid, HY_ORDER, 2, hy_w).transpose(2, 0, 1, 3).reshape(2, hid, half)
    wo = jnp.pad(wo, ((0, 0), (0, half_lanes - hid), (0, 0)))
    zero = jnp.zeros_like(wo[0])
    wop = jnp.stack([jnp.concatenate([wo[0], zero], 0), jnp.concatenate([zero, wo[1]], 0)], 0)
    deltas = jnp.abs(jnp.linspace(math.log(HY_DECAY_TARGET) / HY_SLOW_DECAY,
                                  math.log(HY_DECAY_TARGET) / HY_FAST_DECAY, hy_w, dtype=F32)).reshape(1, hy_w)
    bt = min(bt, l)
    sq = pl.BlockSpec((LANES, LANES), lambda i: (0, 0))
    vec = pl.BlockSpec((1, LANES), lambda i: (0, 0))
    taps, abs_sum = pl.pallas_call(
        functools.partial(_filter_kernel, l), grid=(l // bt,),
        in_specs=[pl.BlockSpec((bt, LANES), lambda i: (i, 0)), sq, vec, sq, vec, sq, vec, vec,
                  pl.BlockSpec((2, LANES, half), lambda i: (0, 0, 0)),
                  pl.BlockSpec((1, hy_w), lambda i: (0, 0))],
        out_specs=[pl.BlockSpec((2, bt, half), lambda i: (0, i, 0)), pl.BlockSpec((1, half), lambda i: (0, 0))],
        out_shape=[jax.ShapeDtypeStruct((2, l, half), F32), jax.ShapeDtypeStruct((1, half), F32)],
        compiler_params=_params("arbitrary"), name="hyena_filter",
    )(feats, twice_diag(w1), twice_vec(b1), twice_diag(w2), twice_vec(b2), twice_diag(w3), twice_vec(b3),
      twice_vec(freq), wop, deltas)
    return taps.reshape(2 * l, half), abs_sum


def _dft_cols(t, width):
    return width if t <= 1024 else 256


def _dft_kernel(f_ref, x_ref, o_ref):
    x = x_ref[...].astype(BF16)
    o_ref[0] = jnp.dot(f_ref[0], x, preferred_element_type=F32).astype(o_ref.dtype)
    o_ref[1] = jnp.dot(f_ref[1], x, preferred_element_type=F32).astype(o_ref.dtype)


def _block_dft(fwd, x, col_off, width, out_dtype, t=HY_BLOCK, ct=256):
    nseg = x.shape[0] // t
    ct = min(ct, width)
    return pl.pallas_call(
        _dft_kernel, grid=(nseg, width // ct),
        in_specs=[pl.BlockSpec((2, t, t), lambda s, j: (0, 0, 0)),
                  pl.BlockSpec((t, ct), lambda s, j: (s, j + col_off))],
        out_specs=pl.BlockSpec((None, 2, t, ct), lambda s, j: (s, 0, 0, j)),
        out_shape=jax.ShapeDtypeStruct((nseg, 2, t, width), out_dtype),
        compiler_params=_params("parallel", "parallel"), name="hyena_block_dft",
    )(fwd, x)


def _segment_index(d, nb):
    return d if d >= 0 else 2 * nb + d


def _hyena_filter_spectra(l, fwd, filt_w, hy_w, t=HY_BLOCK):
    nb = l // t
    taps, abs_sum = _hyena_filter(l, *filt_w, hy_w)
    half = taps.shape[1]
    seg_spec = _block_dft(fwd, taps, 0, half, F32, t, _dft_cols(t, hy_w))
    first_rows = taps.reshape(2 * nb, t, half)[:, 0, :]
    seg0 = jnp.stack([first_rows[_segment_index(d - 1, nb)] for d in range(-(nb - 1), nb)], 0)
    return seg_spec, seg0.reshape(2 * nb - 1, 1, half), 1.0 / abs_sum


MIX_ROWS = 8


def _mix_kernel(nb, z_ref, a_ref, s0_ref, inv_ref, *refs):
    o_ref, k_ref = refs[-2], refs[-1]
    ft, ct = z_ref.shape[2], z_ref.shape[3]
    first_tile = pl.program_id(1) == 0

    tile_rows = lax.broadcasted_iota(jnp.int32, (ft, ct), 0)
    sgn = (1 - 2 * (tile_rows & 1)).astype(F32)
    real_row = jnp.logical_and(tile_rows == 0, first_tile)
    inv = inv_ref[...]
    for slot in range(2 * nb - 1):
        d = slot - (nb - 1)
        ia, ib = _segment_index(d, nb), _segment_index(d - 1, nb)
        s0 = s0_ref[slot]
        k_ref[slot, 0] = (a_ref[ia, 0] + sgn * (a_ref[ib, 0] - s0)) * inv
        k_ref[slot, 1] = (a_ref[ia, 1] + sgn * (a_ref[ib, 1] - jnp.where(real_row, s0, 0.0))) * inv

    n_out = 2 if nb % 2 == 0 else 1

    def out_rows(i, r0, masked):
        rows = pl.ds(r0, MIX_ROWS)
        acc = [[jnp.zeros((MIX_ROWS, ct), F32) for _ in range(2)] for _ in range(n_out)]
        if masked:
            row0 = jnp.logical_and(lax.broadcasted_iota(jnp.int32, (MIX_ROWS, ct), 0) == 0, first_tile)
        d_hi = i + (n_out - 1) + (nb - 1)
        k_prev = (k_ref[d_hi, 0, rows, :], k_ref[d_hi, 1, rows, :]) if n_out == 2 else None
        for j in range(nb):
            d = i - j + (nb - 1)
            zt, zb = z_ref[j, 0, rows, :], z_ref[j, 1, rows, :]
            k_cur = (k_ref[d, 0, rows, :], k_ref[d, 1, rows, :])
            for o, (kt, kb) in enumerate((k_cur, k_prev)[:n_out]):
                bb = zb * kb
                if masked:
                    acc[o][0] = acc[o][0] + (zt * kt - jnp.where(row0, 0.0, bb))
                    acc[o][1] = acc[o][1] + jnp.where(row0, bb, zt * kb + zb * kt)
                else:
                    acc[o][0] = acc[o][0] + (zt * kt - bb)
                    acc[o][1] = acc[o][1] + (zt * kb + zb * kt)
            k_prev = k_cur
        for o in range(n_out):
            o_ref[i + o, 0, rows, :] = acc[o][0].astype(o_ref.dtype)
            o_ref[i + o, 1, rows, :] = acc[o][1].astype(o_ref.dtype)

    def all_outputs(r0, masked):
        def body(ip, carry):
            out_rows(ip * n_out, r0, masked)
            return carry
        lax.fori_loop(0, nb // n_out, body, 0)

    all_outputs(0, True)

    def chunk(r, carry):
        all_outputs(pl.multiple_of(r * MIX_ROWS, MIX_ROWS), False)
        return carry

    lax.fori_loop(1, ft // MIX_ROWS, chunk, 0)


def _hyena_mix(zspec, kspec, k_col, n_seq, nb, blk_off, ct=256):
    seg_spec, seg0, inv_norm = kspec
    _, _, t, width = zspec.shape
    ft = max(MIX_ROWS, min(t, 2048 // nb))
    nct = width // ct
    assert blk_off % nb == 0 and t % ft == 0
    s_off = blk_off // nb
    return pl.pallas_call(
        functools.partial(_mix_kernel, nb), grid=(n_seq, t // ft, nct),
        in_specs=[pl.BlockSpec((nb, 2, ft, ct), lambda s, fi, j: (s + s_off, 0, fi, j)),
                  pl.BlockSpec((2 * nb, 2, ft, ct), lambda s, fi, j: (0, 0, fi, j + k_col * nct)),
                  pl.BlockSpec((2 * nb - 1, 1, ct), lambda s, fi, j: (0, 0, j + k_col * nct)),
                  pl.BlockSpec((1, ct), lambda s, fi, j: (0, j + k_col * nct))],
        out_specs=pl.BlockSpec((nb, 2, ft, ct), lambda s, fi, j: (s, 0, fi, j)),
        out_shape=jax.ShapeDtypeStruct((n_seq * nb, 2, t, width), BF16),
        scratch_shapes=[pltpu.VMEM((2 * nb - 1, 2, ft, ct), F32)],
        compiler_params=_params("parallel", "parallel", "parallel"), name="hyena_mix",
    )(zspec, seg_spec, seg0, inv_norm)


def _inverse_kernel(bounds, *refs):
    n_groups = len(bounds)
    y_refs = refs[:n_groups]
    fi_ref, zin_ref, gate_ref, bias_ref, o_ref = refs[n_groups:]
    i = pl.program_id(0)
    for (lo, hi), y_ref in zip(bounds, y_refs):
        @pl.when(jnp.logical_and(i >= lo, i < hi))
        def _():
            y = (jnp.dot(fi_ref[0], y_ref[0], preferred_element_type=F32)
                 + jnp.dot(fi_ref[1], y_ref[1], preferred_element_type=F32))
            o_ref[...] = (gate_ref[...] * (y + zin_ref[...] * bias_ref[...])).astype(o_ref.dtype)


def _hyena_inverse(yspecs, inv, zin, zin_col, gate, gate_col, bias, out_dtype, ct=256):
    _, _, t, width = yspecs[0].shape
    nct = width // ct
    bounds, lo = [], 0
    for y in yspecs:
        bounds.append((lo, lo + y.shape[0]))
        lo += y.shape[0]

    def group_spec(lo, hi):
        return pl.BlockSpec((None, 2, t, ct), lambda i, j: (jnp.clip(i - lo, 0, hi - lo - 1), 0, 0, j))

    return pl.pallas_call(
        functools.partial(_inverse_kernel, tuple(bounds)), grid=(lo, nct),
        in_specs=[group_spec(*b) for b in bounds] + [
            pl.BlockSpec((2, t, t), lambda i, j: (0, 0, 0), pipeline_mode=pl.Buffered(1)),
            pl.BlockSpec((t, ct), lambda i, j: (i, j + zin_col * nct)),
            pl.BlockSpec((t, ct), lambda i, j: (i, j + gate_col * nct)),
            pl.BlockSpec((1, ct), lambda i, j: (0, j))],
        out_specs=pl.BlockSpec((t, ct), lambda i, j: (i, j)),
        out_shape=jax.ShapeDtypeStruct((lo * t, width), out_dtype),
        compiler_params=_params("arbitrary", "arbitrary"), name="hyena_inverse",
    )(*yspecs, inv, zin, gate, bias)


def _hyena(p, hy_w, groups, conv_w, conv_b, filt_w, bias, t=HY_BLOCK):
    flags = []
    for n_seq, l in groups:
        nb = l // t
        flags += [int(bi > 0) if side == 0 else int(bi < nb - 1)
                  for _ in range(n_seq) for bi in range(nb) for side in range(2)]
    fwd, inv = _dft_matrices(t)
    u = _shortconv(p, 3 * hy_w, conv_w, conv_b, jnp.asarray(np.asarray(flags, np.int32)), t, ct=hy_w)
    kspecs = [_hyena_filter_spectra(l, fwd, filt_w, hy_w, t) for _, l in groups]
    bias = bias.astype(F32)
    ct = _dft_cols(t, hy_w)
    z, z_col = u, 2
    for o in range(HY_ORDER):
        zspec = _block_dft(fwd, z, z_col * (hy_w // ct), hy_w, F32, t, ct)
        yspecs, blk = [], 0
        for (n_seq, l), kspec in zip(groups, kspecs):
            yspecs.append(_hyena_mix(zspec, kspec, o, n_seq, l // t, blk))
            blk += n_seq * (l // t)
        last = o == HY_ORDER - 1
        z = _hyena_inverse(yspecs, inv, z, z_col, u, o, bias[o:o + 1], BF16 if last else F32, ct)
        z_col = 0
    return z


def kernel(x_prompt, x_sample, ln_in_g, ln_in_b, w_in, hy_conv_w, hy_conv_b, hy_pos_w1, hy_pos_b1, hy_pos_w2, hy_pos_b2, hy_pos_w3, hy_pos_b3, hy_sin_freq, hy_pos_wout, hy_bias, gm_ln_g, gm_ln_b, gm_ws, gm_bs, hg_lb_raw, hg_norm_g, w_out, ln1_g, ln1_b, ln2_g, ln2_b, ffn_w1, ffn_w3, ffn_w2, moe_router_w, moe_router_b, moe_w1, moe_w3, moe_w2):
    depth, d_model, in_w = w_in.shape
    hy_w = hy_bias.shape[-1]
    gm_w = gm_ln_g.shape[-1]
    hg_w = hg_norm_g.shape[-1]
    o1 = 3 * hy_w
    o2 = o1 + 2 * gm_w
    alpha = (2 * depth) ** 0.25
    groups = [(x_prompt.shape[0], x_prompt.shape[1]), (x_sample.shape[0], x_sample.shape[1])]
    seq_lens = [l for n_seq, l in groups for _ in range(n_seq)]
    n_prompt = x_prompt.shape[0] * x_prompt.shape[1]
    hg_off = o1 // hg_w
    hg_cols = {"q": hg_off, "zf": hg_off + 1, "zb": hg_off + 2, "i": hg_off + 3, "og": hg_off + 4}
    gm_off = (o1 + 5 * hg_w) // gm_w

    lb_all = jnp.cumsum(jax.nn.softmax(hg_lb_raw.astype(F32), axis=0), axis=0)
    lb_all = lb_all - lb_all[:1]

    x, xb = _layernorm_pair(x_prompt.reshape(-1, d_model), x_sample.reshape(-1, d_model), ln_in_g, ln_in_b)
    for l in range(depth):
        w_l = jnp.concatenate([w_in[l][:, :o1], w_in[l][:, o2:], w_in[l][:, o1:o2]], 1).astype(BF16)
        p = _matmul(xb, w_l)
        filt_w = (hy_pos_w1[l], hy_pos_b1[l], hy_pos_w2[l], hy_pos_b2[l], hy_pos_w3[l], hy_pos_b3[l],
                  hy_sin_freq[l], hy_pos_wout[l])
        y_hy = _hyena(p, hy_w, groups, hy_conv_w[l], hy_conv_b[l], filt_w, hy_bias[l])
        y_hg = _hgrn2(p, hg_cols, hg_w, lb_all[l], hg_norm_g[l], seq_lens)
        y_gm = _gmlp(p, gm_off, gm_off + 1, gm_w, gm_ln_g[l], gm_ln_b[l], gm_ws[l], gm_bs[l])
        wo = w_out[l].astype(BF16)
        parts = [y_hy, y_gm, y_hg]
        wo_parts = [wo[:hy_w], wo[hy_w:hy_w + gm_w], wo[hy_w + gm_w:]]
        j = l // 2
        if l % 2 == 0:
            x, xb = _mix_out_residual_ln(parts, wo_parts, x, ln1_g[l], ln1_b[l], alpha)
            x, xb = _ffn_residual_ln(xb, ffn_w1[j].astype(BF16), ffn_w3[j].astype(BF16), ffn_w2[j].astype(BF16),
                                     x, ln2_g[l], ln2_b[l], alpha)
        else:
            n_exp = moe_router_w.shape[-1]
            wr = jnp.pad(moe_router_w[j].astype(BF16), ((0, 0), (0, LANES - n_exp)))
            x, xw, logits = _mix_out_residual_ln(parts, wo_parts, x, ln1_g[l], ln1_b[l], alpha, wr)
            logits = logits[:, :n_exp] + moe_router_b[j].astype(F32)
            y0, y1, gate = _moe_dispatch_ffn(logits, xw, moe_w1[j], moe_w3[j], _cast_bf16(moe_w2[j]))
            xa, xs = _combine_residual_ln_split(x, y0, y1, gate, ln2_g[l], ln2_b[l], alpha, n_prompt)
            if l == depth - 1:
                return (xa.reshape(x_prompt.shape), xs.reshape(x_sample.shape))
            x = jnp.concatenate([xa, xs], 0)
            xb = x.astype(BF16)
    return (x[:n_prompt].reshape(x_prompt.shape), x[n_prompt:].reshape(x_sample.shape))
```

```python
import functools
import math

import numpy as np
import jax
import jax.numpy as jnp
from jax import lax
from jax.experimental import pallas as pl
from jax.experimental.pallas import tpu as pltpu
from jax.experimental.pallas import tpu_sc as plsc

HEAD_DIM = 128
HY_ORDER = 2
HY_POS_BANDS = 16
HY_FAST_DECAY = 0.3
HY_SLOW_DECAY = 1.5
HY_DECAY_TARGET = 1e-2
GM_CHUNK = 128
N_EXPERTS = 8
TOP_K = 2
LN_EPS = 1e-5
RMS_EPS = 1e-6

HY_BLOCK = 1024
HG_CHUNK = 128
LANES = 128

V7X_VMEM_BYTES = 64 * 1024 * 1024
VMEM_LIMIT = V7X_VMEM_BYTES - 8 * 1024 * 1024

F32 = jnp.float32
BF16 = jnp.bfloat16
NT_DIMS = (((1,), (1,)), ((), ()))
TN_DIMS = (((0,), (0,)), ((), ()))


def _params(*sem):
    return pltpu.CompilerParams(dimension_semantics=sem, vmem_limit_bytes=VMEM_LIMIT)


def _ln_rows(x, g, b):
    mu = jnp.mean(x, -1, keepdims=True)
    xc = x - mu
    var = jnp.mean(xc * xc, -1, keepdims=True)
    return xc * lax.rsqrt(var + LN_EPS) * g + b


def _split_bf16(x):
    hi = x.astype(BF16)
    lo = (x - hi.astype(F32)).astype(BF16)
    return hi, lo


def _dot3(a, b):
    ah, al = _split_bf16(a)
    bh, bl = _split_bf16(b)
    return (jnp.dot(ah, bh, preferred_element_type=F32) + jnp.dot(ah, bl, preferred_element_type=F32)
            + jnp.dot(al, bh, preferred_element_type=F32))


def _pack_bf16_pairs(y):
    w = y.shape[1] // 2
    bits = lax.bitcast_convert_type(y.astype(BF16).astype(F32), jnp.uint32)
    word = (bits[:, :w] >> 16) | (bits[:, w:] & jnp.uint32(0xFFFF0000))
    return lax.bitcast_convert_type(word, jnp.int32)


def _unpack_bf16_pairs(word):
    u = lax.bitcast_convert_type(word, jnp.uint32)
    lo = lax.bitcast_convert_type(u << 16, F32)
    hi = lax.bitcast_convert_type(u & jnp.uint32(0xFFFF0000), F32)
    return jnp.concatenate([lo, hi], axis=1)


SC_GATHER_WINDOW = 128
SC_GATHER_WORDS = 256


def _gather_rows(xw, idx):
    n, words = xw.shape
    m = idx.shape[0]
    parts = words // SC_GATHER_WORDS
    assert words % SC_GATHER_WORDS == 0 and m % (2 * SC_GATHER_WINDOW) == 0
    half = m // SC_GATHER_WINDOW // 2
    mesh = plsc.VectorSubcoreMesh(core_axis_name="core", subcore_axis_name="subcore")

    @pl.kernel(out_type=jax.ShapeDtypeStruct((m, words), xw.dtype), mesh=mesh)
    def gather(x_hbm, i_hbm, o_hbm):
        def body(i_vmem, o_vmem):
            piece = pl.program_id(2)
            pltpu.sync_copy(x_hbm.at[i_vmem.at[0], pl.ds(piece * SC_GATHER_WORDS, SC_GATHER_WORDS)], o_vmem)

        pltpu.emit_pipeline(
            body, grid=(2, half, parts),
            in_specs=[pl.BlockSpec((1, SC_GATHER_WINDOW), index_map=lambda c, i, p: (0, c * half + i))],
            out_specs=[pl.BlockSpec((SC_GATHER_WINDOW, SC_GATHER_WORDS),
                                    index_map=lambda c, i, p: (c * half + i, p))],
            core_axis_name=("core", "subcore"),
            dimension_semantics=(pltpu.PARALLEL, pltpu.PARALLEL, pltpu.ARBITRARY),
        )(i_hbm, o_hbm)

    return gather(xw, idx.astype(jnp.int32).reshape(1, m))


def _ln2_kernel(na_blocks, xa_ref, xb_ref, g_ref, b_ref, of_ref, ob_ref):
    def emit(x_ref):
        y = _ln_rows(x_ref[...], g_ref[...], b_ref[...])
        of_ref[...] = y
        ob_ref[...] = y.astype(BF16)

    first = pl.program_id(0) < na_blocks
    pl.when(first)(lambda: emit(xa_ref))
    pl.when(jnp.logical_not(first))(lambda: emit(xb_ref))


def _layernorm_pair(xa, xb, g, b, bm=512):
    d = xa.shape[1]
    na, nb = xa.shape[0] // bm, xb.shape[0] // bm
    row = pl.BlockSpec((bm, d), lambda i: (i, 0))
    vec = pl.BlockSpec((1, d), lambda i: (0, 0))
    m = (na + nb) * bm
    return pl.pallas_call(
        functools.partial(_ln2_kernel, na), grid=(na + nb,),
        in_specs=[pl.BlockSpec((bm, d), lambda i: (jnp.minimum(i, na - 1), 0)),
                  pl.BlockSpec((bm, d), lambda i: (jnp.maximum(i - na, 0), 0)), vec, vec],
        out_specs=[row, row],
        out_shape=[jax.ShapeDtypeStruct((m, d), F32), jax.ShapeDtypeStruct((m, d), BF16)],
        compiler_params=_params("arbitrary"), name="layernorm",
    )(xa, xb, g.reshape(1, d), b.reshape(1, d))


def _combine_ln_kernel(alpha, na_blocks, x_ref, y0_ref, y1_ref, gate_ref, g_ref, b_ref, oa_ref, ob_ref):
    gate = gate_ref[...]
    ff = _unpack_bf16_pairs(y0_ref[...]) * gate[:, 0:1] + _unpack_bf16_pairs(y1_ref[...]) * gate[:, 1:2]
    y = _ln_rows(alpha * x_ref[...] + ff, g_ref[...], b_ref[...])
    first = pl.program_id(0) < na_blocks

    @pl.when(first)
    def _():
        oa_ref[...] = y

    @pl.when(jnp.logical_not(first))
    def _():
        ob_ref[...] = y


def _combine_residual_ln_split(x, y0, y1, gate, g, b, alpha, n_first, bm=512):
    m, d = x.shape
    na = n_first // bm
    row = pl.BlockSpec((bm, d), lambda i: (i, 0))
    packed = pl.BlockSpec((bm, d // 2), lambda i: (i, 0))
    vec = pl.BlockSpec((1, d), lambda i: (0, 0))
    return pl.pallas_call(
        functools.partial(_combine_ln_kernel, alpha, na), grid=(m // bm,),
        in_specs=[row, packed, packed, pl.BlockSpec((bm, gate.shape[1]), lambda i: (i, 0)), vec, vec],
        out_specs=[pl.BlockSpec((bm, d), lambda i: (jnp.minimum(i, na - 1), 0)),
                   pl.BlockSpec((bm, d), lambda i: (jnp.maximum(i - na, 0), 0))],
        out_shape=[jax.ShapeDtypeStruct((n_first, d), F32), jax.ShapeDtypeStruct((m - n_first, d), F32)],
        compiler_params=_params("arbitrary"), name="moe_combine_residual_ln",
    )(x, y0, y1, gate, g.reshape(1, d), b.reshape(1, d))


def _mm_kernel(a_ref, w_ref, o_ref):
    o_ref[...] = jnp.dot(a_ref[...], w_ref[...], preferred_element_type=F32).astype(o_ref.dtype)


def _matmul(a, w, out_dtype=F32, bm=1024, bn=1024):
    m, k = a.shape
    n = w.shape[1]
    bm, bn = min(bm, m), min(bn, n)
    return pl.pallas_call(
        _mm_kernel, grid=(m // bm, n // bn),
        in_specs=[pl.BlockSpec((bm, k), lambda i, j: (i, 0)), pl.BlockSpec((k, bn), lambda i, j: (0, j))],
        out_specs=pl.BlockSpec((bm, bn), lambda i, j: (i, j)),
        out_shape=jax.ShapeDtypeStruct((m, n), out_dtype),
        compiler_params=_params("parallel", "arbitrary"), name="matmul",
    )(a, w)


def _mix_out_kernel(alpha, routed, a0_ref, a1_ref, a2_ref, w0_ref, w1_ref, w2_ref, x_ref, g_ref, b_ref, *refs):
    mix = (jnp.dot(a0_ref[...], w0_ref[...], preferred_element_type=F32)
           + jnp.dot(a1_ref[...], w1_ref[...], preferred_element_type=F32)
           + jnp.dot(a2_ref[...], w2_ref[...], preferred_element_type=F32))
    y = _ln_rows(alpha * x_ref[...] + mix, g_ref[...], b_ref[...])
    if routed:
        wr_ref, of_ref, ob_ref, lg_ref = refs
        y_hi, y_lo = _split_bf16(y)
        lg_ref[...] = (jnp.dot(y_hi, wr_ref[...], preferred_element_type=F32)
                       + jnp.dot(y_lo, wr_ref[...], preferred_element_type=F32))
        ob_ref[...] = _pack_bf16_pairs(y)
    else:
        of_ref, ob_ref = refs
        ob_ref[...] = y.astype(BF16)
    of_ref[...] = y


def _mix_out_residual_ln(parts, weights, x, g, b, alpha, router_w=None, bm=512):
    m, d = x.shape
    row = pl.BlockSpec((bm, d), lambda i: (i, 0))
    vec = pl.BlockSpec((1, d), lambda i: (0, 0))
    in_specs = [pl.BlockSpec((bm, a.shape[1]), lambda i: (i, 0)) for a in parts]
    in_specs += [pl.BlockSpec(w.shape, lambda i: (0, 0)) for w in weights] + [row, vec, vec]
    args = [*parts, *weights, x, g.reshape(1, d), b.reshape(1, d)]
    out_specs = [row, row]
    out_shape = [jax.ShapeDtypeStruct((m, d), F32), jax.ShapeDtypeStruct((m, d), BF16)]
    if router_w is not None:
        in_specs.append(pl.BlockSpec(router_w.shape, lambda i: (0, 0)))
        args.append(router_w)
        out_specs[1] = pl.BlockSpec((bm, d // 2), lambda i: (i, 0))
        out_shape[1] = jax.ShapeDtypeStruct((m, d // 2), jnp.int32)
        out_specs.append(pl.BlockSpec((bm, router_w.shape[1]), lambda i: (i, 0)))
        out_shape.append(jax.ShapeDtypeStruct((m, router_w.shape[1]), F32))
    return pl.pallas_call(
        functools.partial(_mix_out_kernel, alpha, router_w is not None), grid=(m // bm,),
        in_specs=in_specs, out_specs=out_specs, out_shape=out_shape,
        compiler_params=_params("parallel"), name="out_proj_residual_ln",
    )(*args)


def _swiglu_acc(a_ref, w1_ref, w3_ref, w2_ref, acc_ref):
    f = pl.program_id(1)

    @pl.when(f == 0)
    def _():
        acc_ref[...] = jnp.zeros_like(acc_ref)

    a = a_ref[...]
    h1 = jnp.dot(a, w1_ref[...], preferred_element_type=F32)
    h3 = jnp.dot(a, w3_ref[...], preferred_element_type=F32)
    gated = (h1 * jax.nn.sigmoid(h1) * h3).astype(BF16)
    acc_ref[...] += jnp.dot(gated, w2_ref[...], preferred_element_type=F32)


def _ffn_kernel(alpha, a_ref, w1_ref, w3_ref, w2_ref, x_ref, g_ref, b_ref, of_ref, ob_ref, acc_ref):
    _swiglu_acc(a_ref, w1_ref, w3_ref, w2_ref, acc_ref)

    @pl.when(pl.program_id(1) == pl.num_programs(1) - 1)
    def _():
        y = _ln_rows(alpha * x_ref[...] + acc_ref[...], g_ref[...], b_ref[...])
        of_ref[...] = y
        ob_ref[...] = y.astype(BF16)


def _ffn_residual_ln(a, w1, w3, w2, x, g, b, alpha, bm=512, bf=512):
    m, d = a.shape
    dff = w1.shape[1]
    row = pl.BlockSpec((bm, d), lambda i, f: (i, 0))
    vec = pl.BlockSpec((1, d), lambda i, f: (0, 0))
    up = pl.BlockSpec((d, bf), lambda i, f: (0, f))
    return pl.pallas_call(
        functools.partial(_ffn_kernel, alpha), grid=(m // bm, dff // bf),
        in_specs=[row, up, up, pl.BlockSpec((bf, d), lambda i, f: (f, 0)), row, vec, vec],
        out_specs=[row, row],
        out_shape=[jax.ShapeDtypeStruct((m, d), F32), jax.ShapeDtypeStruct((m, d), BF16)],
        scratch_shapes=[pltpu.VMEM((bm, d), F32)],
        compiler_params=_params("parallel", "arbitrary"), name="ffn_residual_ln",
    )(a, w1, w3, w2, x, g.reshape(1, d), b.reshape(1, d))


def _moe_ffn_kernel(blk_e_ref, used_ref, a_ref, w1_ref, w3_ref, w2_ref, o_ref, acc_ref):
    del blk_e_ref
    f = pl.program_id(1)

    @pl.when(pl.program_id(0) < used_ref[0])
    def _():
        @pl.when(f == 0)
        def _():
            acc_ref[...] = jnp.zeros_like(acc_ref)

        a = _unpack_bf16_pairs(a_ref[...]).astype(BF16)
        h1 = jnp.dot(a, w1_ref[...].astype(BF16), preferred_element_type=F32)
        h3 = jnp.dot(a, w3_ref[...].astype(BF16), preferred_element_type=F32)
        gated = (h1 * jax.nn.sigmoid(h1) * h3).astype(BF16)
        acc_ref[...] += jnp.dot(gated, w2_ref[...].astype(BF16), preferred_element_type=F32)

        @pl.when(f == pl.num_programs(1) - 1)
        def _():
            o_ref[...] = _pack_bf16_pairs(acc_ref[...])


def _moe_grouped_ffn(blk_e, n_used, xw, w1, w3, w2, bm, bf=512):
    cap, words = xw.shape
    d = 2 * words
    dff = w1.shape[2]
    nf = dff // bf

    def blk(i, u):
        return jnp.minimum(i, u[0] - 1)

    def ftile(i, f, u):
        return jnp.where(i < u[0], f, nf - 1)

    row = pl.BlockSpec((bm, words), lambda i, f, e, u: (blk(i, u), 0), pipeline_mode=pl.Buffered(1))
    up = pl.BlockSpec((None, d, bf), lambda i, f, e, u: (e[blk(i, u)], 0, ftile(i, f, u)))
    down = pl.BlockSpec((None, bf, d), lambda i, f, e, u: (e[blk(i, u)], ftile(i, f, u), 0))
    grid_spec = pltpu.PrefetchScalarGridSpec(
        num_scalar_prefetch=2, grid=(cap // bm, nf), in_specs=[row, up, up, down],
        out_specs=row, scratch_shapes=[pltpu.VMEM((bm, d), F32)])
    return pl.pallas_call(
        _moe_ffn_kernel, grid_spec=grid_spec, out_shape=jax.ShapeDtypeStruct((cap, words), jnp.int32),
        compiler_params=_params("arbitrary", "arbitrary"), name="moe_grouped_ffn",
    )(blk_e, n_used, xw, w1, w3, w2)


def _moe_route(logits, bm):
    n = logits.shape[0]
    top_val, top_idx = lax.top_k(logits, TOP_K)
    gate = jax.nn.softmax(top_val, axis=-1)
    nk = n * TOP_K
    assert nk % bm == 0
    experts = jnp.arange(N_EXPERTS, dtype=jnp.int32)
    flat_e = top_idx.reshape(nk).astype(jnp.int32)
    one_hot = flat_e[:, None] == experts[None, :]
    order = jnp.argsort(flat_e, stable=True).astype(jnp.int32)
    rank = jnp.argsort(order).astype(jnp.int32)
    counts = jnp.sum(one_hot.astype(jnp.int32), 0)
    padded = (counts + bm - 1) // bm * bm
    start = jnp.cumsum(counts) - counts
    pend = jnp.cumsum(padded)
    shift = jnp.sum(jnp.where(one_hot, (pend - padded - start)[None, :], 0), 1)
    dest = (rank + shift).reshape(n, TOP_K)
    n_blocks = nk // bm + N_EXPERTS
    blk_e = jnp.minimum(jnp.searchsorted(pend, jnp.arange(n_blocks, dtype=jnp.int32) * bm, side='right'),
                        N_EXPERTS - 1).astype(jnp.int32)
    pad_rank = jnp.arange(bm, dtype=jnp.int32)[None, :]
    pad_key = jnp.where(pad_rank < (padded - counts)[:, None], 2 * experts[:, None] + 1, 2 * N_EXPERTS)
    keys = jnp.concatenate([2 * flat_e, pad_key.reshape(-1)])
    toks = jnp.concatenate([jnp.arange(nk, dtype=jnp.int32) // TOP_K, jnp.zeros((N_EXPERTS * bm,), jnp.int32)])
    _, slot_tok = lax.sort((keys, toks), num_keys=1, is_stable=True)
    n_used = (pend[-1] // bm).astype(jnp.int32).reshape(1)
    return gate, dest, slot_tok, blk_e, n_used


def _moe_dispatch_ffn(logits, xw, w1, w3, w2, bm=1024):
    gate, dest, slot_tok, blk_e, n_used = _moe_route(logits, bm)
    yw = _moe_grouped_ffn(blk_e, n_used, _gather_rows(xw, slot_tok), w1, w3, w2, bm)
    return _gather_rows(yw, dest[:, 0]), _gather_rows(yw, dest[:, 1]), gate


def _gmlp_kernel(groups, pu_ref, pv_ref, g_ref, b_ref, ws_ref, bs_ref, o_ref):
    u = jax.nn.gelu(pu_ref[...])
    v = _ln_rows(jax.nn.gelu(pv_ref[...]), g_ref[...], b_ref[...])
    for n in range(u.shape[0] // GM_CHUNK):
        rows = slice(n * GM_CHUNK, (n + 1) * GM_CHUNK)
        for grp in range(groups):
            cols = slice(grp * HEAD_DIM, (grp + 1) * HEAD_DIM)
            s = _dot3(ws_ref[grp], v[rows, cols]) + bs_ref[grp]
            o_ref[rows, cols] = (u[rows, cols] * s).astype(o_ref.dtype)


def _gmlp(p, col_u, col_v, gm_w, ln_g, ln_b, ws, bs, bt=512):
    n = p.shape[0]
    groups = ws.shape[0]
    bsb = jnp.broadcast_to(bs[:, :, None], (groups, GM_CHUNK, HEAD_DIM)).astype(F32)
    vec = pl.BlockSpec((1, gm_w), lambda i: (0, 0))
    full3 = pl.BlockSpec((groups, GM_CHUNK, HEAD_DIM), lambda i: (0, 0, 0))
    return pl.pallas_call(
        functools.partial(_gmlp_kernel, groups), grid=(n // bt,),
        in_specs=[pl.BlockSpec((bt, gm_w), lambda i: (i, col_u)), pl.BlockSpec((bt, gm_w), lambda i: (i, col_v)),
                  vec, vec, pl.BlockSpec((groups, GM_CHUNK, GM_CHUNK), lambda i: (0, 0, 0)), full3],
        out_specs=pl.BlockSpec((bt, gm_w), lambda i: (i, 0)),
        out_shape=jax.ShapeDtypeStruct((n, gm_w), BF16),
        compiler_params=_params("parallel"), name="gmlp",
    )(p, p, ln_g.reshape(1, gm_w), ln_b.reshape(1, gm_w), ws.astype(F32), bsb)


def _hgrn2_constants(c, reverse):
    n_lvl = int(math.log2(c))
    t = np.arange(c)
    ms, ws = [], []
    for lvl in range(n_lvl):
        m = c >> (lvl + 1)
        mid = (t // (2 * m)) * (2 * m) + m
        upper = t >= mid
        mat = np.zeros((c, c), np.float32)
        for r in range(c):
            if upper[r]:
                mat[r, mid[r]:r + 1] = 1.0
            else:
                mat[r, r + 1:mid[r]] = 1.0
        same = (t[:, None] // (2 * m)) == (t[None, :] // (2 * m))
        ws.append((same & upper[:, None] & ~upper[None, :]).astype(np.float32))
        ms.append(mat)
    ws.append(np.eye(c, dtype=np.float32))
    ms.append(np.tril(np.ones((c, c), np.float32)))
    ms.append(np.triu(np.ones((c, c), np.float32), 1))
    if reverse:
        ms = [a[::-1, ::-1] for a in ms]
        ws = [a[::-1, ::-1] for a in ws]
    return np.concatenate(ms, 0), np.stack(ws, 0)


def _hgrn2_kernel(heads, n_lvl, carry_row, final, reset_ref, *refs):
    if final:
        (q_ref, z_ref, i_ref, og_ref, oo_ref, loglb_ref, log1mlb_ref, onemlb_ref, ng_ref, m_ref, w_ref,
         out_ref, st_ref, e_ref) = refs
    else:
        (q_ref, z_ref, i_ref, loglb_ref, log1mlb_ref, onemlb_ref, m_ref, w_ref, out_ref, st_ref, e_ref) = refs
    n_streams, c = q_ref.shape[0], q_ref.shape[1]
    rows_e = m_ref.shape[0]
    step = pl.program_id(0)

    for s in range(n_streams):
        @pl.when(reset_ref[step * n_streams + s] == 1)
        def _():
            st_ref[s * heads:(s + 1) * heads] = jnp.zeros((heads, HEAD_DIM, HEAD_DIM), F32)

    for s in range(n_streams):
        z = z_ref[s]
        e = jnp.exp(-jnp.abs(z))
        r = 1.0 / (1.0 + e)
        log_sig = jnp.minimum(z, 0.0) - jnp.log(1.0 + e)
        sig_neg = jnp.where(z >= 0.0, e * r, r)
        a = loglb_ref[...]
        b = log1mlb_ref[...] + log_sig
        g = jnp.maximum(a, b) + jnp.log(1.0 + jnp.exp(-jnp.abs(a - b)))
        k = onemlb_ref[...] * sig_neg
        q = q_ref[s]
        qs = q * jax.nn.sigmoid(q)
        v = i_ref[s].astype(BF16)

        g_hi, g_lo = _split_bf16(g)
        m_all = m_ref[...]
        e0 = s * rows_e
        e_ref[e0:e0 + rows_e, :] = (jnp.dot(m_all, g_hi, preferred_element_type=F32)
                                    + jnp.dot(m_all, g_lo, preferred_element_type=F32))

        for h in range(heads):
            cols = slice(h * HEAD_DIM, (h + 1) * HEAD_DIM)
            qh, kh, vh = qs[:, cols], k[:, cols], v[:, cols]
            amat = w_ref[n_lvl] * lax.dot_general(qh.astype(BF16), kh.astype(BF16), NT_DIMS,
                                                  preferred_element_type=F32)
            for lvl in range(n_lvl):
                ex = jnp.exp(e_ref[e0 + lvl * c:e0 + (lvl + 1) * c, cols])
                amat = amat + w_ref[lvl] * lax.dot_general((qh * ex).astype(BF16), (kh * ex).astype(BF16),
                                                           NT_DIMS, preferred_element_type=F32)
            e_in = e_ref[e0 + n_lvl * c:e0 + (n_lvl + 1) * c, cols]
            q_in = (qh * jnp.exp(e_in)).astype(BF16)
            k_st = (kh * jnp.exp(e_ref[e0 + (n_lvl + 1) * c:e0 + (n_lvl + 2) * c, cols])).astype(BF16)
            dec = jnp.exp(e_in[carry_row:carry_row + 1, :])
            st = st_ref[s * heads + h]
            o = (lax.dot_general(q_in, st.astype(BF16), NT_DIMS, preferred_element_type=F32)
                 + jnp.dot(amat.astype(BF16), vh, preferred_element_type=F32))
            st_ref[s * heads + h] = st * dec + lax.dot_general(vh, k_st, TN_DIMS, preferred_element_type=F32)
            if final:
                o = o + oo_ref[s, :, cols]
                o = o * lax.rsqrt(jnp.mean(o * o, -1, keepdims=True) + RMS_EPS)
                og = og_ref[s, :, cols]
                out_ref[s, :, cols] = (o * ng_ref[:, cols] * (og * jax.nn.sigmoid(og))).astype(out_ref.dtype)
            else:
                out_ref[s, :, cols] = o


def _hgrn2_pass(p3, cols, hg_w, z_col, reverse, reset, lbs, extra, c):
    n_streams, n = p3.shape[0], p3.shape[1]
    nc = n // c
    heads = hg_w // HEAD_DIM
    n_lvl = int(math.log2(c))
    m_np, w_np = _hgrn2_constants(c, reverse)
    final = extra is not None

    def cmap(i):
        return (nc - 1 - i) if reverse else i

    def tok(col):
        return pl.BlockSpec((n_streams, c, hg_w), lambda i, r: (0, cmap(i), col))

    vec = pl.BlockSpec((1, hg_w), lambda i, r: (0, 0))
    in_specs = [tok(cols["q"]), tok(z_col), tok(cols["i"])]
    args = [p3, p3, p3]
    if final:
        o_other, norm_g = extra
        in_specs += [tok(cols["og"]), tok(0)]
        args += [p3, o_other]
    in_specs += [vec, vec, vec]
    args += list(lbs)
    if final:
        in_specs += [vec]
        args += [norm_g.reshape(1, hg_w).astype(F32)]
    in_specs += [pl.BlockSpec(m_np.shape, lambda i, r: (0, 0)), pl.BlockSpec(w_np.shape, lambda i, r: (0, 0, 0))]
    args += [jnp.asarray(m_np, BF16), jnp.asarray(w_np, F32)]
    grid_spec = pltpu.PrefetchScalarGridSpec(
        num_scalar_prefetch=1, grid=(nc,), in_specs=in_specs, out_specs=tok(0),
        scratch_shapes=[pltpu.VMEM((n_streams * heads, HEAD_DIM, HEAD_DIM), F32),
                        pltpu.VMEM((n_streams * m_np.shape[0], hg_w), F32)])
    order = np.arange(nc)[::-1] if reverse else np.arange(nc)
    reset_steps = jnp.asarray(np.asarray(reset, np.int32)[order].reshape(-1))
    return pl.pallas_call(
        functools.partial(_hgrn2_kernel, heads, n_lvl, 0 if reverse else c - 1, final), grid_spec=grid_spec,
        out_shape=jax.ShapeDtypeStruct((n_streams, n, hg_w), BF16 if final else F32),
        compiler_params=_params("arbitrary"), name="hgrn2_fwd" if final else "hgrn2_bwd",
    )(reset_steps, *args)


def _hgrn2(p, cols, hg_w, lb, norm_g, seq_lens, n_streams=2, c=HG_CHUNK):
    lb = lb.astype(F32).reshape(1, hg_w)
    lbs = (jnp.log(lb), jnp.log1p(-lb), 1.0 - lb)
    starts = np.cumsum([0] + list(seq_lens))
    n = int(starts[-1])
    per = n // n_streams
    assert n == p.shape[0] and all(s * per in starts for s in range(n_streams)), "a sequence straddles a stream cut"
    first = np.zeros(n // c, np.int32)
    last = np.zeros(n // c, np.int32)
    first[starts[:-1] // c] = 1
    last[starts[1:] // c - 1] = 1
    first = first.reshape(n_streams, per // c).T
    last = last.reshape(n_streams, per // c).T
    p3 = p.reshape(n_streams, per, p.shape[1])
    o_b = _hgrn2_pass(p3, cols, hg_w, cols["zb"], True, last, lbs, None, c)
    return _hgrn2_pass(p3, cols, hg_w, cols["zf"], False, first, lbs, (o_b, norm_g), c).reshape(n, hg_w)


def _dft_matrices(t):
    n = 2 * t
    f = jnp.arange(t, dtype=jnp.int32)
    ang = (2.0 * math.pi / n) * ((f[:, None] * f[None, :]) % n).astype(F32)
    cos, sin = jnp.cos(ang), jnp.sin(ang)
    alt = (1 - 2 * (f % 2)).astype(F32)
    first = f == 0
    fwd = jnp.stack([cos, jnp.where(first[:, None], alt[None, :], -sin)], 0).astype(BF16)
    wgt = jnp.where(first, 1.0, 2.0)[None, :] / n
    inv = jnp.stack([cos * wgt, jnp.where(first[None, :], alt[:, None] / n, -sin * wgt)], 0).astype(BF16)
    return fwd, inv


def _shortconv_kernel(flags_ref, x_ref, prev_ref, next_ref, w_ref, b_ref, o_ref):
    i = pl.program_id(0)
    x = x_ref[...]
    t = x.shape[0]
    rows = lax.broadcasted_iota(jnp.int32, x.shape, 0)
    has_prev = flags_ref[2 * i].astype(F32)
    has_next = flags_ref[2 * i + 1].astype(F32)
    up = jnp.where(rows == 0, prev_ref[7:8, :] * has_prev, pltpu.roll(x, 1, 0))
    dn = jnp.where(rows == t - 1, next_ref[0:1, :] * has_next, pltpu.roll(x, t - 1, 0))
    o_ref[...] = up * w_ref[0:1, :] + x * w_ref[1:2, :] + dn * w_ref[2:3, :] + b_ref[...]


def _shortconv(p, width, conv_w, conv_b, blk_flags, t=HY_BLOCK, ct=768):
    n = p.shape[0]
    nb = n // t
    r8 = t // 8
    grid_spec = pltpu.PrefetchScalarGridSpec(
        num_scalar_prefetch=1, grid=(nb, width // ct),
        in_specs=[pl.BlockSpec((t, ct), lambda i, j, f: (i, j)),
                  pl.BlockSpec((8, ct), lambda i, j, f: (jnp.maximum(i * r8 - 1, 0), j)),
                  pl.BlockSpec((8, ct), lambda i, j, f: (jnp.minimum((i + 1) * r8, nb * r8 - 1), j)),
                  pl.BlockSpec((3, ct), lambda i, j, f: (0, j)), pl.BlockSpec((1, ct), lambda i, j, f: (0, j))],
        out_specs=pl.BlockSpec((t, ct), lambda i, j, f: (i, j)))
    return pl.pallas_call(
        _shortconv_kernel, grid_spec=grid_spec, out_shape=jax.ShapeDtypeStruct((n, width), F32),
        compiler_params=_params("parallel", "parallel"), name="hyena_shortconv",
    )(blk_flags, p, p, p, conv_w.astype(F32), conv_b.reshape(1, width).astype(F32))


def _filter_kernel(l_total, feats_ref, w1_ref, b1_ref, w2_ref, b2_ref, w3_ref, b3_ref, fr_ref, wo_ref, dl_ref,
                   h_ref, sum_ref):
    del l_total
    i = pl.program_id(0)
    bt = feats_ref.shape[0]
    half_lanes = LANES // 2
    fr = fr_ref[...]
    feats = feats_ref[...]
    h = jnp.sin(fr * (_dot3(feats, w1_ref[...]) + b1_ref[...]))
    h = jnp.sin(fr * (_dot3(h, w2_ref[...]) + b2_ref[...]))
    h = jnp.sin(fr * (_dot3(h, w3_ref[...]) + b3_ref[...]))
    reps = wo_ref.shape[2] // dl_ref.shape[1]
    row = lax.broadcasted_iota(jnp.int32, (bt, 1), 0) + i * bt
    total = jnp.zeros((1, wo_ref.shape[2]), F32)
    for side in range(2):
        window = jnp.exp(-feats[:, side * half_lanes:side * half_lanes + 1] * dl_ref[...])
        out = _dot3(h, wo_ref[side]) * jnp.concatenate([window] * reps, axis=1)
        if side == 1:
            out = jnp.where(row == 0, 0.0, out)
        h_ref[side] = out
        total = total + jnp.sum(jnp.abs(out), axis=0, keepdims=True)

    @pl.when(i == 0)
    def _():
        sum_ref[...] = jnp.zeros_like(sum_ref)

    sum_ref[...] += total


def _hyena_filter(l, w1, b1, w2, b2, w3, b3, freq, w_out, hy_w, bt=512):
    hid = w2.shape[0]
    half_lanes = LANES // 2
    assert hid <= half_lanes and w1.shape[0] <= half_lanes
    half = HY_ORDER * hy_w
    bands = jnp.linspace(1e-4, HY_POS_BANDS - 1, HY_POS_BANDS, dtype=F32)

    def features(pos):
        tt = pos / max(l - 1, 1)
        ang = (2.0 * math.pi / l) * pos[:, None] * bands[None, :]
        f = jnp.concatenate([tt[:, None], jnp.cos(ang), -jnp.sin(ang)], -1)
        return jnp.pad(f, ((0, 0), (0, half_lanes - f.shape[1])))

    def twice_diag(w):
        wp = jnp.pad(w.astype(F32), ((0, half_lanes - w.shape[0]), (0, half_lanes - w.shape[1])))
        z = jnp.zeros_like(wp)
        return jnp.concatenate([jnp.concatenate([wp, z], 1), jnp.concatenate([z, wp], 1)], 0)

    def twice_vec(a):
        ap = jnp.pad(a.astype(F32), (0, half_lanes - a.shape[0]))
        return jnp.concatenate([ap, ap]).reshape(1, LANES)

    pos = jnp.arange(l, dtype=F32)
    feats = jnp.concatenate([features(pos), features(l - pos)], 1)
    wo = w_out.astype(F32).reshape(hid, HY_ORDER, 2, hy_w).transpose(2, 0, 1, 3).reshape(2, hid, half)
    wo = jnp.pad(wo, ((0, 0), (0, half_lanes - hid), (0, 0)))
    zero = jnp.zeros_like(wo[0])
    wop = jnp.stack([jnp.concatenate([wo[0], zero], 0), jnp.concatenate([zero, wo[1]], 0)], 0)
    deltas = jnp.abs(jnp.linspace(math.log(HY_DECAY_TARGET) / HY_SLOW_DECAY,
                                  math.log(HY_DECAY_TARGET) / HY_FAST_DECAY, hy_w, dtype=F32)).reshape(1, hy_w)
    bt = min(bt, l)
    sq = pl.BlockSpec((LANES, LANES), lambda i: (0, 0))
    vec = pl.BlockSpec((1, LANES), lambda i: (0, 0))
    taps, abs_sum = pl.pallas_call(
        functools.partial(_filter_kernel, l), grid=(l // bt,),
        in_specs=[pl.BlockSpec((bt, LANES), lambda i: (i, 0)), sq, vec, sq, vec, sq, vec, vec,
                  pl.BlockSpec((2, LANES, half), lambda i: (0, 0, 0)),
                  pl.BlockSpec((1, hy_w), lambda i: (0, 0))],
        out_specs=[pl.BlockSpec((2, bt, half), lambda i: (0, i, 0)), pl.BlockSpec((1, half), lambda i: (0, 0))],
        out_shape=[jax.ShapeDtypeStruct((2, l, half), F32), jax.ShapeDtypeStruct((1, half), F32)],
        compiler_params=_params("arbitrary"), name="hyena_filter",
    )(feats, twice_diag(w1), twice_vec(b1), twice_diag(w2), twice_vec(b2), twice_diag(w3), twice_vec(b3),
      twice_vec(freq), wop, deltas)
    return taps.reshape(2 * l, half), abs_sum


def _dft_cols(t, width):
    return width if t <= 1024 else 256


def _dft_kernel(f_ref, x_ref, o_ref):
    x = x_ref[...].astype(BF16)
    o_ref[0] = jnp.dot(f_ref[0], x, preferred_element_type=F32).astype(o_ref.dtype)
    o_ref[1] = jnp.dot(f_ref[1], x, preferred_element_type=F32).astype(o_ref.dtype)


def _block_dft(fwd, x, col_off, width, out_dtype, t=HY_BLOCK, ct=256):
    nseg = x.shape[0] // t
    ct = min(ct, width)
    return pl.pallas_call(
        _dft_kernel, grid=(nseg, width // ct),
        in_specs=[pl.BlockSpec((2, t, t), lambda s, j: (0, 0, 0)),
                  pl.BlockSpec((t, ct), lambda s, j: (s, j + col_off))],
        out_specs=pl.BlockSpec((None, 2, t, ct), lambda s, j: (s, 0, 0, j)),
        out_shape=jax.ShapeDtypeStruct((nseg, 2, t, width), out_dtype),
        compiler_params=_params("parallel", "parallel"), name="hyena_block_dft",
    )(fwd, x)


def _segment_index(d, nb):
    return d if d >= 0 else 2 * nb + d


def _hyena_filter_spectra(l, fwd, filt_w, hy_w, t=HY_BLOCK):
    nb = l // t
    taps, abs_sum = _hyena_filter(l, *filt_w, hy_w)
    half = taps.shape[1]
    seg_spec = _block_dft(fwd, taps, 0, half, F32, t, _dft_cols(t, hy_w))
    first_rows = taps.reshape(2 * nb, t, half)[:, 0, :]
    seg0 = jnp.stack([first_rows[_segment_index(d - 1, nb)] for d in range(-(nb - 1), nb)], 0)
    return seg_spec, seg0.reshape(2 * nb - 1, 1, half), 1.0 / abs_sum


MIX_ROWS = 8


def _mix_kernel(nb, z_ref, a_ref, s0_ref, inv_ref, *refs):
    o_ref, k_ref = refs[-2], refs[-1]
    ft, ct = z_ref.shape[2], z_ref.shape[3]
    first_tile = pl.program_id(1) == 0

    tile_rows = lax.broadcasted_iota(jnp.int32, (ft, ct), 0)
    sgn = (1 - 2 * (tile_rows & 1)).astype(F32)
    real_row = jnp.logical_and(tile_rows == 0, first_tile)
    inv = inv_ref[...]
    for slot in range(2 * nb - 1):
        d = slot - (nb - 1)
        ia, ib = _segment_index(d, nb), _segment_index(d - 1, nb)
        s0 = s0_ref[slot]
        k_ref[slot, 0] = (a_ref[ia, 0] + sgn * (a_ref[ib, 0] - s0)) * inv
        k_ref[slot, 1] = (a_ref[ia, 1] + sgn * (a_ref[ib, 1] - jnp.where(real_row, s0, 0.0))) * inv

    n_out = 2 if nb % 2 == 0 else 1

    def out_rows(i, r0, masked):
        rows = pl.ds(r0, MIX_ROWS)
        acc = [[jnp.zeros((MIX_ROWS, ct), F32) for _ in range(2)] for _ in range(n_out)]
        if masked:
            row0 = jnp.logical_and(lax.broadcasted_iota(jnp.int32, (MIX_ROWS, ct), 0) == 0, first_tile)
        d_hi = i + (n_out - 1) + (nb - 1)
        k_prev = (k_ref[d_hi, 0, rows, :], k_ref[d_hi, 1, rows, :]) if n_out == 2 else None
        for j in range(nb):
            d = i - j + (nb - 1)
            zt, zb = z_ref[j, 0, rows, :], z_ref[j, 1, rows, :]
            k_cur = (k_ref[d, 0, rows, :], k_ref[d, 1, rows, :])
            for o, (kt, kb) in enumerate((k_cur, k_prev)[:n_out]):
                bb = zb * kb
                if masked:
                    acc[o][0] = acc[o][0] + (zt * kt - jnp.where(row0, 0.0, bb))
                    acc[o][1] = acc[o][1] + jnp.where(row0, bb, zt * kb + zb * kt)
                else:
                    acc[o][0] = acc[o][0] + (zt * kt - bb)
                    acc[o][1] = acc[o][1] + (zt * kb + zb * kt)
            k_prev = k_cur
        for o in range(n_out):
            o_ref[i + o, 0, rows, :] = acc[o][0].astype(o_ref.dtype)
            o_ref[i + o, 1, rows, :] = acc[o][1].astype(o_ref.dtype)

    def all_outputs(r0, masked):
        def body(ip, carry):
            out_rows(ip * n_out, r0, masked)
            return carry
        lax.fori_loop(0, nb // n_out, body, 0)

    all_outputs(0, True)

    def chunk(r, carry):
        all_outputs(pl.multiple_of(r * MIX_ROWS, MIX_ROWS), False)
        return carry

    lax.fori_loop(1, ft // MIX_ROWS, chunk, 0)


def _hyena_mix(zspec, kspec, k_col, n_seq, nb, blk_off, ct=256):
    seg_spec, seg0, inv_norm = kspec
    _, _, t, width = zspec.shape
    ft = max(MIX_ROWS, min(t, 2048 // nb))
    nct = width // ct
    assert blk_off % nb == 0 and t % ft == 0
    s_off = blk_off // nb
    return pl.pallas_call(
        functools.partial(_mix_kernel, nb), grid=(n_seq, t // ft, nct),
        in_specs=[pl.BlockSpec((nb, 2, ft, ct), lambda s, fi, j: (s + s_off, 0, fi, j)),
                  pl.BlockSpec((2 * nb, 2, ft, ct), lambda s, fi, j: (0, 0, fi, j + k_col * nct)),
                  pl.BlockSpec((2 * nb - 1, 1, ct), lambda s, fi, j: (0, 0, j + k_col * nct)),
                  pl.BlockSpec((1, ct), lambda s, fi, j: (0, j + k_col * nct))],
        out_specs=pl.BlockSpec((nb, 2, ft, ct), lambda s, fi, j: (s, 0, fi, j)),
        out_shape=jax.ShapeDtypeStruct((n_seq * nb, 2, t, width), BF16),
        scratch_shapes=[pltpu.VMEM((2 * nb - 1, 2, ft, ct), F32)],
        compiler_params=_params("parallel", "parallel", "parallel"), name="hyena_mix",
    )(zspec, seg_spec, seg0, inv_norm)


def _inverse_kernel(bounds, *refs):
    n_groups = len(bounds)
    y_refs = refs[:n_groups]
    fi_ref, zin_ref, gate_ref, bias_ref, o_ref = refs[n_groups:]
    i = pl.program_id(0)
    for (lo, hi), y_ref in zip(bounds, y_refs):
        @pl.when(jnp.logical_and(i >= lo, i < hi))
        def _():
            y = (jnp.dot(fi_ref[0], y_ref[0], preferred_element_type=F32)
                 + jnp.dot(fi_ref[1], y_ref[1], preferred_element_type=F32))
            o_ref[...] = (gate_ref[...] * (y + zin_ref[...] * bias_ref[...])).astype(o_ref.dtype)


def _hyena_inverse(yspecs, inv, zin, zin_col, gate, gate_col, bias, out_dtype, ct=256):
    _, _, t, width = yspecs[0].shape
    nct = width // ct
    bounds, lo = [], 0
    for y in yspecs:
        bounds.append((lo, lo + y.shape[0]))
        lo += y.shape[0]

    def group_spec(lo, hi):
        return pl.BlockSpec((None, 2, t, ct), lambda i, j: (jnp.clip(i - lo, 0, hi - lo - 1), 0, 0, j))

    return pl.pallas_call(
        functools.partial(_inverse_kernel, tuple(bounds)), grid=(lo, nct),
        in_specs=[group_spec(*b) for b in bounds] + [
            pl.BlockSpec((2, t, t), lambda i, j: (0, 0, 0), pipeline_mode=pl.Buffered(1)),
            pl.BlockSpec((t, ct), lambda i, j: (i, j + zin_col * nct)),
            pl.BlockSpec((t, ct), lambda i, j: (i, j + gate_col * nct)),
            pl.BlockSpec((1, ct), lambda i, j: (0, j))],
        out_specs=pl.BlockSpec((t, ct), lambda i, j: (i, j)),
        out_shape=jax.ShapeDtypeStruct((lo * t, width), out_dtype),
        compiler_params=_params("arbitrary", "arbitrary"), name="hyena_inverse",
    )(*yspecs, inv, zin, gate, bias)


def _hyena(p, hy_w, groups, conv_w, conv_b, filt_w, bias, t=HY_BLOCK):
    flags = []
    for n_seq, l in groups:
        nb = l // t
        flags += [int(bi > 0) if side == 0 else int(bi < nb - 1)
                  for _ in range(n_seq) for bi in range(nb) for side in range(2)]
    fwd, inv = _dft_matrices(t)
    u = _shortconv(p, 3 * hy_w, conv_w, conv_b, jnp.asarray(np.asarray(flags, np.int32)), t, ct=hy_w)
    kspecs = [_hyena_filter_spectra(l, fwd, filt_w, hy_w, t) for _, l in groups]
    bias = bias.astype(F32)
    ct = _dft_cols(t, hy_w)
    z, z_col = u, 2
    for o in range(HY_ORDER):
        zspec = _block_dft(fwd, z, z_col * (hy_w // ct), hy_w, F32, t, ct)
        yspecs, blk = [], 0
        for (n_seq, l), kspec in zip(groups, kspecs):
            yspecs.append(_hyena_mix(zspec, kspec, o, n_seq, l // t, blk))
            blk += n_seq * (l // t)
        last = o == HY_ORDER - 1
        z = _hyena_inverse(yspecs, inv, z, z_col, u, o, bias[o:o + 1], BF16 if last else F32, ct)
        z_col = 0
    return z


def kernel(x_prompt, x_sample, ln_in_g, ln_in_b, w_in, hy_conv_w, hy_conv_b, hy_pos_w1, hy_pos_b1, hy_pos_w2, hy_pos_b2, hy_pos_w3, hy_pos_b3, hy_sin_freq, hy_pos_wout, hy_bias, gm_ln_g, gm_ln_b, gm_ws, gm_bs, hg_lb_raw, hg_norm_g, w_out, ln1_g, ln1_b, ln2_g, ln2_b, ffn_w1, ffn_w3, ffn_w2, moe_router_w, moe_router_b, moe_w1, moe_w3, moe_w2):
    depth, d_model, in_w = w_in.shape
    hy_w = hy_bias.shape[-1]
    gm_w = gm_ln_g.shape[-1]
    hg_w = hg_norm_g.shape[-1]
    o1 = 3 * hy_w
    o2 = o1 + 2 * gm_w
    alpha = (2 * depth) ** 0.25
    groups = [(x_prompt.shape[0], x_prompt.shape[1]), (x_sample.shape[0], x_sample.shape[1])]
    seq_lens = [l for n_seq, l in groups for _ in range(n_seq)]
    n_prompt = x_prompt.shape[0] * x_prompt.shape[1]
    hg_off = o1 // hg_w
    hg_cols = {"q": hg_off, "zf": hg_off + 1, "zb": hg_off + 2, "i": hg_off + 3, "og": hg_off + 4}
    gm_off = (o1 + 5 * hg_w) // gm_w

    lb_all = jnp.cumsum(jax.nn.softmax(hg_lb_raw.astype(F32), axis=0), axis=0)
    lb_all = lb_all - lb_all[:1]

    x, xb = _layernorm_pair(x_prompt.reshape(-1, d_model), x_sample.reshape(-1, d_model), ln_in_g, ln_in_b)
    for l in range(depth):
        w_l = jnp.concatenate([w_in[l][:, :o1], w_in[l][:, o2:], w_in[l][:, o1:o2]], 1).astype(BF16)
        p = _matmul(xb, w_l)
        filt_w = (hy_pos_w1[l], hy_pos_b1[l], hy_pos_w2[l], hy_pos_b2[l], hy_pos_w3[l], hy_pos_b3[l],
                  hy_sin_freq[l], hy_pos_wout[l])
        y_hy = _hyena(p, hy_w, groups, hy_conv_w[l], hy_conv_b[l], filt_w, hy_bias[l])
        y_hg = _hgrn2(p, hg_cols, hg_w, lb_all[l], hg_norm_g[l], seq_lens)
        y_gm = _gmlp(p, gm_off, gm_off + 1, gm_w, gm_ln_g[l], gm_ln_b[l], gm_ws[l], gm_bs[l])
        wo = w_out[l].astype(BF16)
        parts = [y_hy, y_gm, y_hg]
        wo_parts = [wo[:hy_w], wo[hy_w:hy_w + gm_w], wo[hy_w + gm_w:]]
        j = l // 2
        if l % 2 == 0:
            x, xb = _mix_out_residual_ln(parts, wo_parts, x, ln1_g[l], ln1_b[l], alpha)
            x, xb = _ffn_residual_ln(xb, ffn_w1[j].astype(BF16), ffn_w3[j].astype(BF16), ffn_w2[j].astype(BF16),
                                     x, ln2_g[l], ln2_b[l], alpha)
        else:
            n_exp = moe_router_w.shape[-1]
            wr = jnp.pad(moe_router_w[j].astype(BF16), ((0, 0), (0, LANES - n_exp)))
            x, xw, logits = _mix_out_residual_ln(parts, wo_parts, x, ln1_g[l], ln1_b[l], alpha, wr)
            logits = logits[:, :n_exp] + moe_router_b[j].astype(F32)
            y0, y1, gate = _moe_dispatch_ffn(logits, xw, moe_w1[j], moe_w3[j], moe_w2[j])
            xa, xs = _combine_residual_ln_split(x, y0, y1, gate, ln2_g[l], ln2_b[l], alpha, n_prompt)
            if l == depth - 1:
                return (xa.reshape(x_prompt.shape), xs.reshape(x_sample.shape))
            x = jnp.concatenate([xa, xs], 0)
            xb = x.astype(BF16)
    return (x[:n_prompt].reshape(x_prompt.shape), x[n_prompt:].reshape(x_sample.shape))
```

```python
import functools
import math

import numpy as np
import jax
import jax.numpy as jnp
from jax import lax
from jax.experimental import pallas as pl
from jax.experimental.pallas import tpu as pltpu
from jax.experimental.pallas import tpu_sc as plsc

HEAD_DIM = 128
HY_ORDER = 2
HY_POS_BANDS = 16
HY_FAST_DECAY = 0.3
HY_SLOW_DECAY = 1.5
HY_DECAY_TARGET = 1e-2
GM_CHUNK = 128
N_EXPERTS = 8
TOP_K = 2
LN_EPS = 1e-5
RMS_EPS = 1e-6

HY_BLOCK = 1024
HG_CHUNK = 128
LANES = 128

V7X_VMEM_BYTES = 64 * 1024 * 1024
VMEM_LIMIT = V7X_VMEM_BYTES - 8 * 1024 * 1024

F32 = jnp.float32
BF16 = jnp.bfloat16
NT_DIMS = (((1,), (1,)), ((), ()))
TN_DIMS = (((0,), (0,)), ((), ()))


def _params(*sem):
    return pltpu.CompilerParams(dimension_semantics=sem, vmem_limit_bytes=VMEM_LIMIT)


def _ln_rows(x, g, b):
    mu = jnp.mean(x, -1, keepdims=True)
    xc = x - mu
    var = jnp.mean(xc * xc, -1, keepdims=True)
    return xc * lax.rsqrt(var + LN_EPS) * g + b


def _split_bf16(x):
    hi = x.astype(BF16)
    lo = (x - hi.astype(F32)).astype(BF16)
    return hi, lo


def _dot3(a, b):
    ah, al = _split_bf16(a)
    bh, bl = _split_bf16(b)
    return (jnp.dot(ah, bh, preferred_element_type=F32) + jnp.dot(ah, bl, preferred_element_type=F32)
            + jnp.dot(al, bh, preferred_element_type=F32))


def _pack_bf16_pairs(y):
    w = y.shape[1] // 2
    bits = lax.bitcast_convert_type(y.astype(BF16).astype(F32), jnp.uint32)
    word = (bits[:, :w] >> 16) | (bits[:, w:] & jnp.uint32(0xFFFF0000))
    return lax.bitcast_convert_type(word, jnp.int32)


def _unpack_bf16_pairs(word):
    u = lax.bitcast_convert_type(word, jnp.uint32)
    lo = lax.bitcast_convert_type(u << 16, F32)
    hi = lax.bitcast_convert_type(u & jnp.uint32(0xFFFF0000), F32)
    return jnp.concatenate([lo, hi], axis=1)


SC_GATHER_WINDOW = 128
SC_GATHER_WORDS = 256


def _gather_rows(xw, idx):
    n, words = xw.shape
    m = idx.shape[0]
    parts = words // SC_GATHER_WORDS
    assert words % SC_GATHER_WORDS == 0 and m % (2 * SC_GATHER_WINDOW) == 0
    half = m // SC_GATHER_WINDOW // 2
    mesh = plsc.VectorSubcoreMesh(core_axis_name="core", subcore_axis_name="subcore")

    @pl.kernel(out_type=jax.ShapeDtypeStruct((m, words), xw.dtype), mesh=mesh)
    def gather(x_hbm, i_hbm, o_hbm):
        def body(i_vmem, o_vmem):
            piece = pl.program_id(2)
            pltpu.sync_copy(x_hbm.at[i_vmem.at[0], pl.ds(piece * SC_GATHER_WORDS, SC_GATHER_WORDS)], o_vmem)

        pltpu.emit_pipeline(
            body, grid=(2, half, parts),
            in_specs=[pl.BlockSpec((1, SC_GATHER_WINDOW), index_map=lambda c, i, p: (0, c * half + i))],
            out_specs=[pl.BlockSpec((SC_GATHER_WINDOW, SC_GATHER_WORDS),
                                    index_map=lambda c, i, p: (c * half + i, p))],
            core_axis_name=("core", "subcore"),
            dimension_semantics=(pltpu.PARALLEL, pltpu.PARALLEL, pltpu.ARBITRARY),
        )(i_hbm, o_hbm)

    return gather(xw, idx.astype(jnp.int32).reshape(1, m))


def _ln2_kernel(na_blocks, xa_ref, xb_ref, g_ref, b_ref, of_ref, ob_ref):
    def emit(x_ref):
        y = _ln_rows(x_ref[...], g_ref[...], b_ref[...])
        of_ref[...] = y
        ob_ref[...] = y.astype(BF16)

    first = pl.program_id(0) < na_blocks
    pl.when(first)(lambda: emit(xa_ref))
    pl.when(jnp.logical_not(first))(lambda: emit(xb_ref))


def _layernorm_pair(xa, xb, g, b, bm=512):
    d = xa.shape[1]
    na, nb = xa.shape[0] // bm, xb.shape[0] // bm
    row = pl.BlockSpec((bm, d), lambda i: (i, 0))
    vec = pl.BlockSpec((1, d), lambda i: (0, 0))
    m = (na + nb) * bm
    return pl.pallas_call(
        functools.partial(_ln2_kernel, na), grid=(na + nb,),
        in_specs=[pl.BlockSpec((bm, d), lambda i: (jnp.minimum(i, na - 1), 0)),
                  pl.BlockSpec((bm, d), lambda i: (jnp.maximum(i - na, 0), 0)), vec, vec],
        out_specs=[row, row],
        out_shape=[jax.ShapeDtypeStruct((m, d), F32), jax.ShapeDtypeStruct((m, d), BF16)],
        compiler_params=_params("arbitrary"), name="layernorm",
    )(xa, xb, g.reshape(1, d), b.reshape(1, d))


def _combine_ln_kernel(alpha, na_blocks, x_ref, y0_ref, y1_ref, gate_ref, g_ref, b_ref, oa_ref, ob_ref):
    gate = gate_ref[...]
    ff = _unpack_bf16_pairs(y0_ref[...]) * gate[:, 0:1] + _unpack_bf16_pairs(y1_ref[...]) * gate[:, 1:2]
    y = _ln_rows(alpha * x_ref[...] + ff, g_ref[...], b_ref[...])
    first = pl.program_id(0) < na_blocks

    @pl.when(first)
    def _():
        oa_ref[...] = y

    @pl.when(jnp.logical_not(first))
    def _():
        ob_ref[...] = y


def _combine_residual_ln_split(x, y0, y1, gate, g, b, alpha, n_first, bm=512):
    m, d = x.shape
    na = n_first // bm
    row = pl.BlockSpec((bm, d), lambda i: (i, 0))
    packed = pl.BlockSpec((bm, d // 2), lambda i: (i, 0))
    vec = pl.BlockSpec((1, d), lambda i: (0, 0))
    return pl.pallas_call(
        functools.partial(_combine_ln_kernel, alpha, na), grid=(m // bm,),
        in_specs=[row, packed, packed, pl.BlockSpec((bm, gate.shape[1]), lambda i: (i, 0)), vec, vec],
        out_specs=[pl.BlockSpec((bm, d), lambda i: (jnp.minimum(i, na - 1), 0)),
                   pl.BlockSpec((bm, d), lambda i: (jnp.maximum(i - na, 0), 0))],
        out_shape=[jax.ShapeDtypeStruct((n_first, d), F32), jax.ShapeDtypeStruct((m - n_first, d), F32)],
        compiler_params=_params("arbitrary"), name="moe_combine_residual_ln",
    )(x, y0, y1, gate, g.reshape(1, d), b.reshape(1, d))


def _mm_kernel(a_ref, w_ref, o_ref):
    o_ref[...] = jnp.dot(a_ref[...], w_ref[...], preferred_element_type=F32).astype(o_ref.dtype)


def _matmul(a, w, out_dtype=F32, bm=1024, bn=1024):
    m, k = a.shape
    n = w.shape[1]
    bm, bn = min(bm, m), min(bn, n)
    return pl.pallas_call(
        _mm_kernel, grid=(m // bm, n // bn),
        in_specs=[pl.BlockSpec((bm, k), lambda i, j: (i, 0)), pl.BlockSpec((k, bn), lambda i, j: (0, j))],
        out_specs=pl.BlockSpec((bm, bn), lambda i, j: (i, j)),
        out_shape=jax.ShapeDtypeStruct((m, n), out_dtype),
        compiler_params=_params("parallel", "arbitrary"), name="matmul",
    )(a, w)


def _mix_out_kernel(alpha, routed, a0_ref, a1_ref, a2_ref, w0_ref, w1_ref, w2_ref, x_ref, g_ref, b_ref, *refs):
    mix = (jnp.dot(a0_ref[...], w0_ref[...], preferred_element_type=F32)
           + jnp.dot(a1_ref[...], w1_ref[...], preferred_element_type=F32)
           + jnp.dot(a2_ref[...], w2_ref[...], preferred_element_type=F32))
    y = _ln_rows(alpha * x_ref[...] + mix, g_ref[...], b_ref[...])
    if routed:
        wr_ref, of_ref, ob_ref, lg_ref = refs
        y_hi, y_lo = _split_bf16(y)
        lg_ref[...] = (jnp.dot(y_hi, wr_ref[...], preferred_element_type=F32)
                       + jnp.dot(y_lo, wr_ref[...], preferred_element_type=F32))
        ob_ref[...] = _pack_bf16_pairs(y)
    else:
        of_ref, ob_ref = refs
        ob_ref[...] = y.astype(BF16)
    of_ref[...] = y


def _mix_out_residual_ln(parts, weights, x, g, b, alpha, router_w=None, bm=512):
    m, d = x.shape
    row = pl.BlockSpec((bm, d), lambda i: (i, 0))
    vec = pl.BlockSpec((1, d), lambda i: (0, 0))
    in_specs = [pl.BlockSpec((bm, a.shape[1]), lambda i: (i, 0)) for a in parts]
    in_specs += [pl.BlockSpec(w.shape, lambda i: (0, 0)) for w in weights] + [row, vec, vec]
    args = [*parts, *weights, x, g.reshape(1, d), b.reshape(1, d)]
    out_specs = [row, row]
    out_shape = [jax.ShapeDtypeStruct((m, d), F32), jax.ShapeDtypeStruct((m, d), BF16)]
    if router_w is not None:
        in_specs.append(pl.BlockSpec(router_w.shape, lambda i: (0, 0)))
        args.append(router_w)
        out_specs[1] = pl.BlockSpec((bm, d // 2), lambda i: (i, 0))
        out_shape[1] = jax.ShapeDtypeStruct((m, d // 2), jnp.int32)
        out_specs.append(pl.BlockSpec((bm, router_w.shape[1]), lambda i: (i, 0)))
        out_shape.append(jax.ShapeDtypeStruct((m, router_w.shape[1]), F32))
    return pl.pallas_call(
        functools.partial(_mix_out_kernel, alpha, router_w is not None), grid=(m // bm,),
        in_specs=in_specs, out_specs=out_specs, out_shape=out_shape,
        compiler_params=_params("parallel"), name="out_proj_residual_ln",
    )(*args)


def _swiglu_acc(a_ref, w1_ref, w3_ref, w2_ref, acc_ref):
    f = pl.program_id(1)

    @pl.when(f == 0)
    def _():
        acc_ref[...] = jnp.zeros_like(acc_ref)

    a = a_ref[...]
    h1 = jnp.dot(a, w1_ref[...], preferred_element_type=F32)
    h3 = jnp.dot(a, w3_ref[...], preferred_element_type=F32)
    gated = (h1 * jax.nn.sigmoid(h1) * h3).astype(BF16)
    acc_ref[...] += jnp.dot(gated, w2_ref[...], preferred_element_type=F32)


def _ffn_kernel(alpha, a_ref, w1_ref, w3_ref, w2_ref, x_ref, g_ref, b_ref, of_ref, ob_ref):
    _swiglu_acc(a_ref, w1_ref, w3_ref, w2_ref, of_ref)

    @pl.when(pl.program_id(1) == pl.num_programs(1) - 1)
    def _():
        y = _ln_rows(alpha * x_ref[...] + of_ref[...], g_ref[...], b_ref[...])
        of_ref[...] = y
        ob_ref[...] = y.astype(BF16)


def _ffn_residual_ln(a, w1, w3, w2, x, g, b, alpha, bm=1024, bf=512):
    m, d = a.shape
    dff = w1.shape[1]
    bm = min(bm, m)
    row = pl.BlockSpec((bm, d), lambda i, f: (i, 0), pipeline_mode=pl.Buffered(1))
    vec = pl.BlockSpec((1, d), lambda i, f: (0, 0))
    up = pl.BlockSpec((d, bf), lambda i, f: (0, f))
    return pl.pallas_call(
        functools.partial(_ffn_kernel, alpha), grid=(m // bm, dff // bf),
        in_specs=[row, up, up, pl.BlockSpec((bf, d), lambda i, f: (f, 0)), row, vec, vec],
        out_specs=[row, row],
        out_shape=[jax.ShapeDtypeStruct((m, d), F32), jax.ShapeDtypeStruct((m, d), BF16)],
        compiler_params=_params("parallel", "arbitrary"), name="ffn_residual_ln",
    )(a, w1, w3, w2, x, g.reshape(1, d), b.reshape(1, d))


def _moe_ffn_kernel(blk_e_ref, used_ref, a_ref, w1_ref, w3_ref, w2_ref, o_ref, acc_ref):
    del blk_e_ref
    f = pl.program_id(1)

    @pl.when(pl.program_id(0) < used_ref[0])
    def _():
        @pl.when(f == 0)
        def _():
            acc_ref[...] = jnp.zeros_like(acc_ref)

        a = _unpack_bf16_pairs(a_ref[...]).astype(BF16)
        h1 = jnp.dot(a, w1_ref[...].astype(BF16), preferred_element_type=F32)
        h3 = jnp.dot(a, w3_ref[...].astype(BF16), preferred_element_type=F32)
        gated = (h1 * jax.nn.sigmoid(h1) * h3).astype(BF16)
        acc_ref[...] += jnp.dot(gated, w2_ref[...], preferred_element_type=F32)

        @pl.when(f == pl.num_programs(1) - 1)
        def _():
            o_ref[...] = _pack_bf16_pairs(acc_ref[...])


def _cast_kernel(x_ref, o_ref):
    o_ref[...] = x_ref[...].astype(o_ref.dtype)


def _cast_bf16(w, rows=128):
    e, r, c = w.shape
    rows = min(rows, r)
    blk = pl.BlockSpec((None, rows, c), lambda i, j: (i, j, 0))
    return pl.pallas_call(
        _cast_kernel, grid=(e, r // rows), in_specs=[blk], out_specs=blk,
        out_shape=jax.ShapeDtypeStruct(w.shape, BF16),
        compiler_params=_params("parallel", "parallel"), name="cast_bf16",
    )(w)


def _moe_grouped_ffn(blk_e, n_used, xw, w1, w3, w2, bm, bf=512):
    cap, words = xw.shape
    d = 2 * words
    dff = w1.shape[2]
    nf = dff // bf

    def blk(i, u):
        return jnp.minimum(i, u[0] - 1)

    def ftile(i, f, u):
        return jnp.where(i < u[0], f, nf - 1)

    row = pl.BlockSpec((bm, words), lambda i, f, e, u: (blk(i, u), 0))
    up = pl.BlockSpec((None, d, bf), lambda i, f, e, u: (e[blk(i, u)], 0, ftile(i, f, u)))
    down = pl.BlockSpec((None, bf, d), lambda i, f, e, u: (e[blk(i, u)], ftile(i, f, u), 0))
    grid_spec = pltpu.PrefetchScalarGridSpec(
        num_scalar_prefetch=2, grid=(cap // bm, nf), in_specs=[row, up, up, down],
        out_specs=row, scratch_shapes=[pltpu.VMEM((bm, d), F32)])
    return pl.pallas_call(
        _moe_ffn_kernel, grid_spec=grid_spec, out_shape=jax.ShapeDtypeStruct((cap, words), jnp.int32),
        compiler_params=_params("arbitrary", "arbitrary"), name="moe_grouped_ffn",
    )(blk_e, n_used, xw, w1, w3, w2)


def _moe_route(logits, bm):
    n = logits.shape[0]
    top_val, top_idx = lax.top_k(logits, TOP_K)
    gate = jax.nn.softmax(top_val, axis=-1)
    nk = n * TOP_K
    assert nk % bm == 0
    experts = jnp.arange(N_EXPERTS, dtype=jnp.int32)
    flat_e = top_idx.reshape(nk).astype(jnp.int32)
    one_hot = flat_e[:, None] == experts[None, :]
    order = jnp.argsort(flat_e, stable=True).astype(jnp.int32)
    rank = jnp.argsort(order).astype(jnp.int32)
    counts = jnp.sum(one_hot.astype(jnp.int32), 0)
    padded = (counts + bm - 1) // bm * bm
    start = jnp.cumsum(counts) - counts
    pend = jnp.cumsum(padded)
    shift = jnp.sum(jnp.where(one_hot, (pend - padded - start)[None, :], 0), 1)
    dest = (rank + shift).reshape(n, TOP_K)
    n_blocks = nk // bm + N_EXPERTS
    blk_e = jnp.minimum(jnp.searchsorted(pend, jnp.arange(n_blocks, dtype=jnp.int32) * bm, side='right'),
                        N_EXPERTS - 1).astype(jnp.int32)
    pad_rank = jnp.arange(bm, dtype=jnp.int32)[None, :]
    pad_key = jnp.where(pad_rank < (padded - counts)[:, None], 2 * experts[:, None] + 1, 2 * N_EXPERTS)
    keys = jnp.concatenate([2 * flat_e, pad_key.reshape(-1)])
    toks = jnp.concatenate([jnp.arange(nk, dtype=jnp.int32) // TOP_K, jnp.zeros((N_EXPERTS * bm,), jnp.int32)])
    _, slot_tok = lax.sort((keys, toks), num_keys=1, is_stable=True)
    n_used = (pend[-1] // bm).astype(jnp.int32).reshape(1)
    return gate, dest, slot_tok, blk_e, n_used


def _moe_dispatch_ffn(logits, xw, w1, w3, w2, bm=1024):
    gate, dest, slot_tok, blk_e, n_used = _moe_route(logits, bm)
    yw = _moe_grouped_ffn(blk_e, n_used, _gather_rows(xw, slot_tok), w1, w3, w2, bm)
    return _gather_rows(yw, dest[:, 0]), _gather_rows(yw, dest[:, 1]), gate


def _gmlp_kernel(groups, pu_ref, pv_ref, g_ref, b_ref, ws_ref, bs_ref, o_ref):
    u = jax.nn.gelu(pu_ref[...])
    v = _ln_rows(jax.nn.gelu(pv_ref[...]), g_ref[...], b_ref[...])
    for n in range(u.shape[0] // GM_CHUNK):
        rows = slice(n * GM_CHUNK, (n + 1) * GM_CHUNK)
        for grp in range(groups):
            cols = slice(grp * HEAD_DIM, (grp + 1) * HEAD_DIM)
            s = _dot3(ws_ref[grp], v[rows, cols]) + bs_ref[grp]
            o_ref[rows, cols] = (u[rows, cols] * s).astype(o_ref.dtype)


def _gmlp(p, col_u, col_v, gm_w, ln_g, ln_b, ws, bs, bt=512):
    n = p.shape[0]
    groups = ws.shape[0]
    bsb = jnp.broadcast_to(bs[:, :, None], (groups, GM_CHUNK, HEAD_DIM)).astype(F32)
    vec = pl.BlockSpec((1, gm_w), lambda i: (0, 0))
    full3 = pl.BlockSpec((groups, GM_CHUNK, HEAD_DIM), lambda i: (0, 0, 0))
    return pl.pallas_call(
        functools.partial(_gmlp_kernel, groups), grid=(n // bt,),
        in_specs=[pl.BlockSpec((bt, gm_w), lambda i: (i, col_u)), pl.BlockSpec((bt, gm_w), lambda i: (i, col_v)),
                  vec, vec, pl.BlockSpec((groups, GM_CHUNK, GM_CHUNK), lambda i: (0, 0, 0)), full3],
        out_specs=pl.BlockSpec((bt, gm_w), lambda i: (i, 0)),
        out_shape=jax.ShapeDtypeStruct((n, gm_w), BF16),
        compiler_params=_params("parallel"), name="gmlp",
    )(p, p, ln_g.reshape(1, gm_w), ln_b.reshape(1, gm_w), ws.astype(F32), bsb)


def _hgrn2_constants(c, reverse):
    n_lvl = int(math.log2(c))
    t = np.arange(c)
    ms, ws = [], []
    for lvl in range(n_lvl):
        m = c >> (lvl + 1)
        mid = (t // (2 * m)) * (2 * m) + m
        upper = t >= mid
        mat = np.zeros((c, c), np.float32)
        for r in range(c):
            if upper[r]:
                mat[r, mid[r]:r + 1] = 1.0
            else:
                mat[r, r + 1:mid[r]] = 1.0
        same = (t[:, None] // (2 * m)) == (t[None, :] // (2 * m))
        ws.append((same & upper[:, None] & ~upper[None, :]).astype(np.float32))
        ms.append(mat)
    ws.append(np.eye(c, dtype=np.float32))
    ms.append(np.tril(np.ones((c, c), np.float32)))
    ms.append(np.triu(np.ones((c, c), np.float32), 1))
    if reverse:
        ms = [a[::-1, ::-1] for a in ms]
        ws = [a[::-1, ::-1] for a in ws]
    return np.concatenate(ms, 0), np.stack(ws, 0)


def _hgrn2_kernel(heads, n_lvl, carry_row, final, reset_ref, *refs):
    if final:
        (q_ref, z_ref, i_ref, og_ref, oo_ref, loglb_ref, log1mlb_ref, onemlb_ref, ng_ref, m_ref, w_ref,
         out_ref, st_ref, e_ref) = refs
    else:
        (q_ref, z_ref, i_ref, loglb_ref, log1mlb_ref, onemlb_ref, m_ref, w_ref, out_ref, st_ref, e_ref) = refs
    n_streams, c = q_ref.shape[0], q_ref.shape[1]
    rows_e = m_ref.shape[0]
    step = pl.program_id(0)

    for s in range(n_streams):
        @pl.when(reset_ref[step * n_streams + s] == 1)
        def _():
            st_ref[s * heads:(s + 1) * heads] = jnp.zeros((heads, HEAD_DIM, HEAD_DIM), F32)

    for s in range(n_streams):
        z = z_ref[s]
        e = jnp.exp(-jnp.abs(z))
        r = 1.0 / (1.0 + e)
        log_sig = jnp.minimum(z, 0.0) - jnp.log(1.0 + e)
        sig_neg = jnp.where(z >= 0.0, e * r, r)
        a = loglb_ref[...]
        b = log1mlb_ref[...] + log_sig
        g = jnp.maximum(a, b) + jnp.log(1.0 + jnp.exp(-jnp.abs(a - b)))
        k = onemlb_ref[...] * sig_neg
        q = q_ref[s]
        qs = q * jax.nn.sigmoid(q)
        v = i_ref[s].astype(BF16)

        g_hi, g_lo = _split_bf16(g)
        m_all = m_ref[...]
        e0 = s * rows_e
        e_ref[e0:e0 + rows_e, :] = (jnp.dot(m_all, g_hi, preferred_element_type=F32)
                                    + jnp.dot(m_all, g_lo, preferred_element_type=F32))

        for h in range(heads):
            cols = slice(h * HEAD_DIM, (h + 1) * HEAD_DIM)
            qh, kh, vh = qs[:, cols], k[:, cols], v[:, cols]
            amat = w_ref[n_lvl] * lax.dot_general(qh.astype(BF16), kh.astype(BF16), NT_DIMS,
                                                  preferred_element_type=F32)
            for lvl in range(n_lvl):
                ex = jnp.exp(e_ref[e0 + lvl * c:e0 + (lvl + 1) * c, cols])
                amat = amat + w_ref[lvl] * lax.dot_general((qh * ex).astype(BF16), (kh * ex).astype(BF16),
                                                           NT_DIMS, preferred_element_type=F32)
            e_in = e_ref[e0 + n_lvl * c:e0 + (n_lvl + 1) * c, cols]
            q_in = (qh * jnp.exp(e_in)).astype(BF16)
            k_st = (kh * jnp.exp(e_ref[e0 + (n_lvl + 1) * c:e0 + (n_lvl + 2) * c, cols])).astype(BF16)
            dec = jnp.exp(e_in[carry_row:carry_row + 1, :])
            st = st_ref[s * heads + h]
            o = (lax.dot_general(q_in, st.astype(BF16), NT_DIMS, preferred_element_type=F32)
                 + jnp.dot(amat.astype(BF16), vh, preferred_element_type=F32))
            st_ref[s * heads + h] = st * dec + lax.dot_general(vh, k_st, TN_DIMS, preferred_element_type=F32)
            if final:
                o = o + oo_ref[s, :, cols]
                o = o * lax.rsqrt(jnp.mean(o * o, -1, keepdims=True) + RMS_EPS)
                og = og_ref[s, :, cols]
                out_ref[s, :, cols] = (o * ng_ref[:, cols] * (og * jax.nn.sigmoid(og))).astype(out_ref.dtype)
            else:
                out_ref[s, :, cols] = o


def _hgrn2_pass(p3, cols, hg_w, z_col, reverse, reset, lbs, extra, c):
    n_streams, n = p3.shape[0], p3.shape[1]
    nc = n // c
    heads = hg_w // HEAD_DIM
    n_lvl = int(math.log2(c))
    m_np, w_np = _hgrn2_constants(c, reverse)
    final = extra is not None

    def cmap(i):
        return (nc - 1 - i) if reverse else i

    def tok(col):
        return pl.BlockSpec((n_streams, c, hg_w), lambda i, r: (0, cmap(i), col))

    vec = pl.BlockSpec((1, hg_w), lambda i, r: (0, 0))
    in_specs = [tok(cols["q"]), tok(z_col), tok(cols["i"])]
    args = [p3, p3, p3]
    if final:
        o_other, norm_g = extra
        in_specs += [tok(cols["og"]), tok(0)]
        args += [p3, o_other]
    in_specs += [vec, vec, vec]
    args += list(lbs)
    if final:
        in_specs += [vec]
        args += [norm_g.reshape(1, hg_w).astype(F32)]
    in_specs += [pl.BlockSpec(m_np.shape, lambda i, r: (0, 0)), pl.BlockSpec(w_np.shape, lambda i, r: (0, 0, 0))]
    args += [jnp.asarray(m_np, BF16), jnp.asarray(w_np, F32)]
    grid_spec = pltpu.PrefetchScalarGridSpec(
        num_scalar_prefetch=1, grid=(nc,), in_specs=in_specs, out_specs=tok(0),
        scratch_shapes=[pltpu.VMEM((n_streams * heads, HEAD_DIM, HEAD_DIM), F32),
                        pltpu.VMEM((n_streams * m_np.shape[0], hg_w), F32)])
    order = np.arange(nc)[::-1] if reverse else np.arange(nc)
    reset_steps = jnp.asarray(np.asarray(reset, np.int32)[order].reshape(-1))
    return pl.pallas_call(
        functools.partial(_hgrn2_kernel, heads, n_lvl, 0 if reverse else c - 1, final), grid_spec=grid_spec,
        out_shape=jax.ShapeDtypeStruct((n_streams, n, hg_w), BF16 if final else F32),
        compiler_params=_params("arbitrary"), name="hgrn2_fwd" if final else "hgrn2_bwd",
    )(reset_steps, *args)


def _hgrn2(p, cols, hg_w, lb, norm_g, seq_lens, n_streams=2, c=HG_CHUNK):
    lb = lb.astype(F32).reshape(1, hg_w)
    lbs = (jnp.log(lb), jnp.log1p(-lb), 1.0 - lb)
    starts = np.cumsum([0] + list(seq_lens))
    n = int(starts[-1])
    per = n // n_streams
    assert n == p.shape[0] and all(s * per in starts for s in range(n_streams)), "a sequence straddles a stream cut"
    first = np.zeros(n // c, np.int32)
    last = np.zeros(n // c, np.int32)
    first[starts[:-1] // c] = 1
    last[starts[1:] // c - 1] = 1
    first = first.reshape(n_streams, per // c).T
    last = last.reshape(n_streams, per // c).T
    p3 = p.reshape(n_streams, per, p.shape[1])
    o_b = _hgrn2_pass(p3, cols, hg_w, cols["zb"], True, last, lbs, None, c)
    return _hgrn2_pass(p3, cols, hg_w, cols["zf"], False, first, lbs, (o_b, norm_g), c).reshape(n, hg_w)


def _dft_matrices(t):
    n = 2 * t
    f = jnp.arange(t, dtype=jnp.int32)
    ang = (2.0 * math.pi / n) * ((f[:, None] * f[None, :]) % n).astype(F32)
    cos, sin = jnp.cos(ang), jnp.sin(ang)
    alt = (1 - 2 * (f % 2)).astype(F32)
    first = f == 0
    fwd = jnp.stack([cos, jnp.where(first[:, None], alt[None, :], -sin)], 0).astype(BF16)
    wgt = jnp.where(first, 1.0, 2.0)[None, :] / n
    inv = jnp.stack([cos * wgt, jnp.where(first[None, :], alt[:, None] / n, -sin * wgt)], 0).astype(BF16)
    return fwd, inv


def _shortconv_kernel(flags_ref, x_ref, prev_ref, next_ref, w_ref, b_ref, o_ref):
    i = pl.program_id(0)
    x = x_ref[...]
    t = x.shape[0]
    rows = lax.broadcasted_iota(jnp.int32, x.shape, 0)
    has_prev = flags_ref[2 * i].astype(F32)
    has_next = flags_ref[2 * i + 1].astype(F32)
    up = jnp.where(rows == 0, prev_ref[7:8, :] * has_prev, pltpu.roll(x, 1, 0))
    dn = jnp.where(rows == t - 1, next_ref[0:1, :] * has_next, pltpu.roll(x, t - 1, 0))
    o_ref[...] = up * w_ref[0:1, :] + x * w_ref[1:2, :] + dn * w_ref[2:3, :] + b_ref[...]


def _shortconv(p, width, conv_w, conv_b, blk_flags, t=HY_BLOCK, ct=768):
    n = p.shape[0]
    nb = n // t
    r8 = t // 8
    grid_spec = pltpu.PrefetchScalarGridSpec(
        num_scalar_prefetch=1, grid=(nb, width // ct),
        in_specs=[pl.BlockSpec((t, ct), lambda i, j, f: (i, j)),
                  pl.BlockSpec((8, ct), lambda i, j, f: (jnp.maximum(i * r8 - 1, 0), j)),
                  pl.BlockSpec((8, ct), lambda i, j, f: (jnp.minimum((i + 1) * r8, nb * r8 - 1), j)),
                  pl.BlockSpec((3, ct), lambda i, j, f: (0, j)), pl.BlockSpec((1, ct), lambda i, j, f: (0, j))],
        out_specs=pl.BlockSpec((t, ct), lambda i, j, f: (i, j)))
    return pl.pallas_call(
        _shortconv_kernel, grid_spec=grid_spec, out_shape=jax.ShapeDtypeStruct((n, width), F32),
        compiler_params=_params("parallel", "parallel"), name="hyena_shortconv",
    )(blk_flags, p, p, p, conv_w.astype(F32), conv_b.reshape(1, width).astype(F32))


def _filter_kernel(l_total, feats_ref, w1_ref, b1_ref, w2_ref, b2_ref, w3_ref, b3_ref, fr_ref, wo_ref, dl_ref,
                   h_ref, sum_ref):
    del l_total
    i = pl.program_id(0)
    bt = feats_ref.shape[0]
    half_lanes = LANES // 2
    fr = fr_ref[...]
    feats = feats_ref[...]
    h = jnp.sin(fr * (_dot3(feats, w1_ref[...]) + b1_ref[...]))
    h = jnp.sin(fr * (_dot3(h, w2_ref[...]) + b2_ref[...]))
    h = jnp.sin(fr * (_dot3(h, w3_ref[...]) + b3_ref[...]))
    reps = wo_ref.shape[2] // dl_ref.shape[1]
    row = lax.broadcasted_iota(jnp.int32, (bt, 1), 0) + i * bt
    total = jnp.zeros((1, wo_ref.shape[2]), F32)
    for side in range(2):
        window = jnp.exp(-feats[:, side * half_lanes:side * half_lanes + 1] * dl_ref[...])
        out = _dot3(h, wo_ref[side]) * jnp.concatenate([window] * reps, axis=1)
        if side == 1:
            out = jnp.where(row == 0, 0.0, out)
        h_ref[side] = out
        total = total + jnp.sum(jnp.abs(out), axis=0, keepdims=True)

    @pl.when(i == 0)
    def _():
        sum_ref[...] = jnp.zeros_like(sum_ref)

    sum_ref[...] += total


def _hyena_filter(l, w1, b1, w2, b2, w3, b3, freq, w_out, hy_w, bt=512):
    hid = w2.shape[0]
    half_lanes = LANES // 2
    assert hid <= half_lanes and w1.shape[0] <= half_lanes
    half = HY_ORDER * hy_w
    bands = jnp.linspace(1e-4, HY_POS_BANDS - 1, HY_POS_BANDS, dtype=F32)

    def features(pos):
        tt = pos / max(l - 1, 1)
        ang = (2.0 * math.pi / l) * pos[:, None] * bands[None, :]
        f = jnp.concatenate([tt[:, None], jnp.cos(ang), -jnp.sin(ang)], -1)
        return jnp.pad(f, ((0, 0), (0, half_lanes - f.shape[1])))

    def twice_diag(w):
        wp = jnp.pad(w.astype(F32), ((0, half_lanes - w.shape[0]), (0, half_lanes - w.shape[1])))
        z = jnp.zeros_like(wp)
        return jnp.concatenate([jnp.concatenate([wp, z], 1), jnp.concatenate([z, wp], 1)], 0)

    def twice_vec(a):
        ap = jnp.pad(a.astype(F32), (0, half_lanes - a.shape[0]))
        return jnp.concatenate([ap, ap]).reshape(1, LANES)

    pos = jnp.arange(l, dtype=F32)
    feats = jnp.concatenate([features(pos), features(l - pos)], 1)
    wo = w_out.astype(F32).reshape(hid, HY_ORDER, 2, hy_w).transpose(2, 0, 1, 3).reshape(2, hid, half)
    wo = jnp.pad(wo, ((0, 0), (0, half_lanes - hid), (0, 0)))
    zero = jnp.zeros_like(wo[0])
    wop = jnp.stack([jnp.concatenate([wo[0], zero], 0), jnp.concatenate([zero, wo[1]], 0)], 0)
    deltas = jnp.abs(jnp.linspace(math.log(HY_DECAY_TARGET) / HY_SLOW_DECAY,
                                  math.log(HY_DECAY_TARGET) / HY_FAST_DECAY, hy_w, dtype=F32)).reshape(1, hy_w)
    bt = min(bt, l)
    sq = pl.BlockSpec((LANES, LANES), lambda i: (0, 0))
    vec = pl.BlockSpec((1, LANES), lambda i: (0, 0))
    taps, abs_sum = pl.pallas_call(
        functools.partial(_filter_kernel, l), grid=(l // bt,),
        in_specs=[pl.BlockSpec((bt, LANES), lambda i: (i, 0)), sq, vec, sq, vec, sq, vec, vec,
                  pl.BlockSpec((2, LANES, half), lambda i: (0, 0, 0)),
                  pl.BlockSpec((1, hy_w), lambda i: (0, 0))],
        out_specs=[pl.BlockSpec((2, bt, half), lambda i: (0, i, 0)), pl.BlockSpec((1, half), lambda i: (0, 0))],
        out_shape=[jax.ShapeDtypeStruct((2, l, half), F32), jax.ShapeDtypeStruct((1, half), F32)],
        compiler_params=_params("arbitrary"), name="hyena_filter",
    )(feats, twice_diag(w1), twice_vec(b1), twice_diag(w2), twice_vec(b2), twice_diag(w3), twice_vec(b3),
      twice_vec(freq), wop, deltas)
    return taps.reshape(2 * l, half), abs_sum


def _dft_cols(t, width):
    return width if t <= 1024 else 256


def _dft_kernel(f_ref, x_ref, o_ref):
    x = x_ref[...].astype(BF16)
    o_ref[0] = jnp.dot(f_ref[0], x, preferred_element_type=F32).astype(o_ref.dtype)
    o_ref[1] = jnp.dot(f_ref[1], x, preferred_element_type=F32).astype(o_ref.dtype)


def _block_dft(fwd, x, col_off, width, out_dtype, t=HY_BLOCK, ct=256):
    nseg = x.shape[0] // t
    ct = min(ct, width)
    return pl.pallas_call(
        _dft_kernel, grid=(nseg, width // ct),
        in_specs=[pl.BlockSpec((2, t, t), lambda s, j: (0, 0, 0)),
                  pl.BlockSpec((t, ct), lambda s, j: (s, j + col_off))],
        out_specs=pl.BlockSpec((None, 2, t, ct), lambda s, j: (s, 0, 0, j)),
        out_shape=jax.ShapeDtypeStruct((nseg, 2, t, width), out_dtype),
        compiler_params=_params("parallel", "parallel"), name="hyena_block_dft",
    )(fwd, x)


def _segment_index(d, nb):
    return d if d >= 0 else 2 * nb + d


def _hyena_filter_spectra(l, fwd, filt_w, hy_w, t=HY_BLOCK):
    nb = l // t
    taps, abs_sum = _hyena_filter(l, *filt_w, hy_w)
    half = taps.shape[1]
    seg_spec = _block_dft(fwd, taps, 0, half, F32, t, _dft_cols(t, hy_w))
    first_rows = taps.reshape(2 * nb, t, half)[:, 0, :]
    seg0 = jnp.stack([first_rows[_segment_index(d - 1, nb)] for d in range(-(nb - 1), nb)], 0)
    return seg_spec, seg0.reshape(2 * nb - 1, 1, half), 1.0 / abs_sum


MIX_ROWS = 8


def _mix_kernel(nb, z_ref, a_ref, s0_ref, inv_ref, *refs):
    o_ref, k_ref = refs[-2], refs[-1]
    ft, ct = z_ref.shape[2], z_ref.shape[3]
    first_tile = pl.program_id(1) == 0

    tile_rows = lax.broadcasted_iota(jnp.int32, (ft, ct), 0)
    sgn = (1 - 2 * (tile_rows & 1)).astype(F32)
    real_row = jnp.logical_and(tile_rows == 0, first_tile)
    inv = inv_ref[...]
    for slot in range(2 * nb - 1):
        d = slot - (nb - 1)
        ia, ib = _segment_index(d, nb), _segment_index(d - 1, nb)
        s0 = s0_ref[slot]
        k_ref[slot, 0] = (a_ref[ia, 0] + sgn * (a_ref[ib, 0] - s0)) * inv
        k_ref[slot, 1] = (a_ref[ia, 1] + sgn * (a_ref[ib, 1] - jnp.where(real_row, s0, 0.0))) * inv

    n_out = 2 if nb % 2 == 0 else 1

    def out_rows(i, r0, masked):
        rows = pl.ds(r0, MIX_ROWS)
        acc = [[jnp.zeros((MIX_ROWS, ct), F32) for _ in range(2)] for _ in range(n_out)]
        if masked:
            row0 = jnp.logical_and(lax.broadcasted_iota(jnp.int32, (MIX_ROWS, ct), 0) == 0, first_tile)
        d_hi = i + (n_out - 1) + (nb - 1)
        k_prev = (k_ref[d_hi, 0, rows, :], k_ref[d_hi, 1, rows, :]) if n_out == 2 else None
        for j in range(nb):
            d = i - j + (nb - 1)
            zt, zb = z_ref[j, 0, rows, :], z_ref[j, 1, rows, :]
            k_cur = (k_ref[d, 0, rows, :], k_ref[d, 1, rows, :])
            for o, (kt, kb) in enumerate((k_cur, k_prev)[:n_out]):
                bb = zb * kb
                if masked:
                    acc[o][0] = acc[o][0] + (zt * kt - jnp.where(row0, 0.0, bb))
                    acc[o][1] = acc[o][1] + jnp.where(row0, bb, zt * kb + zb * kt)
                else:
                    acc[o][0] = acc[o][0] + (zt * kt - bb)
                    acc[o][1] = acc[o][1] + (zt * kb + zb * kt)
            k_prev = k_cur
        for o in range(n_out):
            o_ref[i + o, 0, rows, :] = acc[o][0].astype(o_ref.dtype)
            o_ref[i + o, 1, rows, :] = acc[o][1].astype(o_ref.dtype)

    def all_outputs(r0, masked):
        def body(ip, carry):
            out_rows(ip * n_out, r0, masked)
            return carry
        lax.fori_loop(0, nb // n_out, body, 0)

    all_outputs(0, True)

    def chunk(r, carry):
        all_outputs(pl.multiple_of(r * MIX_ROWS, MIX_ROWS), False)
        return carry

    lax.fori_loop(1, ft // MIX_ROWS, chunk, 0)


def _hyena_mix(zspec, kspec, k_col, n_seq, nb, blk_off, ct=256):
    seg_spec, seg0, inv_norm = kspec
    _, _, t, width = zspec.shape
    ft = max(MIX_ROWS, min(t, 2048 // nb))
    nct = width // ct
    assert blk_off % nb == 0 and t % ft == 0
    s_off = blk_off // nb
    return pl.pallas_call(
        functools.partial(_mix_kernel, nb), grid=(n_seq, t // ft, nct),
        in_specs=[pl.BlockSpec((nb, 2, ft, ct), lambda s, fi, j: (s + s_off, 0, fi, j)),
                  pl.BlockSpec((2 * nb, 2, ft, ct), lambda s, fi, j: (0, 0, fi, j + k_col * nct)),
                  pl.BlockSpec((2 * nb - 1, 1, ct), lambda s, fi, j: (0, 0, j + k_col * nct)),
                  pl.BlockSpec((1, ct), lambda s, fi, j: (0, j + k_col * nct))],
        out_specs=pl.BlockSpec((nb, 2, ft, ct), lambda s, fi, j: (s, 0, fi, j)),
        out_shape=jax.ShapeDtypeStruct((n_seq * nb, 2, t, width), BF16),
        scratch_shapes=[pltpu.VMEM((2 * nb - 1, 2, ft, ct), F32)],
        compiler_params=_params("parallel", "parallel", "parallel"), name="hyena_mix",
    )(zspec, seg_spec, seg0, inv_norm)


def _inverse_kernel(bounds, *refs):
    n_groups = len(bounds)
    y_refs = refs[:n_groups]
    fi_ref, zin_ref, gate_ref, bias_ref, o_ref = refs[n_groups:]
    i = pl.program_id(0)
    for (lo, hi), y_ref in zip(bounds, y_refs):
        @pl.when(jnp.logical_and(i >= lo, i < hi))
        def _():
            y = (jnp.dot(fi_ref[0], y_ref[0], preferred_element_type=F32)
                 + jnp.dot(fi_ref[1], y_ref[1], preferred_element_type=F32))
            o_ref[...] = (gate_ref[...] * (y + zin_ref[...] * bias_ref[...])).astype(o_ref.dtype)


def _hyena_inverse(yspecs, inv, zin, zin_col, gate, gate_col, bias, out_dtype, ct=256):
    _, _, t, width = yspecs[0].shape
    nct = width // ct
    bounds, lo = [], 0
    for y in yspecs:
        bounds.append((lo, lo + y.shape[0]))
        lo += y.shape[0]

    def group_spec(lo, hi):
        return pl.BlockSpec((None, 2, t, ct), lambda i, j: (jnp.clip(i - lo, 0, hi - lo - 1), 0, 0, j))

    return pl.pallas_call(
        functools.partial(_inverse_kernel, tuple(bounds)), grid=(lo, nct),
        in_specs=[group_spec(*b) for b in bounds] + [
            pl.BlockSpec((2, t, t), lambda i, j: (0, 0, 0), pipeline_mode=pl.Buffered(1)),
            pl.BlockSpec((t, ct), lambda i, j: (i, j + zin_col * nct)),
            pl.BlockSpec((t, ct), lambda i, j: (i, j + gate_col * nct)),
            pl.BlockSpec((1, ct), lambda i, j: (0, j))],
        out_specs=pl.BlockSpec((t, ct), lambda i, j: (i, j)),
        out_shape=jax.ShapeDtypeStruct((lo * t, width), out_dtype),
        compiler_params=_params("arbitrary", "arbitrary"), name="hyena_inverse",
    )(*yspecs, inv, zin, gate, bias)


def _hyena(p, hy_w, groups, conv_w, conv_b, filt_w, bias, t=HY_BLOCK):
    flags = []
    for n_seq, l in groups:
        nb = l // t
        flags += [int(bi > 0) if side == 0 else int(bi < nb - 1)
                  for _ in range(n_seq) for bi in range(nb) for side in range(2)]
    fwd, inv = _dft_matrices(t)
    u = _shortconv(p, 3 * hy_w, conv_w, conv_b, jnp.asarray(np.asarray(flags, np.int32)), t, ct=hy_w)
    kspecs = [_hyena_filter_spectra(l, fwd, filt_w, hy_w, t) for _, l in groups]
    bias = bias.astype(F32)
    ct = _dft_cols(t, hy_w)
    z, z_col = u, 2
    for o in range(HY_ORDER):
        zspec = _block_dft(fwd, z, z_col * (hy_w // ct), hy_w, F32, t, ct)
        yspecs, blk = [], 0
        for (n_seq, l), kspec in zip(groups, kspecs):
            yspecs.append(_hyena_mix(zspec, kspec, o, n_seq, l // t, blk))
            blk += n_seq * (l // t)
        last = o == HY_ORDER - 1
        z = _hyena_inverse(yspecs, inv, z, z_col, u, o, bias[o:o + 1], BF16 if last else F32, ct)
        z_col = 0
    return z


def kernel(x_prompt, x_sample, ln_in_g, ln_in_b, w_in, hy_conv_w, hy_conv_b, hy_pos_w1, hy_pos_b1, hy_pos_w2, hy_pos_b2, hy_pos_w3, hy_pos_b3, hy_sin_freq, hy_pos_wout, hy_bias, gm_ln_g, gm_ln_b, gm_ws, gm_bs, hg_lb_raw, hg_norm_g, w_out, ln1_g, ln1_b, ln2_g, ln2_b, ffn_w1, ffn_w3, ffn_w2, moe_router_w, moe_router_b, moe_w1, moe_w3, moe_w2):
    depth, d_model, in_w = w_in.shape
    hy_w = hy_bias.shape[-1]
    gm_w = gm_ln_g.shape[-1]
    hg_w = hg_norm_g.shape[-1]
    o1 = 3 * hy_w
    o2 = o1 + 2 * gm_w
    alpha = (2 * depth) ** 0.25
    groups = [(x_prompt.shape[0], x_prompt.shape[1]), (x_sample.shape[0], x_sample.shape[1])]
    seq_lens = [l for n_seq, l in groups for _ in range(n_seq)]
    n_prompt = x_prompt.shape[0] * x_prompt.shape[1]
    hg_off = o1 // hg_w
    hg_cols = {"q": hg_off, "zf": hg_off + 1, "zb": hg_off + 2, "i": hg_off + 3, "og": hg_off + 4}
    gm_off = (o1 + 5 * hg_w) // gm_w

    lb_all = jnp.cumsum(jax.nn.softmax(hg_lb_raw.astype(F32), axis=0), axis=0)
    lb_all = lb_all - lb_all[:1]

    x, xb = _layernorm_pair(x_prompt.reshape(-1, d_model), x_sample.reshape(-1, d_model), ln_in_g, ln_in_b)
    for l in range(depth):
        w_l = jnp.concatenate([w_in[l][:, :o1], w_in[l][:, o2:], w_in[l][:, o1:o2]], 1).astype(BF16)
        p = _matmul(xb, w_l)
        filt_w = (hy_pos_w1[l], hy_pos_b1[l], hy_pos_w2[l], hy_pos_b2[l], hy_pos_w3[l], hy_pos_b3[l],
                  hy_sin_freq[l], hy_pos_wout[l])
        y_hy = _hyena(p, hy_w, groups, hy_conv_w[l], hy_conv_b[l], filt_w, hy_bias[l])
        y_hg = _hgrn2(p, hg_cols, hg_w, lb_all[l], hg_norm_g[l], seq_lens)
        y_gm = _gmlp(p, gm_off, gm_off + 1, gm_w, gm_ln_g[l], gm_ln_b[l], gm_ws[l], gm_bs[l])
        wo = w_out[l].astype(BF16)
        parts = [y_hy, y_gm, y_hg]
        wo_parts = [wo[:hy_w], wo[hy_w:hy_w + gm_w], wo[hy_w + gm_w:]]
        j = l // 2
        if l % 2 == 0:
            x, xb = _mix_out_residual_ln(parts, wo_parts, x, ln1_g[l], ln1_b[l], alpha)
            x, xb = _ffn_residual_ln(xb, ffn_w1[j].astype(BF16), ffn_w3[j].astype(BF16), ffn_w2[j].astype(BF16),
                                     x, ln2_g[l], ln2_b[l], alpha)
        else:
            n_exp = moe_router_w.shape[-1]
            wr = jnp.pad(moe_router_w[j].astype(BF16), ((0, 0), (0, LANES - n_exp)))
            x, xw, logits = _mix_out_residual_ln(parts, wo_parts, x, ln1_g[l], ln1_b[l], alpha, wr)
            logits = logits[:, :n_exp] + moe_router_b[j].astype(F32)
            y0, y1, gate = _moe_dispatch_ffn(logits, xw, moe_w1[j], moe_w3[j], _cast_bf16(moe_w2[j]))
            xa, xs = _combine_residual_ln_split(x, y0, y1, gate, ln2_g[l], ln2_b[l], alpha, n_prompt)
            if l == depth - 1:
                return (xa.reshape(x_prompt.shape), xs.reshape(x_sample.shape))
            x = jnp.concatenate([xa, xs], 0)
            xb = x.astype(BF16)
    return (x[:n_prompt].reshape(x_prompt.shape), x[n_prompt:].reshape(x_sample.shape))
```

```python
import functools
import math

import numpy as np
import jax
import jax.numpy as jnp
from jax import lax
from jax.experimental import pallas as pl
from jax.experimental.pallas import tpu as pltpu
from jax.experimental.pallas import tpu_sc as plsc

HEAD_DIM = 128
HY_ORDER = 2
HY_POS_BANDS = 16
HY_FAST_DECAY = 0.3
HY_SLOW_DECAY = 1.5
HY_DECAY_TARGET = 1e-2
GM_CHUNK = 128
N_EXPERTS = 8
TOP_K = 2
LN_EPS = 1e-5
RMS_EPS = 1e-6

HY_BLOCK = 1024
HG_CHUNK = 128
LANES = 128

V7X_VMEM_BYTES = 64 * 1024 * 1024
VMEM_LIMIT = V7X_VMEM_BYTES - 8 * 1024 * 1024

F32 = jnp.float32
BF16 = jnp.bfloat16
NT_DIMS = (((1,), (1,)), ((), ()))
TN_DIMS = (((0,), (0,)), ((), ()))


def _params(*sem):
    return pltpu.CompilerParams(dimension_semantics=sem, vmem_limit_bytes=VMEM_LIMIT)


def _ln_rows(x, g, b):
    mu = jnp.mean(x, -1, keepdims=True)
    xc = x - mu
    var = jnp.mean(xc * xc, -1, keepdims=True)
    return xc * lax.rsqrt(var + LN_EPS) * g + b


def _split_bf16(x):
    hi = x.astype(BF16)
    lo = (x - hi.astype(F32)).astype(BF16)
    return hi, lo


def _dot3(a, b):
    ah, al = _split_bf16(a)
    bh, bl = _split_bf16(b)
    return (jnp.dot(ah, bh, preferred_element_type=F32) + jnp.dot(ah, bl, preferred_element_type=F32)
            + jnp.dot(al, bh, preferred_element_type=F32))


def _pack_bf16_pairs(y):
    w = y.shape[1] // 2
    bits = lax.bitcast_convert_type(y.astype(BF16).astype(F32), jnp.uint32)
    word = (bits[:, :w] >> 16) | (bits[:, w:] & jnp.uint32(0xFFFF0000))
    return lax.bitcast_convert_type(word, jnp.int32)


def _unpack_bf16_pairs(word):
    u = lax.bitcast_convert_type(word, jnp.uint32)
    lo = lax.bitcast_convert_type(u << 16, F32)
    hi = lax.bitcast_convert_type(u & jnp.uint32(0xFFFF0000), F32)
    return jnp.concatenate([lo, hi], axis=1)


SC_GATHER_WINDOW = 128
SC_GATHER_WORDS = 256


def _gather_rows(xw, idx):
    n, words = xw.shape
    m = idx.shape[0]
    parts = words // SC_GATHER_WORDS
    assert words % SC_GATHER_WORDS == 0 and m % (2 * SC_GATHER_WINDOW) == 0
    half = m // SC_GATHER_WINDOW // 2
    mesh = plsc.VectorSubcoreMesh(core_axis_name="core", subcore_axis_name="subcore")

    @pl.kernel(out_type=jax.ShapeDtypeStruct((m, words), xw.dtype), mesh=mesh)
    def gather(x_hbm, i_hbm, o_hbm):
        def body(i_vmem, o_vmem):
            piece = pl.program_id(2)
            pltpu.sync_copy(x_hbm.at[i_vmem.at[0], pl.ds(piece * SC_GATHER_WORDS, SC_GATHER_WORDS)], o_vmem)

        pltpu.emit_pipeline(
            body, grid=(2, half, parts),
            in_specs=[pl.BlockSpec((1, SC_GATHER_WINDOW), index_map=lambda c, i, p: (0, c * half + i))],
            out_specs=[pl.BlockSpec((SC_GATHER_WINDOW, SC_GATHER_WORDS),
                                    index_map=lambda c, i, p: (c * half + i, p))],
            core_axis_name=("core", "subcore"),
            dimension_semantics=(pltpu.PARALLEL, pltpu.PARALLEL, pltpu.ARBITRARY),
        )(i_hbm, o_hbm)

    return gather(xw, idx.astype(jnp.int32).reshape(1, m))


def _ln2_kernel(na_blocks, xa_ref, xb_ref, g_ref, b_ref, of_ref, ob_ref):
    def emit(x_ref):
        y = _ln_rows(x_ref[...], g_ref[...], b_ref[...])
        of_ref[...] = y
        ob_ref[...] = y.astype(BF16)

    first = pl.program_id(0) < na_blocks
    pl.when(first)(lambda: emit(xa_ref))
    pl.when(jnp.logical_not(first))(lambda: emit(xb_ref))


def _layernorm_pair(xa, xb, g, b, bm=512):
    d = xa.shape[1]
    na, nb = xa.shape[0] // bm, xb.shape[0] // bm
    row = pl.BlockSpec((bm, d), lambda i: (i, 0))
    vec = pl.BlockSpec((1, d), lambda i: (0, 0))
    m = (na + nb) * bm
    return pl.pallas_call(
        functools.partial(_ln2_kernel, na), grid=(na + nb,),
        in_specs=[pl.BlockSpec((bm, d), lambda i: (jnp.minimum(i, na - 1), 0)),
                  pl.BlockSpec((bm, d), lambda i: (jnp.maximum(i - na, 0), 0)), vec, vec],
        out_specs=[row, row],
        out_shape=[jax.ShapeDtypeStruct((m, d), F32), jax.ShapeDtypeStruct((m, d), BF16)],
        compiler_params=_params("arbitrary"), name="layernorm",
    )(xa, xb, g.reshape(1, d), b.reshape(1, d))


def _combine_ln_kernel(alpha, na_blocks, x_ref, y0_ref, y1_ref, gate_ref, g_ref, b_ref, oa_ref, ob_ref):
    gate = gate_ref[...]
    ff = _unpack_bf16_pairs(y0_ref[...]) * gate[:, 0:1] + _unpack_bf16_pairs(y1_ref[...]) * gate[:, 1:2]
    y = _ln_rows(alpha * x_ref[...] + ff, g_ref[...], b_ref[...])
    first = pl.program_id(0) < na_blocks

    @pl.when(first)
    def _():
        oa_ref[...] = y

    @pl.when(jnp.logical_not(first))
    def _():
        ob_ref[...] = y


def _combine_residual_ln_split(x, y0, y1, gate, g, b, alpha, n_first, bm=512):
    m, d = x.shape
    na = n_first // bm
    row = pl.BlockSpec((bm, d), lambda i: (i, 0))
    packed = pl.BlockSpec((bm, d // 2), lambda i: (i, 0))
    vec = pl.BlockSpec((1, d), lambda i: (0, 0))
    return pl.pallas_call(
        functools.partial(_combine_ln_kernel, alpha, na), grid=(m // bm,),
        in_specs=[row, packed, packed, pl.BlockSpec((bm, gate.shape[1]), lambda i: (i, 0)), vec, vec],
        out_specs=[pl.BlockSpec((bm, d), lambda i: (jnp.minimum(i, na - 1), 0)),
                   pl.BlockSpec((bm, d), lambda i: (jnp.maximum(i - na, 0), 0))],
        out_shape=[jax.ShapeDtypeStruct((n_first, d), F32), jax.ShapeDtypeStruct((m - n_first, d), F32)],
        compiler_params=_params("arbitrary"), name="moe_combine_residual_ln",
    )(x, y0, y1, gate, g.reshape(1, d), b.reshape(1, d))


def _mm_kernel(a_ref, w_ref, o_ref):
    o_ref[...] = jnp.dot(a_ref[...], w_ref[...], preferred_element_type=F32).astype(o_ref.dtype)


def _matmul(a, w, out_dtype=F32, bm=1024, bn=1024):
    m, k = a.shape
    n = w.shape[1]
    bm, bn = min(bm, m), min(bn, n)
    return pl.pallas_call(
        _mm_kernel, grid=(m // bm, n // bn),
        in_specs=[pl.BlockSpec((bm, k), lambda i, j: (i, 0)), pl.BlockSpec((k, bn), lambda i, j: (0, j))],
        out_specs=pl.BlockSpec((bm, bn), lambda i, j: (i, j)),
        out_shape=jax.ShapeDtypeStruct((m, n), out_dtype),
        compiler_params=_params("parallel", "arbitrary"), name="matmul",
    )(a, w)


def _mix_out_kernel(alpha, routed, a0_ref, a1_ref, a2_ref, w0_ref, w1_ref, w2_ref, x_ref, g_ref, b_ref, *refs):
    mix = (jnp.dot(a0_ref[...], w0_ref[...], preferred_element_type=F32)
           + jnp.dot(a1_ref[...], w1_ref[...], preferred_element_type=F32)
           + jnp.dot(a2_ref[...], w2_ref[...], preferred_element_type=F32))
    y = _ln_rows(alpha * x_ref[...] + mix, g_ref[...], b_ref[...])
    if routed:
        wr_ref, of_ref, ob_ref, lg_ref = refs
        y_hi, y_lo = _split_bf16(y)
        lg_ref[...] = (jnp.dot(y_hi, wr_ref[...], preferred_element_type=F32)
                       + jnp.dot(y_lo, wr_ref[...], preferred_element_type=F32))
        ob_ref[...] = _pack_bf16_pairs(y)
    else:
        of_ref, ob_ref = refs
        ob_ref[...] = y.astype(BF16)
    of_ref[...] = y


def _mix_out_residual_ln(parts, weights, x, g, b, alpha, router_w=None, bm=512):
    m, d = x.shape
    row = pl.BlockSpec((bm, d), lambda i: (i, 0))
    vec = pl.BlockSpec((1, d), lambda i: (0, 0))
    in_specs = [pl.BlockSpec((bm, a.shape[1]), lambda i: (i, 0)) for a in parts]
    in_specs += [pl.BlockSpec(w.shape, lambda i: (0, 0)) for w in weights] + [row, vec, vec]
    args = [*parts, *weights, x, g.reshape(1, d), b.reshape(1, d)]
    out_specs = [row, row]
    out_shape = [jax.ShapeDtypeStruct((m, d), F32), jax.ShapeDtypeStruct((m, d), BF16)]
    if router_w is not None:
        in_specs.append(pl.BlockSpec(router_w.shape, lambda i: (0, 0)))
        args.append(router_w)
        out_specs[1] = pl.BlockSpec((bm, d // 2), lambda i: (i, 0))
        out_shape[1] = jax.ShapeDtypeStruct((m, d // 2), jnp.int32)
        out_specs.append(pl.BlockSpec((bm, router_w.shape[1]), lambda i: (i, 0)))
        out_shape.append(jax.ShapeDtypeStruct((m, router_w.shape[1]), F32))
    return pl.pallas_call(
        functools.partial(_mix_out_kernel, alpha, router_w is not None), grid=(m // bm,),
        in_specs=in_specs, out_specs=out_specs, out_shape=out_shape,
        compiler_params=_params("parallel"), name="out_proj_residual_ln",
    )(*args)


def _swiglu_acc(a_ref, w1_ref, w3_ref, w2_ref, acc_ref):
    f = pl.program_id(1)

    @pl.when(f == 0)
    def _():
        acc_ref[...] = jnp.zeros_like(acc_ref)

    a = a_ref[...]
    h1 = jnp.dot(a, w1_ref[...], preferred_element_type=F32)
    h3 = jnp.dot(a, w3_ref[...], preferred_element_type=F32)
    gated = (h1 * jax.nn.sigmoid(h1) * h3).astype(BF16)
    acc_ref[...] += jnp.dot(gated, w2_ref[...], preferred_element_type=F32)


def _ffn_kernel(alpha, a_ref, w1_ref, w3_ref, w2_ref, x_ref, g_ref, b_ref, of_ref, ob_ref, acc_ref):
    _swiglu_acc(a_ref, w1_ref, w3_ref, w2_ref, acc_ref)

    @pl.when(pl.program_id(1) == pl.num_programs(1) - 1)
    def _():
        y = _ln_rows(alpha * x_ref[...] + acc_ref[...], g_ref[...], b_ref[...])
        of_ref[...] = y
        ob_ref[...] = y.astype(BF16)


def _ffn_residual_ln(a, w1, w3, w2, x, g, b, alpha, bm=512, bf=512):
    m, d = a.shape
    dff = w1.shape[1]
    row = pl.BlockSpec((bm, d), lambda i, f: (i, 0))
    vec = pl.BlockSpec((1, d), lambda i, f: (0, 0))
    up = pl.BlockSpec((d, bf), lambda i, f: (0, f))
    return pl.pallas_call(
        functools.partial(_ffn_kernel, alpha), grid=(m // bm, dff // bf),
        in_specs=[row, up, up, pl.BlockSpec((bf, d), lambda i, f: (f, 0)), row, vec, vec],
        out_specs=[row, row],
        out_shape=[jax.ShapeDtypeStruct((m, d), F32), jax.ShapeDtypeStruct((m, d), BF16)],
        scratch_shapes=[pltpu.VMEM((bm, d), F32)],
        compiler_params=_params("parallel", "arbitrary"), name="ffn_residual_ln",
    )(a, w1, w3, w2, x, g.reshape(1, d), b.reshape(1, d))


def _moe_ffn_kernel(blk_e_ref, used_ref, a_ref, w1_ref, w3_ref, w2_ref, o_ref, acc_ref):
    del blk_e_ref
    f = pl.program_id(1)

    @pl.when(pl.program_id(0) < used_ref[0])
    def _():
        @pl.when(f == 0)
        def _():
            acc_ref[...] = jnp.zeros_like(acc_ref)

        a = _unpack_bf16_pairs(a_ref[...]).astype(BF16)
        h1 = jnp.dot(a, w1_ref[...].astype(BF16), preferred_element_type=F32)
        h3 = jnp.dot(a, w3_ref[...].astype(BF16), preferred_element_type=F32)
        gated = (h1 * jax.nn.sigmoid(h1) * h3).astype(BF16)
        acc_ref[...] += jnp.dot(gated, w2_ref[...], preferred_element_type=F32)

        @pl.when(f == pl.num_programs(1) - 1)
        def _():
            o_ref[...] = _pack_bf16_pairs(acc_ref[...])


def _cast_kernel(x_ref, o_ref):
    o_ref[...] = x_ref[...].astype(o_ref.dtype)


def _cast_bf16(w, rows=128):
    e, r, c = w.shape
    rows = min(rows, r)
    blk = pl.BlockSpec((None, rows, c), lambda i, j: (i, j, 0))
    return pl.pallas_call(
        _cast_kernel, grid=(e, r // rows), in_specs=[blk], out_specs=blk,
        out_shape=jax.ShapeDtypeStruct(w.shape, BF16),
        compiler_params=_params("parallel", "parallel"), name="cast_bf16",
    )(w)


def _moe_grouped_ffn(blk_e, n_used, xw, w1, w3, w2, bm, bf=512):
    cap, words = xw.shape
    d = 2 * words
    dff = w1.shape[2]
    nf = dff // bf

    def blk(i, u):
        return jnp.minimum(i, u[0] - 1)

    def ftile(i, f, u):
        return jnp.where(i < u[0], f, nf - 1)

    row = pl.BlockSpec((bm, words), lambda i, f, e, u: (blk(i, u), 0))
    up = pl.BlockSpec((None, d, bf), lambda i, f, e, u: (e[blk(i, u)], 0, ftile(i, f, u)))
    down = pl.BlockSpec((None, bf, d), lambda i, f, e, u: (e[blk(i, u)], ftile(i, f, u), 0))
    grid_spec = pltpu.PrefetchScalarGridSpec(
        num_scalar_prefetch=2, grid=(cap // bm, nf), in_specs=[row, up, up, down],
        out_specs=row, scratch_shapes=[pltpu.VMEM((bm, d), F32)])
    return pl.pallas_call(
        _moe_ffn_kernel, grid_spec=grid_spec, out_shape=jax.ShapeDtypeStruct((cap, words), jnp.int32),
        compiler_params=_params("arbitrary", "arbitrary"), name="moe_grouped_ffn",
    )(blk_e, n_used, xw, w1, w3, w2)


def _moe_route(logits, bm):
    n = logits.shape[0]
    top_val, top_idx = lax.top_k(logits, TOP_K)
    gate = jax.nn.softmax(top_val, axis=-1)
    nk = n * TOP_K
    assert nk % bm == 0
    experts = jnp.arange(N_EXPERTS, dtype=jnp.int32)
    flat_e = top_idx.reshape(nk).astype(jnp.int32)
    one_hot = flat_e[:, None] == experts[None, :]
    order = jnp.argsort(flat_e, stable=True).astype(jnp.int32)
    rank = jnp.argsort(order).astype(jnp.int32)
    counts = jnp.sum(one_hot.astype(jnp.int32), 0)
    padded = (counts + bm - 1) // bm * bm
    start = jnp.cumsum(counts) - counts
    pend = jnp.cumsum(padded)
    shift = jnp.sum(jnp.where(one_hot, (pend - padded - start)[None, :], 0), 1)
    dest = (rank + shift).reshape(n, TOP_K)
    n_blocks = nk // bm + N_EXPERTS
    blk_e = jnp.minimum(jnp.searchsorted(pend, jnp.arange(n_blocks, dtype=jnp.int32) * bm, side='right'),
                        N_EXPERTS - 1).astype(jnp.int32)
    pad_rank = jnp.arange(bm, dtype=jnp.int32)[None, :]
    pad_key = jnp.where(pad_rank < (padded - counts)[:, None], 2 * experts[:, None] + 1, 2 * N_EXPERTS)
    keys = jnp.concatenate([2 * flat_e, pad_key.reshape(-1)])
    toks = jnp.concatenate([jnp.arange(nk, dtype=jnp.int32) // TOP_K, jnp.zeros((N_EXPERTS * bm,), jnp.int32)])
    _, slot_tok = lax.sort((keys, toks), num_keys=1, is_stable=True)
    n_used = (pend[-1] // bm).astype(jnp.int32).reshape(1)
    return gate, dest, slot_tok, blk_e, n_used


def _moe_dispatch_ffn(logits, xw, w1, w3, w2, bm=1024):
    gate, dest, slot_tok, blk_e, n_used = _moe_route(logits, bm)
    yw = _moe_grouped_ffn(blk_e, n_used, _gather_rows(xw, slot_tok), w1, w3, w2, bm)
    return _gather_rows(yw, dest[:, 0]), _gather_rows(yw, dest[:, 1]), gate


def _gmlp_kernel(groups, pu_ref, pv_ref, g_ref, b_ref, ws_ref, bs_ref, o_ref):
    u = jax.nn.gelu(pu_ref[...])
    v = _ln_rows(jax.nn.gelu(pv_ref[...]), g_ref[...], b_ref[...])
    for n in range(u.shape[0] // GM_CHUNK):
        rows = slice(n * GM_CHUNK, (n + 1) * GM_CHUNK)
        for grp in range(groups):
            cols = slice(grp * HEAD_DIM, (grp + 1) * HEAD_DIM)
            s = _dot3(ws_ref[grp], v[rows, cols]) + bs_ref[grp]
            o_ref[rows, cols] = (u[rows, cols] * s).astype(o_ref.dtype)


def _gmlp(p, col_u, col_v, gm_w, ln_g, ln_b, ws, bs, bt=512):
    n = p.shape[0]
    groups = ws.shape[0]
    bsb = jnp.broadcast_to(bs[:, :, None], (groups, GM_CHUNK, HEAD_DIM)).astype(F32)
    vec = pl.BlockSpec((1, gm_w), lambda i: (0, 0))
    full3 = pl.BlockSpec((groups, GM_CHUNK, HEAD_DIM), lambda i: (0, 0, 0))
    return pl.pallas_call(
        functools.partial(_gmlp_kernel, groups), grid=(n // bt,),
        in_specs=[pl.BlockSpec((bt, gm_w), lambda i: (i, col_u)), pl.BlockSpec((bt, gm_w), lambda i: (i, col_v)),
                  vec, vec, pl.BlockSpec((groups, GM_CHUNK, GM_CHUNK), lambda i: (0, 0, 0)), full3],
        out_specs=pl.BlockSpec((bt, gm_w), lambda i: (i, 0)),
        out_shape=jax.ShapeDtypeStruct((n, gm_w), BF16),
        compiler_params=_params("parallel"), name="gmlp",
    )(p, p, ln_g.reshape(1, gm_w), ln_b.reshape(1, gm_w), ws.astype(F32), bsb)


def _hgrn2_constants(c, reverse):
    n_lvl = int(math.log2(c))
    t = np.arange(c)
    ms, ws = [], []
    for lvl in range(n_lvl):
        m = c >> (lvl + 1)
        mid = (t // (2 * m)) * (2 * m) + m
        upper = t >= mid
        mat = np.zeros((c, c), np.float32)
        for r in range(c):
            if upper[r]:
                mat[r, mid[r]:r + 1] = 1.0
            else:
                mat[r, r + 1:mid[r]] = 1.0
        same = (t[:, None] // (2 * m)) == (t[None, :] // (2 * m))
        ws.append((same & upper[:, None] & ~upper[None, :]).astype(np.float32))
        ms.append(mat)
    ws.append(np.eye(c, dtype=np.float32))
    ms.append(np.tril(np.ones((c, c), np.float32)))
    ms.append(np.triu(np.ones((c, c), np.float32), 1))
    if reverse:
        ms = [a[::-1, ::-1] for a in ms]
        ws = [a[::-1, ::-1] for a in ws]
    return np.concatenate(ms, 0), np.stack(ws, 0)


def _hgrn2_kernel(heads, n_lvl, carry_row, final, reset_ref, *refs):
    if final:
        (q_ref, z_ref, i_ref, og_ref, oo_ref, loglb_ref, log1mlb_ref, onemlb_ref, ng_ref, m_ref, w_ref,
         out_ref, st_ref) = refs
    else:
        (q_ref, z_ref, i_ref, loglb_ref, log1mlb_ref, onemlb_ref, m_ref, w_ref, out_ref, st_ref) = refs
    n_streams, c = q_ref.shape[0], q_ref.shape[1]
    step = pl.program_id(0)

    for s in range(n_streams):
        @pl.when(reset_ref[step * n_streams + s] == 1)
        def _():
            st_ref[s * heads:(s + 1) * heads] = jnp.zeros((heads, HEAD_DIM, HEAD_DIM), F32)

    for s in range(n_streams):
        z = z_ref[s]
        e = jnp.exp(-jnp.abs(z))
        r = 1.0 / (1.0 + e)
        log_sig = jnp.minimum(z, 0.0) - jnp.log(1.0 + e)
        sig_neg = jnp.where(z >= 0.0, e * r, r)
        a = loglb_ref[...]
        b = log1mlb_ref[...] + log_sig
        g = jnp.maximum(a, b) + jnp.log(1.0 + jnp.exp(-jnp.abs(a - b)))
        k = onemlb_ref[...] * sig_neg
        q = q_ref[s]
        qs = q * jax.nn.sigmoid(q)
        v = i_ref[s].astype(BF16)

        g_hi, g_lo = _split_bf16(g)
        m_all = m_ref[...]
        e_all = (jnp.dot(m_all, g_hi, preferred_element_type=F32)
                 + jnp.dot(m_all, g_lo, preferred_element_type=F32))

        for h in range(heads):
            cols = slice(h * HEAD_DIM, (h + 1) * HEAD_DIM)
            qh, kh, vh = qs[:, cols], k[:, cols], v[:, cols]
            amat = w_ref[n_lvl] * lax.dot_general(qh.astype(BF16), kh.astype(BF16), NT_DIMS,
                                                  preferred_element_type=F32)
            for lvl in range(n_lvl):
                ex = jnp.exp(e_all[lvl * c:(lvl + 1) * c, cols])
                amat = amat + w_ref[lvl] * lax.dot_general((qh * ex).astype(BF16), (kh * ex).astype(BF16),
                                                           NT_DIMS, preferred_element_type=F32)
            e_in = e_all[n_lvl * c:(n_lvl + 1) * c, cols]
            q_in = (qh * jnp.exp(e_in)).astype(BF16)
            k_st = (kh * jnp.exp(e_all[(n_lvl + 1) * c:(n_lvl + 2) * c, cols])).astype(BF16)
            dec = jnp.exp(e_in[carry_row:carry_row + 1, :])
            st = st_ref[s * heads + h]
            o = (lax.dot_general(q_in, st.astype(BF16), NT_DIMS, preferred_element_type=F32)
                 + jnp.dot(amat.astype(BF16), vh, preferred_element_type=F32))
            st_ref[s * heads + h] = st * dec + lax.dot_general(vh, k_st, TN_DIMS, preferred_element_type=F32)
            if final:
                o = o + oo_ref[s, :, cols]
                o = o * lax.rsqrt(jnp.mean(o * o, -1, keepdims=True) + RMS_EPS)
                og = og_ref[s, :, cols]
                out_ref[s, :, cols] = (o * ng_ref[:, cols] * (og * jax.nn.sigmoid(og))).astype(out_ref.dtype)
            else:
                out_ref[s, :, cols] = o


def _hgrn2_pass(p3, cols, hg_w, z_col, reverse, reset, lbs, extra, c):
    n_streams, n = p3.shape[0], p3.shape[1]
    nc = n // c
    heads = hg_w // HEAD_DIM
    n_lvl = int(math.log2(c))
    m_np, w_np = _hgrn2_constants(c, reverse)
    final = extra is not None

    def cmap(i):
        return (nc - 1 - i) if reverse else i

    def tok(col):
        return pl.BlockSpec((n_streams, c, hg_w), lambda i, r: (0, cmap(i), col))

    vec = pl.BlockSpec((1, hg_w), lambda i, r: (0, 0))
    in_specs = [tok(cols["q"]), tok(z_col), tok(cols["i"])]
    args = [p3, p3, p3]
    if final:
        o_other, norm_g = extra
        in_specs += [tok(cols["og"]), tok(0)]
        args += [p3, o_other]
    in_specs += [vec, vec, vec]
    args += list(lbs)
    if final:
        in_specs += [vec]
        args += [norm_g.reshape(1, hg_w).astype(F32)]
    in_specs += [pl.BlockSpec(m_np.shape, lambda i, r: (0, 0)), pl.BlockSpec(w_np.shape, lambda i, r: (0, 0, 0))]
    args += [jnp.asarray(m_np, BF16), jnp.asarray(w_np, F32)]
    grid_spec = pltpu.PrefetchScalarGridSpec(
        num_scalar_prefetch=1, grid=(nc,), in_specs=in_specs, out_specs=tok(0),
        scratch_shapes=[pltpu.VMEM((n_streams * heads, HEAD_DIM, HEAD_DIM), F32)])
    order = np.arange(nc)[::-1] if reverse else np.arange(nc)
    reset_steps = jnp.asarray(np.asarray(reset, np.int32)[order].reshape(-1))
    return pl.pallas_call(
        functools.partial(_hgrn2_kernel, heads, n_lvl, 0 if reverse else c - 1, final), grid_spec=grid_spec,
        out_shape=jax.ShapeDtypeStruct((n_streams, n, hg_w), BF16 if final else F32),
        compiler_params=_params("arbitrary"), name="hgrn2_fwd" if final else "hgrn2_bwd",
    )(reset_steps, *args)


def _hgrn2(p, cols, hg_w, lb, norm_g, seq_lens, n_streams=2, c=HG_CHUNK):
    lb = lb.astype(F32).reshape(1, hg_w)
    lbs = (jnp.log(lb), jnp.log1p(-lb), 1.0 - lb)
    starts = np.cumsum([0] + list(seq_lens))
    n = int(starts[-1])
    per = n // n_streams
    assert n == p.shape[0] and all(s * per in starts for s in range(n_streams)), "a sequence straddles a stream cut"
    first = np.zeros(n // c, np.int32)
    last = np.zeros(n // c, np.int32)
    first[starts[:-1] // c] = 1
    last[starts[1:] // c - 1] = 1
    first = first.reshape(n_streams, per // c).T
    last = last.reshape(n_streams, per // c).T
    p3 = p.reshape(n_streams, per, p.shape[1])
    o_b = _hgrn2_pass(p3, cols, hg_w, cols["zb"], True, last, lbs, None, c)
    return _hgrn2_pass(p3, cols, hg_w, cols["zf"], False, first, lbs, (o_b, norm_g), c).reshape(n, hg_w)


def _dft_matrices(t):
    n = 2 * t
    f = jnp.arange(t, dtype=jnp.int32)
    ang = (2.0 * math.pi / n) * ((f[:, None] * f[None, :]) % n).astype(F32)
    cos, sin = jnp.cos(ang), jnp.sin(ang)
    alt = (1 - 2 * (f % 2)).astype(F32)
    first = f == 0
    fwd = jnp.stack([cos, jnp.where(first[:, None], alt[None, :], -sin)], 0).astype(BF16)
    wgt = jnp.where(first, 1.0, 2.0)[None, :] / n
    inv = jnp.stack([cos * wgt, jnp.where(first[None, :], alt[:, None] / n, -sin * wgt)], 0).astype(BF16)
    return fwd, inv


def _shortconv_kernel(flags_ref, x_ref, prev_ref, next_ref, w_ref, b_ref, o_ref):
    i = pl.program_id(0)
    x = x_ref[...]
    t = x.shape[0]
    rows = lax.broadcasted_iota(jnp.int32, x.shape, 0)
    has_prev = flags_ref[2 * i].astype(F32)
    has_next = flags_ref[2 * i + 1].astype(F32)
    up = jnp.where(rows == 0, prev_ref[7:8, :] * has_prev, pltpu.roll(x, 1, 0))
    dn = jnp.where(rows == t - 1, next_ref[0:1, :] * has_next, pltpu.roll(x, t - 1, 0))
    o_ref[...] = up * w_ref[0:1, :] + x * w_ref[1:2, :] + dn * w_ref[2:3, :] + b_ref[...]


def _shortconv(p, width, conv_w, conv_b, blk_flags, t=HY_BLOCK, ct=768):
    n = p.shape[0]
    nb = n // t
    r8 = t // 8
    grid_spec = pltpu.PrefetchScalarGridSpec(
        num_scalar_prefetch=1, grid=(nb, width // ct),
        in_specs=[pl.BlockSpec((t, ct), lambda i, j, f: (i, j)),
                  pl.BlockSpec((8, ct), lambda i, j, f: (jnp.maximum(i * r8 - 1, 0), j)),
                  pl.BlockSpec((8, ct), lambda i, j, f: (jnp.minimum((i + 1) * r8, nb * r8 - 1), j)),
                  pl.BlockSpec((3, ct), lambda i, j, f: (0, j)), pl.BlockSpec((1, ct), lambda i, j, f: (0, j))],
        out_specs=pl.BlockSpec((t, ct), lambda i, j, f: (i, j)))
    return pl.pallas_call(
        _shortconv_kernel, grid_spec=grid_spec, out_shape=jax.ShapeDtypeStruct((n, width), F32),
        compiler_params=_params("parallel", "parallel"), name="hyena_shortconv",
    )(blk_flags, p, p, p, conv_w.astype(F32), conv_b.reshape(1, width).astype(F32))


def _filter_kernel(l_total, feats_ref, w1_ref, b1_ref, w2_ref, b2_ref, w3_ref, b3_ref, fr_ref, wo_ref, dl_ref,
                   h_ref, sum_ref):
    del l_total
    i = pl.program_id(0)
    bt = feats_ref.shape[0]
    half_lanes = LANES // 2
    fr = fr_ref[...]
    feats = feats_ref[...]
    h = jnp.sin(fr * (_dot3(feats, w1_ref[...]) + b1_ref[...]))
    h = jnp.sin(fr * (_dot3(h, w2_ref[...]) + b2_ref[...]))
    h = jnp.sin(fr * (_dot3(h, w3_ref[...]) + b3_ref[...]))
    reps = wo_ref.shape[2] // dl_ref.shape[1]
    row = lax.broadcasted_iota(jnp.int32, (bt, 1), 0) + i * bt
    total = jnp.zeros((1, wo_ref.shape[2]), F32)
    for side in range(2):
        window = jnp.exp(-feats[:, side * half_lanes:side * half_lanes + 1] * dl_ref[...])
        out = _dot3(h, wo_ref[side]) * jnp.concatenate([window] * reps, axis=1)
        if side == 1:
            out = jnp.where(row == 0, 0.0, out)
        h_ref[side] = out
        total = total + jnp.sum(jnp.abs(out), axis=0, keepdims=True)

    @pl.when(i == 0)
    def _():
        sum_ref[...] = jnp.zeros_like(sum_ref)

    sum_ref[...] += total


def _hyena_filter(l, w1, b1, w2, b2, w3, b3, freq, w_out, hy_w, bt=512):
    hid = w2.shape[0]
    half_lanes = LANES // 2
    assert hid <= half_lanes and w1.shape[0] <= half_lanes
    half = HY_ORDER * hy_w
    bands = jnp.linspace(1e-4, HY_POS_BANDS - 1, HY_POS_BANDS, dtype=F32)

    def features(pos):
        tt = pos / max(l - 1, 1)
        ang = (2.0 * math.pi / l) * pos[:, None] * bands[None, :]
        f = jnp.concatenate([tt[:, None], jnp.cos(ang), -jnp.sin(ang)], -1)
        return jnp.pad(f, ((0, 0), (0, half_lanes - f.shape[1])))

    def twice_diag(w):
        wp = jnp.pad(w.astype(F32), ((0, half_lanes - w.shape[0]), (0, half_lanes - w.shape[1])))
        z = jnp.zeros_like(wp)
        return jnp.concatenate([jnp.concatenate([wp, z], 1), jnp.concatenate([z, wp], 1)], 0)

    def twice_vec(a):
        ap = jnp.pad(a.astype(F32), (0, half_lanes - a.shape[0]))
        return jnp.concatenate([ap, ap]).reshape(1, LANES)

    pos = jnp.arange(l, dtype=F32)
    feats = jnp.concatenate([features(pos), features(l - pos)], 1)
    wo = w_out.astype(F32).reshape(hid, HY_ORDER, 2, hy_w).transpose(2, 0, 1, 3).reshape(2, hid, half)
    wo = jnp.pad(wo, ((0, 0), (0, half_lanes - hid), (0, 0)))
    zero = jnp.zeros_like(wo[0])
    wop = jnp.stack([jnp.concatenate([wo[0], zero], 0), jnp.concatenate([zero, wo[1]], 0)], 0)
    deltas = jnp.abs(jnp.linspace(math.log(HY_DECAY_TARGET) / HY_SLOW_DECAY,
                                  math.log(HY_DECAY_TARGET) / HY_FAST_DECAY, hy_w, dtype=F32)).reshape(1, hy_w)
    bt = min(bt, l)
    sq = pl.BlockSpec((LANES, LANES), lambda i: (0, 0))
    vec = pl.BlockSpec((1, LANES), lambda i: (0, 0))
    taps, abs_sum = pl.pallas_call(
        functools.partial(_filter_kernel, l), grid=(l // bt,),
        in_specs=[pl.BlockSpec((bt, LANES), lambda i: (i, 0)), sq, vec, sq, vec, sq, vec, vec,
                  pl.BlockSpec((2, LANES, half), lambda i: (0, 0, 0)),
                  pl.BlockSpec((1, hy_w), lambda i: (0, 0))],
        out_specs=[pl.BlockSpec((2, bt, half), lambda i: (0, i, 0)), pl.BlockSpec((1, half), lambda i: (0, 0))],
        out_shape=[jax.ShapeDtypeStruct((2, l, half), F32), jax.ShapeDtypeStruct((1, half), F32)],
        compiler_params=_params("arbitrary"), name="hyena_filter",
    )(feats, twice_diag(w1), twice_vec(b1), twice_diag(w2), twice_vec(b2), twice_diag(w3), twice_vec(b3),
      twice_vec(freq), wop, deltas)
    return taps.reshape(2 * l, half), abs_sum


def _dft_cols(t, width):
    return width if t <= 1024 else 256


def _dft_kernel(f_ref, x_ref, o_ref):
    x = x_ref[...].astype(BF16)
    o_ref[0] = jnp.dot(f_ref[0], x, preferred_element_type=F32).astype(o_ref.dtype)
    o_ref[1] = jnp.dot(f_ref[1], x, preferred_element_type=F32).astype(o_ref.dtype)


def _block_dft(fwd, x, col_off, width, out_dtype, t=HY_BLOCK, ct=256):
    nseg = x.shape[0] // t
    ct = min(ct, width)
    return pl.pallas_call(
        _dft_kernel, grid=(nseg, width // ct),
        in_specs=[pl.BlockSpec((2, t, t), lambda s, j: (0, 0, 0)),
                  pl.BlockSpec((t, ct), lambda s, j: (s, j + col_off))],
        out_specs=pl.BlockSpec((None, 2, t, ct), lambda s, j: (s, 0, 0, j)),
        out_shape=jax.ShapeDtypeStruct((nseg, 2, t, width), out_dtype),
        compiler_params=_params("parallel", "parallel"), name="hyena_block_dft",
    )(fwd, x)


def _segment_index(d, nb):
    return d if d >= 0 else 2 * nb + d


def _hyena_filter_spectra(l, fwd, filt_w, hy_w, t=HY_BLOCK):
    nb = l // t
    taps, abs_sum = _hyena_filter(l, *filt_w, hy_w)
    half = taps.shape[1]
    seg_spec = _block_dft(fwd, taps, 0, half, F32, t, _dft_cols(t, hy_w))
    first_rows = taps.reshape(2 * nb, t, half)[:, 0, :]
    seg0 = jnp.stack([first_rows[_segment_index(d - 1, nb)] for d in range(-(nb - 1), nb)], 0)
    return seg_spec, seg0.reshape(2 * nb - 1, 1, half), 1.0 / abs_sum


MIX_ROWS = 8


def _mix_kernel(nb, z_ref, a_ref, s0_ref, inv_ref, *refs):
    o_ref, k_ref = refs[-2], refs[-1]
    ft, ct = z_ref.shape[2], z_ref.shape[3]
    first_tile = pl.program_id(1) == 0

    tile_rows = lax.broadcasted_iota(jnp.int32, (ft, ct), 0)
    sgn = (1 - 2 * (tile_rows & 1)).astype(F32)
    real_row = jnp.logical_and(tile_rows == 0, first_tile)
    inv = inv_ref[...]
    for slot in range(2 * nb - 1):
        d = slot - (nb - 1)
        ia, ib = _segment_index(d, nb), _segment_index(d - 1, nb)
        s0 = s0_ref[slot]
        k_ref[slot, 0] = (a_ref[ia, 0] + sgn * (a_ref[ib, 0] - s0)) * inv
        k_ref[slot, 1] = (a_ref[ia, 1] + sgn * (a_ref[ib, 1] - jnp.where(real_row, s0, 0.0))) * inv

    n_out = 2 if nb % 2 == 0 else 1

    def out_rows(i, r0, masked):
        rows = pl.ds(r0, MIX_ROWS)
        acc = [[jnp.zeros((MIX_ROWS, ct), F32) for _ in range(2)] for _ in range(n_out)]
        if masked:
            row0 = jnp.logical_and(lax.broadcasted_iota(jnp.int32, (MIX_ROWS, ct), 0) == 0, first_tile)
        d_hi = i + (n_out - 1) + (nb - 1)
        k_prev = (k_ref[d_hi, 0, rows, :], k_ref[d_hi, 1, rows, :]) if n_out == 2 else None
        for j in range(nb):
            d = i - j + (nb - 1)
            zt, zb = z_ref[j, 0, rows, :], z_ref[j, 1, rows, :]
            k_cur = (k_ref[d, 0, rows, :], k_ref[d, 1, rows, :])
            for o, (kt, kb) in enumerate((k_cur, k_prev)[:n_out]):
                bb = zb * kb
                if masked:
                    acc[o][0] = acc[o][0] + (zt * kt - jnp.where(row0, 0.0, bb))
                    acc[o][1] = acc[o][1] + jnp.where(row0, bb, zt * kb + zb * kt)
                else:
                    acc[o][0] = acc[o][0] + (zt * kt - bb)
                    acc[o][1] = acc[o][1] + (zt * kb + zb * kt)
            k_prev = k_cur
        for o in range(n_out):
            o_ref[i + o, 0, rows, :] = acc[o][0].astype(o_ref.dtype)
            o_ref[i + o, 1, rows, :] = acc[o][1].astype(o_ref.dtype)

    def all_outputs(r0, masked):
        def body(ip, carry):
            out_rows(ip * n_out, r0, masked)
            return carry
        lax.fori_loop(0, nb // n_out, body, 0)

    all_outputs(0, True)

    def chunk(r, carry):
        all_outputs(pl.multiple_of(r * MIX_ROWS, MIX_ROWS), False)
        return carry

    lax.fori_loop(1, ft // MIX_ROWS, chunk, 0)


def _hyena_mix(zspec, kspec, k_col, n_seq, nb, blk_off, ct=256):
    seg_spec, seg0, inv_norm = kspec
    _, _, t, width = zspec.shape
    ft = max(MIX_ROWS, min(t, 2048 // nb))
    nct = width // ct
    assert blk_off % nb == 0 and t % ft == 0
    s_off = blk_off // nb
    return pl.pallas_call(
        functools.partial(_mix_kernel, nb), grid=(n_seq, t // ft, nct),
        in_specs=[pl.BlockSpec((nb, 2, ft, ct), lambda s, fi, j: (s + s_off, 0, fi, j)),
                  pl.BlockSpec((2 * nb, 2, ft, ct), lambda s, fi, j: (0, 0, fi, j + k_col * nct)),
                  pl.BlockSpec((2 * nb - 1, 1, ct), lambda s, fi, j: (0, 0, j + k_col * nct)),
                  pl.BlockSpec((1, ct), lambda s, fi, j: (0, j + k_col * nct))],
        out_specs=pl.BlockSpec((nb, 2, ft, ct), lambda s, fi, j: (s, 0, fi, j)),
        out_shape=jax.ShapeDtypeStruct((n_seq * nb, 2, t, width), BF16),
        scratch_shapes=[pltpu.VMEM((2 * nb - 1, 2, ft, ct), F32)],
        compiler_params=_params("parallel", "parallel", "parallel"), name="hyena_mix",
    )(zspec, seg_spec, seg0, inv_norm)


def _inverse_kernel(bounds, *refs):
    n_groups = len(bounds)
    y_refs = refs[:n_groups]
    fi_ref, zin_ref, gate_ref, bias_ref, o_ref = refs[n_groups:]
    i = pl.program_id(0)
    for (lo, hi), y_ref in zip(bounds, y_refs):
        @pl.when(jnp.logical_and(i >= lo, i < hi))
        def _():
            y = (jnp.dot(fi_ref[0], y_ref[0], preferred_element_type=F32)
                 + jnp.dot(fi_ref[1], y_ref[1], preferred_element_type=F32))
            o_ref[...] = (gate_ref[...] * (y + zin_ref[...] * bias_ref[...])).astype(o_ref.dtype)


def _hyena_inverse(yspecs, inv, zin, zin_col, gate, gate_col, bias, out_dtype, ct=256):
    _, _, t, width = yspecs[0].shape
    nct = width // ct
    bounds, lo = [], 0
    for y in yspecs:
        bounds.append((lo, lo + y.shape[0]))
        lo += y.shape[0]

    def group_spec(lo, hi):
        return pl.BlockSpec((None, 2, t, ct), lambda i, j: (jnp.clip(i - lo, 0, hi - lo - 1), 0, 0, j))

    return pl.pallas_call(
        functools.partial(_inverse_kernel, tuple(bounds)), grid=(lo, nct),
        in_specs=[group_spec(*b) for b in bounds] + [
            pl.BlockSpec((2, t, t), lambda i, j: (0, 0, 0), pipeline_mode=pl.Buffered(1)),
            pl.BlockSpec((t, ct), lambda i, j: (i, j + zin_col * nct)),
            pl.BlockSpec((t, ct), lambda i, j: (i, j + gate_col * nct)),
            pl.BlockSpec((1, ct), lambda i, j: (0, j))],
        out_specs=pl.BlockSpec((t, ct), lambda i, j: (i, j)),
        out_shape=jax.ShapeDtypeStruct((lo * t, width), out_dtype),
        compiler_params=_params("arbitrary", "arbitrary"), name="hyena_inverse",
    )(*yspecs, inv, zin, gate, bias)


def _hyena(p, hy_w, groups, conv_w, conv_b, filt_w, bias, t=HY_BLOCK):
    flags = []
    for n_seq, l in groups:
        nb = l // t
        flags += [int(bi > 0) if side == 0 else int(bi < nb - 1)
                  for _ in range(n_seq) for bi in range(nb) for side in range(2)]
    fwd, inv = _dft_matrices(t)
    u = _shortconv(p, 3 * hy_w, conv_w, conv_b, jnp.asarray(np.asarray(flags, np.int32)), t, ct=hy_w)
    kspecs = [_hyena_filter_spectra(l, fwd, filt_w, hy_w, t) for _, l in groups]
    bias = bias.astype(F32)
    ct = _dft_cols(t, hy_w)
    z, z_col = u, 2
    for o in range(HY_ORDER):
        zspec = _block_dft(fwd, z, z_col * (hy_w // ct), hy_w, F32, t, ct)
        yspecs, blk = [], 0
        for (n_seq, l), kspec in zip(groups, kspecs):
            yspecs.append(_hyena_mix(zspec, kspec, o, n_seq, l // t, blk))
            blk += n_seq * (l // t)
        last = o == HY_ORDER - 1
        z = _hyena_inverse(yspecs, inv, z, z_col, u, o, bias[o:o + 1], BF16 if last else F32, ct)
        z_col = 0
    return z


def kernel(x_prompt, x_sample, ln_in_g, ln_in_b, w_in, hy_conv_w, hy_conv_b, hy_pos_w1, hy_pos_b1, hy_pos_w2, hy_pos_b2, hy_pos_w3, hy_pos_b3, hy_sin_freq, hy_pos_wout, hy_bias, gm_ln_g, gm_ln_b, gm_ws, gm_bs, hg_lb_raw, hg_norm_g, w_out, ln1_g, ln1_b, ln2_g, ln2_b, ffn_w1, ffn_w3, ffn_w2, moe_router_w, moe_router_b, moe_w1, moe_w3, moe_w2):
    depth, d_model, in_w = w_in.shape
    hy_w = hy_bias.shape[-1]
    gm_w = gm_ln_g.shape[-1]
    hg_w = hg_norm_g.shape[-1]
    o1 = 3 * hy_w
    o2 = o1 + 2 * gm_w
    alpha = (2 * depth) ** 0.25
    groups = [(x_prompt.shape[0], x_prompt.shape[1]), (x_sample.shape[0], x_sample.shape[1])]
    seq_lens = [l for n_seq, l in groups for _ in range(n_seq)]
    n_prompt = x_prompt.shape[0] * x_prompt.shape[1]
    hg_off = o1 // hg_w
    hg_cols = {"q": hg_off, "zf": hg_off + 1, "zb": hg_off + 2, "i": hg_off + 3, "og": hg_off + 4}
    gm_off = (o1 + 5 * hg_w) // gm_w

    lb_all = jnp.cumsum(jax.nn.softmax(hg_lb_raw.astype(F32), axis=0), axis=0)
    lb_all = lb_all - lb_all[:1]

    x, xb = _layernorm_pair(x_prompt.reshape(-1, d_model), x_sample.reshape(-1, d_model), ln_in_g, ln_in_b)
    for l in range(depth):
        w_l = jnp.concatenate([w_in[l][:, :o1], w_in[l][:, o2:], w_in[l][:, o1:o2]], 1).astype(BF16)
        p = _matmul(xb, w_l)
        filt_w = (hy_pos_w1[l], hy_pos_b1[l], hy_pos_w2[l], hy_pos_b2[l], hy_pos_w3[l], hy_pos_b3[l],
                  hy_sin_freq[l], hy_pos_wout[l])
        y_hy = _hyena(p, hy_w, groups, hy_conv_w[l], hy_conv_b[l], filt_w, hy_bias[l])
        y_hg = _hgrn2(p, hg_cols, hg_w, lb_all[l], hg_norm_g[l], seq_lens)
        y_gm = _gmlp(p, gm_off, gm_off + 1, gm_w, gm_ln_g[l], gm_ln_b[l], gm_ws[l], gm_bs[l])
        wo = w_out[l].astype(BF16)
        parts = [y_hy, y_gm, y_hg]
        wo_parts = [wo[:hy_w], wo[hy_w:hy_w + gm_w], wo[hy_w + gm_w:]]
        j = l // 2
        if l % 2 == 0:
            x, xb = _mix_out_residual_ln(parts, wo_parts, x, ln1_g[l], ln1_b[l], alpha)
            x, xb = _ffn_residual_ln(xb, ffn_w1[j].astype(BF16), ffn_w3[j].astype(BF16), ffn_w2[j].astype(BF16),
                                     x, ln2_g[l], ln2_b[l], alpha)
        else:
            n_exp = moe_router_w.shape[-1]
            wr = jnp.pad(moe_router_w[j].astype(BF16), ((0, 0), (0, LANES - n_exp)))
            x, xw, logits = _mix_out_residual_ln(parts, wo_parts, x, ln1_g[l], ln1_b[l], alpha, wr)
            logits = logits[:, :n_exp] + moe_router_b[j].astype(F32)
            y0, y1, gate = _moe_dispatch_ffn(logits, xw, moe_w1[j], moe_w3[j], _cast_bf16(moe_w2[j]))
            xa, xs = _combine_residual_ln_split(x, y0, y1, gate, ln2_g[l], ln2_b[l], alpha, n_prompt)
            if l == depth - 1:
                return (xa.reshape(x_prompt.shape), xs.reshape(x_sample.shape))
            x = jnp.concatenate([xa, xs], 0)
            xb = x.astype(BF16)
    return (x[:n_prompt].reshape(x_prompt.shape), x[n_prompt:].reshape(x_sample.shape))
```
